```python
import math
import jax, jax.numpy as jnp
from jax import lax
import numpy as np

D_MODEL = 1024
BATCH = 8
SEQ = 4096
DEPTH = 2

CHUNK = 64
Q_BLOCK = 128
N_MEM = 256
H_A = 4
DQK_A = 64
DV_A = 2 * DQK_A
H_B = 4
DH_B = 64
LEFT_CHUNKS = 8
REL_MAX = 128
H_C = 4
DH_C = 64
H_IDX = 8
D_IDX = 64
TOPK_MAX = 256
D_MIX = H_A * DV_A + H_B * DH_B + H_C * DH_C
SPLITS = (H_A * 2 * DQK_A, H_A * 2 * DQK_A, H_A * DV_A,
          H_B * DH_B, H_B * DH_B, H_B * DH_B,
          H_C * DH_C, H_C * DH_C, H_C * DH_C,
          H_IDX * D_IDX, D_IDX, H_IDX)
N_IN = sum(SPLITS)
H_X = 4
DH_X = D_MODEL // H_X
D_FF = 2816
N_EXPERTS = 8
TOP_K = 2
D_FF_E = 3584
ALPHA = (2.0 * DEPTH) ** 0.25
BETA = (8.0 * DEPTH) ** -0.25
LN_EPS = 1e-5
RMS_EPS = 1e-6
NEG_INF = -1e30

kernel_name = "hybrid_chunk_causal_diff_band_dsa_moe_encoder"


def _layer_norm(x, g, b):
    xf = x.astype(jnp.float32)
    mu = jnp.mean(xf, axis=-1, keepdims=True)
    var = jnp.mean(jnp.square(xf - mu), axis=-1, keepdims=True)
    y = (xf - mu) * lax.rsqrt(var + LN_EPS) * g.astype(jnp.float32) + b.astype(jnp.float32)
    return y.astype(x.dtype)


def _alibi_slopes():
    n = H_A + H_C
    s = jnp.asarray(2.0 ** (-8.0 * np.arange(1, n + 1) / n), dtype=jnp.float32)
    return s[0::2], s[1::2]


def _chunk_allowed(qpos, kpos):
    return (kpos[None, :] // CHUNK) <= (qpos[:, None] // CHUNK)


def _diff_attention(qa, ka, va, lam, lam_init, gain, slopes):
    b, s_len = qa.shape[0], qa.shape[1]
    nb = s_len // Q_BLOCK
    q_blocks = jnp.swapaxes(qa.reshape(b, nb, Q_BLOCK, H_A, 2, DQK_A), 0, 1)
    kpos = jnp.arange(s_len)
    scale = DQK_A ** -0.5

    def block(args):
        qb, i = args
        qpos = i * Q_BLOCK + jnp.arange(Q_BLOCK)
        sc = jnp.einsum('bqhmd,bkhmd->bhmqk', qb, ka).astype(jnp.float32) * scale
        dist = jnp.abs(qpos[:, None] - kpos[None, :]).astype(jnp.float32)
        sc = sc - slopes[None, :, None, None, None] * dist
        sc = jnp.where(_chunk_allowed(qpos, kpos), sc, NEG_INF)
        p = jax.nn.softmax(sc, axis=-1)
        a = p[:, :, 0] - lam * p[:, :, 1]
        return jnp.einsum('bhqk,bkhe->bqhe', a.astype(va.dtype), va)

    o = lax.map(block, (q_blocks, jnp.arange(nb)))
    o = jnp.swapaxes(o, 0, 1).reshape(b, s_len, H_A, DV_A).astype(jnp.float32)
    o = o * lax.rsqrt(jnp.mean(jnp.square(o), axis=-1, keepdims=True) + RMS_EPS)
    o = o * gain.astype(jnp.float32) * (1.0 - lam_init)
    return o.reshape(b, s_len, H_A * DV_A).astype(va.dtype)


def _band_attention(qb, kb, vb, table):
    b, s_len = qb.shape[0], qb.shape[1]
    nc = s_len // CHUNK
    left = LEFT_CHUNKS * CHUNK
    band = (LEFT_CHUNKS + 1) * CHUNK
    qc = qb.reshape(b, nc, CHUNK, H_B, DH_B)

    def gather_band(t):
        tp = jnp.pad(t, ((0, 0), (left, 0), (0, 0), (0, 0)))
        tp = tp.reshape(b, nc + LEFT_CHUNKS, CHUNK, H_B, DH_B)
        return jnp.concatenate([tp[:, j:j + nc] for j in range(LEFT_CHUNKS + 1)], axis=2)

    kband = gather_band(kb)
    vband = gather_band(vb)
    sc = jnp.einsum('bcqhd,bckhd->bchqk', qc, kband).astype(jnp.float32) * DH_B ** -0.5
    qi = jnp.arange(CHUNK)
    kj = jnp.arange(band)
    rel = qi[:, None] - (kj[None, :] - left)
    bias = table[:, jnp.clip(rel, -REL_MAX, REL_MAX) + REL_MAX].astype(jnp.float32)
    kabs = jnp.arange(nc)[:, None] * CHUNK + kj[None, :] - left
    sc = sc + bias[None, None]
    sc = jnp.where((kabs >= 0)[None, :, None, None, :], sc, NEG_INF)
    p = jax.nn.softmax(sc, axis=-1)
    o = jnp.einsum('bchqk,bckhd->bcqhd', p.astype(vb.dtype), vband)
    return o.reshape(b, s_len, H_B * DH_B)


def _dsa_attention(qc, kc, vc, qi, ki, wi, slopes):
    b, s_len = qc.shape[0], qc.shape[1]
    nb = s_len // Q_BLOCK
    topk = min(TOPK_MAX, s_len // 4)
    kpos = jnp.arange(s_len)

    def to_blocks(t):
        return jnp.swapaxes(t.reshape((b, nb, Q_BLOCK) + t.shape[2:]), 0, 1)

    def block(args):
        qb, qib, wib, i = args
        qpos = i * Q_BLOCK + jnp.arange(Q_BLOCK)
        dots = jnp.einsum('bqhd,bsd->bqhs', qib, ki).astype(jnp.float32) * D_IDX ** -0.5
        isc = jnp.einsum('bqhs,bqh->bqs', jax.nn.relu(dots), wib.astype(jnp.float32)) * H_IDX ** -0.5
        isc = jnp.where(_chunk_allowed(qpos, kpos)[None], isc, NEG_INF)
        top_val, top_idx = lax.top_k(isc, topk)
        kg = jax.vmap(lambda t, ix: t[ix])(kc, top_idx)
        vg = jax.vmap(lambda t, ix: t[ix])(vc, top_idx)
        sc = jnp.einsum('bqhd,bqkhd->bhqk', qb, kg).astype(jnp.float32) * DH_C ** -0.5
        dist = jnp.abs(qpos[None, :, None] - top_idx).astype(jnp.float32)
        sc = sc - slopes[None, :, None, None] * dist[:, None]
        sc = jnp.where((top_val > 0.5 * NEG_INF)[:, None], sc, NEG_INF)
        p = jax.nn.softmax(sc, axis=-1)
        return jnp.einsum('bhqk,bqkhd->bqhd', p.astype(vc.dtype), vg)

    o = lax.map(block, (to_blocks(qc), to_blocks(qi), to_blocks(wi), jnp.arange(nb)))
    return jnp.swapaxes(o, 0, 1).reshape(b, s_len, H_C * DH_C)


def _hybrid_mixer(h, w_in, lq1, lk1, lq2, lk2, dgain, rel_table, w_out, layer_idx, slopes_a, slopes_c):
    b, s_len, _ = h.shape
    proj = h @ w_in
    offsets = [int(o) for o in np.cumsum(SPLITS)[:-1]]
    qa, ka, va, qb, kb, vb, qc, kc, vc, qi, ki, wi = jnp.split(proj, offsets, axis=-1)
    lam_init = 0.8 - 0.6 * math.exp(-0.3 * layer_idx)
    lam = (jnp.exp(jnp.sum(lq1.astype(jnp.float32) * lk1.astype(jnp.float32)))
           - jnp.exp(jnp.sum(lq2.astype(jnp.float32) * lk2.astype(jnp.float32))) + lam_init)
    ya = _diff_attention(qa.reshape(b, s_len, H_A, 2, DQK_A), ka.reshape(b, s_len, H_A, 2, DQK_A),
                         va.reshape(b, s_len, H_A, DV_A), lam, lam_init, dgain, slopes_a)
    yb = _band_attention(qb.reshape(b, s_len, H_B, DH_B), kb.reshape(b, s_len, H_B, DH_B),
                         vb.reshape(b, s_len, H_B, DH_B), rel_table)
    yc = _dsa_attention(qc.reshape(b, s_len, H_C, DH_C), kc.reshape(b, s_len, H_C, DH_C),
                        vc.reshape(b, s_len, H_C, DH_C), qi.reshape(b, s_len, H_IDX, D_IDX),
                        ki, wi, slopes_c)
    return jnp.concatenate([ya, yb, yc], axis=-1) @ w_out


def _mem_cross_attention(h, mem, wq, wk, wv, wo):
    b, s_len, _ = h.shape
    q = (h @ wq).reshape(b, s_len, H_X, DH_X)
    k = (mem @ wk).reshape(b, -1, H_X, DH_X)
    v = (mem @ wv).reshape(b, -1, H_X, DH_X)
    sc = jnp.einsum('bqhd,bkhd->bhqk', q, k).astype(jnp.float32) * DH_X ** -0.5
    p = jax.nn.softmax(sc, axis=-1)
    o = jnp.einsum('bhqk,bkhd->bqhd', p.astype(v.dtype), v).reshape(b, s_len, D_MODEL)
    return o @ wo


def _swiglu(h, w1, w3, w2):
    return (jax.nn.silu(h @ w1) * (h @ w3)) @ w2


def _moe(h, router, w1, w3, w2):
    b, s_len, d = h.shape
    t = h.reshape(-1, d)
    logits = (t @ router).astype(jnp.float32)
    top_val, top_idx = lax.top_k(logits, TOP_K)
    gates = jax.nn.softmax(top_val, axis=-1)
    gate_full = jnp.sum(jax.nn.one_hot(top_idx, N_EXPERTS, dtype=jnp.float32) * gates[..., None], axis=1)
    out = jnp.zeros_like(t)
    for e in range(N_EXPERTS):
        out = out + gate_full[:, e:e + 1].astype(t.dtype) * _swiglu(t, w1[e], w3[e], w2[e])
    return out.reshape(b, s_len, d)


def setup_inputs(seed: int = 0) -> dict:
    key = jax.random.key(seed)
    ks = jax.random.split(key, 27)

    def nrm(k, shape, scale):
        return jax.random.normal(k, shape, jnp.float32) * scale

    n_dense = (DEPTH + 1) // 2
    n_moe = DEPTH // 2
    return {
        'x': nrm(ks[0], (BATCH, SEQ, D_MODEL), 1.0),
        'mem': nrm(ks[1], (BATCH, N_MEM, D_MODEL), 1.0),
        'w_in': nrm(ks[2], (DEPTH, D_MODEL, N_IN), D_MODEL ** -0.5),
        'lam_q1': nrm(ks[3], (DEPTH, DQK_A), 0.1),
        'lam_k1': nrm(ks[4], (DEPTH, DQK_A), 0.1),
        'lam_q2': nrm(ks[5], (DEPTH, DQK_A), 0.1),
        'lam_k2': nrm(ks[6], (DEPTH, DQK_A), 0.1),
        'diff_norm_g': 1.0 + nrm(ks[7], (DEPTH, DV_A), 0.02),
        'rel_bias': nrm(ks[8], (DEPTH, H_B, 2 * REL_MAX + 1), 0.1),
        'w_out': nrm(ks[9], (DEPTH, D_MIX, D_MODEL), D_MIX ** -0.5 * BETA),
        'ln1_g': 1.0 + nrm(ks[10], (DEPTH, D_MODEL), 0.02),
        'ln1_b': nrm(ks[11], (DEPTH, D_MODEL), 0.02),
        'xattn_wq': nrm(ks[12], (DEPTH, D_MODEL, D_MODEL), D_MODEL ** -0.5),
        'xattn_wk': nrm(ks[13], (DEPTH, D_MODEL, D_MODEL), D_MODEL ** -0.5),
        'xattn_wv': nrm(ks[14], (DEPTH, D_MODEL, D_MODEL), D_MODEL ** -0.5),
        'xattn_wo': nrm(ks[15], (DEPTH, D_MODEL, D_MODEL), D_MODEL ** -0.5 * BETA),
        'ln2_g': 1.0 + nrm(ks[16], (DEPTH, D_MODEL), 0.02),
        'ln2_b': nrm(ks[17], (DEPTH, D_MODEL), 0.02),
        'ffn_w1': nrm(ks[18], (n_dense, D_MODEL, D_FF), D_MODEL ** -0.5),
        'ffn_w3': nrm(ks[19], (n_dense, D_MODEL, D_FF), D_MODEL ** -0.5),
        'ffn_w2': nrm(ks[20], (n_dense, D_FF, D_MODEL), D_FF ** -0.5 * BETA),
        'moe_router': nrm(ks[21], (n_moe, D_MODEL, N_EXPERTS), D_MODEL ** -0.5),
        'moe_w1': nrm(ks[22], (n_moe, N_EXPERTS, D_MODEL, D_FF_E), D_MODEL ** -0.5),
        'moe_w3': nrm(ks[23], (n_moe, N_EXPERTS, D_MODEL, D_FF_E), D_MODEL ** -0.5),
        'moe_w2': nrm(ks[24], (n_moe, N_EXPERTS, D_FF_E, D_MODEL), D_FF_E ** -0.5 * BETA),
        'ln3_g': 1.0 + nrm(ks[25], (DEPTH, D_MODEL), 0.02),
        'ln3_b': nrm(ks[26], (DEPTH, D_MODEL), 0.02),
    }


def reference(x, mem, w_in, lam_q1, lam_k1, lam_q2, lam_k2, diff_norm_g, rel_bias, w_out,
              ln1_g, ln1_b, xattn_wq, xattn_wk, xattn_wv, xattn_wo, ln2_g, ln2_b,
              ffn_w1, ffn_w3, ffn_w2, moe_router, moe_w1, moe_w3, moe_w2, ln3_g, ln3_b):
    slopes_a, slopes_c = _alibi_slopes()
    h = x
    for l in range(DEPTH):
        mix = _hybrid_mixer(h, w_in[l], lam_q1[l], lam_k1[l], lam_q2[l], lam_k2[l],
                            diff_norm_g[l], rel_bias[l], w_out[l], l, slopes_a, slopes_c)
        h = _layer_norm(ALPHA * h + mix, ln1_g[l], ln1_b[l])
        xa = _mem_cross_attention(h, mem, xattn_wq[l], xattn_wk[l], xattn_wv[l], xattn_wo[l])
        h = _layer_norm(ALPHA * h + xa, ln2_g[l], ln2_b[l])
        if l % 2 == 0:
            f = _swiglu(h, ffn_w1[l // 2], ffn_w3[l // 2], ffn_w2[l // 2])
        else:
            f = _moe(h, moe_router[l // 2], moe_w1[l // 2], moe_w3[l // 2], moe_w2[l // 2])
        h = _layer_norm(ALPHA * h + f, ln3_g[l], ln3_b[l])
    return h
```

```python
import functools
import math

import numpy as np
import jax
import jax.numpy as jnp
from jax import lax
from jax.experimental import pallas as pl
from jax.experimental.pallas import tpu as pltpu

F32 = jnp.float32
BF16 = jnp.bfloat16
I32 = jnp.int32

CHUNK = 64
H_A, DQK_A = 4, 64
DV_A = 2 * DQK_A
H_B, DH_B = 4, 64
LEFT_CHUNKS = 8
REL_MAX = 128
H_C, DH_C = 4, 64
H_IDX, D_IDX = 8, 64
TOPK_MAX = 256
H_X = 4
N_EXPERTS = 8
LN_EPS = 1e-5
RMS_EPS = 1e-6
NEG_INF = -1e30
MASKED = -2e30

LANES = 128
VMEM_LIMIT = 56 * 1024 * 1024
ATT_TILE = 256

_NT = (((1,), (1,)), ((), ()))


def _alibi_slopes():
    n = H_A + H_C
    s = (2.0 ** (-8.0 * np.arange(1, n + 1) / n)).astype(np.float32)
    return [float(v) for v in s[0::2]], [float(v) for v in s[1::2]]


def _params(*sem):
    return pltpu.CompilerParams(dimension_semantics=sem, vmem_limit_bytes=VMEM_LIMIT)


def _dot(a, b):
    return jnp.dot(a, b, preferred_element_type=F32)


def _dot_nt(a, b):
    return lax.dot_general(a, b, _NT, preferred_element_type=F32)


def _layer_norm_rows(y, g, b):
    mu = jnp.mean(y, axis=-1, keepdims=True)
    d = y - mu
    var = jnp.mean(d * d, axis=-1, keepdims=True)
    return d * lax.rsqrt(var + LN_EPS) * g + b


def _silu(a):
    return a / (1.0 + jnp.exp(-a))


def _mm_body(x_ref, w_ref, o_ref):
    o_ref[...] = _dot(x_ref[...], w_ref[...]).astype(o_ref.dtype)


def _matmul(x, w, out_dtype, tm=1024, tn=512):
    m, k = x.shape
    n = w.shape[1]
    tm, tn = min(tm, m), min(tn, n)
    return pl.pallas_call(
        _mm_body,
        grid=(m // tm, n // tn),
        in_specs=[pl.BlockSpec((tm, k), lambda i, j: (i, 0)),
                  pl.BlockSpec((k, tn), lambda i, j: (0, j))],
        out_specs=pl.BlockSpec((tm, tn), lambda i, j: (i, j)),
        out_shape=jax.ShapeDtypeStruct((m, n), out_dtype),
        compiler_params=_params("parallel", "parallel"),
    )(x, w)


def _mm_ln_body(*refs, n_in, alpha):
    xs, ws = refs[:n_in], refs[n_in:2 * n_in]
    res_ref, g_ref, b_ref, of_ref, ob_ref = refs[2 * n_in:]
    acc = _dot(xs[0][...], ws[0][...])
    for x_ref, w_ref in zip(xs[1:], ws[1:]):
        acc = acc + _dot(x_ref[...], w_ref[...])
    out = _layer_norm_rows(alpha * res_ref[...] + acc, g_ref[...], b_ref[...])
    of_ref[...] = out
    ob_ref[...] = out.astype(BF16)


def _matmul_ln(xs, ws, res, g, b, alpha, tm=512):
    m, d = res.shape
    tm = min(tm, m)
    n_in = len(xs)
    in_specs = ([pl.BlockSpec((tm, x.shape[1]), lambda i: (i, 0)) for x in xs]
                + [pl.BlockSpec(w.shape, lambda i: (0, 0)) for w in ws]
                + [pl.BlockSpec((tm, d), lambda i: (i, 0)),
                   pl.BlockSpec((1, d), lambda i: (0, 0)),
                   pl.BlockSpec((1, d), lambda i: (0, 0))])
    return pl.pallas_call(
        functools.partial(_mm_ln_body, n_in=n_in, alpha=alpha),
        grid=(m // tm,),
        in_specs=in_specs,
        out_specs=[pl.BlockSpec((tm, d), lambda i: (i, 0)),
                   pl.BlockSpec((tm, d), lambda i: (i, 0))],
        out_shape=[jax.ShapeDtypeStruct((m, d), F32), jax.ShapeDtypeStruct((m, d), BF16)],
        compiler_params=_params("parallel"),
    )(*xs, *ws, res, g.reshape(1, d), b.reshape(1, d))


def _swiglu_body(x_ref, w1_ref, w3_ref, o_ref):
    x = x_ref[...]
    a = _dot(x, w1_ref[...])
    b = _dot(x, w3_ref[...])
    o_ref[...] = (_silu(a) * b).astype(o_ref.dtype)


def _swiglu_hidden(x, w1, w3, tm=512, tn=1408):
    m, k = x.shape
    n = w1.shape[1]
    tm = min(tm, m)
    if n % tn:
        tn = n
    return pl.pallas_call(
        _swiglu_body,
        grid=(m // tm, n // tn),
        in_specs=[pl.BlockSpec((tm, k), lambda i, j: (i, 0)),
                  pl.BlockSpec((k, tn), lambda i, j: (0, j)),
                  pl.BlockSpec((k, tn), lambda i, j: (0, j))],
        out_specs=pl.BlockSpec((tm, tn), lambda i, j: (i, j)),
        out_shape=jax.ShapeDtypeStruct((m, n), BF16),
        compiler_params=_params("parallel", "parallel"),
    )(x, w1, w3)


def _diff_body(q_ref, k_ref, v_ref, lq1_ref, lk1_ref, lq2_ref, lk2_ref, gain_ref, o_ref,
               m_scr, l_scr, acc_scr, *, tile, slopes, lam_init):
    i = pl.program_id(1)
    q = q_ref[0] * DQK_A ** -0.5
    m_scr[...] = jnp.full(m_scr.shape, NEG_INF, F32)
    l_scr[...] = jnp.zeros(l_scr.shape, F32)
    acc_scr[...] = jnp.zeros(acc_scr.shape, F32)
    rq = lax.broadcasted_iota(I32, (tile, tile), 0)
    rk = lax.broadcasted_iota(I32, (tile, tile), 1)
    rel = (rq - rk).astype(F32)
    allowed = (rk // CHUNK) <= (rq // CHUNK)

    def block(j, diag):
        k0 = pl.multiple_of(j * tile, tile)
        kb = k_ref[0, pl.ds(k0, tile), :]
        vb = v_ref[0, pl.ds(k0, tile), :]
        if diag:
            dist = jnp.abs(rel)
        else:
            dist = rel + ((i - j) * tile).astype(F32)
        for h in range(H_A):
            bias = slopes[h] * dist
            for mm in range(2):
                idx = 2 * h + mm
                c = idx * DQK_A
                s = _dot_nt(q[:, c:c + DQK_A], kb[:, c:c + DQK_A]) - bias
                if diag:
                    s = jnp.where(allowed, s, MASKED)
                m_old = m_scr[idx]
                m_new = jnp.maximum(m_old, jnp.max(s, axis=-1, keepdims=True))
                p = jnp.exp(s - m_new)
                a = jnp.exp(m_old - m_new)
                l_scr[idx] = a * l_scr[idx] + jnp.sum(p, axis=-1, keepdims=True)
                acc_scr[idx] = a * acc_scr[idx] + _dot(p.astype(BF16), vb[:, h * DV_A:(h + 1) * DV_A])
                m_scr[idx] = m_new

    def full_block(j, carry):
        block(j, False)
        return carry

    lax.fori_loop(0, i, full_block, 0)
    block(i, True)

    lam = (jnp.exp(jnp.sum(lq1_ref[...] * lk1_ref[...], axis=-1, keepdims=True))
           - jnp.exp(jnp.sum(lq2_ref[...] * lk2_ref[...], axis=-1, keepdims=True)) + lam_init)
    gain = gain_ref[...] * (1.0 - lam_init)
    outs = []
    for h in range(H_A):
        o = acc_scr[2 * h] / l_scr[2 * h] - lam * (acc_scr[2 * h + 1] / l_scr[2 * h + 1])
        o = o * lax.rsqrt(jnp.mean(o * o, axis=-1, keepdims=True) + RMS_EPS)
        outs.append((o * gain).astype(BF16))
    o_ref[0] = jnp.concatenate(outs, axis=-1)


def _diff_attention(proj, lq1, lk1, lq2, lk2, gain, slopes, lam_init):
    b, s, _ = proj.shape
    tile = min(ATT_TILE, s)
    w = H_A * DV_A
    vec = lambda a: a.reshape(1, -1).astype(F32)
    small = lambda n: pl.BlockSpec((1, n), lambda bi, i: (0, 0))
    return pl.pallas_call(
        functools.partial(_diff_body, tile=tile, slopes=slopes, lam_init=lam_init),
        grid=(b, s // tile),
        in_specs=[pl.BlockSpec((1, tile, w), lambda bi, i: (bi, i, 0)),
                  pl.BlockSpec((1, s, w), lambda bi, i: (bi, 0, 1)),
                  pl.BlockSpec((1, s, w), lambda bi, i: (bi, 0, 2)),
                  small(DQK_A), small(DQK_A), small(DQK_A), small(DQK_A), small(DV_A)],
        out_specs=pl.BlockSpec((1, tile, w), lambda bi, i: (bi, i, 0)),
        out_shape=jax.ShapeDtypeStruct((b, s, w), BF16),
        scratch_shapes=[pltpu.VMEM((2 * H_A, tile, 1), F32),
                        pltpu.VMEM((2 * H_A, tile, 1), F32),
                        pltpu.VMEM((2 * H_A, tile, DV_A), F32)],
        compiler_params=_params("parallel", "parallel"),
    )(proj, proj, proj, vec(lq1), vec(lk1), vec(lq2), vec(lk2), vec(gain))


def _band_body(q_ref, k_ref, v_ref, bias_ref, o_ref, *, tile):
    i = pl.program_id(1)
    band = (LEFT_CHUNKS + 1) * CHUNK
    left = LEFT_CHUNKS * CHUNK
    kj = lax.broadcasted_iota(I32, (CHUNK, band), 1)
    for cc in range(tile // CHUNK):
        cg = i * (tile // CHUNK) + cc
        start = pl.multiple_of(cg * CHUNK, CHUNK)
        qc = q_ref[0, cc * CHUNK:(cc + 1) * CHUNK, :] * DH_B ** -0.5
        kw = k_ref[0, pl.ds(start, band), :]
        vw = v_ref[0, pl.ds(start, band), :]
        valid = kj >= left - cg * CHUNK
        outs = []
        for h in range(H_B):
            c = h * DH_B
            s = _dot_nt(qc[:, c:c + DH_B], kw[:, c:c + DH_B]) + bias_ref[h]
            s = jnp.where(valid, s, NEG_INF)
            p = jnp.exp(s - jnp.max(s, axis=-1, keepdims=True))
            l = jnp.sum(p, axis=-1, keepdims=True)
            outs.append(_dot(p.astype(BF16), vw[:, c:c + DH_B]) / l)
        o_ref[0, cc * CHUNK:(cc + 1) * CHUNK, :] = jnp.concatenate(outs, axis=-1).astype(BF16)


def _band_attention(proj, kpad, vpad, bias):
    b, s, _ = proj.shape
    tile = min(ATT_TILE, s)
    w = H_B * DH_B
    band = (LEFT_CHUNKS + 1) * CHUNK
    sp = kpad.shape[1]
    return pl.pallas_call(
        functools.partial(_band_body, tile=tile),
        grid=(b, s // tile),
        in_specs=[pl.BlockSpec((1, tile, w), lambda bi, i: (bi, i, 6)),
                  pl.BlockSpec((1, sp, w), lambda bi, i: (bi, 0, 0)),
                  pl.BlockSpec((1, sp, w), lambda bi, i: (bi, 0, 0)),
                  pl.BlockSpec((H_B, CHUNK, band), lambda bi, i: (0, 0, 0))],
        out_specs=pl.BlockSpec((1, tile, w), lambda bi, i: (bi, i, 0)),
        out_shape=jax.ShapeDtypeStruct((b, s, w), BF16),
        compiler_params=_params("parallel", "parallel"),
    )(proj, kpad, vpad, bias)


def _sortable(x):
    bits = pltpu.bitcast(x + 0.0, I32)
    return bits ^ ((bits >> 31) & 0x7FFFFFFF)


_INT_MIN = -2 ** 31
_VALID_KEY = int(np.array(0.5 * NEG_INF, np.float32).view(np.int32))
_VALID_KEY = _VALID_KEY ^ ((_VALID_KEY >> 31) & 0x7FFFFFFF)


def _dsa_body(qc_ref, kc_ref, vc_ref, qi_ref, kiw_ref, wq_ref, o_ref,
              key_scr, jstar_scr, m_scr, l_scr, acc_scr, *, tile, topk, slopes, seq):
    i = pl.program_id(1)
    nblk = i + 1
    nchunk = tile // LANES
    rq = lax.broadcasted_iota(I32, (tile, tile), 0)
    rk = lax.broadcasted_iota(I32, (tile, tile), 1)
    allowed = (rk // CHUNK) <= (rq // CHUNK)
    lane_idx = lax.broadcasted_iota(I32, (tile, LANES), 1)

    qi = qi_ref[0] * D_IDX ** -0.5
    wq = wq_ref[0][:, D_IDX:D_IDX + H_IDX] * H_IDX ** -0.5

    def score_block(j, diag):
        k0 = pl.multiple_of(j * tile, tile)
        kib = kiw_ref[0, pl.ds(k0, tile), :][:, :D_IDX].astype(BF16)
        isc = jnp.zeros((tile, tile), F32)
        for h in range(H_IDX):
            d = _dot_nt(qi[:, h * D_IDX:(h + 1) * D_IDX], kib)
            isc = isc + jnp.maximum(d, 0.0) * wq[:, h:h + 1]
        if diag:
            isc = jnp.where(allowed, isc, NEG_INF)
        key_scr[:, pl.ds(k0, tile)] = _sortable(isc)

    def score_full(j, carry):
        score_block(j, False)
        return carry

    lax.fori_loop(0, i, score_full, 0)
    score_block(i, True)

    def count(pred):
        def body(j, part):
            k0 = pl.multiple_of(j * tile, tile)
            blk = key_scr[:, pl.ds(k0, tile)]
            for c in range(nchunk):
                hit = pred(blk[:, c * LANES:(c + 1) * LANES], lane_idx + (k0 + c * LANES))
                part = part + jnp.where(hit, 1.0, 0.0)
            return part
        part = lax.fori_loop(0, nblk, body, jnp.zeros((tile, LANES), F32))
        return jnp.sum(part, axis=-1, keepdims=True)

    kf = float(topk)

    def value_bit(p, u):
        cand_u = u | jnp.left_shift(jnp.int32(1), 31 - p)
        cand = jnp.broadcast_to(cand_u ^ _INT_MIN, (tile, LANES))
        cnt = count(lambda key, idx: key >= cand)
        return jnp.where(cnt >= kf, cand_u, u)

    u = lax.fori_loop(0, 32, value_bit, jnp.zeros((tile, 1), I32))
    thr = u ^ _INT_MIN

    thr_b = jnp.broadcast_to(thr, (tile, LANES))
    c_gt = count(lambda key, idx: key > thr_b)
    c_ge = count(lambda key, idx: key >= thr_b)
    tied = (c_ge > kf) & (thr > _VALID_KEY)
    need = kf - c_gt
    jstar_scr[...] = jnp.full((tile, 1), seq, I32)

    @pl.when(jnp.max(jnp.where(tied, 1.0, 0.0)) > 0.0)
    def _():
        nbits = max(1, int(math.ceil(math.log2(seq))))

        def index_bit(p, lo):
            cand = lo | jnp.left_shift(jnp.int32(1), nbits - 1 - p)
            cand_b = jnp.broadcast_to(cand, (tile, LANES))
            cnt = count(lambda key, idx: (key == thr_b) & (idx < cand_b))
            return jnp.where(cnt < need, cand, lo)

        lo = lax.fori_loop(0, nbits, index_bit, jnp.zeros((tile, 1), I32))
        jstar_scr[...] = jnp.where(tied, lo, seq)

    jstar = jstar_scr[...]
    thr2 = jnp.maximum(thr, _VALID_KEY + 1)

    qc = qc_ref[0] * DH_C ** -0.5
    m_scr[...] = jnp.full(m_scr.shape, NEG_INF, F32)
    l_scr[...] = jnp.zeros(l_scr.shape, F32)
    acc_scr[...] = jnp.zeros(acc_scr.shape, F32)
    rel = (rq - rk).astype(F32)

    def attend(j, carry):
        k0 = pl.multiple_of(j * tile, tile)
        key = key_scr[:, pl.ds(k0, tile)]
        kidx = rk + k0
        sel = (key > thr2) | ((key == thr2) & (kidx <= jstar))
        kb = kc_ref[0, pl.ds(k0, tile), :]
        vb = vc_ref[0, pl.ds(k0, tile), :]
        dist = jnp.abs(rel + ((i - j) * tile).astype(F32))
        for h in range(H_C):
            c = h * DH_C
            s = _dot_nt(qc[:, c:c + DH_C], kb[:, c:c + DH_C]) - slopes[h] * dist
            s = jnp.where(sel, s, MASKED)
            m_old = m_scr[h]
            m_new = jnp.maximum(m_old, jnp.max(s, axis=-1, keepdims=True))
            p = jnp.exp(s - m_new)
            a = jnp.exp(m_old - m_new)
            l_scr[h] = a * l_scr[h] + jnp.sum(p, axis=-1, keepdims=True)
            acc_scr[h] = a * acc_scr[h] + _dot(p.astype(BF16), vb[:, c:c + DH_C])
            m_scr[h] = m_new
        return carry

    lax.fori_loop(0, nblk, attend, 0)
    o_ref[0] = jnp.concatenate([acc_scr[h] / l_scr[h] for h in range(H_C)], axis=-1).astype(BF16)


def _dsa_attention(proj, kiw, slopes):
    b, s, _ = proj.shape
    tile = min(ATT_TILE, s)
    w = H_C * DH_C
    topk = min(TOPK_MAX, s // 4)
    return pl.pallas_call(
        functools.partial(_dsa_body, tile=tile, topk=topk, slopes=slopes, seq=s),
        grid=(b, s // tile),
        in_specs=[pl.BlockSpec((1, tile, w), lambda bi, i: (bi, i, 9)),
                  pl.BlockSpec((1, s, w), lambda bi, i: (bi, 0, 10)),
                  pl.BlockSpec((1, s, w), lambda bi, i: (bi, 0, 11)),
                  pl.BlockSpec((1, tile, H_IDX * D_IDX), lambda bi, i: (bi, i, 6)),
                  pl.BlockSpec((1, s, LANES), lambda bi, i: (bi, 0, 0)),
                  pl.BlockSpec((1, tile, LANES), lambda bi, i: (bi, i, 0))],
        out_specs=pl.BlockSpec((1, tile, w), lambda bi, i: (bi, i, 0)),
        out_shape=jax.ShapeDtypeStruct((b, s, w), BF16),
        scratch_shapes=[pltpu.VMEM((tile, s), I32),
                        pltpu.VMEM((tile, 1), I32),
                        pltpu.VMEM((H_C, tile, 1), F32),
                        pltpu.VMEM((H_C, tile, 1), F32),
                        pltpu.VMEM((H_C, tile, DH_C), F32)],
        compiler_params=_params("parallel", "parallel"),
    )(proj, proj, proj, proj, kiw, kiw)


def _xattn_body(q_ref, k_ref, v_ref, o_ref, *, dh):
    outs = []
    for h in range(H_X):
        c = h * dh
        q = q_ref[0, :, c:c + dh] * dh ** -0.5
        s = _dot_nt(q, k_ref[0, :, c:c + dh])
        p = jnp.exp(s - jnp.max(s, axis=-1, keepdims=True))
        l = jnp.sum(p, axis=-1, keepdims=True)
        outs.append((_dot(p.astype(BF16), v_ref[0, :, c:c + dh]) / l).astype(BF16))
    o_ref[0] = jnp.concatenate(outs, axis=-1)


def _xattn_core(q, k, v, tq=512):
    b, s, d = q.shape
    n_mem = k.shape[1]
    tq = min(tq, s)
    return pl.pallas_call(
        functools.partial(_xattn_body, dh=d // H_X),
        grid=(b, s // tq),
        in_specs=[pl.BlockSpec((1, tq, d), lambda bi, i: (bi, i, 0)),
                  pl.BlockSpec((1, n_mem, d), lambda bi, i: (bi, 0, 0)),
                  pl.BlockSpec((1, n_mem, d), lambda bi, i: (bi, 0, 0))],
        out_specs=pl.BlockSpec((1, tq, d), lambda bi, i: (bi, i, 0)),
        out_shape=jax.ShapeDtypeStruct((b, s, d), BF16),
        compiler_params=_params("parallel", "parallel"),
    )(q, k, v)


def _router_body(x_ref, r_ref, g_ref):
    logits = _dot(x_ref[...], r_ref[...])
    lane = lax.broadcasted_iota(I32, logits.shape, 1).astype(F32)
    big = float(LANES)
    lg = jnp.where(lane < N_EXPERTS, logits, -jnp.inf)
    m1 = jnp.max(lg, axis=-1, keepdims=True)
    i1 = jnp.min(jnp.where(lg == m1, lane, big), axis=-1, keepdims=True)
    lg2 = jnp.where(lane == i1, -jnp.inf, lg)
    m2 = jnp.max(lg2, axis=-1, keepdims=True)
    i2 = jnp.min(jnp.where(lg2 == m2, lane, big), axis=-1, keepdims=True)
    e = jnp.exp(m2 - m1)
    den = 1.0 + e
    g_ref[...] = jnp.where(lane == i1, 1.0 / den, 0.0) + jnp.where(lane == i2, e / den, 0.0)


def _router_gates(x, router, tm=1024):
    m, d = x.shape
    tm = min(tm, m)
    rpad = jnp.zeros((d, LANES), BF16).at[:, :N_EXPERTS].set(router.astype(BF16))
    return pl.pallas_call(
        _router_body,
        grid=(m // tm,),
        in_specs=[pl.BlockSpec((tm, d), lambda i: (i, 0)),
                  pl.BlockSpec((d, LANES), lambda i: (0, 0))],
        out_specs=pl.BlockSpec((tm, LANES), lambda i: (i, 0)),
        out_shape=jax.ShapeDtypeStruct((m, LANES), F32),
        compiler_params=_params("parallel"),
    )(x, rpad)


def _moe_body(x_ref, gate_ref, w1_ref, w3_ref, w2_ref, res_ref, g_ref, b_ref, of_ref, ob_ref,
              acc_ref, acce_ref, *, alpha):
    e = pl.program_id(1)
    f = pl.program_id(2)
    ne = pl.num_programs(1)
    nf = pl.num_programs(2)

    @pl.when((e == 0) & (f == 0))
    def _():
        acc_ref[...] = jnp.zeros(acc_ref.shape, F32)

    @pl.when(f == 0)
    def _():
        acce_ref[...] = jnp.zeros(acce_ref.shape, F32)

    x = x_ref[...]
    a = _dot(x, w1_ref[0])
    b = _dot(x, w3_ref[0])
    acce_ref[...] += _dot((_silu(a) * b).astype(BF16), w2_ref[0])

    @pl.when(f == nf - 1)
    def _():
        lane = lax.broadcasted_iota(I32, gate_ref.shape, 1)
        gcol = jnp.sum(jnp.where(lane == e, gate_ref[...], 0.0), axis=-1, keepdims=True)
        acc_ref[...] += gcol * acce_ref[...]

    @pl.when((e == ne - 1) & (f == nf - 1))
    def _():
        out = _layer_norm_rows(alpha * res_ref[...] + acc_ref[...], g_ref[...], b_ref[...])
        of_ref[...] = out
        ob_ref[...] = out.astype(BF16)


def _moe_ln(x, gates, w1, w3, w2, res, g, b, alpha, tm=512, tf=512):
    m, d = x.shape
    ne, _, ff = w1.shape
    tm = min(tm, m)
    if ff % tf:
        tf = ff
    row = lambda i, e, f: (i, 0)
    return pl.pallas_call(
        functools.partial(_moe_body, alpha=alpha),
        grid=(m // tm, ne, ff // tf),
        in_specs=[pl.BlockSpec((tm, d), row),
                  pl.BlockSpec((tm, LANES), row),
                  pl.BlockSpec((1, d, tf), lambda i, e, f: (e, 0, f)),
                  pl.BlockSpec((1, d, tf), lambda i, e, f: (e, 0, f)),
                  pl.BlockSpec((1, tf, d), lambda i, e, f: (e, f, 0)),
                  pl.BlockSpec((tm, d), row),
                  pl.BlockSpec((1, d), lambda i, e, f: (0, 0)),
                  pl.BlockSpec((1, d), lambda i, e, f: (0, 0))],
        out_specs=[pl.BlockSpec((tm, d), row), pl.BlockSpec((tm, d), row)],
        out_shape=[jax.ShapeDtypeStruct((m, d), F32), jax.ShapeDtypeStruct((m, d), BF16)],
        scratch_shapes=[pltpu.VMEM((tm, d), F32), pltpu.VMEM((tm, d), F32)],
        compiler_params=_params("parallel", "arbitrary", "arbitrary"),
    )(x, gates, w1, w3, w2, res, g.reshape(1, d), b.reshape(1, d))


def _band_bias(table):
    band = (LEFT_CHUNKS + 1) * CHUNK
    qi = np.arange(CHUNK)[:, None]
    kj = np.arange(band)[None, :]
    rel = np.clip(qi - (kj - LEFT_CHUNKS * CHUNK), -REL_MAX, REL_MAX) + REL_MAX
    return table.astype(F32)[:, rel]


def kernel(x, mem, w_in, lam_q1, lam_k1, lam_q2, lam_k2, diff_norm_g, rel_bias, w_out,
           ln1_g, ln1_b, xattn_wq, xattn_wk, xattn_wv, xattn_wo, ln2_g, ln2_b,
           ffn_w1, ffn_w3, ffn_w2, moe_router, moe_w1, moe_w3, moe_w2, ln3_g, ln3_b):
    b, s, d = x.shape
    depth = w_in.shape[0]
    t = b * s
    alpha = (2.0 * depth) ** 0.25
    slopes_a, slopes_c = _alibi_slopes()
    n_main = 3 * H_A * DV_A + 3 * H_B * DH_B + 3 * H_C * DH_C + H_IDX * D_IDX
    n_tail = D_IDX + H_IDX
    wa, wb = H_A * DV_A, H_B * DH_B
    left = LEFT_CHUNKS * CHUNK

    h = x.reshape(t, d)
    hb = h.astype(BF16)
    memb = mem.reshape(-1, d).astype(BF16)
    for l in range(depth):
        w_main = w_in[l, :, :n_main].astype(BF16)
        w_tail = jnp.zeros((d, LANES), BF16).at[:, :n_tail].set(w_in[l, :, n_main:].astype(BF16))
        proj = _matmul(hb, w_main, BF16).reshape(b, s, n_main)
        kiw = _matmul(hb, w_tail, F32, tn=LANES).reshape(b, s, LANES)
        lam_init = 0.8 - 0.6 * math.exp(-0.3 * l)
        ya = _diff_attention(proj, lam_q1[l], lam_k1[l], lam_q2[l], lam_k2[l], diff_norm_g[l],
                             slopes_a, lam_init)
        kb = proj[:, :, 3 * wa + wb:3 * wa + 2 * wb]
        vb = proj[:, :, 3 * wa + 2 * wb:3 * wa + 3 * wb]
        kpad = jnp.pad(kb, ((0, 0), (left, 0), (0, 0)))
        vpad = jnp.pad(vb, ((0, 0), (left, 0), (0, 0)))
        yb = _band_attention(proj, kpad, vpad, _band_bias(rel_bias[l]))
        yc = _dsa_attention(proj, kiw, slopes_c)
        wo = w_out[l].astype(BF16)
        h, hb = _matmul_ln(
            [ya.reshape(t, wa), yb.reshape(t, wb), yc.reshape(t, -1)],
            [wo[:wa], wo[wa:wa + wb], wo[wa + wb:]],
            h, ln1_g[l], ln1_b[l], alpha)
        q = _matmul(hb, xattn_wq[l].astype(BF16), BF16).reshape(b, s, d)
        km = _matmul(memb, xattn_wk[l].astype(BF16), BF16).reshape(b, -1, d)
        vm = _matmul(memb, xattn_wv[l].astype(BF16), BF16).reshape(b, -1, d)
        xo = _xattn_core(q, km, vm).reshape(t, d)
        h, hb = _matmul_ln([xo], [xattn_wo[l].astype(BF16)], h, ln2_g[l], ln2_b[l], alpha)
        if l % 2 == 0:
            hid = _swiglu_hidden(hb, ffn_w1[l // 2].astype(BF16), ffn_w3[l // 2].astype(BF16))
            h, hb = _matmul_ln([hid], [ffn_w2[l // 2].astype(BF16)], h, ln3_g[l], ln3_b[l], alpha)
        else:
            gates = _router_gates(hb, moe_router[l // 2])
            h, hb = _moe_ln(hb, gates, moe_w1[l // 2].astype(BF16), moe_w3[l // 2].astype(BF16),
                            moe_w2[l // 2].astype(BF16), h, ln3_g[l], ln3_b[l], alpha)
    return h.reshape(b, s, d)
```

```python
import functools
import math

import numpy as np
import jax
import jax.numpy as jnp
from jax import lax
from jax.experimental import pallas as pl
from jax.experimental.pallas import tpu as pltpu

F32 = jnp.float32
BF16 = jnp.bfloat16
I32 = jnp.int32

CHUNK = 64
H_A, DQK_A = 4, 64
DV_A = 2 * DQK_A
H_B, DH_B = 4, 64
LEFT_CHUNKS = 8
REL_MAX = 128
H_C, DH_C = 4, 64
H_IDX, D_IDX = 8, 64
TOPK_MAX = 256
H_X = 4
N_EXPERTS = 8
LN_EPS = 1e-5
RMS_EPS = 1e-6
NEG_INF = -1e30
MASKED = -2e30

LANES = 128
SUBLANES = 8
VMEM_LIMIT = 56 * 1024 * 1024
ATT_TILE = 256

_NT = (((1,), (1,)), ((), ()))


def _alibi_slopes():
    n = H_A + H_C
    s = (2.0 ** (-8.0 * np.arange(1, n + 1) / n)).astype(np.float32)
    return [float(v) for v in s[0::2]], [float(v) for v in s[1::2]]


def _params(*sem):
    return pltpu.CompilerParams(dimension_semantics=sem, vmem_limit_bytes=VMEM_LIMIT)


def _dot(a, b):
    return jnp.dot(a, b, preferred_element_type=F32)


def _dot_nt(a, b):
    return lax.dot_general(a, b, _NT, preferred_element_type=F32)


def _layer_norm_rows(y, g, b):
    mu = jnp.mean(y, axis=-1, keepdims=True)
    d = y - mu
    var = jnp.mean(d * d, axis=-1, keepdims=True)
    return d * lax.rsqrt(var + LN_EPS) * g + b


def _silu(a):
    return a / (1.0 + jnp.exp(-a))


def _mm_body(x_ref, w_ref, o_ref):
    o_ref[...] = _dot(x_ref[...], w_ref[...]).astype(o_ref.dtype)


def _matmul(x, w, out_dtype, tm=1024, tn=512):
    m, k = x.shape
    n = w.shape[1]
    tm, tn = min(tm, m), min(tn, n)
    return pl.pallas_call(
        _mm_body,
        grid=(m // tm, n // tn),
        in_specs=[pl.BlockSpec((tm, k), lambda i, j: (i, 0)),
                  pl.BlockSpec((k, tn), lambda i, j: (0, j))],
        out_specs=pl.BlockSpec((tm, tn), lambda i, j: (i, j)),
        out_shape=jax.ShapeDtypeStruct((m, n), out_dtype),
        compiler_params=_params("parallel", "parallel"),
    )(x, w)


def _mm_nt_body(wt_ref, x_ref, o_ref):
    o_ref[0] = _dot_nt(wt_ref[...], x_ref[...]).astype(o_ref.dtype)


def _matmul_nt_tiles(wt, x, tm):
    n, k = wt.shape
    m = x.shape[0]
    return pl.pallas_call(
        _mm_nt_body,
        grid=(m // tm,),
        in_specs=[pl.BlockSpec((n, k), lambda i: (0, 0)),
                  pl.BlockSpec((tm, k), lambda i: (i, 0))],
        out_specs=pl.BlockSpec((1, n, tm), lambda i: (i, 0, 0)),
        out_shape=jax.ShapeDtypeStruct((m // tm, n, tm), BF16),
        compiler_params=_params("parallel"),
    )(wt, x)


def _mm_ln_body(*refs, n_in, alpha):
    xs, ws = refs[:n_in], refs[n_in:2 * n_in]
    res_ref, g_ref, b_ref, of_ref, ob_ref = refs[2 * n_in:]
    acc = _dot(xs[0][...], ws[0][...])
    for x_ref, w_ref in zip(xs[1:], ws[1:]):
        acc = acc + _dot(x_ref[...], w_ref[...])
    out = _layer_norm_rows(alpha * res_ref[...] + acc, g_ref[...], b_ref[...])
    of_ref[...] = out
    ob_ref[...] = out.astype(BF16)


def _matmul_ln(xs, ws, res, g, b, alpha, tm=512):
    m, d = res.shape
    tm = min(tm, m)
    n_in = len(xs)
    in_specs = ([pl.BlockSpec((tm, x.shape[1]), lambda i: (i, 0)) for x in xs]
                + [pl.BlockSpec(w.shape, lambda i: (0, 0)) for w in ws]
                + [pl.BlockSpec((tm, d), lambda i: (i, 0)),
                   pl.BlockSpec((1, d), lambda i: (0, 0)),
                   pl.BlockSpec((1, d), lambda i: (0, 0))])
    return pl.pallas_call(
        functools.partial(_mm_ln_body, n_in=n_in, alpha=alpha),
        grid=(m // tm,),
        in_specs=in_specs,
        out_specs=[pl.BlockSpec((tm, d), lambda i: (i, 0)),
                   pl.BlockSpec((tm, d), lambda i: (i, 0))],
        out_shape=[jax.ShapeDtypeStruct((m, d), F32), jax.ShapeDtypeStruct((m, d), BF16)],
        compiler_params=_params("parallel"),
    )(*xs, *ws, res, g.reshape(1, d), b.reshape(1, d))


def _swiglu_body(x_ref, w1_ref, w3_ref, o_ref):
    x = x_ref[...]
    a = _dot(x, w1_ref[...])
    b = _dot(x, w3_ref[...])
    o_ref[...] = (_silu(a) * b).astype(o_ref.dtype)


def _swiglu_hidden(x, w1, w3, tm=512, tn=1408):
    m, k = x.shape
    n = w1.shape[1]
    tm = min(tm, m)
    if n % tn:
        tn = n
    return pl.pallas_call(
        _swiglu_body,
        grid=(m // tm, n // tn),
        in_specs=[pl.BlockSpec((tm, k), lambda i, j: (i, 0)),
                  pl.BlockSpec((k, tn), lambda i, j: (0, j)),
                  pl.BlockSpec((k, tn), lambda i, j: (0, j))],
        out_specs=pl.BlockSpec((tm, tn), lambda i, j: (i, j)),
        out_shape=jax.ShapeDtypeStruct((m, n), BF16),
        compiler_params=_params("parallel", "parallel"),
    )(x, w1, w3)


def _key_query_iotas(tile):
    kpos = lax.broadcasted_iota(I32, (tile, tile), 0)
    qpos = lax.broadcasted_iota(I32, (tile, tile), 1)
    return kpos, qpos


def _scaled_transpose(x, scale):
    return (x.astype(F32) * scale).T.astype(BF16)


def _softmax_probs(c, t_ref, p_ref, a_ref, m_ref, l_ref):
    m_old = m_ref[...]
    m_new = jnp.maximum(m_old, jnp.max(t_ref[...], axis=0, keepdims=True) - c)
    p = jnp.exp(t_ref[...] - (m_new + c))
    a = jnp.exp(m_old - m_new)
    l_ref[...] = a * l_ref[...] + jnp.sum(p, axis=0, keepdims=True)
    m_ref[...] = m_new
    a_ref[...] = a
    p_ref[...] = p.astype(BF16)


def _accumulate(vt, p_ref, a_ref, acc_ref):
    acc_ref[...] = a_ref[...] * acc_ref[...] + _dot(vt, p_ref[...])


def _pipelined_maps(n_maps, scores, probs, accumulate, t_scr, p_scr, a_scr):
    nt, npb = t_scr.shape[0], p_scr.shape[0]

    def stage_probs(k):
        probs(k, t_scr.at[k % nt], p_scr.at[k % npb], a_scr.at[k % npb])

    t_scr[0] = scores(0)
    if n_maps > 1:
        t_scr[1 % nt] = scores(1)
    stage_probs(0)
    for k in range(n_maps):
        if k + 2 < n_maps:
            t_scr[(k + 2) % nt] = scores(k + 2)
        if k + 1 < n_maps:
            stage_probs(k + 1)
        accumulate(k, p_scr.at[k % npb], a_scr.at[k % npb])


def _diff_body(q_ref, k_ref, vt_ref, lq1_ref, lk1_ref, lq2_ref, lk2_ref, gain_ref, o_ref,
               qt_scr, srel_scr, t_scr, p_scr, a_scr, m_scr, l_scr, acc_scr,
               *, tile, slopes, lam_init):
    i = pl.program_id(1)
    kpos, qpos = _key_query_iotas(tile)

    @pl.when((pl.program_id(0) == 0) & (i == 0))
    def _():
        rel = (qpos - kpos).astype(F32)
        for h in range(H_A):
            srel_scr[h] = slopes[h] * rel

    qt_scr[...] = _scaled_transpose(q_ref[0], DQK_A ** -0.5)
    m_scr[...] = jnp.full(m_scr.shape, NEG_INF, F32)
    l_scr[...] = jnp.zeros(l_scr.shape, F32)
    acc_scr[...] = jnp.zeros(acc_scr.shape, F32)

    def block(j, diag):
        k0 = pl.multiple_of(j * tile, tile)
        kb = k_ref[0, pl.ds(k0, tile), :]
        vtb = vt_ref[0, j]
        off = ((i - j) * tile).astype(F32)

        def scores(idx):
            h, c = idx // 2, idx * DQK_A
            s = _dot(kb[:, c:c + DQK_A], qt_scr[c:c + DQK_A, :])
            if diag:
                allowed = (kpos // CHUNK) <= (qpos // CHUNK)
                return jnp.where(allowed, s - jnp.abs(srel_scr[h]), MASKED)
            return s - srel_scr[h]

        def probs(idx, t_ref, p_ref, a_ref):
            shift = 0.0 if diag else slopes[idx // 2] * off
            _softmax_probs(shift, t_ref, p_ref, a_ref, m_scr.at[idx], l_scr.at[idx])

        def accumulate(idx, p_ref, a_ref):
            h = idx // 2
            _accumulate(vtb[h * DV_A:(h + 1) * DV_A, :], p_ref, a_ref, acc_scr.at[idx])

        _pipelined_maps(2 * H_A, scores, probs, accumulate, t_scr, p_scr, a_scr)

    def full_block(j, carry):
        block(j, False)
        return carry

    lax.fori_loop(0, i, full_block, 0)
    block(i, True)

    lam = (jnp.exp(jnp.sum(lq1_ref[...] * lk1_ref[...], axis=-1, keepdims=True))
           - jnp.exp(jnp.sum(lq2_ref[...] * lk2_ref[...], axis=-1, keepdims=True)) + lam_init)
    gain = gain_ref[...] * (1.0 - lam_init)
    outs = []
    for h in range(H_A):
        o = acc_scr[2 * h] / l_scr[2 * h] - lam * (acc_scr[2 * h + 1] / l_scr[2 * h + 1])
        o = o * lax.rsqrt(jnp.mean(o * o, axis=0, keepdims=True) + RMS_EPS)
        outs.append((o * gain).T)
    o_ref[0] = jnp.concatenate(outs, axis=-1).astype(BF16)


def _diff_attention(proj, vt, lq1, lk1, lq2, lk2, gain, slopes, lam_init):
    b, s, _ = proj.shape
    tile = min(ATT_TILE, s)
    nkb = s // tile
    w = H_A * DV_A
    vec = lambda a: a.reshape(1, -1).astype(F32)
    small = lambda n: pl.BlockSpec((1, n), lambda bi, i: (0, 0))
    return pl.pallas_call(
        functools.partial(_diff_body, tile=tile, slopes=slopes, lam_init=lam_init),
        grid=(b, nkb),
        in_specs=[pl.BlockSpec((1, tile, w), lambda bi, i: (bi, i, 0)),
                  pl.BlockSpec((1, s, w), lambda bi, i: (bi, 0, 1)),
                  pl.BlockSpec((1, nkb, w, tile), lambda bi, i: (bi, 0, 0, 0)),
                  small(DQK_A), small(DQK_A), small(DQK_A), small(DQK_A),
                  pl.BlockSpec((DV_A, 1), lambda bi, i: (0, 0))],
        out_specs=pl.BlockSpec((1, tile, w), lambda bi, i: (bi, i, 0)),
        out_shape=jax.ShapeDtypeStruct((b, s, w), BF16),
        scratch_shapes=[pltpu.VMEM((w, tile), BF16),
                        pltpu.VMEM((H_A, tile, tile), F32),
                        pltpu.VMEM((3, tile, tile), F32),
                        pltpu.VMEM((2, tile, tile), BF16),
                        pltpu.VMEM((2, 1, tile), F32),
                        pltpu.VMEM((2 * H_A, 1, tile), F32),
                        pltpu.VMEM((2 * H_A, 1, tile), F32),
                        pltpu.VMEM((2 * H_A, DV_A, tile), F32)],
        compiler_params=_params("arbitrary", "arbitrary"),
    )(proj, proj, vt, vec(lq1), vec(lk1), vec(lq2), vec(lk2), gain.reshape(-1, 1).astype(F32))


def _band_body(q_ref, k_ref, v_ref, bias_ref, o_ref, *, tile):
    i = pl.program_id(1)
    band = (LEFT_CHUNKS + 1) * CHUNK
    left = LEFT_CHUNKS * CHUNK
    kj = lax.broadcasted_iota(I32, (CHUNK, band), 1)
    for cc in range(tile // CHUNK):
        cg = i * (tile // CHUNK) + cc
        start = pl.multiple_of(cg * CHUNK, CHUNK)
        qc = q_ref[0, cc * CHUNK:(cc + 1) * CHUNK, :] * DH_B ** -0.5
        kw = k_ref[0, pl.ds(start, band), :]
        vw = v_ref[0, pl.ds(start, band), :]
        valid = kj >= left - cg * CHUNK
        outs = []
        for h in range(H_B):
            c = h * DH_B
            s = _dot_nt(qc[:, c:c + DH_B], kw[:, c:c + DH_B]) + bias_ref[h]
            s = jnp.where(valid, s, NEG_INF)
            p = jnp.exp(s - jnp.max(s, axis=-1, keepdims=True))
            l = jnp.sum(p, axis=-1, keepdims=True)
            outs.append(_dot(p.astype(BF16), vw[:, c:c + DH_B]) / l)
        o_ref[0, cc * CHUNK:(cc + 1) * CHUNK, :] = jnp.concatenate(outs, axis=-1).astype(BF16)


def _band_attention(proj, kpad, vpad, bias):
    b, s, _ = proj.shape
    tile = min(ATT_TILE, s)
    w = H_B * DH_B
    band = (LEFT_CHUNKS + 1) * CHUNK
    sp = kpad.shape[1]
    return pl.pallas_call(
        functools.partial(_band_body, tile=tile),
        grid=(b, s // tile),
        in_specs=[pl.BlockSpec((1, tile, w), lambda bi, i: (bi, i, 6)),
                  pl.BlockSpec((1, sp, w), lambda bi, i: (bi, 0, 0)),
                  pl.BlockSpec((1, sp, w), lambda bi, i: (bi, 0, 0)),
                  pl.BlockSpec((H_B, CHUNK, band), lambda bi, i: (0, 0, 0))],
        out_specs=pl.BlockSpec((1, tile, w), lambda bi, i: (bi, i, 0)),
        out_shape=jax.ShapeDtypeStruct((b, s, w), BF16),
        compiler_params=_params("parallel", "parallel"),
    )(proj, kpad, vpad, bias)


def _sortable(x):
    bits = pltpu.bitcast(x + 0.0, I32)
    return bits ^ ((bits >> 31) & 0x7FFFFFFF)


_INT_MIN = -2 ** 31
_VALID_KEY = int(np.array(0.5 * NEG_INF, np.float32).view(np.int32))
_VALID_KEY = _VALID_KEY ^ ((_VALID_KEY >> 31) & 0x7FFFFFFF)


def _dsa_body(qc_ref, kc_ref, vct_ref, qi_ref, kiw_ref, wq_ref, o_ref,
              qt_scr, qit_scr, srel_scr, t_scr, p_scr, a_scr, key_scr, jstar_scr, m_scr, l_scr, acc_scr,
              *, tile, topk, slopes, seq):
    i = pl.program_id(1)
    nblk = i + 1
    kpos, qpos = _key_query_iotas(tile)

    @pl.when((pl.program_id(0) == 0) & (i == 0))
    def _():
        rel = (qpos - kpos).astype(F32)
        for h in range(H_C):
            srel_scr[h] = slopes[h] * rel

    qit_scr[...] = _scaled_transpose(qi_ref[0], D_IDX ** -0.5)
    wt = wq_ref[0].T
    wrow = [wt[D_IDX + h:D_IDX + h + 1, :] * H_IDX ** -0.5 for h in range(H_IDX)]

    def score_block(j, diag):
        k0 = pl.multiple_of(j * tile, tile)
        kib = kiw_ref[0, pl.ds(k0, tile), :][:, :D_IDX].astype(BF16)
        isc = jnp.zeros((tile, tile), F32)
        for h in range(H_IDX):
            d = _dot(kib, qit_scr[h * D_IDX:(h + 1) * D_IDX, :])
            isc = isc + jnp.maximum(d, 0.0) * wrow[h]
        if diag:
            isc = jnp.where((kpos // CHUNK) <= (qpos // CHUNK), isc, NEG_INF)
        key_scr[pl.ds(k0, tile), :] = _sortable(isc)

    def score_full(j, carry):
        score_block(j, False)
        return carry

    lax.fori_loop(0, i, score_full, 0)
    score_block(i, True)

    def count(pred):
        def body(j, part):
            k0 = pl.multiple_of(j * tile, tile)
            hit = jnp.where(pred(key_scr[pl.ds(k0, tile), :], kpos + k0), 1.0, 0.0)
            return part + jnp.sum(hit.reshape(tile // SUBLANES, SUBLANES, tile), axis=0)
        part = lax.fori_loop(0, nblk, body, jnp.zeros((SUBLANES, tile), F32))
        return jnp.sum(part, axis=0, keepdims=True)

    kf = float(topk)

    def value_bit(p, u):
        cand_u = u | jnp.left_shift(jnp.int32(1), 31 - p)
        cand = cand_u ^ _INT_MIN
        cnt = count(lambda key, idx: key >= cand)
        return jnp.where(cnt >= kf, cand_u, u)

    u = lax.fori_loop(0, 32, value_bit, jnp.zeros((1, tile), I32))
    thr = u ^ _INT_MIN

    thr_b = thr
    c_gt = count(lambda key, idx: key > thr_b)
    c_ge = count(lambda key, idx: key >= thr_b)
    tied = (c_ge > kf) & (thr > _VALID_KEY)
    need = kf - c_gt
    jstar_scr[...] = jnp.full((1, tile), seq, I32)

    @pl.when(jnp.max(jnp.where(tied, 1.0, 0.0)) > 0.0)
    def _():
        nbits = max(1, int(math.ceil(math.log2(seq))))

        def index_bit(p, lo):
            cand = lo | jnp.left_shift(jnp.int32(1), nbits - 1 - p)
            cnt = count(lambda key, idx: (key == thr_b) & (idx < cand))
            return jnp.where(cnt < need, cand, lo)

        lo = lax.fori_loop(0, nbits, index_bit, jnp.zeros((1, tile), I32))
        jstar_scr[...] = jnp.where(tied, lo, seq)

    jstar = jstar_scr[...]
    thr2 = jnp.maximum(thr, _VALID_KEY + 1)

    qt_scr[...] = _scaled_transpose(qc_ref[0], DH_C ** -0.5)
    m_scr[...] = jnp.full(m_scr.shape, NEG_INF, F32)
    l_scr[...] = jnp.zeros(l_scr.shape, F32)
    acc_scr[...] = jnp.zeros(acc_scr.shape, F32)

    def attend(j, diag):
        k0 = pl.multiple_of(j * tile, tile)
        key = key_scr[pl.ds(k0, tile), :]
        sel = (key > thr2) | ((key == thr2) & (kpos + k0 <= jstar))
        kb = kc_ref[0, pl.ds(k0, tile), :]
        vtb = vct_ref[0, j]
        off = ((i - j) * tile).astype(F32)

        def scores(h):
            c = h * DH_C
            s = _dot(kb[:, c:c + DH_C], qt_scr[c:c + DH_C, :])
            bias = jnp.abs(srel_scr[h]) if diag else srel_scr[h]
            return jnp.where(sel, s - bias, MASKED)

        def probs(h, t_ref, p_ref, a_ref):
            shift = 0.0 if diag else slopes[h] * off
            _softmax_probs(shift, t_ref, p_ref, a_ref, m_scr.at[h], l_scr.at[h])

        def accumulate(h, p_ref, a_ref):
            _accumulate(vtb[h * DH_C:(h + 1) * DH_C, :], p_ref, a_ref, acc_scr.at[h])

        _pipelined_maps(H_C, scores, probs, accumulate, t_scr, p_scr, a_scr)

    def attend_full(j, carry):
        attend(j, False)
        return carry

    lax.fori_loop(0, i, attend_full, 0)
    attend(i, True)
    outs = [(acc_scr[h] / l_scr[h]).T for h in range(H_C)]
    o_ref[0] = jnp.concatenate(outs, axis=-1).astype(BF16)


def _dsa_attention(proj, vt, kiw, slopes):
    b, s, _ = proj.shape
    tile = min(ATT_TILE, s)
    nkb = s // tile
    w = H_C * DH_C
    wa = H_A * DV_A
    topk = min(TOPK_MAX, s // 4)
    return pl.pallas_call(
        functools.partial(_dsa_body, tile=tile, topk=topk, slopes=slopes, seq=s),
        grid=(b, nkb),
        in_specs=[pl.BlockSpec((1, tile, w), lambda bi, i: (bi, i, 9)),
                  pl.BlockSpec((1, s, w), lambda bi, i: (bi, 0, 10)),
                  pl.BlockSpec((1, nkb, w, tile), lambda bi, i: (bi, 0, wa // w, 0)),
                  pl.BlockSpec((1, tile, H_IDX * D_IDX), lambda bi, i: (bi, i, 6)),
                  pl.BlockSpec((1, s, LANES), lambda bi, i: (bi, 0, 0)),
                  pl.BlockSpec((1, tile, LANES), lambda bi, i: (bi, i, 0))],
        out_specs=pl.BlockSpec((1, tile, w), lambda bi, i: (bi, i, 0)),
        out_shape=jax.ShapeDtypeStruct((b, s, w), BF16),
        scratch_shapes=[pltpu.VMEM((w, tile), BF16),
                        pltpu.VMEM((H_IDX * D_IDX, tile), BF16),
                        pltpu.VMEM((H_C, tile, tile), F32),
                        pltpu.VMEM((3, tile, tile), F32),
                        pltpu.VMEM((2, tile, tile), BF16),
                        pltpu.VMEM((2, 1, tile), F32),
                        pltpu.VMEM((s, tile), I32),
                        pltpu.VMEM((1, tile), I32),
                        pltpu.VMEM((H_C, 1, tile), F32),
                        pltpu.VMEM((H_C, 1, tile), F32),
                        pltpu.VMEM((H_C, DH_C, tile), F32)],
        compiler_params=_params("arbitrary", "arbitrary"),
    )(proj, proj, vt, proj, kiw, kiw)


def _xattn_body(q_ref, k_ref, v_ref, o_ref, *, dh):
    outs = []
    for h in range(H_X):
        c = h * dh
        q = q_ref[0, :, c:c + dh] * dh ** -0.5
        s = _dot_nt(q, k_ref[0, :, c:c + dh])
        p = jnp.exp(s - jnp.max(s, axis=-1, keepdims=True))
        l = jnp.sum(p, axis=-1, keepdims=True)
        outs.append((_dot(p.astype(BF16), v_ref[0, :, c:c + dh]) / l).astype(BF16))
    o_ref[0] = jnp.concatenate(outs, axis=-1)


def _xattn_core(q, k, v, tq=512):
    b, s, d = q.shape
    n_mem = k.shape[1]
    tq = min(tq, s)
    return pl.pallas_call(
        functools.partial(_xattn_body, dh=d // H_X),
        grid=(b, s // tq),
        in_specs=[pl.BlockSpec((1, tq, d), lambda bi, i: (bi, i, 0)),
                  pl.BlockSpec((1, n_mem, d), lambda bi, i: (bi, 0, 0)),
                  pl.BlockSpec((1, n_mem, d), lambda bi, i: (bi, 0, 0))],
        out_specs=pl.BlockSpec((1, tq, d), lambda bi, i: (bi, i, 0)),
        out_shape=jax.ShapeDtypeStruct((b, s, d), BF16),
        compiler_params=_params("parallel", "parallel"),
    )(q, k, v)


def _router_body(x_ref, r_ref, g_ref):
    logits = _dot(x_ref[...], r_ref[...])
    lane = lax.broadcasted_iota(I32, logits.shape, 1).astype(F32)
    big = float(LANES)
    lg = jnp.where(lane < N_EXPERTS, logits, -jnp.inf)
    m1 = jnp.max(lg, axis=-1, keepdims=True)
    i1 = jnp.min(jnp.where(lg == m1, lane, big), axis=-1, keepdims=True)
    lg2 = jnp.where(lane == i1, -jnp.inf, lg)
    m2 = jnp.max(lg2, axis=-1, keepdims=True)
    i2 = jnp.min(jnp.where(lg2 == m2, lane, big), axis=-1, keepdims=True)
    e = jnp.exp(m2 - m1)
    den = 1.0 + e
    g_ref[...] = jnp.where(lane == i1, 1.0 / den, 0.0) + jnp.where(lane == i2, e / den, 0.0)


def _router_gates(x, router, tm=1024):
    m, d = x.shape
    tm = min(tm, m)
    rpad = jnp.zeros((d, LANES), BF16).at[:, :N_EXPERTS].set(router.astype(BF16))
    return pl.pallas_call(
        _router_body,
        grid=(m // tm,),
        in_specs=[pl.BlockSpec((tm, d), lambda i: (i, 0)),
                  pl.BlockSpec((d, LANES), lambda i: (0, 0))],
        out_specs=pl.BlockSpec((tm, LANES), lambda i: (i, 0)),
        out_shape=jax.ShapeDtypeStruct((m, LANES), F32),
        compiler_params=_params("parallel"),
    )(x, rpad)


def _moe_body(x_ref, gate_ref, w1_ref, w3_ref, w2_ref, res_ref, g_ref, b_ref, of_ref, ob_ref,
              acc_ref, acce_ref, *, alpha):
    e = pl.program_id(1)
    f = pl.program_id(2)
    ne = pl.num_programs(1)
    nf = pl.num_programs(2)

    @pl.when((e == 0) & (f == 0))
    def _():
        acc_ref[...] = jnp.zeros(acc_ref.shape, F32)

    @pl.when(f == 0)
    def _():
        acce_ref[...] = jnp.zeros(acce_ref.shape, F32)

    x = x_ref[...]
    a = _dot(x, w1_ref[0])
    b = _dot(x, w3_ref[0])
    acce_ref[...] += _dot((_silu(a) * b).astype(BF16), w2_ref[0])

    @pl.when(f == nf - 1)
    def _():
        lane = lax.broadcasted_iota(I32, gate_ref.shape, 1)
        gcol = jnp.sum(jnp.where(lane == e, gate_ref[...], 0.0), axis=-1, keepdims=True)
        acc_ref[...] += gcol * acce_ref[...]

    @pl.when((e == ne - 1) & (f == nf - 1))
    def _():
        out = _layer_norm_rows(alpha * res_ref[...] + acc_ref[...], g_ref[...], b_ref[...])
        of_ref[...] = out
        ob_ref[...] = out.astype(BF16)


def _moe_ln(x, gates, w1, w3, w2, res, g, b, alpha, tm=512, tf=512):
    m, d = x.shape
    ne, _, ff = w1.shape
    tm = min(tm, m)
    if ff % tf:
        tf = ff
    row = lambda i, e, f: (i, 0)
    return pl.pallas_call(
        functools.partial(_moe_body, alpha=alpha),
        grid=(m // tm, ne, ff // tf),
        in_specs=[pl.BlockSpec((tm, d), row),
                  pl.BlockSpec((tm, LANES), row),
                  pl.BlockSpec((1, d, tf), lambda i, e, f: (e, 0, f)),
                  pl.BlockSpec((1, d, tf), lambda i, e, f: (e, 0, f)),
                  pl.BlockSpec((1, tf, d), lambda i, e, f: (e, f, 0)),
                  pl.BlockSpec((tm, d), row),
                  pl.BlockSpec((1, d), lambda i, e, f: (0, 0)),
                  pl.BlockSpec((1, d), lambda i, e, f: (0, 0))],
        out_specs=[pl.BlockSpec((tm, d), row), pl.BlockSpec((tm, d), row)],
        out_shape=[jax.ShapeDtypeStruct((m, d), F32), jax.ShapeDtypeStruct((m, d), BF16)],
        scratch_shapes=[pltpu.VMEM((tm, d), F32), pltpu.VMEM((tm, d), F32)],
        compiler_params=_params("parallel", "arbitrary", "arbitrary"),
    )(x, gates, w1, w3, w2, res, g.reshape(1, d), b.reshape(1, d))


def _band_bias(table):
    band = (LEFT_CHUNKS + 1) * CHUNK
    left = LEFT_CHUNKS * CHUNK
    rev = table.astype(F32)[:, ::-1]
    pad = left - REL_MAX + CHUNK - 1
    n_rev = band - 1 - (left - REL_MAX) + 1
    assert n_rev <= 2 * REL_MAX + 1
    ext = jnp.concatenate([jnp.broadcast_to(rev[:, :1], (rev.shape[0], pad)), rev[:, :n_rev]], axis=1)
    rows = [ext[:, CHUNK - 1 - q:CHUNK - 1 - q + band] for q in range(CHUNK)]
    return jnp.stack(rows, axis=1)


def kernel(x, mem, w_in, lam_q1, lam_k1, lam_q2, lam_k2, diff_norm_g, rel_bias, w_out,
           ln1_g, ln1_b, xattn_wq, xattn_wk, xattn_wv, xattn_wo, ln2_g, ln2_b,
           ffn_w1, ffn_w3, ffn_w2, moe_router, moe_w1, moe_w3, moe_w2, ln3_g, ln3_b):
    b, s, d = x.shape
    depth = w_in.shape[0]
    t = b * s
    alpha = (2.0 * depth) ** 0.25
    slopes_a, slopes_c = _alibi_slopes()
    n_main = 3 * H_A * DV_A + 3 * H_B * DH_B + 3 * H_C * DH_C + H_IDX * D_IDX
    n_tail = D_IDX + H_IDX
    wa, wb, wc = H_A * DV_A, H_B * DH_B, H_C * DH_C
    left = LEFT_CHUNKS * CHUNK

    h = x.reshape(t, d)
    hb = h.astype(BF16)
    memb = mem.reshape(-1, d).astype(BF16)
    for l in range(depth):
        w_main = w_in[l, :, :n_main].astype(BF16)
        w_tail = jnp.zeros((d, LANES), BF16).at[:, :n_tail].set(w_in[l, :, n_main:].astype(BF16))
        proj = _matmul(hb, w_main, BF16).reshape(b, s, n_main)
        kiw = _matmul(hb, w_tail, F32, tn=LANES).reshape(b, s, LANES)
        c0 = 3 * wa + 3 * wb + 2 * wc
        w_vt = jnp.concatenate([w_in[l, :, 2 * wa:3 * wa], w_in[l, :, c0:c0 + wc]], axis=1).T.astype(BF16)
        tile = min(ATT_TILE, s)
        vt = _matmul_nt_tiles(w_vt, hb, tile).reshape(b, s // tile, wa + wc, tile)
        lam_init = 0.8 - 0.6 * math.exp(-0.3 * l)
        ya = _diff_attention(proj, vt, lam_q1[l], lam_k1[l], lam_q2[l], lam_k2[l], diff_norm_g[l],
                             slopes_a, lam_init)
        kb = proj[:, :, 3 * wa + wb:3 * wa + 2 * wb]
        vb = proj[:, :, 3 * wa + 2 * wb:3 * wa + 3 * wb]
        kpad = jnp.pad(kb, ((0, 0), (left, 0), (0, 0)))
        vpad = jnp.pad(vb, ((0, 0), (left, 0), (0, 0)))
        yb = _band_attention(proj, kpad, vpad, _band_bias(rel_bias[l]))
        yc = _dsa_attention(proj, vt, kiw, slopes_c)
        wo = w_out[l].astype(BF16)
        h, hb = _matmul_ln(
            [ya.reshape(t, wa), yb.reshape(t, wb), yc.reshape(t, -1)],
            [wo[:wa], wo[wa:wa + wb], wo[wa + wb:]],
            h, ln1_g[l], ln1_b[l], alpha)
        q = _matmul(hb, xattn_wq[l].astype(BF16), BF16).reshape(b, s, d)
        km = _matmul(memb, xattn_wk[l].astype(BF16), BF16).reshape(b, -1, d)
        vm = _matmul(memb, xattn_wv[l].astype(BF16), BF16).reshape(b, -1, d)
        xo = _xattn_core(q, km, vm).reshape(t, d)
        h, hb = _matmul_ln([xo], [xattn_wo[l].astype(BF16)], h, ln2_g[l], ln2_b[l], alpha)
        if l % 2 == 0:
            hid = _swiglu_hidden(hb, ffn_w1[l // 2].astype(BF16), ffn_w3[l // 2].astype(BF16))
            h, hb = _matmul_ln([hid], [ffn_w2[l // 2].astype(BF16)], h, ln3_g[l], ln3_b[l], alpha)
        else:
            gates = _router_gates(hb, moe_router[l // 2])
            h, hb = _moe_ln(hb, gates, moe_w1[l // 2].astype(BF16), moe_w3[l // 2].astype(BF16),
                            moe_w2[l // 2].astype(BF16), h, ln3_g[l], ln3_b[l], alpha)
    return h.reshape(b, s, d)
```

```python
import functools
import math

import numpy as np
import jax
import jax.numpy as jnp
from jax import lax
from jax.experimental import pallas as pl
from jax.experimental.pallas import tpu as pltpu

F32 = jnp.float32
BF16 = jnp.bfloat16
I32 = jnp.int32

CHUNK = 64
H_A, DQK_A = 4, 64
DV_A = 2 * DQK_A
H_B, DH_B = 4, 64
LEFT_CHUNKS = 8
REL_MAX = 128
H_C, DH_C = 4, 64
H_IDX, D_IDX = 8, 64
TOPK_MAX = 256
H_X = 4
N_EXPERTS = 8
LN_EPS = 1e-5
RMS_EPS = 1e-6
NEG_INF = -1e30
MASKED = -2e30

LANES = 128
SUBLANES = 8
VMEM_LIMIT = 56 * 1024 * 1024
ATT_TILE = 256

_NT = (((1,), (1,)), ((), ()))


def _alibi_slopes():
    n = H_A + H_C
    s = (2.0 ** (-8.0 * np.arange(1, n + 1) / n)).astype(np.float32)
    return [float(v) for v in s[0::2]], [float(v) for v in s[1::2]]


def _params(*sem):
    return pltpu.CompilerParams(dimension_semantics=sem, vmem_limit_bytes=VMEM_LIMIT)


def _dot(a, b):
    return jnp.dot(a, b, preferred_element_type=F32)


def _dot_nt(a, b):
    return lax.dot_general(a, b, _NT, preferred_element_type=F32)


def _layer_norm_rows(y, g, b):
    mu = jnp.mean(y, axis=-1, keepdims=True)
    d = y - mu
    var = jnp.mean(d * d, axis=-1, keepdims=True)
    return d * lax.rsqrt(var + LN_EPS) * g + b


def _silu(a):
    return a / (1.0 + jnp.exp(-a))


def _mm_body(x_ref, w_ref, o_ref):
    o_ref[...] = _dot(x_ref[...], w_ref[...]).astype(o_ref.dtype)


def _matmul(x, w, out_dtype, tm=1024, tn=512):
    m, k = x.shape
    n = w.shape[1]
    tm, tn = min(tm, m), min(tn, n)
    return pl.pallas_call(
        _mm_body,
        grid=(m // tm, n // tn),
        in_specs=[pl.BlockSpec((tm, k), lambda i, j: (i, 0)),
                  pl.BlockSpec((k, tn), lambda i, j: (0, j))],
        out_specs=pl.BlockSpec((tm, tn), lambda i, j: (i, j)),
        out_shape=jax.ShapeDtypeStruct((m, n), out_dtype),
        compiler_params=_params("parallel", "parallel"),
    )(x, w)


def _mm_nt_body(wt_ref, x_ref, o_ref):
    o_ref[0] = _dot_nt(wt_ref[...], x_ref[...]).astype(o_ref.dtype)


def _matmul_nt_tiles(wt, x, tm):
    n, k = wt.shape
    m = x.shape[0]
    return pl.pallas_call(
        _mm_nt_body,
        grid=(m // tm,),
        in_specs=[pl.BlockSpec((n, k), lambda i: (0, 0)),
                  pl.BlockSpec((tm, k), lambda i: (i, 0))],
        out_specs=pl.BlockSpec((1, n, tm), lambda i: (i, 0, 0)),
        out_shape=jax.ShapeDtypeStruct((m // tm, n, tm), BF16),
        compiler_params=_params("parallel"),
    )(wt, x)


def _mm_ln_body(*refs, n_in, alpha):
    xs, ws = refs[:n_in], refs[n_in:2 * n_in]
    res_ref, g_ref, b_ref, of_ref, ob_ref = refs[2 * n_in:]
    acc = _dot(xs[0][...], ws[0][...])
    for x_ref, w_ref in zip(xs[1:], ws[1:]):
        acc = acc + _dot(x_ref[...], w_ref[...])
    out = _layer_norm_rows(alpha * res_ref[...] + acc, g_ref[...], b_ref[...])
    of_ref[...] = out
    ob_ref[...] = out.astype(BF16)


def _matmul_ln(xs, ws, res, g, b, alpha, tm=512):
    m, d = res.shape
    tm = min(tm, m)
    n_in = len(xs)
    in_specs = ([pl.BlockSpec((tm, x.shape[1]), lambda i: (i, 0)) for x in xs]
                + [pl.BlockSpec(w.shape, lambda i: (0, 0)) for w in ws]
                + [pl.BlockSpec((tm, d), lambda i: (i, 0)),
                   pl.BlockSpec((1, d), lambda i: (0, 0)),
                   pl.BlockSpec((1, d), lambda i: (0, 0))])
    return pl.pallas_call(
        functools.partial(_mm_ln_body, n_in=n_in, alpha=alpha),
        grid=(m // tm,),
        in_specs=in_specs,
        out_specs=[pl.BlockSpec((tm, d), lambda i: (i, 0)),
                   pl.BlockSpec((tm, d), lambda i: (i, 0))],
        out_shape=[jax.ShapeDtypeStruct((m, d), F32), jax.ShapeDtypeStruct((m, d), BF16)],
        compiler_params=_params("parallel"),
    )(*xs, *ws, res, g.reshape(1, d), b.reshape(1, d))


def _swiglu_body(x_ref, w1_ref, w3_ref, o_ref):
    x = x_ref[...]
    a = _dot(x, w1_ref[...])
    b = _dot(x, w3_ref[...])
    o_ref[...] = (_silu(a) * b).astype(o_ref.dtype)


def _swiglu_hidden(x, w1, w3, tm=512, tn=1408):
    m, k = x.shape
    n = w1.shape[1]
    tm = min(tm, m)
    if n % tn:
        tn = n
    return pl.pallas_call(
        _swiglu_body,
        grid=(m // tm, n // tn),
        in_specs=[pl.BlockSpec((tm, k), lambda i, j: (i, 0)),
                  pl.BlockSpec((k, tn), lambda i, j: (0, j)),
                  pl.BlockSpec((k, tn), lambda i, j: (0, j))],
        out_specs=pl.BlockSpec((tm, tn), lambda i, j: (i, j)),
        out_shape=jax.ShapeDtypeStruct((m, n), BF16),
        compiler_params=_params("parallel", "parallel"),
    )(x, w1, w3)


def _key_query_iotas(tile):
    kpos = lax.broadcasted_iota(I32, (tile, tile), 0)
    qpos = lax.broadcasted_iota(I32, (tile, tile), 1)
    return kpos, qpos


def _scaled_transpose(x, scale):
    return (x.astype(F32) * scale).T.astype(BF16)


def _softmax_probs(c, t_ref, p_ref, a_ref, m_ref, l_ref):
    m_old = m_ref[...]
    m_new = jnp.maximum(m_old, jnp.max(t_ref[...], axis=0, keepdims=True) - c)
    p = jnp.exp(t_ref[...] - (m_new + c))
    a = jnp.exp(m_old - m_new)
    l_ref[...] = a * l_ref[...] + jnp.sum(p, axis=0, keepdims=True)
    m_ref[...] = m_new
    a_ref[...] = a
    p_ref[...] = p.astype(BF16)


def _accumulate(vt, p_ref, a_ref, acc_ref):
    acc_ref[...] = a_ref[...] * acc_ref[...] + _dot(vt, p_ref[...])


def _pipelined_maps(n_maps, scores, probs, accumulate, t_scr, p_scr, a_scr):
    nt, npb = t_scr.shape[0], p_scr.shape[0]

    def stage_probs(k):
        probs(k, t_scr.at[k % nt], p_scr.at[k % npb], a_scr.at[k % npb])

    t_scr[0] = scores(0)
    if n_maps > 1:
        t_scr[1 % nt] = scores(1)
    stage_probs(0)
    for k in range(n_maps):
        if k + 2 < n_maps:
            t_scr[(k + 2) % nt] = scores(k + 2)
        if k + 1 < n_maps:
            stage_probs(k + 1)
        accumulate(k, p_scr.at[k % npb], a_scr.at[k % npb])


def _diff_body(q_ref, k_ref, vt_ref, lq1_ref, lk1_ref, lq2_ref, lk2_ref, gain_ref, o_ref,
               qt_scr, srel_scr, t_scr, p_scr, a_scr, m_scr, l_scr, acc_scr,
               *, tile, slopes, lam_init):
    i = pl.program_id(1)
    kpos, qpos = _key_query_iotas(tile)

    @pl.when((pl.program_id(0) == 0) & (i == 0))
    def _():
        rel = (qpos - kpos).astype(F32)
        for h in range(H_A):
            srel_scr[h] = slopes[h] * rel

    qt_scr[...] = _scaled_transpose(q_ref[0], DQK_A ** -0.5)
    m_scr[...] = jnp.full(m_scr.shape, NEG_INF, F32)
    l_scr[...] = jnp.zeros(l_scr.shape, F32)
    acc_scr[...] = jnp.zeros(acc_scr.shape, F32)

    def block(j, diag):
        k0 = pl.multiple_of(j * tile, tile)
        kb = k_ref[0, pl.ds(k0, tile), :]
        vtb = vt_ref[0, j]
        off = ((i - j) * tile).astype(F32)

        def scores(idx):
            h, c = idx // 2, idx * DQK_A
            s = _dot(kb[:, c:c + DQK_A], qt_scr[c:c + DQK_A, :])
            if diag:
                allowed = (kpos // CHUNK) <= (qpos // CHUNK)
                return jnp.where(allowed, s - jnp.abs(srel_scr[h]), MASKED)
            return s - srel_scr[h]

        def probs(idx, t_ref, p_ref, a_ref):
            shift = 0.0 if diag else slopes[idx // 2] * off
            _softmax_probs(shift, t_ref, p_ref, a_ref, m_scr.at[idx], l_scr.at[idx])

        def accumulate(idx, p_ref, a_ref):
            h = idx // 2
            _accumulate(vtb[h * DV_A:(h + 1) * DV_A, :], p_ref, a_ref, acc_scr.at[idx])

        _pipelined_maps(2 * H_A, scores, probs, accumulate, t_scr, p_scr, a_scr)

    def full_block(j, carry):
        block(j, False)
        return carry

    lax.fori_loop(0, i, full_block, 0)
    block(i, True)

    lam = (jnp.exp(jnp.sum(lq1_ref[...] * lk1_ref[...], axis=-1, keepdims=True))
           - jnp.exp(jnp.sum(lq2_ref[...] * lk2_ref[...], axis=-1, keepdims=True)) + lam_init)
    gain = gain_ref[...] * (1.0 - lam_init)
    outs = []
    for h in range(H_A):
        o = acc_scr[2 * h] / l_scr[2 * h] - lam * (acc_scr[2 * h + 1] / l_scr[2 * h + 1])
        o = o * lax.rsqrt(jnp.mean(o * o, axis=0, keepdims=True) + RMS_EPS)
        outs.append((o * gain).T)
    o_ref[0] = jnp.concatenate(outs, axis=-1).astype(BF16)


def _diff_attention(proj, vt, lq1, lk1, lq2, lk2, gain, slopes, lam_init):
    b, s, _ = proj.shape
    tile = min(ATT_TILE, s)
    nkb = s // tile
    w = H_A * DV_A
    vec = lambda a: a.reshape(1, -1).astype(F32)
    small = lambda n: pl.BlockSpec((1, n), lambda bi, i: (0, 0))
    return pl.pallas_call(
        functools.partial(_diff_body, tile=tile, slopes=slopes, lam_init=lam_init),
        grid=(b, nkb),
        in_specs=[pl.BlockSpec((1, tile, w), lambda bi, i: (bi, i, 0)),
                  pl.BlockSpec((1, s, w), lambda bi, i: (bi, 0, 1)),
                  pl.BlockSpec((1, nkb, w, tile), lambda bi, i: (bi, 0, 0, 0)),
                  small(DQK_A), small(DQK_A), small(DQK_A), small(DQK_A),
                  pl.BlockSpec((DV_A, 1), lambda bi, i: (0, 0))],
        out_specs=pl.BlockSpec((1, tile, w), lambda bi, i: (bi, i, 0)),
        out_shape=jax.ShapeDtypeStruct((b, s, w), BF16),
        scratch_shapes=[pltpu.VMEM((w, tile), BF16),
                        pltpu.VMEM((H_A, tile, tile), F32),
                        pltpu.VMEM((3, tile, tile), F32),
                        pltpu.VMEM((2, tile, tile), BF16),
                        pltpu.VMEM((2, 1, tile), F32),
                        pltpu.VMEM((2 * H_A, 1, tile), F32),
                        pltpu.VMEM((2 * H_A, 1, tile), F32),
                        pltpu.VMEM((2 * H_A, DV_A, tile), F32)],
        compiler_params=_params("arbitrary", "arbitrary"),
    )(proj, proj, vt, vec(lq1), vec(lk1), vec(lq2), vec(lk2), gain.reshape(-1, 1).astype(F32))


def _band_body(q_ref, k_ref, v_ref, bias_ref, o_ref, *, tile):
    i = pl.program_id(1)
    band = (LEFT_CHUNKS + 1) * CHUNK
    left = LEFT_CHUNKS * CHUNK
    kj = lax.broadcasted_iota(I32, (CHUNK, band), 1)
    for cc in range(tile // CHUNK):
        cg = i * (tile // CHUNK) + cc
        start = pl.multiple_of(cg * CHUNK, CHUNK)
        qc = q_ref[0, cc * CHUNK:(cc + 1) * CHUNK, :] * DH_B ** -0.5
        kw = k_ref[0, pl.ds(start, band), :]
        vw = v_ref[0, pl.ds(start, band), :]
        valid = kj >= left - cg * CHUNK
        outs = []
        for h in range(H_B):
            c = h * DH_B
            s = _dot_nt(qc[:, c:c + DH_B], kw[:, c:c + DH_B]) + bias_ref[h]
            s = jnp.where(valid, s, NEG_INF)
            p = jnp.exp(s - jnp.max(s, axis=-1, keepdims=True))
            l = jnp.sum(p, axis=-1, keepdims=True)
            outs.append(_dot(p.astype(BF16), vw[:, c:c + DH_B]) / l)
        o_ref[0, cc * CHUNK:(cc + 1) * CHUNK, :] = jnp.concatenate(outs, axis=-1).astype(BF16)


def _band_attention(proj, kpad, vpad, bias):
    b, s, _ = proj.shape
    tile = min(ATT_TILE, s)
    w = H_B * DH_B
    band = (LEFT_CHUNKS + 1) * CHUNK
    sp = kpad.shape[1]
    return pl.pallas_call(
        functools.partial(_band_body, tile=tile),
        grid=(b, s // tile),
        in_specs=[pl.BlockSpec((1, tile, w), lambda bi, i: (bi, i, 6)),
                  pl.BlockSpec((1, sp, w), lambda bi, i: (bi, 0, 0)),
                  pl.BlockSpec((1, sp, w), lambda bi, i: (bi, 0, 0)),
                  pl.BlockSpec((H_B, CHUNK, band), lambda bi, i: (0, 0, 0))],
        out_specs=pl.BlockSpec((1, tile, w), lambda bi, i: (bi, i, 0)),
        out_shape=jax.ShapeDtypeStruct((b, s, w), BF16),
        compiler_params=_params("parallel", "parallel"),
    )(proj, kpad, vpad, bias)


def _sortable(x):
    bits = pltpu.bitcast(x + 0.0, I32)
    return bits ^ ((bits >> 31) & 0x7FFFFFFF)


_INT_MIN = -2 ** 31
_VALID_KEY = int(np.array(0.5 * NEG_INF, np.float32).view(np.int32))
_VALID_KEY = _VALID_KEY ^ ((_VALID_KEY >> 31) & 0x7FFFFFFF)


def _dsa_body(qc_ref, kc_ref, vct_ref, qi_ref, kiw_ref, wq_ref, o_ref,
              qt_scr, qit_scr, srel_scr, t_scr, p_scr, a_scr, key_scr, jstar_scr, m_scr, l_scr, acc_scr,
              *, tile, topk, slopes, seq):
    i = pl.program_id(1)
    nblk = i + 1
    kpos, qpos = _key_query_iotas(tile)

    @pl.when((pl.program_id(0) == 0) & (i == 0))
    def _():
        rel = (qpos - kpos).astype(F32)
        for h in range(H_C):
            srel_scr[h] = slopes[h] * rel

    qit_scr[...] = _scaled_transpose(qi_ref[0], D_IDX ** -0.5)
    wt = wq_ref[0].T
    wrow = [wt[D_IDX + h:D_IDX + h + 1, :] * H_IDX ** -0.5 for h in range(H_IDX)]

    def score_block(j, diag):
        k0 = pl.multiple_of(j * tile, tile)
        kib = kiw_ref[0, pl.ds(k0, tile), :][:, :D_IDX].astype(BF16)
        isc = jnp.zeros((tile, tile), F32)
        for h in range(H_IDX):
            d = _dot(kib, qit_scr[h * D_IDX:(h + 1) * D_IDX, :])
            isc = isc + jnp.maximum(d, 0.0) * wrow[h]
        if diag:
            isc = jnp.where((kpos // CHUNK) <= (qpos // CHUNK), isc, NEG_INF)
        key_scr[pl.ds(k0, tile), :] = _sortable(isc)

    def score_full(j, carry):
        score_block(j, False)
        return carry

    lax.fori_loop(0, i, score_full, 0)
    score_block(i, True)

    def count(pred):
        def body(j, part):
            k0 = pl.multiple_of(j * tile, tile)
            hit = jnp.where(pred(key_scr[pl.ds(k0, tile), :], kpos + k0), 1.0, 0.0)
            return part + jnp.sum(hit.reshape(tile // SUBLANES, SUBLANES, tile), axis=0)
        part = lax.fori_loop(0, nblk, body, jnp.zeros((SUBLANES, tile), F32))
        return jnp.sum(part, axis=0, keepdims=True)

    kf = float(topk)

    def value_bit(p, u):
        cand_u = u | jnp.left_shift(jnp.int32(1), 31 - p)
        cand = cand_u ^ _INT_MIN
        cnt = count(lambda key, idx: key >= cand)
        return jnp.where(cnt >= kf, cand_u, u)

    u = lax.fori_loop(0, 32, value_bit, jnp.zeros((1, tile), I32))
    thr = u ^ _INT_MIN

    thr_b = thr
    c_gt = count(lambda key, idx: key > thr_b)
    c_ge = count(lambda key, idx: key >= thr_b)
    tied = (c_ge > kf) & (thr > _VALID_KEY)
    need = kf - c_gt
    jstar_scr[...] = jnp.full((1, tile), seq, I32)

    @pl.when(jnp.max(jnp.where(tied, 1.0, 0.0)) > 0.0)
    def _():
        nbits = max(1, int(math.ceil(math.log2(seq))))

        def index_bit(p, lo):
            cand = lo | jnp.left_shift(jnp.int32(1), nbits - 1 - p)
            cnt = count(lambda key, idx: (key == thr_b) & (idx < cand))
            return jnp.where(cnt < need, cand, lo)

        lo = lax.fori_loop(0, nbits, index_bit, jnp.zeros((1, tile), I32))
        jstar_scr[...] = jnp.where(tied, lo, seq)

    jstar = jstar_scr[...]
    thr2 = jnp.maximum(thr, _VALID_KEY + 1)

    qt_scr[...] = _scaled_transpose(qc_ref[0], DH_C ** -0.5)
    m_scr[...] = jnp.full(m_scr.shape, NEG_INF, F32)
    l_scr[...] = jnp.zeros(l_scr.shape, F32)
    acc_scr[...] = jnp.zeros(acc_scr.shape, F32)

    def attend(j, diag):
        k0 = pl.multiple_of(j * tile, tile)
        key = key_scr[pl.ds(k0, tile), :]
        sel = (key > thr2) | ((key == thr2) & (kpos + k0 <= jstar))
        kb = kc_ref[0, pl.ds(k0, tile), :]
        vtb = vct_ref[0, j]
        off = ((i - j) * tile).astype(F32)

        def scores(h):
            c = h * DH_C
            s = _dot(kb[:, c:c + DH_C], qt_scr[c:c + DH_C, :])
            bias = jnp.abs(srel_scr[h]) if diag else srel_scr[h]
            return jnp.where(sel, s - bias, MASKED)

        def probs(h, t_ref, p_ref, a_ref):
            shift = 0.0 if diag else slopes[h] * off
            _softmax_probs(shift, t_ref, p_ref, a_ref, m_scr.at[h], l_scr.at[h])

        def accumulate(h, p_ref, a_ref):
            _accumulate(vtb[h * DH_C:(h + 1) * DH_C, :], p_ref, a_ref, acc_scr.at[h])

        _pipelined_maps(H_C, scores, probs, accumulate, t_scr, p_scr, a_scr)

    def attend_full(j, carry):
        attend(j, False)
        return carry

    lax.fori_loop(0, i, attend_full, 0)
    attend(i, True)
    outs = [(acc_scr[h] / l_scr[h]).T for h in range(H_C)]
    o_ref[0] = jnp.concatenate(outs, axis=-1).astype(BF16)


def _dsa_attention(proj, vt, kiw, slopes):
    b, s, _ = proj.shape
    tile = min(ATT_TILE, s)
    nkb = s // tile
    w = H_C * DH_C
    wa = H_A * DV_A
    topk = min(TOPK_MAX, s // 4)
    return pl.pallas_call(
        functools.partial(_dsa_body, tile=tile, topk=topk, slopes=slopes, seq=s),
        grid=(b, nkb),
        in_specs=[pl.BlockSpec((1, tile, w), lambda bi, i: (bi, i, 9)),
                  pl.BlockSpec((1, s, w), lambda bi, i: (bi, 0, 10)),
                  pl.BlockSpec((1, nkb, w, tile), lambda bi, i: (bi, 0, wa // w, 0)),
                  pl.BlockSpec((1, tile, H_IDX * D_IDX), lambda bi, i: (bi, i, 6)),
                  pl.BlockSpec((1, s, LANES), lambda bi, i: (bi, 0, 0)),
                  pl.BlockSpec((1, tile, LANES), lambda bi, i: (bi, i, 0))],
        out_specs=pl.BlockSpec((1, tile, w), lambda bi, i: (bi, i, 0)),
        out_shape=jax.ShapeDtypeStruct((b, s, w), BF16),
        scratch_shapes=[pltpu.VMEM((w, tile), BF16),
                        pltpu.VMEM((H_IDX * D_IDX, tile), BF16),
                        pltpu.VMEM((H_C, tile, tile), F32),
                        pltpu.VMEM((3, tile, tile), F32),
                        pltpu.VMEM((2, tile, tile), BF16),
                        pltpu.VMEM((2, 1, tile), F32),
                        pltpu.VMEM((s, tile), I32),
                        pltpu.VMEM((1, tile), I32),
                        pltpu.VMEM((H_C, 1, tile), F32),
                        pltpu.VMEM((H_C, 1, tile), F32),
                        pltpu.VMEM((H_C, DH_C, tile), F32)],
        compiler_params=_params("arbitrary", "arbitrary"),
    )(proj, proj, vt, proj, kiw, kiw)


def _xattn_body(q_ref, k_ref, v_ref, o_ref, *, dh):
    outs = []
    for h in range(H_X):
        c = h * dh
        q = q_ref[0, :, c:c + dh] * dh ** -0.5
        s = _dot_nt(q, k_ref[0, :, c:c + dh])
        p = jnp.exp(s - jnp.max(s, axis=-1, keepdims=True))
        l = jnp.sum(p, axis=-1, keepdims=True)
        outs.append((_dot(p.astype(BF16), v_ref[0, :, c:c + dh]) / l).astype(BF16))
    o_ref[0] = jnp.concatenate(outs, axis=-1)


def _xattn_core(q, k, v, tq=512):
    b, s, d = q.shape
    n_mem = k.shape[1]
    tq = min(tq, s)
    return pl.pallas_call(
        functools.partial(_xattn_body, dh=d // H_X),
        grid=(b, s // tq),
        in_specs=[pl.BlockSpec((1, tq, d), lambda bi, i: (bi, i, 0)),
                  pl.BlockSpec((1, n_mem, d), lambda bi, i: (bi, 0, 0)),
                  pl.BlockSpec((1, n_mem, d), lambda bi, i: (bi, 0, 0))],
        out_specs=pl.BlockSpec((1, tq, d), lambda bi, i: (bi, i, 0)),
        out_shape=jax.ShapeDtypeStruct((b, s, d), BF16),
        compiler_params=_params("parallel", "parallel"),
    )(q, k, v)


def _router_body(x_ref, r_ref, g_ref):
    logits = _dot(x_ref[...], r_ref[...])
    lane = lax.broadcasted_iota(I32, logits.shape, 1).astype(F32)
    big = float(LANES)
    lg = jnp.where(lane < N_EXPERTS, logits, -jnp.inf)
    m1 = jnp.max(lg, axis=-1, keepdims=True)
    i1 = jnp.min(jnp.where(lg == m1, lane, big), axis=-1, keepdims=True)
    lg2 = jnp.where(lane == i1, -jnp.inf, lg)
    m2 = jnp.max(lg2, axis=-1, keepdims=True)
    i2 = jnp.min(jnp.where(lg2 == m2, lane, big), axis=-1, keepdims=True)
    e = jnp.exp(m2 - m1)
    den = 1.0 + e
    g_ref[...] = jnp.where(lane == i1, 1.0 / den, 0.0) + jnp.where(lane == i2, e / den, 0.0)


def _router_gates(x, router, tm=1024):
    m, d = x.shape
    tm = min(tm, m)
    rpad = jnp.zeros((d, LANES), BF16).at[:, :N_EXPERTS].set(router.astype(BF16))
    return pl.pallas_call(
        _router_body,
        grid=(m // tm,),
        in_specs=[pl.BlockSpec((tm, d), lambda i: (i, 0)),
                  pl.BlockSpec((d, LANES), lambda i: (0, 0))],
        out_specs=pl.BlockSpec((tm, LANES), lambda i: (i, 0)),
        out_shape=jax.ShapeDtypeStruct((m, LANES), F32),
        compiler_params=_params("parallel"),
    )(x, rpad)


MOE_BLOCK = 1024
MOE_CHUNK = 128
MOE_GROUP = 4
MOE_FF_TILE = 1792


def _moe_body(x_ref, gate_ref, tri_ref, w1_ref, w3_ref, w2_ref, o_ref,
              xs_scr, acc_scr, rcol_scr, rrow_scr, cnt_smem):
    e = pl.program_id(1)
    f = pl.program_id(2)
    nf = pl.num_programs(2)
    tb = x_ref.shape[0]
    ch = MOE_CHUNK
    gw = MOE_GROUP * ch

    @pl.when((e == 0) & (f == 0))
    def _():
        o_ref[...] = jnp.zeros(o_ref.shape, F32)
        member = jnp.where(gate_ref[...] > 0.0, 1.0, 0.0)
        rank = _dot(tri_ref[...], member.astype(BF16))
        rcol = jnp.where(member > 0.0, rank, -1.0)
        rcol_scr[...] = rcol
        rrow_scr[...] = rcol.T
        counts = jnp.sum(member, axis=0, keepdims=True)
        for ee in range(N_EXPERTS):
            cnt_smem[ee] = counts[0, ee].astype(I32)

    nchunk = (cnt_smem[e] + (ch - 1)) // ch

    @pl.when(f == 0)
    def _():
        acc_scr[...] = jnp.zeros(acc_scr.shape, F32)
        sub = lax.broadcasted_iota(I32, (SUBLANES, tb), 0)
        r_row = jnp.sum(jnp.where(sub == e, rrow_scr[0:SUBLANES, :], 0.0), axis=0, keepdims=True)
        row_id = lax.broadcasted_iota(I32, (ch, tb), 0).astype(F32)

        def gather(c, carry):
            r0 = pl.multiple_of(c * ch, ch)
            sel = jnp.where(row_id + r0.astype(F32) == r_row, 1.0, 0.0).astype(BF16)
            xs_scr[pl.ds(r0, ch), :] = _dot(sel, x_ref[...]).astype(BF16)
            return carry

        lax.fori_loop(0, nchunk, gather, 0)

    def ffn(c, carry):
        r0 = pl.multiple_of(c * ch, ch)
        xs = xs_scr[pl.ds(r0, ch), :]
        a = _dot(xs, w1_ref[0])
        b = _dot(xs, w3_ref[0])
        acc_scr[pl.ds(r0, ch), :] += _dot((_silu(a) * b).astype(BF16), w2_ref[0])
        return carry

    lax.fori_loop(0, nchunk, ffn, 0)

    @pl.when(f == nf - 1)
    def _():
        pick = lax.broadcasted_iota(I32, (tb, LANES), 1) == e
        r_col = jnp.sum(jnp.where(pick, rcol_scr[...], 0.0), axis=1, keepdims=True)
        g_col = jnp.sum(jnp.where(pick, gate_ref[...], 0.0), axis=1, keepdims=True)
        col_id = lax.broadcasted_iota(I32, (tb, gw), 1).astype(F32)

        def combine(gi, carry):
            r0 = pl.multiple_of(gi * gw, gw)
            selt = jnp.where(col_id + r0.astype(F32) == r_col, 1.0, 0.0).astype(BF16)
            y = acc_scr[pl.ds(r0, gw), :]
            y_hi = y.astype(BF16)
            y_lo = (y - y_hi.astype(F32)).astype(BF16)
            o_ref[...] += g_col * (_dot(selt, y_hi) + _dot(selt, y_lo))
            return carry

        lax.fori_loop(0, (nchunk + (MOE_GROUP - 1)) // MOE_GROUP, combine, 0)


def _moe_experts(x, gates, w1, w3, w2):
    m, d = x.shape
    ne, _, ff = w1.shape
    tb = min(MOE_BLOCK, m)
    tf = MOE_FF_TILE if ff % MOE_FF_TILE == 0 else ff
    assert m % tb == 0 and tb % (MOE_GROUP * MOE_CHUNK) == 0 and ne <= SUBLANES
    tri = jnp.asarray(np.tril(np.ones((tb, tb), np.float32), -1), BF16)
    row = lambda i, e, f: (i, 0)
    return pl.pallas_call(
        _moe_body,
        grid=(m // tb, ne, ff // tf),
        in_specs=[pl.BlockSpec((tb, d), row),
                  pl.BlockSpec((tb, LANES), row),
                  pl.BlockSpec((tb, tb), lambda i, e, f: (0, 0)),
                  pl.BlockSpec((1, d, tf), lambda i, e, f: (e, 0, f)),
                  pl.BlockSpec((1, d, tf), lambda i, e, f: (e, 0, f)),
                  pl.BlockSpec((1, tf, d), lambda i, e, f: (e, f, 0))],
        out_specs=pl.BlockSpec((tb, d), row),
        out_shape=jax.ShapeDtypeStruct((m, d), F32),
        scratch_shapes=[pltpu.VMEM((tb, d), BF16),
                        pltpu.VMEM((tb, d), F32),
                        pltpu.VMEM((tb, LANES), F32),
                        pltpu.VMEM((LANES, tb), F32),
                        pltpu.SMEM((N_EXPERTS,), I32)],
        compiler_params=_params("parallel", "arbitrary", "arbitrary"),
    )(x, gates, tri, w1, w3, w2)


def _res_ln_body(res_ref, y_ref, g_ref, b_ref, of_ref, ob_ref, *, alpha):
    out = _layer_norm_rows(alpha * res_ref[...] + y_ref[...], g_ref[...], b_ref[...])
    of_ref[...] = out
    ob_ref[...] = out.astype(BF16)


def _residual_ln(res, y, g, b, alpha, tm=512):
    m, d = res.shape
    tm = min(tm, m)
    row = pl.BlockSpec((tm, d), lambda i: (i, 0))
    vec = pl.BlockSpec((1, d), lambda i: (0, 0))
    return pl.pallas_call(
        functools.partial(_res_ln_body, alpha=alpha),
        grid=(m // tm,),
        in_specs=[row, row, vec, vec],
        out_specs=[row, row],
        out_shape=[jax.ShapeDtypeStruct((m, d), F32), jax.ShapeDtypeStruct((m, d), BF16)],
        compiler_params=_params("parallel"),
    )(res, y, g.reshape(1, d), b.reshape(1, d))


def _band_bias(table):
    band = (LEFT_CHUNKS + 1) * CHUNK
    left = LEFT_CHUNKS * CHUNK
    rev = table.astype(F32)[:, ::-1]
    pad = left - REL_MAX + CHUNK - 1
    n_rev = band - 1 - (left - REL_MAX) + 1
    assert n_rev <= 2 * REL_MAX + 1
    ext = jnp.concatenate([jnp.broadcast_to(rev[:, :1], (rev.shape[0], pad)), rev[:, :n_rev]], axis=1)
    rows = [ext[:, CHUNK - 1 - q:CHUNK - 1 - q + band] for q in range(CHUNK)]
    return jnp.stack(rows, axis=1)


def kernel(x, mem, w_in, lam_q1, lam_k1, lam_q2, lam_k2, diff_norm_g, rel_bias, w_out,
           ln1_g, ln1_b, xattn_wq, xattn_wk, xattn_wv, xattn_wo, ln2_g, ln2_b,
           ffn_w1, ffn_w3, ffn_w2, moe_router, moe_w1, moe_w3, moe_w2, ln3_g, ln3_b):
    b, s, d = x.shape
    depth = w_in.shape[0]
    t = b * s
    alpha = (2.0 * depth) ** 0.25
    slopes_a, slopes_c = _alibi_slopes()
    n_main = 3 * H_A * DV_A + 3 * H_B * DH_B + 3 * H_C * DH_C + H_IDX * D_IDX
    n_tail = D_IDX + H_IDX
    wa, wb, wc = H_A * DV_A, H_B * DH_B, H_C * DH_C
    left = LEFT_CHUNKS * CHUNK

    h = x.reshape(t, d)
    hb = h.astype(BF16)
    memb = mem.reshape(-1, d).astype(BF16)
    for l in range(depth):
        w_main = w_in[l, :, :n_main].astype(BF16)
        w_tail = jnp.zeros((d, LANES), BF16).at[:, :n_tail].set(w_in[l, :, n_main:].astype(BF16))
        proj = _matmul(hb, w_main, BF16).reshape(b, s, n_main)
        kiw = _matmul(hb, w_tail, F32, tn=LANES).reshape(b, s, LANES)
        c0 = 3 * wa + 3 * wb + 2 * wc
        w_vt = jnp.concatenate([w_in[l, :, 2 * wa:3 * wa], w_in[l, :, c0:c0 + wc]], axis=1).T.astype(BF16)
        tile = min(ATT_TILE, s)
        vt = _matmul_nt_tiles(w_vt, hb, tile).reshape(b, s // tile, wa + wc, tile)
        lam_init = 0.8 - 0.6 * math.exp(-0.3 * l)
        ya = _diff_attention(proj, vt, lam_q1[l], lam_k1[l], lam_q2[l], lam_k2[l], diff_norm_g[l],
                             slopes_a, lam_init)
        kb = proj[:, :, 3 * wa + wb:3 * wa + 2 * wb]
        vb = proj[:, :, 3 * wa + 2 * wb:3 * wa + 3 * wb]
        kpad = jnp.pad(kb, ((0, 0), (left, 0), (0, 0)))
        vpad = jnp.pad(vb, ((0, 0), (left, 0), (0, 0)))
        yb = _band_attention(proj, kpad, vpad, _band_bias(rel_bias[l]))
        yc = _dsa_attention(proj, vt, kiw, slopes_c)
        wo = w_out[l].astype(BF16)
        h, hb = _matmul_ln(
            [ya.reshape(t, wa), yb.reshape(t, wb), yc.reshape(t, -1)],
            [wo[:wa], wo[wa:wa + wb], wo[wa + wb:]],
            h, ln1_g[l], ln1_b[l], alpha)
        q = _matmul(hb, xattn_wq[l].astype(BF16), BF16).reshape(b, s, d)
        km = _matmul(memb, xattn_wk[l].astype(BF16), BF16).reshape(b, -1, d)
        vm = _matmul(memb, xattn_wv[l].astype(BF16), BF16).reshape(b, -1, d)
        xo = _xattn_core(q, km, vm).reshape(t, d)
        h, hb = _matmul_ln([xo], [xattn_wo[l].astype(BF16)], h, ln2_g[l], ln2_b[l], alpha)
        if l % 2 == 0:
            hid = _swiglu_hidden(hb, ffn_w1[l // 2].astype(BF16), ffn_w3[l // 2].astype(BF16))
            h, hb = _matmul_ln([hid], [ffn_w2[l // 2].astype(BF16)], h, ln3_g[l], ln3_b[l], alpha)
        else:
            gates = _router_gates(hb, moe_router[l // 2])
            y = _moe_experts(hb, gates, moe_w1[l // 2].astype(BF16), moe_w3[l // 2].astype(BF16),
                             moe_w2[l // 2].astype(BF16))
            h, hb = _residual_ln(h, y, ln3_g[l], ln3_b[l], alpha)
    return h.reshape(b, s, d)
```

```python
import functools
import math

import numpy as np
import jax
import jax.numpy as jnp
from jax import lax
from jax.experimental import pallas as pl
from jax.experimental.pallas import tpu as pltpu

F32 = jnp.float32
BF16 = jnp.bfloat16
I32 = jnp.int32
I16 = jnp.int16

CHUNK = 64
H_A, DQK_A = 4, 64
DV_A = 2 * DQK_A
H_B, DH_B = 4, 64
LEFT_CHUNKS = 8
REL_MAX = 128
H_C, DH_C = 4, 64
H_IDX, D_IDX = 8, 64
TOPK_MAX = 256
H_X = 4
N_EXPERTS = 8
LN_EPS = 1e-5
RMS_EPS = 1e-6
NEG_INF = -1e30
MASKED = -2e30

LANES = 128
SUBLANES = 8
VMEM_LIMIT = 56 * 1024 * 1024
ATT_TILE = 256

_NT = (((1,), (1,)), ((), ()))


def _alibi_slopes():
    n = H_A + H_C
    s = (2.0 ** (-8.0 * np.arange(1, n + 1) / n)).astype(np.float32)
    return [float(v) for v in s[0::2]], [float(v) for v in s[1::2]]


def _params(*sem):
    return pltpu.CompilerParams(dimension_semantics=sem, vmem_limit_bytes=VMEM_LIMIT)


def _dot(a, b):
    return jnp.dot(a, b, preferred_element_type=F32)


SCORE_ROW_CHUNK = 128


def _dot_row_chunks(a, b):
    rows = a.shape[0]
    step = min(SCORE_ROW_CHUNK, rows)
    return jnp.concatenate([_dot(a[r:r + step], b) for r in range(0, rows, step)], axis=0)


def _dot_nt(a, b):
    return lax.dot_general(a, b, _NT, preferred_element_type=F32)


def _layer_norm_rows(y, g, b):
    mu = jnp.mean(y, axis=-1, keepdims=True)
    d = y - mu
    var = jnp.mean(d * d, axis=-1, keepdims=True)
    return d * lax.rsqrt(var + LN_EPS) * g + b


def _silu(a):
    return a / (1.0 + jnp.exp(-a))


def _mm_body(x_ref, w_ref, o_ref):
    o_ref[...] = _dot(x_ref[...], w_ref[...]).astype(o_ref.dtype)


def _matmul(x, w, out_dtype, tm=1024, tn=512):
    m, k = x.shape
    n = w.shape[1]
    tm, tn = min(tm, m), min(tn, n)
    return pl.pallas_call(
        _mm_body,
        grid=(m // tm, n // tn),
        in_specs=[pl.BlockSpec((tm, k), lambda i, j: (i, 0)),
                  pl.BlockSpec((k, tn), lambda i, j: (0, j))],
        out_specs=pl.BlockSpec((tm, tn), lambda i, j: (i, j)),
        out_shape=jax.ShapeDtypeStruct((m, n), out_dtype),
        compiler_params=_params("parallel", "parallel"),
    )(x, w)


def _mm_nt_body(wt_ref, x_ref, o_ref):
    o_ref[0] = _dot_nt(wt_ref[...], x_ref[...]).astype(o_ref.dtype)


def _matmul_nt_tiles(wt, x, tm):
    n, k = wt.shape
    m = x.shape[0]
    return pl.pallas_call(
        _mm_nt_body,
        grid=(m // tm,),
        in_specs=[pl.BlockSpec((n, k), lambda i: (0, 0)),
                  pl.BlockSpec((tm, k), lambda i: (i, 0))],
        out_specs=pl.BlockSpec((1, n, tm), lambda i: (i, 0, 0)),
        out_shape=jax.ShapeDtypeStruct((m // tm, n, tm), BF16),
        compiler_params=_params("parallel"),
    )(wt, x)


def _mm_ln_body(*refs, n_in, alpha):
    xs, ws = refs[:n_in], refs[n_in:2 * n_in]
    res_ref, g_ref, b_ref, of_ref, ob_ref = refs[2 * n_in:]
    acc = _dot(xs[0][...], ws[0][...])
    for x_ref, w_ref in zip(xs[1:], ws[1:]):
        acc = acc + _dot(x_ref[...], w_ref[...])
    out = _layer_norm_rows(alpha * res_ref[...] + acc, g_ref[...], b_ref[...])
    of_ref[...] = out
    ob_ref[...] = out.astype(BF16)


def _matmul_ln(xs, ws, res, g, b, alpha, tm=512):
    m, d = res.shape
    tm = min(tm, m)
    n_in = len(xs)
    in_specs = ([pl.BlockSpec((tm, x.shape[1]), lambda i: (i, 0)) for x in xs]
                + [pl.BlockSpec(w.shape, lambda i: (0, 0)) for w in ws]
                + [pl.BlockSpec((tm, d), lambda i: (i, 0)),
                   pl.BlockSpec((1, d), lambda i: (0, 0)),
                   pl.BlockSpec((1, d), lambda i: (0, 0))])
    return pl.pallas_call(
        functools.partial(_mm_ln_body, n_in=n_in, alpha=alpha),
        grid=(m // tm,),
        in_specs=in_specs,
        out_specs=[pl.BlockSpec((tm, d), lambda i: (i, 0)),
                   pl.BlockSpec((tm, d), lambda i: (i, 0))],
        out_shape=[jax.ShapeDtypeStruct((m, d), F32), jax.ShapeDtypeStruct((m, d), BF16)],
        compiler_params=_params("parallel"),
    )(*xs, *ws, res, g.reshape(1, d), b.reshape(1, d))


def _swiglu_body(x_ref, w1_ref, w3_ref, o_ref):
    x = x_ref[...]
    a = _dot(x, w1_ref[...])
    b = _dot(x, w3_ref[...])
    o_ref[...] = (_silu(a) * b).astype(o_ref.dtype)


def _swiglu_hidden(x, w1, w3, tm=512, tn=1408):
    m, k = x.shape
    n = w1.shape[1]
    tm = min(tm, m)
    if n % tn:
        tn = n
    return pl.pallas_call(
        _swiglu_body,
        grid=(m // tm, n // tn),
        in_specs=[pl.BlockSpec((tm, k), lambda i, j: (i, 0)),
                  pl.BlockSpec((k, tn), lambda i, j: (0, j)),
                  pl.BlockSpec((k, tn), lambda i, j: (0, j))],
        out_specs=pl.BlockSpec((tm, tn), lambda i, j: (i, j)),
        out_shape=jax.ShapeDtypeStruct((m, n), BF16),
        compiler_params=_params("parallel", "parallel"),
    )(x, w1, w3)


def _key_query_iotas(tile):
    kpos = lax.broadcasted_iota(I32, (tile, tile), 0)
    qpos = lax.broadcasted_iota(I32, (tile, tile), 1)
    return kpos, qpos


def _scaled_transpose(x, scale):
    return (x.astype(F32) * scale).T.astype(BF16)


LOG2E = math.log2(math.e)
ALIBI_PARTS = 3
ALIBI_ROWS = 16


def _bf16_parts(x):
    parts, r = [], np.float32(x)
    for _ in range(ALIBI_PARTS):
        p = np.float32(r.astype(jnp.bfloat16))
        parts.append(float(p))
        r = np.float32(r - p)
    return parts


def _alibi_key_cols(tile, parts):
    lane = lax.broadcasted_iota(I32, (tile, LANES), 1)
    kr = lax.broadcasted_iota(I32, (tile, LANES), 0).astype(F32)
    v = jnp.where((lane >= ALIBI_PARTS) & (lane < 2 * ALIBI_PARTS), kr, 0.0)
    for n, c in enumerate(parts):
        v = jnp.where(lane == n, c, v)
    return v.astype(BF16)


def _alibi_query_rows(tile, parts):
    row = lax.broadcasted_iota(I32, (ALIBI_ROWS, tile), 0)
    qr = lax.broadcasted_iota(I32, (ALIBI_ROWS, tile), 1).astype(F32)
    v = jnp.where(row < ALIBI_PARTS, -qr, 0.0)
    for n, c in enumerate(parts):
        v = jnp.where(row == ALIBI_PARTS + n, c, v)
    return v.astype(BF16)


def _init_alibi(slopes, dneg_scr, kx_scr, qa_scr, tile, maps_per_head):
    kpos, qpos = _key_query_iotas(tile)
    rel = (qpos - kpos).astype(F32)
    qa_scr[...] = jnp.zeros(qa_scr.shape, BF16)
    for h, slope in enumerate(slopes):
        parts = _bf16_parts(slope * LOG2E)
        dneg_scr[h] = 2.0 * jnp.minimum(np.float32(slope * LOG2E) * rel, 0.0)
        kx_scr[h] = _alibi_key_cols(tile, parts)
        for mm in range(maps_per_head):
            qa_scr[h * maps_per_head + mm, LANES:LANES + ALIBI_ROWS, :] = _alibi_query_rows(tile, parts)


def _store_query_maps(qa_scr, qt, n_maps, dh):
    for idx in range(n_maps):
        r0 = (idx % 2) * dh
        qa_scr[idx, r0:r0 + dh, :] = qt[idx * dh:(idx + 1) * dh, :]


def _softmax_probs(c, t_ref, p_ref, a_ref, m_ref, l_ref):
    m_old = m_ref[...]
    m_new = jnp.maximum(m_old, jnp.max(t_ref[...], axis=0, keepdims=True) - c)
    p = jnp.exp2(t_ref[...] - (m_new + c))
    a = jnp.exp2(m_old - m_new)
    l_ref[...] = a * l_ref[...] + jnp.sum(p, axis=0, keepdims=True)
    m_ref[...] = m_new
    a_ref[...] = a
    p_ref[...] = p.astype(BF16)


def _accumulate(vt, p_ref, a_ref, acc_ref):
    acc_ref[...] = a_ref[...] * acc_ref[...] + _dot(vt, p_ref[...])


N_T_SLOTS, N_P_SLOTS = 3, 2


def _stage_scratch(tile):
    return ([pltpu.VMEM((tile, tile), F32)] * N_T_SLOTS + [pltpu.VMEM((tile, tile), BF16)] * N_P_SLOTS
            + [pltpu.VMEM((1, tile), F32)] * N_P_SLOTS)


def _pipelined_maps(n_maps, scores, probs, accumulate, stage):
    t_slots = stage[:N_T_SLOTS]
    p_slots = stage[N_T_SLOTS:N_T_SLOTS + N_P_SLOTS]
    a_slots = stage[N_T_SLOTS + N_P_SLOTS:]
    nt, npb = N_T_SLOTS, N_P_SLOTS

    def stage_probs(k):
        probs(k, t_slots[k % nt], p_slots[k % npb], a_slots[k % npb])

    t_slots[0][...] = scores(0)
    if n_maps > 1:
        t_slots[1][...] = scores(1)
    stage_probs(0)
    for k in range(n_maps):
        if k + 2 < n_maps:
            t_slots[(k + 2) % nt][...] = scores(k + 2)
        if k + 1 < n_maps:
            stage_probs(k + 1)
        accumulate(k, p_slots[k % npb], a_slots[k % npb])


def _diff_body(q_ref, k_ref, vt_ref, lq1_ref, lk1_ref, lq2_ref, lk2_ref, gain_ref, o_ref,
               qa_scr, kx_scr, dneg_scr, m_scr, l_scr, acc_scr, *stage,
               tile, slopes, lam_init):
    i = pl.program_id(1)
    kpos, qpos = _key_query_iotas(tile)

    @pl.when((pl.program_id(0) == 0) & (i == 0))
    def _():
        _init_alibi(slopes, dneg_scr, kx_scr, qa_scr, tile, 2)

    _store_query_maps(qa_scr, _scaled_transpose(q_ref[0], 1.0), 2 * H_A, DQK_A)
    m_scr[...] = jnp.full(m_scr.shape, NEG_INF, F32)
    l_scr[...] = jnp.zeros(l_scr.shape, F32)
    acc_scr[...] = jnp.zeros(acc_scr.shape, F32)

    def block(j, diag):
        k0 = pl.multiple_of(j * tile, tile)
        kb = k_ref[0, pl.ds(k0, tile), :]
        vtb = vt_ref[0, j]
        off = ((i - j) * tile).astype(F32)

        def scores(idx):
            h, pair = idx // 2, idx // 2
            keys = jnp.concatenate([kb[:, pair * LANES:(pair + 1) * LANES], kx_scr[h]], axis=1)
            s = _dot_row_chunks(keys, qa_scr[idx])
            if diag:
                allowed = (kpos // CHUNK) <= (qpos // CHUNK)
                return jnp.where(allowed, s + dneg_scr[h], MASKED)
            return s

        def probs(idx, t_ref, p_ref, a_ref):
            shift = 0.0 if diag else (slopes[idx // 2] * LOG2E) * off
            _softmax_probs(shift, t_ref, p_ref, a_ref, m_scr.at[idx], l_scr.at[idx])

        def accumulate(idx, p_ref, a_ref):
            h = idx // 2
            _accumulate(vtb[h * DV_A:(h + 1) * DV_A, :], p_ref, a_ref, acc_scr.at[idx])

        _pipelined_maps(2 * H_A, scores, probs, accumulate, stage)

    def full_block(j, carry):
        block(j, False)
        return carry

    lax.fori_loop(0, i, full_block, 0)
    block(i, True)

    lam = (jnp.exp(jnp.sum(lq1_ref[...] * lk1_ref[...], axis=-1, keepdims=True))
           - jnp.exp(jnp.sum(lq2_ref[...] * lk2_ref[...], axis=-1, keepdims=True)) + lam_init)
    gain = gain_ref[...] * (1.0 - lam_init)
    outs = []
    for h in range(H_A):
        o = acc_scr[2 * h] / l_scr[2 * h] - lam * (acc_scr[2 * h + 1] / l_scr[2 * h + 1])
        o = o * lax.rsqrt(jnp.mean(o * o, axis=0, keepdims=True) + RMS_EPS)
        outs.append((o * gain).T)
    o_ref[0] = jnp.concatenate(outs, axis=-1).astype(BF16)


def _diff_attention(proj, vt, lq1, lk1, lq2, lk2, gain, slopes, lam_init):
    b, s, _ = proj.shape
    tile = min(ATT_TILE, s)
    nkb = s // tile
    w = H_A * DV_A
    vec = lambda a: a.reshape(1, -1).astype(F32)
    small = lambda n: pl.BlockSpec((1, n), lambda bi, i: (0, 0))
    return pl.pallas_call(
        functools.partial(_diff_body, tile=tile, slopes=slopes, lam_init=lam_init),
        grid=(b, nkb),
        in_specs=[pl.BlockSpec((1, tile, w), lambda bi, i: (bi, i, 0)),
                  pl.BlockSpec((1, s, w), lambda bi, i: (bi, 0, 1)),
                  pl.BlockSpec((1, nkb, w, tile), lambda bi, i: (bi, 0, 0, 0)),
                  small(DQK_A), small(DQK_A), small(DQK_A), small(DQK_A),
                  pl.BlockSpec((DV_A, 1), lambda bi, i: (0, 0))],
        out_specs=pl.BlockSpec((1, tile, w), lambda bi, i: (bi, i, 0)),
        out_shape=jax.ShapeDtypeStruct((b, s, w), BF16),
        scratch_shapes=[pltpu.VMEM((2 * H_A, 2 * LANES, tile), BF16),
                        pltpu.VMEM((H_A, tile, LANES), BF16),
                        pltpu.VMEM((H_A, tile, tile), F32),
                        pltpu.VMEM((2 * H_A, 1, tile), F32),
                        pltpu.VMEM((2 * H_A, 1, tile), F32),
                        pltpu.VMEM((2 * H_A, DV_A, tile), F32)] + _stage_scratch(tile),
        compiler_params=_params("arbitrary", "arbitrary"),
    )(proj, proj, vt, vec(lq1), vec(lk1), vec(lq2), vec(lk2), gain.reshape(-1, 1).astype(F32))


def _band_body(q_ref, k_ref, v_ref, bias_ref, o_ref, *, tile):
    i = pl.program_id(1)
    band = (LEFT_CHUNKS + 1) * CHUNK
    left = LEFT_CHUNKS * CHUNK
    kj = lax.broadcasted_iota(I32, (CHUNK, band), 1)
    for cc in range(tile // CHUNK):
        cg = i * (tile // CHUNK) + cc
        start = pl.multiple_of(cg * CHUNK, CHUNK)
        qc = q_ref[0, cc * CHUNK:(cc + 1) * CHUNK, :] * DH_B ** -0.5
        kw = k_ref[0, pl.ds(start, band), :]
        vw = v_ref[0, pl.ds(start, band), :]
        valid = kj >= left - cg * CHUNK
        outs = []
        for h in range(H_B):
            c = h * DH_B
            s = _dot_nt(qc[:, c:c + DH_B], kw[:, c:c + DH_B]) + bias_ref[h]
            s = jnp.where(valid, s, NEG_INF)
            p = jnp.exp(s - jnp.max(s, axis=-1, keepdims=True))
            l = jnp.sum(p, axis=-1, keepdims=True)
            outs.append(_dot(p.astype(BF16), vw[:, c:c + DH_B]) / l)
        o_ref[0, cc * CHUNK:(cc + 1) * CHUNK, :] = jnp.concatenate(outs, axis=-1).astype(BF16)


def _band_attention(proj, kpad, vpad, bias):
    b, s, _ = proj.shape
    tile = min(ATT_TILE, s)
    w = H_B * DH_B
    band = (LEFT_CHUNKS + 1) * CHUNK
    sp = kpad.shape[1]
    return pl.pallas_call(
        functools.partial(_band_body, tile=tile),
        grid=(b, s // tile),
        in_specs=[pl.BlockSpec((1, tile, w), lambda bi, i: (bi, i, 6)),
                  pl.BlockSpec((1, sp, w), lambda bi, i: (bi, 0, 0)),
                  pl.BlockSpec((1, sp, w), lambda bi, i: (bi, 0, 0)),
                  pl.BlockSpec((H_B, CHUNK, band), lambda bi, i: (0, 0, 0))],
        out_specs=pl.BlockSpec((1, tile, w), lambda bi, i: (bi, i, 0)),
        out_shape=jax.ShapeDtypeStruct((b, s, w), BF16),
        compiler_params=_params("parallel", "parallel"),
    )(proj, kpad, vpad, bias)


def _sortable(x):
    bits = pltpu.bitcast(x + 0.0, I32)
    return bits ^ ((bits >> 31) & 0x7FFFFFFF)


_INT_MIN = -2 ** 31
_VALID_KEY = int(np.array(0.5 * NEG_INF, np.float32).view(np.int32))
_VALID_KEY = _VALID_KEY ^ ((_VALID_KEY >> 31) & 0x7FFFFFFF)


def _dsa_body(qc_ref, kc_ref, vct_ref, qi_ref, kiw_ref, wq_ref, o_ref,
              qa_scr, kx_scr, dneg_scr, qit_scr, key_scr, hi_scr, lo_scr, jstar_scr,
              m_scr, l_scr, acc_scr, *stage,
              tile, topk, slopes, seq):
    i = pl.program_id(1)
    nblk = i + 1
    kpos, qpos = _key_query_iotas(tile)

    @pl.when((pl.program_id(0) == 0) & (i == 0))
    def _():
        _init_alibi(slopes, dneg_scr, kx_scr, qa_scr, tile, 1)

    qit_scr[...] = _scaled_transpose(qi_ref[0], D_IDX ** -0.5)
    wt = wq_ref[0].T
    wrow = [wt[D_IDX + h:D_IDX + h + 1, :] * H_IDX ** -0.5 for h in range(H_IDX)]

    def score_block(j, diag):
        k0 = pl.multiple_of(j * tile, tile)
        kib = kiw_ref[0, pl.ds(k0, tile), :][:, :D_IDX].astype(BF16)
        isc = jnp.zeros((tile, tile), F32)
        for h in range(H_IDX):
            d = _dot(kib, qit_scr[h * D_IDX:(h + 1) * D_IDX, :])
            isc = isc + jnp.maximum(d, 0.0) * wrow[h]
        if diag:
            isc = jnp.where((kpos // CHUNK) <= (qpos // CHUNK), isc, NEG_INF)
        key = _sortable(isc)
        key_scr[pl.ds(k0, tile), :] = key
        hi_scr[pl.ds(k0, tile), :] = (key >> 16).astype(I16)
        lo_scr[pl.ds(k0, tile), :] = ((key & 0xFFFF) - 2 ** 15).astype(I16)

    def score_full(j, carry):
        score_block(j, False)
        return carry

    lax.fori_loop(0, i, score_full, 0)
    score_block(i, True)

    def count(pred):
        def body(j, part):
            k0 = pl.multiple_of(j * tile, tile)
            hit = jnp.where(pred(key_scr[pl.ds(k0, tile), :], kpos + k0), 1.0, 0.0)
            return part + jnp.sum(hit.reshape(tile // SUBLANES, SUBLANES, tile), axis=0)
        part = lax.fori_loop(0, nblk, body, jnp.zeros((SUBLANES, tile), F32))
        return jnp.sum(part, axis=0, keepdims=True)

    kf = float(topk)
    rows16 = 2 * SUBLANES

    def count16(src_ref, pred):
        def body(j, part):
            k0 = pl.multiple_of(j * tile, tile)
            hit = jnp.where(pred(src_ref[pl.ds(k0, tile), :]), jnp.int16(1), jnp.int16(0))
            for g in range(tile // rows16):
                part = part + hit[g * rows16:(g + 1) * rows16, :]
            return part
        part = lax.fori_loop(0, nblk, body, jnp.zeros((rows16, tile), I16))
        return jnp.sum(part.astype(I32), axis=0, keepdims=True).astype(F32)

    def search16(src_ref, base):
        def bit(p, u):
            cand_u = u | jnp.left_shift(jnp.int32(1), 15 - p)
            cand = (cand_u - 2 ** 15).astype(I16)
            cnt = base + count16(src_ref, lambda v: v >= cand)
            return jnp.where(cnt >= kf, cand_u, u)
        return lax.fori_loop(0, 16, bit, jnp.zeros((1, tile), I32))

    thr_hi = search16(hi_scr, 0.0) - 2 ** 15
    thr_hi16 = thr_hi.astype(I16)
    above = count16(hi_scr, lambda v: v > thr_hi16)

    def mask_low(j, carry):
        k0 = pl.multiple_of(j * tile, tile)
        rows = pl.ds(k0, tile)
        lo_scr[rows, :] = jnp.where(hi_scr[rows, :] == thr_hi16, lo_scr[rows, :], jnp.int16(-2 ** 15))
        return carry

    lax.fori_loop(0, nblk, mask_low, 0)
    thr = jnp.left_shift(thr_hi, 16) + search16(lo_scr, above)

    thr_b = thr
    c_gt = count(lambda key, idx: key > thr_b)
    c_ge = count(lambda key, idx: key >= thr_b)
    tied = (c_ge > kf) & (thr > _VALID_KEY)
    need = kf - c_gt
    jstar_scr[...] = jnp.full((1, tile), seq, I32)

    @pl.when(jnp.max(jnp.where(tied, 1.0, 0.0)) > 0.0)
    def _():
        nbits = max(1, int(math.ceil(math.log2(seq))))

        def index_bit(p, lo):
            cand = lo | jnp.left_shift(jnp.int32(1), nbits - 1 - p)
            cnt = count(lambda key, idx: (key == thr_b) & (idx < cand))
            return jnp.where(cnt < need, cand, lo)

        lo = lax.fori_loop(0, nbits, index_bit, jnp.zeros((1, tile), I32))
        jstar_scr[...] = jnp.where(tied, lo, seq)

    jstar = jstar_scr[...]
    thr2 = jnp.maximum(thr, _VALID_KEY + 1)

    _store_query_maps(qa_scr, _scaled_transpose(qc_ref[0], 1.0), H_C, DH_C)
    m_scr[...] = jnp.full(m_scr.shape, NEG_INF, F32)
    l_scr[...] = jnp.zeros(l_scr.shape, F32)
    acc_scr[...] = jnp.zeros(acc_scr.shape, F32)

    def attend(j, diag):
        k0 = pl.multiple_of(j * tile, tile)
        key = key_scr[pl.ds(k0, tile), :]
        sel = (key > thr2) | ((key == thr2) & (kpos + k0 <= jstar))
        kb = kc_ref[0, pl.ds(k0, tile), :]
        vtb = vct_ref[0, j]
        off = ((i - j) * tile).astype(F32)

        def scores(h):
            pair = h // 2
            keys = jnp.concatenate([kb[:, pair * LANES:(pair + 1) * LANES], kx_scr[h]], axis=1)
            s = _dot_row_chunks(keys, qa_scr[h])
            if diag:
                s = s + dneg_scr[h]
            return jnp.where(sel, s, MASKED)

        def probs(h, t_ref, p_ref, a_ref):
            shift = 0.0 if diag else (slopes[h] * LOG2E) * off
            _softmax_probs(shift, t_ref, p_ref, a_ref, m_scr.at[h], l_scr.at[h])

        def accumulate(h, p_ref, a_ref):
            _accumulate(vtb[h * DH_C:(h + 1) * DH_C, :], p_ref, a_ref, acc_scr.at[h])

        _pipelined_maps(H_C, scores, probs, accumulate, stage)

    def attend_full(j, carry):
        attend(j, False)
        return carry

    lax.fori_loop(0, i, attend_full, 0)
    attend(i, True)
    outs = [(acc_scr[h] / l_scr[h]).T for h in range(H_C)]
    o_ref[0] = jnp.concatenate(outs, axis=-1).astype(BF16)


def _dsa_attention(proj, vt, kiw, slopes):
    b, s, _ = proj.shape
    tile = min(ATT_TILE, s)
    nkb = s // tile
    w = H_C * DH_C
    wa = H_A * DV_A
    topk = min(TOPK_MAX, s // 4)
    return pl.pallas_call(
        functools.partial(_dsa_body, tile=tile, topk=topk, slopes=slopes, seq=s),
        grid=(b, nkb),
        in_specs=[pl.BlockSpec((1, tile, w), lambda bi, i: (bi, i, 9)),
                  pl.BlockSpec((1, s, w), lambda bi, i: (bi, 0, 10)),
                  pl.BlockSpec((1, nkb, w, tile), lambda bi, i: (bi, 0, wa // w, 0)),
                  pl.BlockSpec((1, tile, H_IDX * D_IDX), lambda bi, i: (bi, i, 6)),
                  pl.BlockSpec((1, s, LANES), lambda bi, i: (bi, 0, 0)),
                  pl.BlockSpec((1, tile, LANES), lambda bi, i: (bi, i, 0))],
        out_specs=pl.BlockSpec((1, tile, w), lambda bi, i: (bi, i, 0)),
        out_shape=jax.ShapeDtypeStruct((b, s, w), BF16),
        scratch_shapes=[pltpu.VMEM((H_C, 2 * LANES, tile), BF16),
                        pltpu.VMEM((H_C, tile, LANES), BF16),
                        pltpu.VMEM((H_C, tile, tile), F32),
                        pltpu.VMEM((H_IDX * D_IDX, tile), BF16),
                        pltpu.VMEM((s, tile), I32),
                        pltpu.VMEM((s, tile), I16),
                        pltpu.VMEM((s, tile), I16),
                        pltpu.VMEM((1, tile), I32),
                        pltpu.VMEM((H_C, 1, tile), F32),
                        pltpu.VMEM((H_C, 1, tile), F32),
                        pltpu.VMEM((H_C, DH_C, tile), F32)] + _stage_scratch(tile),
        compiler_params=_params("arbitrary", "arbitrary"),
    )(proj, proj, vt, proj, kiw, kiw)


def _xattn_body(q_ref, k_ref, v_ref, o_ref, *, dh):
    outs = []
    for h in range(H_X):
        c = h * dh
        q = q_ref[0, :, c:c + dh] * dh ** -0.5
        s = _dot_nt(q, k_ref[0, :, c:c + dh])
        p = jnp.exp(s - jnp.max(s, axis=-1, keepdims=True))
        l = jnp.sum(p, axis=-1, keepdims=True)
        outs.append((_dot(p.astype(BF16), v_ref[0, :, c:c + dh]) / l).astype(BF16))
    o_ref[0] = jnp.concatenate(outs, axis=-1)


def _xattn_core(q, k, v, tq=512):
    b, s, d = q.shape
    n_mem = k.shape[1]
    tq = min(tq, s)
    return pl.pallas_call(
        functools.partial(_xattn_body, dh=d // H_X),
        grid=(b, s // tq),
        in_specs=[pl.BlockSpec((1, tq, d), lambda bi, i: (bi, i, 0)),
                  pl.BlockSpec((1, n_mem, d), lambda bi, i: (bi, 0, 0)),
                  pl.BlockSpec((1, n_mem, d), lambda bi, i: (bi, 0, 0))],
        out_specs=pl.BlockSpec((1, tq, d), lambda bi, i: (bi, i, 0)),
        out_shape=jax.ShapeDtypeStruct((b, s, d), BF16),
        compiler_params=_params("parallel", "parallel"),
    )(q, k, v)


def _router_body(x_ref, r_ref, g_ref):
    logits = _dot(x_ref[...], r_ref[...])
    lane = lax.broadcasted_iota(I32, logits.shape, 1).astype(F32)
    big = float(LANES)
    lg = jnp.where(lane < N_EXPERTS, logits, -jnp.inf)
    m1 = jnp.max(lg, axis=-1, keepdims=True)
    i1 = jnp.min(jnp.where(lg == m1, lane, big), axis=-1, keepdims=True)
    lg2 = jnp.where(lane == i1, -jnp.inf, lg)
    m2 = jnp.max(lg2, axis=-1, keepdims=True)
    i2 = jnp.min(jnp.where(lg2 == m2, lane, big), axis=-1, keepdims=True)
    e = jnp.exp(m2 - m1)
    den = 1.0 + e
    g_ref[...] = jnp.where(lane == i1, 1.0 / den, 0.0) + jnp.where(lane == i2, e / den, 0.0)


def _router_gates(x, router, tm=1024):
    m, d = x.shape
    tm = min(tm, m)
    rpad = jnp.zeros((d, LANES), BF16).at[:, :N_EXPERTS].set(router.astype(BF16))
    return pl.pallas_call(
        _router_body,
        grid=(m // tm,),
        in_specs=[pl.BlockSpec((tm, d), lambda i: (i, 0)),
                  pl.BlockSpec((d, LANES), lambda i: (0, 0))],
        out_specs=pl.BlockSpec((tm, LANES), lambda i: (i, 0)),
        out_shape=jax.ShapeDtypeStruct((m, LANES), F32),
        compiler_params=_params("parallel"),
    )(x, rpad)


MOE_BLOCK = 1024
MOE_CHUNK = 128
MOE_GROUP = 4
MOE_FF_TILE = 1792


def _moe_body(x_ref, gate_ref, tri_ref, w1_ref, w3_ref, w2_ref, o_ref,
              xs_scr, acc_scr, rcol_scr, rrow_scr, cnt_smem):
    e = pl.program_id(1)
    f = pl.program_id(2)
    nf = pl.num_programs(2)
    tb = x_ref.shape[0]
    ch = MOE_CHUNK
    gw = MOE_GROUP * ch

    @pl.when((e == 0) & (f == 0))
    def _():
        o_ref[...] = jnp.zeros(o_ref.shape, F32)
        member = jnp.where(gate_ref[...] > 0.0, 1.0, 0.0)
        rank = _dot(tri_ref[...], member.astype(BF16))
        rcol = jnp.where(member > 0.0, rank, -1.0)
        rcol_scr[...] = rcol
        rrow_scr[...] = rcol.T
        counts = jnp.sum(member, axis=0, keepdims=True)
        for ee in range(N_EXPERTS):
            cnt_smem[ee] = counts[0, ee].astype(I32)

    nchunk = (cnt_smem[e] + (ch - 1)) // ch

    @pl.when(f == 0)
    def _():
        acc_scr[...] = jnp.zeros(acc_scr.shape, F32)
        sub = lax.broadcasted_iota(I32, (SUBLANES, tb), 0)
        r_row = jnp.sum(jnp.where(sub == e, rrow_scr[0:SUBLANES, :], 0.0), axis=0, keepdims=True)
        row_id = lax.broadcasted_iota(I32, (ch, tb), 0).astype(F32)

        def gather(c, carry):
            r0 = pl.multiple_of(c * ch, ch)
            sel = jnp.where(row_id + r0.astype(F32) == r_row, 1.0, 0.0).astype(BF16)
            xs_scr[pl.ds(r0, ch), :] = _dot(sel, x_ref[...]).astype(BF16)
            return carry

        lax.fori_loop(0, nchunk, gather, 0)

    def ffn(c, carry):
        r0 = pl.multiple_of(c * ch, ch)
        xs = xs_scr[pl.ds(r0, ch), :]
        a = _dot(xs, w1_ref[0])
        b = _dot(xs, w3_ref[0])
        acc_scr[pl.ds(r0, ch), :] += _dot((_silu(a) * b).astype(BF16), w2_ref[0])
        return carry

    lax.fori_loop(0, nchunk, ffn, 0)

    @pl.when(f == nf - 1)
    def _():
        pick = lax.broadcasted_iota(I32, (tb, LANES), 1) == e
        r_col = jnp.sum(jnp.where(pick, rcol_scr[...], 0.0), axis=1, keepdims=True)
        g_col = jnp.sum(jnp.where(pick, gate_ref[...], 0.0), axis=1, keepdims=True)
        col_id = lax.broadcasted_iota(I32, (tb, gw), 1).astype(F32)

        def combine(gi, carry):
            r0 = pl.multiple_of(gi * gw, gw)
            selt = jnp.where(col_id + r0.astype(F32) == r_col, 1.0, 0.0).astype(BF16)
            y = acc_scr[pl.ds(r0, gw), :]
            y_hi = y.astype(BF16)
            y_lo = (y - y_hi.astype(F32)).astype(BF16)
            o_ref[...] += g_col * (_dot(selt, y_hi) + _dot(selt, y_lo))
            return carry

        lax.fori_loop(0, (nchunk + (MOE_GROUP - 1)) // MOE_GROUP, combine, 0)


def _moe_experts(x, gates, w1, w3, w2):
    m, d = x.shape
    ne, _, ff = w1.shape
    tb = min(MOE_BLOCK, m)
    tf = MOE_FF_TILE if ff % MOE_FF_TILE == 0 else ff
    assert m % tb == 0 and tb % (MOE_GROUP * MOE_CHUNK) == 0 and ne <= SUBLANES
    tri = jnp.asarray(np.tril(np.ones((tb, tb), np.float32), -1), BF16)
    row = lambda i, e, f: (i, 0)
    return pl.pallas_call(
        _moe_body,
        grid=(m // tb, ne, ff // tf),
        in_specs=[pl.BlockSpec((tb, d), row),
                  pl.BlockSpec((tb, LANES), row),
                  pl.BlockSpec((tb, tb), lambda i, e, f: (0, 0)),
                  pl.BlockSpec((1, d, tf), lambda i, e, f: (e, 0, f)),
                  pl.BlockSpec((1, d, tf), lambda i, e, f: (e, 0, f)),
                  pl.BlockSpec((1, tf, d), lambda i, e, f: (e, f, 0))],
        out_specs=pl.BlockSpec((tb, d), row),
        out_shape=jax.ShapeDtypeStruct((m, d), F32),
        scratch_shapes=[pltpu.VMEM((tb, d), BF16),
                        pltpu.VMEM((tb, d), F32),
                        pltpu.VMEM((tb, LANES), F32),
                        pltpu.VMEM((LANES, tb), F32),
                        pltpu.SMEM((N_EXPERTS,), I32)],
        compiler_params=_params("parallel", "arbitrary", "arbitrary"),
    )(x, gates, tri, w1, w3, w2)


def _res_ln_body(res_ref, y_ref, g_ref, b_ref, of_ref, ob_ref, *, alpha):
    out = _layer_norm_rows(alpha * res_ref[...] + y_ref[...], g_ref[...], b_ref[...])
    of_ref[...] = out
    ob_ref[...] = out.astype(BF16)


def _residual_ln(res, y, g, b, alpha, tm=512):
    m, d = res.shape
    tm = min(tm, m)
    row = pl.BlockSpec((tm, d), lambda i: (i, 0))
    vec = pl.BlockSpec((1, d), lambda i: (0, 0))
    return pl.pallas_call(
        functools.partial(_res_ln_body, alpha=alpha),
        grid=(m // tm,),
        in_specs=[row, row, vec, vec],
        out_specs=[row, row],
        out_shape=[jax.ShapeDtypeStruct((m, d), F32), jax.ShapeDtypeStruct((m, d), BF16)],
        compiler_params=_params("parallel"),
    )(res, y, g.reshape(1, d), b.reshape(1, d))


def _band_bias(table):
    band = (LEFT_CHUNKS + 1) * CHUNK
    left = LEFT_CHUNKS * CHUNK
    rev = table.astype(F32)[:, ::-1]
    pad = left - REL_MAX + CHUNK - 1
    n_rev = band - 1 - (left - REL_MAX) + 1
    assert n_rev <= 2 * REL_MAX + 1
    ext = jnp.concatenate([jnp.broadcast_to(rev[:, :1], (rev.shape[0], pad)), rev[:, :n_rev]], axis=1)
    rows = [ext[:, CHUNK - 1 - q:CHUNK - 1 - q + band] for q in range(CHUNK)]
    return jnp.stack(rows, axis=1)


def kernel(x, mem, w_in, lam_q1, lam_k1, lam_q2, lam_k2, diff_norm_g, rel_bias, w_out,
           ln1_g, ln1_b, xattn_wq, xattn_wk, xattn_wv, xattn_wo, ln2_g, ln2_b,
           ffn_w1, ffn_w3, ffn_w2, moe_router, moe_w1, moe_w3, moe_w2, ln3_g, ln3_b):
    b, s, d = x.shape
    depth = w_in.shape[0]
    t = b * s
    alpha = (2.0 * depth) ** 0.25
    slopes_a, slopes_c = _alibi_slopes()
    n_main = 3 * H_A * DV_A + 3 * H_B * DH_B + 3 * H_C * DH_C + H_IDX * D_IDX
    n_tail = D_IDX + H_IDX
    wa, wb, wc = H_A * DV_A, H_B * DH_B, H_C * DH_C
    left = LEFT_CHUNKS * CHUNK

    h = x.reshape(t, d)
    hb = h.astype(BF16)
    memb = mem.reshape(-1, d).astype(BF16)
    for l in range(depth):
        col_scale = np.ones((n_main,), np.float32)
        col_scale[:wa] = DQK_A ** -0.5 * LOG2E
        col_scale[3 * wa + 3 * wb:3 * wa + 3 * wb + wc] = DH_C ** -0.5 * LOG2E
        w_main = (w_in[l, :, :n_main] * col_scale).astype(BF16)
        w_tail = jnp.zeros((d, LANES), BF16).at[:, :n_tail].set(w_in[l, :, n_main:].astype(BF16))
        proj = _matmul(hb, w_main, BF16).reshape(b, s, n_main)
        kiw = _matmul(hb, w_tail, F32, tn=LANES).reshape(b, s, LANES)
        c0 = 3 * wa + 3 * wb + 2 * wc
        w_vt = jnp.concatenate([w_in[l, :, 2 * wa:3 * wa], w_in[l, :, c0:c0 + wc]], axis=1).T.astype(BF16)
        tile = min(ATT_TILE, s)
        vt = _matmul_nt_tiles(w_vt, hb, tile).reshape(b, s // tile, wa + wc, tile)
        lam_init = 0.8 - 0.6 * math.exp(-0.3 * l)
        ya = _diff_attention(proj, vt, lam_q1[l], lam_k1[l], lam_q2[l], lam_k2[l], diff_norm_g[l],
                             slopes_a, lam_init)
        kb = proj[:, :, 3 * wa + wb:3 * wa + 2 * wb]
        vb = proj[:, :, 3 * wa + 2 * wb:3 * wa + 3 * wb]
        kpad = jnp.pad(kb, ((0, 0), (left, 0), (0, 0)))
        vpad = jnp.pad(vb, ((0, 0), (left, 0), (0, 0)))
        yb = _band_attention(proj, kpad, vpad, _band_bias(rel_bias[l]))
        yc = _dsa_attention(proj, vt, kiw, slopes_c)
        wo = w_out[l].astype(BF16)
        h, hb = _matmul_ln(
            [ya.reshape(t, wa), yb.reshape(t, wb), yc.reshape(t, -1)],
            [wo[:wa], wo[wa:wa + wb], wo[wa + wb:]],
            h, ln1_g[l], ln1_b[l], alpha)
        q = _matmul(hb, xattn_wq[l].astype(BF16), BF16).reshape(b, s, d)
        km = _matmul(memb, xattn_wk[l].astype(BF16), BF16).reshape(b, -1, d)
        vm = _matmul(memb, xattn_wv[l].astype(BF16), BF16).reshape(b, -1, d)
        xo = _xattn_core(q, km, vm).reshape(t, d)
        h, hb = _matmul_ln([xo], [xattn_wo[l].astype(BF16)], h, ln2_g[l], ln2_b[l], alpha)
        if l % 2 == 0:
            hid = _swiglu_hidden(hb, ffn_w1[l // 2].astype(BF16), ffn_w3[l // 2].astype(BF16))
            h, hb = _matmul_ln([hid], [ffn_w2[l // 2].astype(BF16)], h, ln3_g[l], ln3_b[l], alpha)
        else:
            gates = _router_gates(hb, moe_router[l // 2])
            y = _moe_experts(hb, gates, moe_w1[l // 2].astype(BF16), moe_w3[l // 2].astype(BF16),
                             moe_w2[l // 2].astype(BF16))
            h, hb = _residual_ln(h, y, ln3_g[l], ln3_b[l], alpha)
    return h.reshape(b, s, d)
```

```python
import functools
import math

import numpy as np
import jax
import jax.numpy as jnp
from jax import lax
from jax.experimental import pallas as pl
from jax.experimental.pallas import tpu as pltpu

F32 = jnp.float32
BF16 = jnp.bfloat16
I32 = jnp.int32
I16 = jnp.int16

CHUNK = 64
H_A, DQK_A = 4, 64
DV_A = 2 * DQK_A
H_B, DH_B = 4, 64
LEFT_CHUNKS = 8
REL_MAX = 128
H_C, DH_C = 4, 64
H_IDX, D_IDX = 8, 64
TOPK_MAX = 256
H_X = 4
N_EXPERTS = 8
LN_EPS = 1e-5
RMS_EPS = 1e-6
NEG_INF = -1e30
MASKED = -2e30

LANES = 128
SUBLANES = 8
VMEM_LIMIT = 56 * 1024 * 1024
ATT_TILE = 256

_NT = (((1,), (1,)), ((), ()))


def _alibi_slopes():
    n = H_A + H_C
    s = (2.0 ** (-8.0 * np.arange(1, n + 1) / n)).astype(np.float32)
    return [float(v) for v in s[0::2]], [float(v) for v in s[1::2]]


def _params(*sem):
    return pltpu.CompilerParams(dimension_semantics=sem, vmem_limit_bytes=VMEM_LIMIT)


def _dot(a, b):
    return jnp.dot(a, b, preferred_element_type=F32)


SCORE_ROW_CHUNK = 128


def _dot_row_chunks(a, b):
    rows = a.shape[0]
    step = min(SCORE_ROW_CHUNK, rows)
    return jnp.concatenate([_dot(a[r:r + step], b) for r in range(0, rows, step)], axis=0)


def _dot_nt(a, b):
    return lax.dot_general(a, b, _NT, preferred_element_type=F32)


def _layer_norm_rows(y, g, b):
    mu = jnp.mean(y, axis=-1, keepdims=True)
    d = y - mu
    var = jnp.mean(d * d, axis=-1, keepdims=True)
    return d * lax.rsqrt(var + LN_EPS) * g + b


def _silu(a):
    return a / (1.0 + jnp.exp(-a))


def _mm_body(x_ref, w_ref, o_ref):
    o_ref[...] = _dot(x_ref[...], w_ref[...]).astype(o_ref.dtype)


def _matmul(x, w, out_dtype, tm=1024, tn=512):
    m, k = x.shape
    n = w.shape[1]
    tm, tn = min(tm, m), min(tn, n)
    return pl.pallas_call(
        _mm_body,
        grid=(m // tm, n // tn),
        in_specs=[pl.BlockSpec((tm, k), lambda i, j: (i, 0)),
                  pl.BlockSpec((k, tn), lambda i, j: (0, j))],
        out_specs=pl.BlockSpec((tm, tn), lambda i, j: (i, j)),
        out_shape=jax.ShapeDtypeStruct((m, n), out_dtype),
        compiler_params=_params("parallel", "parallel"),
    )(x, w)


def _mm_t_body(x_ref, w_ref, o_ref, *, tile):
    y = _dot(x_ref[...], w_ref[...])
    for r in range(o_ref.shape[0]):
        o_ref[r] = y[r * tile:(r + 1) * tile, :].T.astype(o_ref.dtype)


def _matmul_t_tiles(x, w, tile, tm=1024):
    m, k = x.shape
    n = w.shape[1]
    tm = min(tm, m)
    return pl.pallas_call(
        functools.partial(_mm_t_body, tile=tile),
        grid=(m // tm,),
        in_specs=[pl.BlockSpec((tm, k), lambda i: (i, 0)),
                  pl.BlockSpec((k, n), lambda i: (0, 0))],
        out_specs=pl.BlockSpec((tm // tile, n, tile), lambda i: (i, 0, 0)),
        out_shape=jax.ShapeDtypeStruct((m // tile, n, tile), BF16),
        compiler_params=_params("parallel"),
    )(x, w)


def _mm_ln_body(*refs, n_in, alpha):
    xs, ws = refs[:n_in], refs[n_in:2 * n_in]
    res_ref, g_ref, b_ref, of_ref, ob_ref = refs[2 * n_in:]
    acc = _dot(xs[0][...], ws[0][...])
    for x_ref, w_ref in zip(xs[1:], ws[1:]):
        acc = acc + _dot(x_ref[...], w_ref[...])
    out = _layer_norm_rows(alpha * res_ref[...] + acc, g_ref[...], b_ref[...])
    of_ref[...] = out
    ob_ref[...] = out.astype(BF16)


def _matmul_ln(xs, ws, res, g, b, alpha):
    m, d = res.shape
    k_total = sum(x.shape[1] for x in xs)
    tm = min(1024 if k_total <= 1024 else 512, m)
    n_in = len(xs)
    in_specs = ([pl.BlockSpec((tm, x.shape[1]), lambda i: (i, 0)) for x in xs]
                + [pl.BlockSpec(w.shape, lambda i: (0, 0)) for w in ws]
                + [pl.BlockSpec((tm, d), lambda i: (i, 0)),
                   pl.BlockSpec((1, d), lambda i: (0, 0)),
                   pl.BlockSpec((1, d), lambda i: (0, 0))])
    return pl.pallas_call(
        functools.partial(_mm_ln_body, n_in=n_in, alpha=alpha),
        grid=(m // tm,),
        in_specs=in_specs,
        out_specs=[pl.BlockSpec((tm, d), lambda i: (i, 0)),
                   pl.BlockSpec((tm, d), lambda i: (i, 0))],
        out_shape=[jax.ShapeDtypeStruct((m, d), F32), jax.ShapeDtypeStruct((m, d), BF16)],
        compiler_params=_params("parallel"),
    )(*xs, *ws, res, g.reshape(1, d), b.reshape(1, d))


def _swiglu_body(x_ref, w1_ref, w3_ref, o_ref):
    x = x_ref[...]
    a = _dot(x, w1_ref[...])
    b = _dot(x, w3_ref[...])
    o_ref[...] = (_silu(a) * b).astype(o_ref.dtype)


def _swiglu_hidden(x, w1, w3, tm=512, tn=1408):
    m, k = x.shape
    n = w1.shape[1]
    tm = min(tm, m)
    if n % tn:
        tn = n
    return pl.pallas_call(
        _swiglu_body,
        grid=(m // tm, n // tn),
        in_specs=[pl.BlockSpec((tm, k), lambda i, j: (i, 0)),
                  pl.BlockSpec((k, tn), lambda i, j: (0, j)),
                  pl.BlockSpec((k, tn), lambda i, j: (0, j))],
        out_specs=pl.BlockSpec((tm, tn), lambda i, j: (i, j)),
        out_shape=jax.ShapeDtypeStruct((m, n), BF16),
        compiler_params=_params("parallel", "parallel"),
    )(x, w1, w3)


def _key_query_iotas(tile):
    kpos = lax.broadcasted_iota(I32, (tile, tile), 0)
    qpos = lax.broadcasted_iota(I32, (tile, tile), 1)
    return kpos, qpos


def _scaled_transpose(x, scale):
    return (x.astype(F32) * scale).T.astype(BF16)


LOG2E = math.log2(math.e)
ALIBI_PARTS = 3
ALIBI_ROWS = 16


def _bf16_parts(x):
    parts, r = [], np.float32(x)
    for _ in range(ALIBI_PARTS):
        p = np.float32(r.astype(jnp.bfloat16))
        parts.append(float(p))
        r = np.float32(r - p)
    return parts


def _alibi_key_cols(tile, parts):
    lane = lax.broadcasted_iota(I32, (tile, LANES), 1)
    kr = lax.broadcasted_iota(I32, (tile, LANES), 0).astype(F32)
    v = jnp.where((lane >= ALIBI_PARTS) & (lane < 2 * ALIBI_PARTS), kr, 0.0)
    for n, c in enumerate(parts):
        v = jnp.where(lane == n, c, v)
    return v.astype(BF16)


def _alibi_query_rows(tile, parts):
    row = lax.broadcasted_iota(I32, (ALIBI_ROWS, tile), 0)
    qr = lax.broadcasted_iota(I32, (ALIBI_ROWS, tile), 1).astype(F32)
    v = jnp.where(row < ALIBI_PARTS, -qr, 0.0)
    for n, c in enumerate(parts):
        v = jnp.where(row == ALIBI_PARTS + n, c, v)
    return v.astype(BF16)


def _init_alibi(slopes, dneg_scr, kx_scr, qa_scr, tile, maps_per_head):
    kpos, qpos = _key_query_iotas(tile)
    rel = (qpos - kpos).astype(F32)
    qa_scr[...] = jnp.zeros(qa_scr.shape, BF16)
    for h, slope in enumerate(slopes):
        parts = _bf16_parts(slope * LOG2E)
        dneg_scr[h] = 2.0 * jnp.minimum(np.float32(slope * LOG2E) * rel, 0.0)
        kx_scr[h] = _alibi_key_cols(tile, parts)
        for mm in range(maps_per_head):
            qa_scr[h * maps_per_head + mm, LANES:LANES + ALIBI_ROWS, :] = _alibi_query_rows(tile, parts)


def _store_query_maps(qa_scr, qt, n_maps, dh):
    for idx in range(n_maps):
        r0 = (idx % 2) * dh
        qa_scr[idx, r0:r0 + dh, :] = qt[idx * dh:(idx + 1) * dh, :]


def _softmax_probs(c, t_ref, p_ref, a_ref, m_ref, l_ref):
    m_old = m_ref[...]
    m_new = jnp.maximum(m_old, jnp.max(t_ref[...], axis=0, keepdims=True) - c)
    p = jnp.exp2(t_ref[...] - (m_new + c))
    a = jnp.exp2(m_old - m_new)
    l_ref[...] = a * l_ref[...] + jnp.sum(p, axis=0, keepdims=True)
    m_ref[...] = m_new
    a_ref[...] = a
    p_ref[...] = p.astype(BF16)


def _accumulate(vt, p_ref, a_ref, acc_ref):
    acc_ref[...] = a_ref[...] * acc_ref[...] + _dot(vt, p_ref[...])


N_T_SLOTS, N_P_SLOTS = 3, 2


def _stage_scratch(tile):
    return ([pltpu.VMEM((tile, tile), F32)] * N_T_SLOTS + [pltpu.VMEM((tile, tile), BF16)] * N_P_SLOTS
            + [pltpu.VMEM((1, tile), F32)] * N_P_SLOTS)


def _pipelined_maps(n_maps, scores, probs, accumulate, stage):
    t_slots = stage[:N_T_SLOTS]
    p_slots = stage[N_T_SLOTS:N_T_SLOTS + N_P_SLOTS]
    a_slots = stage[N_T_SLOTS + N_P_SLOTS:]
    nt, npb = N_T_SLOTS, N_P_SLOTS

    def stage_probs(k):
        probs(k, t_slots[k % nt], p_slots[k % npb], a_slots[k % npb])

    t_slots[0][...] = scores(0)
    if n_maps > 1:
        t_slots[1][...] = scores(1)
    stage_probs(0)
    for k in range(n_maps):
        if k + 2 < n_maps:
            t_slots[(k + 2) % nt][...] = scores(k + 2)
        if k + 1 < n_maps:
            stage_probs(k + 1)
        accumulate(k, p_slots[k % npb], a_slots[k % npb])


def _diff_body(q_ref, k_ref, vt_ref, lq1_ref, lk1_ref, lq2_ref, lk2_ref, gain_ref, o_ref,
               qa_scr, kx_scr, dneg_scr, m_scr, l_scr, acc_scr, *stage,
               tile, slopes, lam_init):
    i = pl.program_id(1)
    kpos, qpos = _key_query_iotas(tile)

    @pl.when((pl.program_id(0) == 0) & (i == 0))
    def _():
        _init_alibi(slopes, dneg_scr, kx_scr, qa_scr, tile, 2)

    _store_query_maps(qa_scr, _scaled_transpose(q_ref[0], 1.0), 2 * H_A, DQK_A)
    m_scr[...] = jnp.full(m_scr.shape, NEG_INF, F32)
    l_scr[...] = jnp.zeros(l_scr.shape, F32)
    acc_scr[...] = jnp.zeros(acc_scr.shape, F32)

    def block(j, diag):
        k0 = pl.multiple_of(j * tile, tile)
        kb = k_ref[0, pl.ds(k0, tile), :]
        vtb = vt_ref[0, j]
        off = ((i - j) * tile).astype(F32)

        def scores(idx):
            h, pair = idx // 2, idx // 2
            keys = jnp.concatenate([kb[:, pair * LANES:(pair + 1) * LANES], kx_scr[h]], axis=1)
            s = _dot_row_chunks(keys, qa_scr[idx])
            if diag:
                allowed = (kpos // CHUNK) <= (qpos // CHUNK)
                return jnp.where(allowed, s + dneg_scr[h], MASKED)
            return s

        def probs(idx, t_ref, p_ref, a_ref):
            shift = 0.0 if diag else (slopes[idx // 2] * LOG2E) * off
            _softmax_probs(shift, t_ref, p_ref, a_ref, m_scr.at[idx], l_scr.at[idx])

        def accumulate(idx, p_ref, a_ref):
            h = idx // 2
            _accumulate(vtb[h * DV_A:(h + 1) * DV_A, :], p_ref, a_ref, acc_scr.at[idx])

        _pipelined_maps(2 * H_A, scores, probs, accumulate, stage)

    def full_block(j, carry):
        block(j, False)
        return carry

    lax.fori_loop(0, i, full_block, 0)
    block(i, True)

    lam = (jnp.exp(jnp.sum(lq1_ref[...] * lk1_ref[...], axis=-1, keepdims=True))
           - jnp.exp(jnp.sum(lq2_ref[...] * lk2_ref[...], axis=-1, keepdims=True)) + lam_init)
    gain = gain_ref[...] * (1.0 - lam_init)
    outs = []
    for h in range(H_A):
        o = acc_scr[2 * h] / l_scr[2 * h] - lam * (acc_scr[2 * h + 1] / l_scr[2 * h + 1])
        o = o * lax.rsqrt(jnp.mean(o * o, axis=0, keepdims=True) + RMS_EPS)
        outs.append((o * gain).T)
    o_ref[0] = jnp.concatenate(outs, axis=-1).astype(BF16)


def _diff_attention(proj, vt, lq1, lk1, lq2, lk2, gain, slopes, lam_init):
    b, s, _ = proj.shape
    tile = min(ATT_TILE, s)
    nkb = s // tile
    w = H_A * DV_A
    vec = lambda a: a.reshape(1, -1).astype(F32)
    small = lambda n: pl.BlockSpec((1, n), lambda bi, i: (0, 0))
    return pl.pallas_call(
        functools.partial(_diff_body, tile=tile, slopes=slopes, lam_init=lam_init),
        grid=(b, nkb),
        in_specs=[pl.BlockSpec((1, tile, w), lambda bi, i: (bi, i, 0)),
                  pl.BlockSpec((1, s, w), lambda bi, i: (bi, 0, 1)),
                  pl.BlockSpec((1, nkb, w, tile), lambda bi, i: (bi, 0, 0, 0)),
                  small(DQK_A), small(DQK_A), small(DQK_A), small(DQK_A),
                  pl.BlockSpec((DV_A, 1), lambda bi, i: (0, 0))],
        out_specs=pl.BlockSpec((1, tile, w), lambda bi, i: (bi, i, 0)),
        out_shape=jax.ShapeDtypeStruct((b, s, w), BF16),
        scratch_shapes=[pltpu.VMEM((2 * H_A, 2 * LANES, tile), BF16),
                        pltpu.VMEM((H_A, tile, LANES), BF16),
                        pltpu.VMEM((H_A, tile, tile), F32),
                        pltpu.VMEM((2 * H_A, 1, tile), F32),
                        pltpu.VMEM((2 * H_A, 1, tile), F32),
                        pltpu.VMEM((2 * H_A, DV_A, tile), F32)] + _stage_scratch(tile),
        compiler_params=_params("arbitrary", "arbitrary"),
    )(proj, proj, vt, vec(lq1), vec(lk1), vec(lq2), vec(lk2), gain.reshape(-1, 1).astype(F32))


BAND_CHUNKS = LANES // CHUNK
BAND_WINDOW = (LEFT_CHUNKS + BAND_CHUNKS) * CHUNK


def _band_body(q_ref, k_ref, vt_ref, bias_ref, o_ref, qz_scr):
    i = pl.program_id(1)
    ntile = BAND_WINDOW // LANES

    @pl.when((pl.program_id(0) == 0) & (i == 0))
    def _():
        qz_scr[...] = jnp.zeros(qz_scr.shape, BF16)

    qt = _scaled_transpose(q_ref[0], 1.0)
    for h in range(H_B):
        r0 = (h % 2) * DH_B
        qz_scr[h, r0:r0 + DH_B, :] = qt[h * DH_B:(h + 1) * DH_B, :]

    start = pl.multiple_of(i * LANES, LANES)
    kw = k_ref[0, pl.ds(start, BAND_WINDOW), :]
    vt = vt_ref[0, pl.ds(i, ntile)]
    kk = lax.broadcasted_iota(I32, (BAND_WINDOW, LANES), 0)
    valid = kk >= LEFT_CHUNKS * CHUNK - start
    outs = []
    for h in range(H_B):
        pair = h // 2
        s = _dot_row_chunks(kw[:, pair * LANES:(pair + 1) * LANES], qz_scr[h])
        t = jnp.where(valid, s + bias_ref[h], MASKED)
        p = jnp.exp2(t - jnp.max(t, axis=0, keepdims=True))
        l = jnp.sum(p, axis=0, keepdims=True)
        vth = jnp.concatenate([vt[n, h * DH_B:(h + 1) * DH_B, :] for n in range(ntile)], axis=1)
        outs.append(_dot(vth, p.astype(BF16)) / l)
    o_ref[0] = jnp.concatenate(outs, axis=0).T.astype(BF16)


def _band_bias_table(table):
    bias = _band_bias(table).transpose(0, 2, 1) * LOG2E
    cols = [jnp.pad(bias, ((0, 0), (c * CHUNK, (BAND_CHUNKS - 1 - c) * CHUNK), (0, 0)), constant_values=MASKED)
            for c in range(BAND_CHUNKS)]
    return jnp.concatenate(cols, axis=2)


def _band_attention(proj, kpad, vt_pad, bias):
    b, s, _ = proj.shape
    w = H_B * DH_B
    sp = kpad.shape[1]
    nvt = vt_pad.shape[1]
    return pl.pallas_call(
        _band_body,
        grid=(b, s // LANES),
        in_specs=[pl.BlockSpec((1, LANES, w), lambda bi, i: (bi, i, 6)),
                  pl.BlockSpec((1, sp, w), lambda bi, i: (bi, 0, 0)),
                  pl.BlockSpec((1, nvt, w, LANES), lambda bi, i: (bi, 0, 0, 0)),
                  pl.BlockSpec((H_B, BAND_WINDOW, LANES), lambda bi, i: (0, 0, 0))],
        out_specs=pl.BlockSpec((1, LANES, w), lambda bi, i: (bi, i, 0)),
        out_shape=jax.ShapeDtypeStruct((b, s, w), BF16),
        scratch_shapes=[pltpu.VMEM((H_B, LANES, LANES), BF16)],
        compiler_params=_params("arbitrary", "arbitrary"),
    )(proj, kpad, vt_pad, bias)


def _sortable(x):
    bits = pltpu.bitcast(x + 0.0, I32)
    return bits ^ ((bits >> 31) & 0x7FFFFFFF)


_INT_MIN = -2 ** 31
_VALID_KEY = int(np.array(0.5 * NEG_INF, np.float32).view(np.int32))
_VALID_KEY = _VALID_KEY ^ ((_VALID_KEY >> 31) & 0x7FFFFFFF)


def _dsa_body(qc_ref, kc_ref, vct_ref, qi_ref, kiw_ref, wq_ref, o_ref,
              qa_scr, kx_scr, dneg_scr, qit_scr, key_scr, hi_scr, lo_scr, jstar_scr,
              m_scr, l_scr, acc_scr, *stage,
              tile, topk, slopes, seq):
    i = pl.program_id(1)
    nblk = i + 1
    kpos, qpos = _key_query_iotas(tile)

    @pl.when((pl.program_id(0) == 0) & (i == 0))
    def _():
        _init_alibi(slopes, dneg_scr, kx_scr, qa_scr, tile, 1)

    qit_scr[...] = _scaled_transpose(qi_ref[0], D_IDX ** -0.5)
    wt = wq_ref[0].T
    wrow = [wt[D_IDX + h:D_IDX + h + 1, :] * H_IDX ** -0.5 for h in range(H_IDX)]

    def score_block(j, diag):
        k0 = pl.multiple_of(j * tile, tile)
        kib = kiw_ref[0, pl.ds(k0, tile), :][:, :D_IDX].astype(BF16)
        isc = jnp.zeros((tile, tile), F32)
        for h in range(H_IDX):
            d = _dot(kib, qit_scr[h * D_IDX:(h + 1) * D_IDX, :])
            isc = isc + jnp.maximum(d, 0.0) * wrow[h]
        if diag:
            isc = jnp.where((kpos // CHUNK) <= (qpos // CHUNK), isc, NEG_INF)
        key = _sortable(isc)
        key_scr[pl.ds(k0, tile), :] = key
        hi_scr[pl.ds(k0, tile), :] = (key >> 16).astype(I16)
        lo_scr[pl.ds(k0, tile), :] = ((key & 0xFFFF) - 2 ** 15).astype(I16)

    def score_full(j, carry):
        score_block(j, False)
        return carry

    lax.fori_loop(0, i, score_full, 0)
    score_block(i, True)

    def count(pred):
        def body(j, part):
            k0 = pl.multiple_of(j * tile, tile)
            hit = jnp.where(pred(key_scr[pl.ds(k0, tile), :], kpos + k0), 1.0, 0.0)
            return part + jnp.sum(hit.reshape(tile // SUBLANES, SUBLANES, tile), axis=0)
        part = lax.fori_loop(0, nblk, body, jnp.zeros((SUBLANES, tile), F32))
        return jnp.sum(part, axis=0, keepdims=True)

    kf = float(topk)
    rows16 = 2 * SUBLANES

    def count16(src_ref, pred):
        def body(j, part):
            k0 = pl.multiple_of(j * tile, tile)
            hit = jnp.where(pred(src_ref[pl.ds(k0, tile), :]), jnp.int16(1), jnp.int16(0))
            for g in range(tile // rows16):
                part = part + hit[g * rows16:(g + 1) * rows16, :]
            return part
        part = lax.fori_loop(0, nblk, body, jnp.zeros((rows16, tile), I16))
        return jnp.sum(part.astype(I32), axis=0, keepdims=True).astype(F32)

    def search16(src_ref, base):
        def bit(p, u):
            cand_u = u | jnp.left_shift(jnp.int32(1), 15 - p)
            cand = (cand_u - 2 ** 15).astype(I16)
            cnt = base + count16(src_ref, lambda v: v >= cand)
            return jnp.where(cnt >= kf, cand_u, u)
        return lax.fori_loop(0, 16, bit, jnp.zeros((1, tile), I32))

    thr_hi = search16(hi_scr, 0.0) - 2 ** 15
    thr_hi16 = thr_hi.astype(I16)
    above = count16(hi_scr, lambda v: v > thr_hi16)

    def mask_low(j, carry):
        k0 = pl.multiple_of(j * tile, tile)
        rows = pl.ds(k0, tile)
        lo_scr[rows, :] = jnp.where(hi_scr[rows, :] == thr_hi16, lo_scr[rows, :], jnp.int16(-2 ** 15))
        return carry

    lax.fori_loop(0, nblk, mask_low, 0)
    thr = jnp.left_shift(thr_hi, 16) + search16(lo_scr, above)

    thr_b = thr
    c_gt = count(lambda key, idx: key > thr_b)
    c_ge = count(lambda key, idx: key >= thr_b)
    tied = (c_ge > kf) & (thr > _VALID_KEY)
    need = kf - c_gt
    jstar_scr[...] = jnp.full((1, tile), seq, I32)

    @pl.when(jnp.max(jnp.where(tied, 1.0, 0.0)) > 0.0)
    def _():
        nbits = max(1, int(math.ceil(math.log2(seq))))

        def index_bit(p, lo):
            cand = lo | jnp.left_shift(jnp.int32(1), nbits - 1 - p)
            cnt = count(lambda key, idx: (key == thr_b) & (idx < cand))
            return jnp.where(cnt < need, cand, lo)

        lo = lax.fori_loop(0, nbits, index_bit, jnp.zeros((1, tile), I32))
        jstar_scr[...] = jnp.where(tied, lo, seq)

    jstar = jstar_scr[...]
    thr2 = jnp.maximum(thr, _VALID_KEY + 1)

    _store_query_maps(qa_scr, _scaled_transpose(qc_ref[0], 1.0), H_C, DH_C)
    m_scr[...] = jnp.full(m_scr.shape, NEG_INF, F32)
    l_scr[...] = jnp.zeros(l_scr.shape, F32)
    acc_scr[...] = jnp.zeros(acc_scr.shape, F32)

    def attend(j, diag):
        k0 = pl.multiple_of(j * tile, tile)
        key = key_scr[pl.ds(k0, tile), :]
        sel = (key > thr2) | ((key == thr2) & (kpos + k0 <= jstar))
        kb = kc_ref[0, pl.ds(k0, tile), :]
        vtb = vct_ref[0, j]
        off = ((i - j) * tile).astype(F32)

        def scores(h):
            pair = h // 2
            keys = jnp.concatenate([kb[:, pair * LANES:(pair + 1) * LANES], kx_scr[h]], axis=1)
            s = _dot_row_chunks(keys, qa_scr[h])
            if diag:
                s = s + dneg_scr[h]
            return jnp.where(sel, s, MASKED)

        def probs(h, t_ref, p_ref, a_ref):
            shift = 0.0 if diag else (slopes[h] * LOG2E) * off
            _softmax_probs(shift, t_ref, p_ref, a_ref, m_scr.at[h], l_scr.at[h])

        def accumulate(h, p_ref, a_ref):
            _accumulate(vtb[h * DH_C:(h + 1) * DH_C, :], p_ref, a_ref, acc_scr.at[h])

        _pipelined_maps(H_C, scores, probs, accumulate, stage)

    def attend_full(j, carry):
        attend(j, False)
        return carry

    lax.fori_loop(0, i, attend_full, 0)
    attend(i, True)
    outs = [(acc_scr[h] / l_scr[h]).T for h in range(H_C)]
    o_ref[0] = jnp.concatenate(outs, axis=-1).astype(BF16)


def _dsa_attention(proj, vt, kiw, slopes):
    b, s, _ = proj.shape
    tile = min(ATT_TILE, s)
    nkb = s // tile
    w = H_C * DH_C
    wa = H_A * DV_A
    topk = min(TOPK_MAX, s // 4)
    return pl.pallas_call(
        functools.partial(_dsa_body, tile=tile, topk=topk, slopes=slopes, seq=s),
        grid=(b, nkb),
        in_specs=[pl.BlockSpec((1, tile, w), lambda bi, i: (bi, i, 8)),
                  pl.BlockSpec((1, s, w), lambda bi, i: (bi, 0, 9)),
                  pl.BlockSpec((1, nkb, w, tile), lambda bi, i: (bi, 0, wa // w, 0)),
                  pl.BlockSpec((1, tile, H_IDX * D_IDX), lambda bi, i: (bi, i, 2)),
                  pl.BlockSpec((1, s, LANES), lambda bi, i: (bi, 0, 0)),
                  pl.BlockSpec((1, tile, LANES), lambda bi, i: (bi, i, 0))],
        out_specs=pl.BlockSpec((1, tile, w), lambda bi, i: (bi, i, 0)),
        out_shape=jax.ShapeDtypeStruct((b, s, w), BF16),
        scratch_shapes=[pltpu.VMEM((H_C, 2 * LANES, tile), BF16),
                        pltpu.VMEM((H_C, tile, LANES), BF16),
                        pltpu.VMEM((H_C, tile, tile), F32),
                        pltpu.VMEM((H_IDX * D_IDX, tile), BF16),
                        pltpu.VMEM((s, tile), I32),
                        pltpu.VMEM((s, tile), I16),
                        pltpu.VMEM((s, tile), I16),
                        pltpu.VMEM((1, tile), I32),
                        pltpu.VMEM((H_C, 1, tile), F32),
                        pltpu.VMEM((H_C, 1, tile), F32),
                        pltpu.VMEM((H_C, DH_C, tile), F32)] + _stage_scratch(tile),
        compiler_params=_params("arbitrary", "arbitrary"),
    )(proj, proj, vt, proj, kiw, kiw)


def _xattn_body(q_ref, k_ref, v_ref, o_ref, *, dh):
    outs = []
    for h in range(H_X):
        c = h * dh
        q = q_ref[0, :, c:c + dh] * dh ** -0.5
        s = _dot_nt(q, k_ref[0, :, c:c + dh])
        p = jnp.exp(s - jnp.max(s, axis=-1, keepdims=True))
        l = jnp.sum(p, axis=-1, keepdims=True)
        outs.append((_dot(p.astype(BF16), v_ref[0, :, c:c + dh]) / l).astype(BF16))
    o_ref[0] = jnp.concatenate(outs, axis=-1)


def _xattn_core(q, k, v, tq=512):
    b, s, d = q.shape
    n_mem = k.shape[1]
    tq = min(tq, s)
    return pl.pallas_call(
        functools.partial(_xattn_body, dh=d // H_X),
        grid=(b, s // tq),
        in_specs=[pl.BlockSpec((1, tq, d), lambda bi, i: (bi, i, 0)),
                  pl.BlockSpec((1, n_mem, d), lambda bi, i: (bi, 0, 0)),
                  pl.BlockSpec((1, n_mem, d), lambda bi, i: (bi, 0, 0))],
        out_specs=pl.BlockSpec((1, tq, d), lambda bi, i: (bi, i, 0)),
        out_shape=jax.ShapeDtypeStruct((b, s, d), BF16),
        compiler_params=_params("parallel", "parallel"),
    )(q, k, v)


def _router_body(x_ref, r_ref, g_ref):
    logits = _dot(x_ref[...], r_ref[...])
    lane = lax.broadcasted_iota(I32, logits.shape, 1).astype(F32)
    big = float(LANES)
    lg = jnp.where(lane < N_EXPERTS, logits, -jnp.inf)
    m1 = jnp.max(lg, axis=-1, keepdims=True)
    i1 = jnp.min(jnp.where(lg == m1, lane, big), axis=-1, keepdims=True)
    lg2 = jnp.where(lane == i1, -jnp.inf, lg)
    m2 = jnp.max(lg2, axis=-1, keepdims=True)
    i2 = jnp.min(jnp.where(lg2 == m2, lane, big), axis=-1, keepdims=True)
    e = jnp.exp(m2 - m1)
    den = 1.0 + e
    g_ref[...] = jnp.where(lane == i1, 1.0 / den, 0.0) + jnp.where(lane == i2, e / den, 0.0)


def _router_gates(x, router, tm=1024):
    m, d = x.shape
    tm = min(tm, m)
    rpad = jnp.zeros((d, LANES), BF16).at[:, :N_EXPERTS].set(router.astype(BF16))
    return pl.pallas_call(
        _router_body,
        grid=(m // tm,),
        in_specs=[pl.BlockSpec((tm, d), lambda i: (i, 0)),
                  pl.BlockSpec((d, LANES), lambda i: (0, 0))],
        out_specs=pl.BlockSpec((tm, LANES), lambda i: (i, 0)),
        out_shape=jax.ShapeDtypeStruct((m, LANES), F32),
        compiler_params=_params("parallel"),
    )(x, rpad)


MOE_BLOCK = 1024
MOE_CHUNK = 128
MOE_GROUP = 4
MOE_FF_TILE = 1792


def _moe_body(x_ref, gate_ref, tri_ref, w1_ref, w3_ref, w2_ref, o_ref,
              xs_scr, acc_scr, rcol_scr, rrow_scr, cnt_smem):
    e = pl.program_id(1)
    f = pl.program_id(2)
    nf = pl.num_programs(2)
    tb = x_ref.shape[0]
    ch = MOE_CHUNK
    gw = MOE_GROUP * ch

    @pl.when((e == 0) & (f == 0))
    def _():
        o_ref[...] = jnp.zeros(o_ref.shape, F32)
        member = jnp.where(gate_ref[...] > 0.0, 1.0, 0.0)
        rank = _dot(tri_ref[...], member.astype(BF16))
        rcol = jnp.where(member > 0.0, rank, -1.0)
        rcol_scr[...] = rcol
        rrow_scr[...] = rcol.T
        counts = jnp.sum(member, axis=0, keepdims=True)
        for ee in range(N_EXPERTS):
            cnt_smem[ee] = counts[0, ee].astype(I32)

    nchunk = (cnt_smem[e] + (ch - 1)) // ch

    @pl.when(f == 0)
    def _():
        acc_scr[...] = jnp.zeros(acc_scr.shape, F32)
        sub = lax.broadcasted_iota(I32, (SUBLANES, tb), 0)
        r_row = jnp.sum(jnp.where(sub == e, rrow_scr[0:SUBLANES, :], 0.0), axis=0, keepdims=True)
        row_id = lax.broadcasted_iota(I32, (ch, tb), 0).astype(F32)

        def gather(c, carry):
            r0 = pl.multiple_of(c * ch, ch)
            sel = jnp.where(row_id + r0.astype(F32) == r_row, 1.0, 0.0).astype(BF16)
            xs_scr[pl.ds(r0, ch), :] = _dot(sel, x_ref[...]).astype(BF16)
            return carry

        lax.fori_loop(0, nchunk, gather, 0)

    def ffn(c, carry):
        r0 = pl.multiple_of(c * ch, ch)
        xs = xs_scr[pl.ds(r0, ch), :]
        a = _dot(xs, w1_ref[0])
        b = _dot(xs, w3_ref[0])
        acc_scr[pl.ds(r0, ch), :] += _dot((_silu(a) * b).astype(BF16), w2_ref[0])
        return carry

    lax.fori_loop(0, nchunk, ffn, 0)

    @pl.when(f == nf - 1)
    def _():
        pick = lax.broadcasted_iota(I32, (tb, LANES), 1) == e
        r_col = jnp.sum(jnp.where(pick, rcol_scr[...], 0.0), axis=1, keepdims=True)
        g_col = jnp.sum(jnp.where(pick, gate_ref[...], 0.0), axis=1, keepdims=True)
        col_id = lax.broadcasted_iota(I32, (tb, gw), 1).astype(F32)

        def combine(gi, carry):
            r0 = pl.multiple_of(gi * gw, gw)
            selt = jnp.where(col_id + r0.astype(F32) == r_col, 1.0, 0.0).astype(BF16)
            y = acc_scr[pl.ds(r0, gw), :]
            y_hi = y.astype(BF16)
            y_lo = (y - y_hi.astype(F32)).astype(BF16)
            o_ref[...] += g_col * (_dot(selt, y_hi) + _dot(selt, y_lo))
            return carry

        lax.fori_loop(0, (nchunk + (MOE_GROUP - 1)) // MOE_GROUP, combine, 0)


def _moe_experts(x, gates, w1, w3, w2):
    m, d = x.shape
    ne, _, ff = w1.shape
    tb = min(MOE_BLOCK, m)
    tf = MOE_FF_TILE if ff % MOE_FF_TILE == 0 else ff
    assert m % tb == 0 and tb % (MOE_GROUP * MOE_CHUNK) == 0 and ne <= SUBLANES
    tri = jnp.asarray(np.tril(np.ones((tb, tb), np.float32), -1), BF16)
    row = lambda i, e, f: (i, 0)
    return pl.pallas_call(
        _moe_body,
        grid=(m // tb, ne, ff // tf),
        in_specs=[pl.BlockSpec((tb, d), row),
                  pl.BlockSpec((tb, LANES), row),
                  pl.BlockSpec((tb, tb), lambda i, e, f: (0, 0)),
                  pl.BlockSpec((1, d, tf), lambda i, e, f: (e, 0, f)),
                  pl.BlockSpec((1, d, tf), lambda i, e, f: (e, 0, f)),
                  pl.BlockSpec((1, tf, d), lambda i, e, f: (e, f, 0))],
        out_specs=pl.BlockSpec((tb, d), row),
        out_shape=jax.ShapeDtypeStruct((m, d), F32),
        scratch_shapes=[pltpu.VMEM((tb, d), BF16),
                        pltpu.VMEM((tb, d), F32),
                        pltpu.VMEM((tb, LANES), F32),
                        pltpu.VMEM((LANES, tb), F32),
                        pltpu.SMEM((N_EXPERTS,), I32)],
        compiler_params=_params("parallel", "arbitrary", "arbitrary"),
    )(x, gates, tri, w1, w3, w2)


def _res_ln_body(res_ref, y_ref, g_ref, b_ref, of_ref, ob_ref, *, alpha):
    out = _layer_norm_rows(alpha * res_ref[...] + y_ref[...], g_ref[...], b_ref[...])
    of_ref[...] = out
    ob_ref[...] = out.astype(BF16)


def _residual_ln(res, y, g, b, alpha, tm=512):
    m, d = res.shape
    tm = min(tm, m)
    row = pl.BlockSpec((tm, d), lambda i: (i, 0))
    vec = pl.BlockSpec((1, d), lambda i: (0, 0))
    return pl.pallas_call(
        functools.partial(_res_ln_body, alpha=alpha),
        grid=(m // tm,),
        in_specs=[row, row, vec, vec],
        out_specs=[row, row],
        out_shape=[jax.ShapeDtypeStruct((m, d), F32), jax.ShapeDtypeStruct((m, d), BF16)],
        compiler_params=_params("parallel"),
    )(res, y, g.reshape(1, d), b.reshape(1, d))


def _band_bias(table):
    band = (LEFT_CHUNKS + 1) * CHUNK
    left = LEFT_CHUNKS * CHUNK
    rev = table.astype(F32)[:, ::-1]
    pad = left - REL_MAX + CHUNK - 1
    n_rev = band - 1 - (left - REL_MAX) + 1
    assert n_rev <= 2 * REL_MAX + 1
    ext = jnp.concatenate([jnp.broadcast_to(rev[:, :1], (rev.shape[0], pad)), rev[:, :n_rev]], axis=1)
    rows = [ext[:, CHUNK - 1 - q:CHUNK - 1 - q + band] for q in range(CHUNK)]
    return jnp.stack(rows, axis=1)


def kernel(x, mem, w_in, lam_q1, lam_k1, lam_q2, lam_k2, diff_norm_g, rel_bias, w_out,
           ln1_g, ln1_b, xattn_wq, xattn_wk, xattn_wv, xattn_wo, ln2_g, ln2_b,
           ffn_w1, ffn_w3, ffn_w2, moe_router, moe_w1, moe_w3, moe_w2, ln3_g, ln3_b):
    b, s, d = x.shape
    depth = w_in.shape[0]
    t = b * s
    alpha = (2.0 * depth) ** 0.25
    slopes_a, slopes_c = _alibi_slopes()
    n_main = 3 * H_A * DV_A + 3 * H_B * DH_B + 3 * H_C * DH_C + H_IDX * D_IDX
    n_tail = D_IDX + H_IDX
    wa, wb, wc = H_A * DV_A, H_B * DH_B, H_C * DH_C
    left = LEFT_CHUNKS * CHUNK

    h = x.reshape(t, d)
    hb = h.astype(BF16)
    memb = mem.reshape(-1, d).astype(BF16)
    for l in range(depth):
        wl = w_in[l]
        o_b, o_c, o_i = 3 * wa, 3 * wa + 3 * wb, 3 * wa + 3 * wb + 3 * wc
        w_main = jnp.concatenate(
            [wl[:, :wa] * (DQK_A ** -0.5 * LOG2E), wl[:, wa:2 * wa], wl[:, o_i:o_i + H_IDX * D_IDX],
             wl[:, o_b:o_b + wb] * (DH_B ** -0.5 * LOG2E), wl[:, o_b + wb:o_b + 2 * wb],
             wl[:, o_c:o_c + wc] * (DH_C ** -0.5 * LOG2E), wl[:, o_c + wc:o_c + 2 * wc]], axis=1).astype(BF16)
        w_tail = jnp.zeros((d, LANES), BF16).at[:, :n_tail].set(wl[:, n_main:].astype(BF16))
        proj = _matmul(hb, w_main, BF16, tn=w_main.shape[1] // 2).reshape(b, s, -1)
        kiw = _matmul(hb, w_tail, F32, tn=LANES).reshape(b, s, LANES)
        w_v = jnp.concatenate([wl[:, 2 * wa:3 * wa], wl[:, o_c + 2 * wc:o_c + 3 * wc]], axis=1).astype(BF16)
        tile = min(ATT_TILE, s)
        vt = _matmul_t_tiles(hb, w_v, tile).reshape(b, s // tile, wa + wc, tile)
        lam_init = 0.8 - 0.6 * math.exp(-0.3 * l)
        ya = _diff_attention(proj, vt, lam_q1[l], lam_k1[l], lam_q2[l], lam_k2[l], diff_norm_g[l],
                             slopes_a, lam_init)
        kpad = jnp.pad(proj[:, :, 3 * wa + wb:3 * wa + 2 * wb], ((0, 0), (left, 0), (0, 0)))
        vbt = _matmul_t_tiles(hb, wl[:, o_b + 2 * wb:o_b + 3 * wb].astype(BF16), LANES)
        vbt = jnp.pad(vbt.reshape(b, s // LANES, wb, LANES), ((0, 0), (left // LANES, 0), (0, 0), (0, 0)))
        yb = _band_attention(proj, kpad, vbt, _band_bias_table(rel_bias[l]))
        yc = _dsa_attention(proj, vt, kiw, slopes_c)
        wo = w_out[l].astype(BF16)
        h, hb = _matmul_ln(
            [ya.reshape(t, wa), yb.reshape(t, wb), yc.reshape(t, -1)],
            [wo[:wa], wo[wa:wa + wb], wo[wa + wb:]],
            h, ln1_g[l], ln1_b[l], alpha)
        q = _matmul(hb, xattn_wq[l].astype(BF16), BF16, tn=d).reshape(b, s, d)
        km = _matmul(memb, xattn_wk[l].astype(BF16), BF16).reshape(b, -1, d)
        vm = _matmul(memb, xattn_wv[l].astype(BF16), BF16).reshape(b, -1, d)
        xo = _xattn_core(q, km, vm).reshape(t, d)
        h, hb = _matmul_ln([xo], [xattn_wo[l].astype(BF16)], h, ln2_g[l], ln2_b[l], alpha)
        if l % 2 == 0:
            hid = _swiglu_hidden(hb, ffn_w1[l // 2].astype(BF16), ffn_w3[l // 2].astype(BF16))
            h, hb = _matmul_ln([hid], [ffn_w2[l // 2].astype(BF16)], h, ln3_g[l], ln3_b[l], alpha)
        else:
            gates = _router_gates(hb, moe_router[l // 2])
            y = _moe_experts(hb, gates, moe_w1[l // 2].astype(BF16), moe_w3[l // 2].astype(BF16),
                             moe_w2[l // 2].astype(BF16))
            h, hb = _residual_ln(h, y, ln3_g[l], ln3_b[l], alpha)
    return h.reshape(b, s, d)
```

```python
import functools
import math

import numpy as np
import jax
import jax.numpy as jnp
from jax import lax
from jax.experimental import pallas as pl
from jax.experimental.pallas import tpu as pltpu

F32 = jnp.float32
BF16 = jnp.bfloat16
I32 = jnp.int32
I16 = jnp.int16

CHUNK = 64
H_A, DQK_A = 4, 64
DV_A = 2 * DQK_A
H_B, DH_B = 4, 64
LEFT_CHUNKS = 8
REL_MAX = 128
H_C, DH_C = 4, 64
H_IDX, D_IDX = 8, 64
TOPK_MAX = 256
H_X = 4
N_EXPERTS = 8
LN_EPS = 1e-5
RMS_EPS = 1e-6
NEG_INF = -1e30
MASKED = -2e30

LANES = 128
SUBLANES = 8
VMEM_LIMIT = 56 * 1024 * 1024
ATT_TILE = 256

_NT = (((1,), (1,)), ((), ()))


def _alibi_slopes():
    n = H_A + H_C
    s = (2.0 ** (-8.0 * np.arange(1, n + 1) / n)).astype(np.float32)
    return [float(v) for v in s[0::2]], [float(v) for v in s[1::2]]


def _params(*sem):
    return pltpu.CompilerParams(dimension_semantics=sem, vmem_limit_bytes=VMEM_LIMIT)


def _dot(a, b):
    return jnp.dot(a, b, preferred_element_type=F32)


SCORE_ROW_CHUNK = 128


def _dot_row_chunks(a, b):
    rows = a.shape[0]
    step = min(SCORE_ROW_CHUNK, rows)
    return jnp.concatenate([_dot(a[r:r + step], b) for r in range(0, rows, step)], axis=0)


def _dot_nt(a, b):
    return lax.dot_general(a, b, _NT, preferred_element_type=F32)


def _layer_norm_rows(y, g, b):
    mu = jnp.mean(y, axis=-1, keepdims=True)
    d = y - mu
    var = jnp.mean(d * d, axis=-1, keepdims=True)
    return d * lax.rsqrt(var + LN_EPS) * g + b


def _silu(a):
    return a / (1.0 + jnp.exp(-a))


def _mm_body(x_ref, w_ref, o_ref):
    o_ref[...] = _dot(x_ref[...], w_ref[...]).astype(o_ref.dtype)


def _matmul(x, w, out_dtype, tm=1024, tn=512):
    m, k = x.shape
    n = w.shape[1]
    tm, tn = min(tm, m), min(tn, n)
    return pl.pallas_call(
        _mm_body,
        grid=(m // tm, n // tn),
        in_specs=[pl.BlockSpec((tm, k), lambda i, j: (i, 0)),
                  pl.BlockSpec((k, tn), lambda i, j: (0, j))],
        out_specs=pl.BlockSpec((tm, tn), lambda i, j: (i, j)),
        out_shape=jax.ShapeDtypeStruct((m, n), out_dtype),
        compiler_params=_params("parallel", "parallel"),
    )(x, w)


def _mm_t_body(x_ref, w_ref, o_ref, *, tile):
    y = _dot(x_ref[...], w_ref[...])
    for r in range(o_ref.shape[0]):
        o_ref[r] = y[r * tile:(r + 1) * tile, :].T.astype(o_ref.dtype)


def _matmul_t_tiles(x, w, tile, tm=1024):
    m, k = x.shape
    n = w.shape[1]
    tm = min(tm, m)
    return pl.pallas_call(
        functools.partial(_mm_t_body, tile=tile),
        grid=(m // tm,),
        in_specs=[pl.BlockSpec((tm, k), lambda i: (i, 0)),
                  pl.BlockSpec((k, n), lambda i: (0, 0))],
        out_specs=pl.BlockSpec((tm // tile, n, tile), lambda i: (i, 0, 0)),
        out_shape=jax.ShapeDtypeStruct((m // tile, n, tile), BF16),
        compiler_params=_params("parallel"),
    )(x, w)


def _mm_ln_body(*refs, n_in, alpha):
    xs, ws = refs[:n_in], refs[n_in:2 * n_in]
    res_ref, g_ref, b_ref, of_ref, ob_ref = refs[2 * n_in:]
    acc = _dot(xs[0][...], ws[0][...])
    for x_ref, w_ref in zip(xs[1:], ws[1:]):
        acc = acc + _dot(x_ref[...], w_ref[...])
    out = _layer_norm_rows(alpha * res_ref[...] + acc, g_ref[...], b_ref[...])
    of_ref[...] = out
    ob_ref[...] = out.astype(BF16)


def _matmul_ln(xs, ws, res, g, b, alpha):
    m, d = res.shape
    k_total = sum(x.shape[1] for x in xs)
    tm = min(1024 if k_total <= 1024 else 512, m)
    n_in = len(xs)
    in_specs = ([pl.BlockSpec((tm, x.shape[1]), lambda i: (i, 0)) for x in xs]
                + [pl.BlockSpec(w.shape, lambda i: (0, 0)) for w in ws]
                + [pl.BlockSpec((tm, d), lambda i: (i, 0)),
                   pl.BlockSpec((1, d), lambda i: (0, 0)),
                   pl.BlockSpec((1, d), lambda i: (0, 0))])
    return pl.pallas_call(
        functools.partial(_mm_ln_body, n_in=n_in, alpha=alpha),
        grid=(m // tm,),
        in_specs=in_specs,
        out_specs=[pl.BlockSpec((tm, d), lambda i: (i, 0)),
                   pl.BlockSpec((tm, d), lambda i: (i, 0))],
        out_shape=[jax.ShapeDtypeStruct((m, d), F32), jax.ShapeDtypeStruct((m, d), BF16)],
        compiler_params=_params("parallel"),
    )(*xs, *ws, res, g.reshape(1, d), b.reshape(1, d))


def _swiglu_body(x_ref, w1_ref, w3_ref, o_ref):
    x = x_ref[...]
    a = _dot(x, w1_ref[...])
    b = _dot(x, w3_ref[...])
    o_ref[...] = (_silu(a) * b).astype(o_ref.dtype)


def _swiglu_hidden(x, w1, w3, tm=512, tn=1408):
    m, k = x.shape
    n = w1.shape[1]
    tm = min(tm, m)
    if n % tn:
        tn = n
    return pl.pallas_call(
        _swiglu_body,
        grid=(m // tm, n // tn),
        in_specs=[pl.BlockSpec((tm, k), lambda i, j: (i, 0)),
                  pl.BlockSpec((k, tn), lambda i, j: (0, j)),
                  pl.BlockSpec((k, tn), lambda i, j: (0, j))],
        out_specs=pl.BlockSpec((tm, tn), lambda i, j: (i, j)),
        out_shape=jax.ShapeDtypeStruct((m, n), BF16),
        compiler_params=_params("parallel", "parallel"),
    )(x, w1, w3)


def _key_query_iotas(tile):
    kpos = lax.broadcasted_iota(I32, (tile, tile), 0)
    qpos = lax.broadcasted_iota(I32, (tile, tile), 1)
    return kpos, qpos


def _scaled_transpose(x, scale):
    return (x.astype(F32) * scale).T.astype(BF16)


LOG2E = math.log2(math.e)
ALIBI_PARTS = 3
ALIBI_ROWS = 16


def _bf16_parts(x):
    parts, r = [], np.float32(x)
    for _ in range(ALIBI_PARTS):
        p = np.float32(r.astype(jnp.bfloat16))
        parts.append(float(p))
        r = np.float32(r - p)
    return parts


def _alibi_key_cols(tile, parts):
    lane = lax.broadcasted_iota(I32, (tile, LANES), 1)
    kr = lax.broadcasted_iota(I32, (tile, LANES), 0).astype(F32)
    v = jnp.where((lane >= ALIBI_PARTS) & (lane < 2 * ALIBI_PARTS), kr, 0.0)
    for n, c in enumerate(parts):
        v = jnp.where(lane == n, c, v)
    return v.astype(BF16)


def _alibi_query_rows(tile, parts):
    row = lax.broadcasted_iota(I32, (ALIBI_ROWS, tile), 0)
    qr = lax.broadcasted_iota(I32, (ALIBI_ROWS, tile), 1).astype(F32)
    v = jnp.where(row < ALIBI_PARTS, -qr, 0.0)
    for n, c in enumerate(parts):
        v = jnp.where(row == ALIBI_PARTS + n, c, v)
    return v.astype(BF16)


def _init_alibi(slopes, dneg_scr, kx_scr, qa_scr, tile, maps_per_head):
    kpos, qpos = _key_query_iotas(tile)
    rel = (qpos - kpos).astype(F32)
    qa_scr[...] = jnp.zeros(qa_scr.shape, BF16)
    for h, slope in enumerate(slopes):
        parts = _bf16_parts(slope * LOG2E)
        dneg_scr[h] = 2.0 * jnp.minimum(np.float32(slope * LOG2E) * rel, 0.0)
        kx_scr[h] = _alibi_key_cols(tile, parts)
        for mm in range(maps_per_head):
            qa_scr[h * maps_per_head + mm, LANES:LANES + ALIBI_ROWS, :] = _alibi_query_rows(tile, parts)


def _store_query_maps(qa_scr, qt, n_maps, dh):
    for idx in range(n_maps):
        r0 = (idx % 2) * dh
        qa_scr[idx, r0:r0 + dh, :] = qt[idx * dh:(idx + 1) * dh, :]


def _softmax_probs(c, t_ref, tmax_ref, p_ref, a_ref, m_ref, l_ref):
    m_old = m_ref[...]
    m_new = jnp.maximum(m_old, tmax_ref[...] - c)
    p = jnp.exp2(t_ref[...] - (m_new + c))
    a = jnp.exp2(m_old - m_new)
    l_ref[...] = a * l_ref[...] + jnp.sum(p, axis=0, keepdims=True)
    m_ref[...] = m_new
    a_ref[...] = a
    p_ref[...] = p.astype(BF16)


def _accumulate(vt, p_ref, a_ref, acc_ref):
    acc_ref[...] = a_ref[...] * acc_ref[...] + _dot(vt, p_ref[...])


N_T_SLOTS, N_P_SLOTS = 3, 2


def _stage_scratch(tile):
    return ([pltpu.VMEM((tile, tile), F32)] * N_T_SLOTS + [pltpu.VMEM((1, tile), F32)] * N_T_SLOTS
            + [pltpu.VMEM((tile, tile), BF16)] * N_P_SLOTS + [pltpu.VMEM((1, tile), F32)] * N_P_SLOTS)


def _pipelined_maps(n_maps, scores, probs, accumulate, stage):
    nt, npb = N_T_SLOTS, N_P_SLOTS
    t_slots, x_slots = stage[:nt], stage[nt:2 * nt]
    p_slots, a_slots = stage[2 * nt:2 * nt + npb], stage[2 * nt + npb:]

    def stage_scores(k):
        s = scores(k)
        t_slots[k % nt][...] = s
        x_slots[k % nt][...] = jnp.max(s, axis=0, keepdims=True)

    def stage_probs(k):
        probs(k, t_slots[k % nt], x_slots[k % nt], p_slots[k % npb], a_slots[k % npb])

    stage_scores(0)
    if n_maps > 1:
        stage_scores(1)
    stage_probs(0)
    for k in range(n_maps):
        if k + 2 < n_maps:
            stage_scores(k + 2)
        if k + 1 < n_maps:
            stage_probs(k + 1)
        accumulate(k, p_slots[k % npb], a_slots[k % npb])


def _diff_body(q_ref, k_ref, vt_ref, lq1_ref, lk1_ref, lq2_ref, lk2_ref, gain_ref, o_ref,
               qa_scr, kx_scr, dneg_scr, m_scr, l_scr, acc_scr, *stage,
               tile, slopes, lam_init):
    i = pl.program_id(1)
    kpos, qpos = _key_query_iotas(tile)

    @pl.when((pl.program_id(0) == 0) & (i == 0))
    def _():
        _init_alibi(slopes, dneg_scr, kx_scr, qa_scr, tile, 2)

    _store_query_maps(qa_scr, _scaled_transpose(q_ref[0], 1.0), 2 * H_A, DQK_A)
    m_scr[...] = jnp.full(m_scr.shape, NEG_INF, F32)
    l_scr[...] = jnp.zeros(l_scr.shape, F32)
    acc_scr[...] = jnp.zeros(acc_scr.shape, F32)

    def block(j, diag):
        k0 = pl.multiple_of(j * tile, tile)
        kb = k_ref[0, pl.ds(k0, tile), :]
        vtb = vt_ref[0, j]
        off = ((i - j) * tile).astype(F32)

        def scores(idx):
            h, pair = idx // 2, idx // 2
            keys = jnp.concatenate([kb[:, pair * LANES:(pair + 1) * LANES], kx_scr[h]], axis=1)
            s = _dot_row_chunks(keys, qa_scr[idx])
            if diag:
                allowed = (kpos // CHUNK) <= (qpos // CHUNK)
                return jnp.where(allowed, s + dneg_scr[h], MASKED)
            return s

        def probs(idx, t_ref, tmax_ref, p_ref, a_ref):
            shift = 0.0 if diag else (slopes[idx // 2] * LOG2E) * off
            _softmax_probs(shift, t_ref, tmax_ref, p_ref, a_ref, m_scr.at[idx], l_scr.at[idx])

        def accumulate(idx, p_ref, a_ref):
            h = idx // 2
            _accumulate(vtb[h * DV_A:(h + 1) * DV_A, :], p_ref, a_ref, acc_scr.at[idx])

        _pipelined_maps(2 * H_A, scores, probs, accumulate, stage)

    def full_block(j, carry):
        block(j, False)
        return carry

    lax.fori_loop(0, i, full_block, 0)
    block(i, True)

    lam = (jnp.exp(jnp.sum(lq1_ref[...] * lk1_ref[...], axis=-1, keepdims=True))
           - jnp.exp(jnp.sum(lq2_ref[...] * lk2_ref[...], axis=-1, keepdims=True)) + lam_init)
    gain = gain_ref[...] * (1.0 - lam_init)
    outs = []
    for h in range(H_A):
        o = acc_scr[2 * h] / l_scr[2 * h] - lam * (acc_scr[2 * h + 1] / l_scr[2 * h + 1])
        o = o * lax.rsqrt(jnp.mean(o * o, axis=0, keepdims=True) + RMS_EPS)
        outs.append((o * gain).T)
    o_ref[0] = jnp.concatenate(outs, axis=-1).astype(BF16)


def _diff_attention(proj, vt, lq1, lk1, lq2, lk2, gain, slopes, lam_init):
    b, s, _ = proj.shape
    tile = min(ATT_TILE, s)
    nkb = s // tile
    w = H_A * DV_A
    vec = lambda a: a.reshape(1, -1).astype(F32)
    small = lambda n: pl.BlockSpec((1, n), lambda bi, i: (0, 0))
    return pl.pallas_call(
        functools.partial(_diff_body, tile=tile, slopes=slopes, lam_init=lam_init),
        grid=(b, nkb),
        in_specs=[pl.BlockSpec((1, tile, w), lambda bi, i: (bi, i, 0)),
                  pl.BlockSpec((1, s, w), lambda bi, i: (bi, 0, 1)),
                  pl.BlockSpec((1, nkb, w, tile), lambda bi, i: (bi, 0, 0, 0)),
                  small(DQK_A), small(DQK_A), small(DQK_A), small(DQK_A),
                  pl.BlockSpec((DV_A, 1), lambda bi, i: (0, 0))],
        out_specs=pl.BlockSpec((1, tile, w), lambda bi, i: (bi, i, 0)),
        out_shape=jax.ShapeDtypeStruct((b, s, w), BF16),
        scratch_shapes=[pltpu.VMEM((2 * H_A, 2 * LANES, tile), BF16),
                        pltpu.VMEM((H_A, tile, LANES), BF16),
                        pltpu.VMEM((H_A, tile, tile), F32),
                        pltpu.VMEM((2 * H_A, 1, tile), F32),
                        pltpu.VMEM((2 * H_A, 1, tile), F32),
                        pltpu.VMEM((2 * H_A, DV_A, tile), F32)] + _stage_scratch(tile),
        compiler_params=_params("arbitrary", "arbitrary"),
    )(proj, proj, vt, vec(lq1), vec(lk1), vec(lq2), vec(lk2), gain.reshape(-1, 1).astype(F32))


BAND_CHUNKS = LANES // CHUNK
BAND_WINDOW = (LEFT_CHUNKS + BAND_CHUNKS) * CHUNK


def _band_body(q_ref, k_ref, vt_ref, bias_ref, o_ref, qz_scr):
    i = pl.program_id(1)
    ntile = BAND_WINDOW // LANES

    @pl.when((pl.program_id(0) == 0) & (i == 0))
    def _():
        qz_scr[...] = jnp.zeros(qz_scr.shape, BF16)

    qt = _scaled_transpose(q_ref[0], 1.0)
    for h in range(H_B):
        r0 = (h % 2) * DH_B
        qz_scr[h, r0:r0 + DH_B, :] = qt[h * DH_B:(h + 1) * DH_B, :]

    start = pl.multiple_of(i * LANES, LANES)
    kw = k_ref[0, pl.ds(start, BAND_WINDOW), :]
    vt = vt_ref[0, pl.ds(i, ntile)]
    kk = lax.broadcasted_iota(I32, (BAND_WINDOW, LANES), 0)
    valid = kk >= LEFT_CHUNKS * CHUNK - start
    outs = []
    for h in range(H_B):
        pair = h // 2
        s = _dot_row_chunks(kw[:, pair * LANES:(pair + 1) * LANES], qz_scr[h])
        t = jnp.where(valid, s + bias_ref[h], MASKED)
        p = jnp.exp2(t - jnp.max(t, axis=0, keepdims=True))
        l = jnp.sum(p, axis=0, keepdims=True)
        vth = jnp.concatenate([vt[n, h * DH_B:(h + 1) * DH_B, :] for n in range(ntile)], axis=1)
        outs.append(_dot(vth, p.astype(BF16)) / l)
    o_ref[0] = jnp.concatenate(outs, axis=0).T.astype(BF16)


def _band_bias_table(table):
    bias = _band_bias(table).transpose(0, 2, 1) * LOG2E
    cols = [jnp.pad(bias, ((0, 0), (c * CHUNK, (BAND_CHUNKS - 1 - c) * CHUNK), (0, 0)), constant_values=MASKED)
            for c in range(BAND_CHUNKS)]
    return jnp.concatenate(cols, axis=2)


def _band_attention(proj, kpad, vt_pad, bias):
    b, s, _ = proj.shape
    w = H_B * DH_B
    sp = kpad.shape[1]
    nvt = vt_pad.shape[1]
    return pl.pallas_call(
        _band_body,
        grid=(b, s // LANES),
        in_specs=[pl.BlockSpec((1, LANES, w), lambda bi, i: (bi, i, 6)),
                  pl.BlockSpec((1, sp, w), lambda bi, i: (bi, 0, 0)),
                  pl.BlockSpec((1, nvt, w, LANES), lambda bi, i: (bi, 0, 0, 0)),
                  pl.BlockSpec((H_B, BAND_WINDOW, LANES), lambda bi, i: (0, 0, 0))],
        out_specs=pl.BlockSpec((1, LANES, w), lambda bi, i: (bi, i, 0)),
        out_shape=jax.ShapeDtypeStruct((b, s, w), BF16),
        scratch_shapes=[pltpu.VMEM((H_B, LANES, LANES), BF16)],
        compiler_params=_params("arbitrary", "arbitrary"),
    )(proj, kpad, vt_pad, bias)


def _sortable(x):
    bits = pltpu.bitcast(x + 0.0, I32)
    return bits ^ ((bits >> 31) & 0x7FFFFFFF)


_INT_MIN = -2 ** 31
_VALID_KEY = int(np.array(0.5 * NEG_INF, np.float32).view(np.int32))
_VALID_KEY = _VALID_KEY ^ ((_VALID_KEY >> 31) & 0x7FFFFFFF)


def _dsa_body(qc_ref, kc_ref, vct_ref, qi_ref, kiw_ref, wq_ref, o_ref,
              qa_scr, kx_scr, dneg_scr, qit_scr, key_scr, hi_scr, lo_scr, jstar_scr,
              m_scr, l_scr, acc_scr, *stage,
              tile, topk, slopes, seq):
    i = pl.program_id(1)
    nblk = i + 1
    kpos, qpos = _key_query_iotas(tile)

    @pl.when((pl.program_id(0) == 0) & (i == 0))
    def _():
        _init_alibi(slopes, dneg_scr, kx_scr, qa_scr, tile, 1)

    qit_scr[...] = _scaled_transpose(qi_ref[0], D_IDX ** -0.5)
    wt = wq_ref[0].T
    wrow = [wt[D_IDX + h:D_IDX + h + 1, :] * H_IDX ** -0.5 for h in range(H_IDX)]

    def score_block(j, diag):
        k0 = pl.multiple_of(j * tile, tile)
        kib = kiw_ref[0, pl.ds(k0, tile), :][:, :D_IDX].astype(BF16)
        isc = jnp.zeros((tile, tile), F32)
        for h in range(H_IDX):
            d = _dot(kib, qit_scr[h * D_IDX:(h + 1) * D_IDX, :])
            isc = isc + jnp.maximum(d, 0.0) * wrow[h]
        if diag:
            isc = jnp.where((kpos // CHUNK) <= (qpos // CHUNK), isc, NEG_INF)
        key = _sortable(isc)
        key_scr[pl.ds(k0, tile), :] = key
        hi_scr[pl.ds(k0, tile), :] = (key >> 16).astype(I16)
        lo_scr[pl.ds(k0, tile), :] = ((key & 0xFFFF) - 2 ** 15).astype(I16)

    def score_full(j, carry):
        score_block(j, False)
        return carry

    lax.fori_loop(0, i, score_full, 0)
    score_block(i, True)

    def count(pred):
        def body(j, part):
            k0 = pl.multiple_of(j * tile, tile)
            hit = jnp.where(pred(key_scr[pl.ds(k0, tile), :], kpos + k0), 1.0, 0.0)
            return part + jnp.sum(hit.reshape(tile // SUBLANES, SUBLANES, tile), axis=0)
        part = lax.fori_loop(0, nblk, body, jnp.zeros((SUBLANES, tile), F32))
        return jnp.sum(part, axis=0, keepdims=True)

    kf = float(topk)
    rows16 = 2 * SUBLANES

    pair = 2 if (seq // tile) % 2 == 0 else 1
    span = pair * tile
    lowest16 = jnp.full((tile, tile), -2 ** 15, I16)

    @pl.when(nblk % pair == 1)
    def _():
        pad_rows = pl.ds(pl.multiple_of(nblk * tile, tile), tile)
        hi_scr[pad_rows, :] = lowest16
        lo_scr[pad_rows, :] = lowest16

    def count16(src_ref, pred):
        def body(j, part):
            k0 = pl.multiple_of(j * span, span)
            hit = jnp.where(pred(src_ref[pl.ds(k0, span), :]), jnp.int16(1), jnp.int16(0))
            for g in range(span // rows16):
                part = part + hit[g * rows16:(g + 1) * rows16, :]
            return part
        part = lax.fori_loop(0, (nblk + pair - 1) // pair, body, jnp.zeros((rows16, tile), I16))
        return jnp.sum(part.astype(I32), axis=0, keepdims=True).astype(F32)

    def search16(src_ref, base):
        def bit(p, u):
            cand_u = u | jnp.left_shift(jnp.int32(1), 15 - p)
            cand = (cand_u - 2 ** 15).astype(I16)
            cnt = base + count16(src_ref, lambda v: v >= cand)
            return jnp.where(cnt >= kf, cand_u, u)
        return lax.fori_loop(0, 16, bit, jnp.zeros((1, tile), I32))

    thr_hi = search16(hi_scr, 0.0) - 2 ** 15
    thr_hi16 = thr_hi.astype(I16)
    above = count16(hi_scr, lambda v: v > thr_hi16)

    def mask_low(j, carry):
        k0 = pl.multiple_of(j * tile, tile)
        rows = pl.ds(k0, tile)
        lo_scr[rows, :] = jnp.where(hi_scr[rows, :] == thr_hi16, lo_scr[rows, :], jnp.int16(-2 ** 15))
        return carry

    lax.fori_loop(0, nblk, mask_low, 0)
    thr = jnp.left_shift(thr_hi, 16) + search16(lo_scr, above)

    thr_b = thr
    c_gt = count(lambda key, idx: key > thr_b)
    c_ge = count(lambda key, idx: key >= thr_b)
    tied = (c_ge > kf) & (thr > _VALID_KEY)
    need = kf - c_gt
    jstar_scr[...] = jnp.full((1, tile), seq, I32)

    @pl.when(jnp.max(jnp.where(tied, 1.0, 0.0)) > 0.0)
    def _():
        nbits = max(1, int(math.ceil(math.log2(seq))))

        def index_bit(p, lo):
            cand = lo | jnp.left_shift(jnp.int32(1), nbits - 1 - p)
            cnt = count(lambda key, idx: (key == thr_b) & (idx < cand))
            return jnp.where(cnt < need, cand, lo)

        lo = lax.fori_loop(0, nbits, index_bit, jnp.zeros((1, tile), I32))
        jstar_scr[...] = jnp.where(tied, lo, seq)

    jstar = jstar_scr[...]
    thr2 = jnp.maximum(thr, _VALID_KEY + 1)

    _store_query_maps(qa_scr, _scaled_transpose(qc_ref[0], 1.0), H_C, DH_C)
    m_scr[...] = jnp.full(m_scr.shape, NEG_INF, F32)
    l_scr[...] = jnp.zeros(l_scr.shape, F32)
    acc_scr[...] = jnp.zeros(acc_scr.shape, F32)

    def attend(j, diag):
        k0 = pl.multiple_of(j * tile, tile)
        key = key_scr[pl.ds(k0, tile), :]
        sel = (key > thr2) | ((key == thr2) & (kpos + k0 <= jstar))
        kb = kc_ref[0, pl.ds(k0, tile), :]
        vtb = vct_ref[0, j]
        off = ((i - j) * tile).astype(F32)

        def scores(h):
            pair = h // 2
            keys = jnp.concatenate([kb[:, pair * LANES:(pair + 1) * LANES], kx_scr[h]], axis=1)
            s = _dot_row_chunks(keys, qa_scr[h])
            if diag:
                s = s + dneg_scr[h]
            return jnp.where(sel, s, MASKED)

        def probs(h, t_ref, tmax_ref, p_ref, a_ref):
            shift = 0.0 if diag else (slopes[h] * LOG2E) * off
            _softmax_probs(shift, t_ref, tmax_ref, p_ref, a_ref, m_scr.at[h], l_scr.at[h])

        def accumulate(h, p_ref, a_ref):
            _accumulate(vtb[h * DH_C:(h + 1) * DH_C, :], p_ref, a_ref, acc_scr.at[h])

        _pipelined_maps(H_C, scores, probs, accumulate, stage)

    def attend_full(j, carry):
        attend(j, False)
        return carry

    lax.fori_loop(0, i, attend_full, 0)
    attend(i, True)
    outs = [(acc_scr[h] / l_scr[h]).T for h in range(H_C)]
    o_ref[0] = jnp.concatenate(outs, axis=-1).astype(BF16)


def _dsa_attention(proj, vt, kiw, slopes):
    b, s, _ = proj.shape
    tile = min(ATT_TILE, s)
    nkb = s // tile
    w = H_C * DH_C
    wa = H_A * DV_A
    topk = min(TOPK_MAX, s // 4)
    return pl.pallas_call(
        functools.partial(_dsa_body, tile=tile, topk=topk, slopes=slopes, seq=s),
        grid=(b, nkb),
        in_specs=[pl.BlockSpec((1, tile, w), lambda bi, i: (bi, i, 8)),
                  pl.BlockSpec((1, s, w), lambda bi, i: (bi, 0, 9)),
                  pl.BlockSpec((1, nkb, w, tile), lambda bi, i: (bi, 0, wa // w, 0)),
                  pl.BlockSpec((1, tile, H_IDX * D_IDX), lambda bi, i: (bi, i, 2)),
                  pl.BlockSpec((1, s, LANES), lambda bi, i: (bi, 0, 0)),
                  pl.BlockSpec((1, tile, LANES), lambda bi, i: (bi, i, 0))],
        out_specs=pl.BlockSpec((1, tile, w), lambda bi, i: (bi, i, 0)),
        out_shape=jax.ShapeDtypeStruct((b, s, w), BF16),
        scratch_shapes=[pltpu.VMEM((H_C, 2 * LANES, tile), BF16),
                        pltpu.VMEM((H_C, tile, LANES), BF16),
                        pltpu.VMEM((H_C, tile, tile), F32),
                        pltpu.VMEM((H_IDX * D_IDX, tile), BF16),
                        pltpu.VMEM((s, tile), I32),
                        pltpu.VMEM((s, tile), I16),
                        pltpu.VMEM((s, tile), I16),
                        pltpu.VMEM((1, tile), I32),
                        pltpu.VMEM((H_C, 1, tile), F32),
                        pltpu.VMEM((H_C, 1, tile), F32),
                        pltpu.VMEM((H_C, DH_C, tile), F32)] + _stage_scratch(tile),
        compiler_params=_params("arbitrary", "arbitrary"),
    )(proj, proj, vt, proj, kiw, kiw)


def _xattn_body(q_ref, k_ref, v_ref, o_ref, *, dh):
    outs = []
    for h in range(H_X):
        c = h * dh
        q = q_ref[0, :, c:c + dh] * dh ** -0.5
        s = _dot_nt(q, k_ref[0, :, c:c + dh])
        p = jnp.exp(s - jnp.max(s, axis=-1, keepdims=True))
        l = jnp.sum(p, axis=-1, keepdims=True)
        outs.append((_dot(p.astype(BF16), v_ref[0, :, c:c + dh]) / l).astype(BF16))
    o_ref[0] = jnp.concatenate(outs, axis=-1)


def _xattn_core(q, k, v, tq=512):
    b, s, d = q.shape
    n_mem = k.shape[1]
    tq = min(tq, s)
    return pl.pallas_call(
        functools.partial(_xattn_body, dh=d // H_X),
        grid=(b, s // tq),
        in_specs=[pl.BlockSpec((1, tq, d), lambda bi, i: (bi, i, 0)),
                  pl.BlockSpec((1, n_mem, d), lambda bi, i: (bi, 0, 0)),
                  pl.BlockSpec((1, n_mem, d), lambda bi, i: (bi, 0, 0))],
        out_specs=pl.BlockSpec((1, tq, d), lambda bi, i: (bi, i, 0)),
        out_shape=jax.ShapeDtypeStruct((b, s, d), BF16),
        compiler_params=_params("parallel", "parallel"),
    )(q, k, v)


def _router_body(x_ref, r_ref, g_ref):
    logits = _dot(x_ref[...], r_ref[...])
    lane = lax.broadcasted_iota(I32, logits.shape, 1).astype(F32)
    big = float(LANES)
    lg = jnp.where(lane < N_EXPERTS, logits, -jnp.inf)
    m1 = jnp.max(lg, axis=-1, keepdims=True)
    i1 = jnp.min(jnp.where(lg == m1, lane, big), axis=-1, keepdims=True)
    lg2 = jnp.where(lane == i1, -jnp.inf, lg)
    m2 = jnp.max(lg2, axis=-1, keepdims=True)
    i2 = jnp.min(jnp.where(lg2 == m2, lane, big), axis=-1, keepdims=True)
    e = jnp.exp(m2 - m1)
    den = 1.0 + e
    g_ref[...] = jnp.where(lane == i1, 1.0 / den, 0.0) + jnp.where(lane == i2, e / den, 0.0)


def _router_gates(x, router, tm=1024):
    m, d = x.shape
    tm = min(tm, m)
    rpad = jnp.zeros((d, LANES), BF16).at[:, :N_EXPERTS].set(router.astype(BF16))
    return pl.pallas_call(
        _router_body,
        grid=(m // tm,),
        in_specs=[pl.BlockSpec((tm, d), lambda i: (i, 0)),
                  pl.BlockSpec((d, LANES), lambda i: (0, 0))],
        out_specs=pl.BlockSpec((tm, LANES), lambda i: (i, 0)),
        out_shape=jax.ShapeDtypeStruct((m, LANES), F32),
        compiler_params=_params("parallel"),
    )(x, rpad)


MOE_BLOCK = 1024
MOE_CHUNK = 128
MOE_GROUP = 2
MOE_FF_TILE = 1792


def _moe_body(x_ref, gate_ref, tri_ref, w1_ref, w3_ref, w2_ref, o_ref,
              xs_scr, acc_scr, rcol_scr, rrow_scr, cnt_smem):
    e = pl.program_id(1)
    f = pl.program_id(2)
    nf = pl.num_programs(2)
    tb = x_ref.shape[0]
    ch = MOE_CHUNK
    gw = MOE_GROUP * ch

    @pl.when((e == 0) & (f == 0))
    def _():
        o_ref[...] = jnp.zeros(o_ref.shape, F32)
        member = jnp.where(gate_ref[...] > 0.0, 1.0, 0.0)
        rank = _dot(tri_ref[...], member.astype(BF16))
        rcol = jnp.where(member > 0.0, rank, -1.0)
        rcol_scr[...] = rcol
        rrow_scr[...] = rcol.T
        counts = jnp.sum(member, axis=0, keepdims=True)
        for ee in range(N_EXPERTS):
            cnt_smem[ee] = counts[0, ee].astype(I32)

    half = ch // 2
    n_half = (cnt_smem[e] + (half - 1)) // half
    nfull = n_half // 2
    has_tail = n_half % 2 == 1
    tail0 = pl.multiple_of(nfull * ch, ch)
    nchunk = nfull + n_half % 2

    @pl.when(f == 0)
    def _():
        acc_scr[...] = jnp.zeros(acc_scr.shape, F32)
        sub = lax.broadcasted_iota(I32, (SUBLANES, tb), 0)
        r_row = jnp.sum(jnp.where(sub == e, rrow_scr[0:SUBLANES, :], 0.0), axis=0, keepdims=True)

        def gather_rows(r0, rows):
            row_id = lax.broadcasted_iota(I32, (rows, tb), 0).astype(F32) + r0.astype(F32)
            sel = jnp.where(row_id == r_row, 1.0, 0.0).astype(BF16)
            xs_scr[pl.ds(r0, rows), :] = _dot(sel, x_ref[...]).astype(BF16)

        def gather(c, carry):
            gather_rows(pl.multiple_of(c * ch, ch), ch)
            return carry

        lax.fori_loop(0, nfull, gather, 0)

        @pl.when(has_tail)
        def _():
            gather_rows(tail0, half)

    def ffn_rows(r0, rows):
        xs = xs_scr[pl.ds(r0, rows), :]
        a = _dot(xs, w1_ref[0])
        b = _dot(xs, w3_ref[0])
        acc_scr[pl.ds(r0, rows), :] += _dot((_silu(a) * b).astype(BF16), w2_ref[0])

    def ffn(c, carry):
        ffn_rows(pl.multiple_of(c * ch, ch), ch)
        return carry

    lax.fori_loop(0, nfull, ffn, 0)

    @pl.when(has_tail)
    def _():
        ffn_rows(tail0, half)

    @pl.when(f == nf - 1)
    def _():
        pick = lax.broadcasted_iota(I32, (tb, LANES), 1) == e
        r_col = jnp.sum(jnp.where(pick, rcol_scr[...], 0.0), axis=1, keepdims=True)
        g_col = jnp.sum(jnp.where(pick, gate_ref[...], 0.0), axis=1, keepdims=True)
        col_id = lax.broadcasted_iota(I32, (tb, gw), 1).astype(F32)

        def combine(gi, carry):
            r0 = pl.multiple_of(gi * gw, gw)
            selt = jnp.where(col_id + r0.astype(F32) == r_col, 1.0, 0.0).astype(BF16)
            y = acc_scr[pl.ds(r0, gw), :]
            y_hi = y.astype(BF16)
            y_lo = (y - y_hi.astype(F32)).astype(BF16)
            o_ref[...] += g_col * (_dot(selt, y_hi) + _dot(selt, y_lo))
            return carry

        lax.fori_loop(0, (nchunk + (MOE_GROUP - 1)) // MOE_GROUP, combine, 0)


def _moe_experts(x, gates, w1, w3, w2):
    m, d = x.shape
    ne, _, ff = w1.shape
    tb = min(MOE_BLOCK, m)
    tf = MOE_FF_TILE if ff % MOE_FF_TILE == 0 else ff
    assert m % tb == 0 and tb % (MOE_GROUP * MOE_CHUNK) == 0 and ne <= SUBLANES
    tri = jnp.asarray(np.tril(np.ones((tb, tb), np.float32), -1), BF16)
    row = lambda i, e, f: (i, 0)
    return pl.pallas_call(
        _moe_body,
        grid=(m // tb, ne, ff // tf),
        in_specs=[pl.BlockSpec((tb, d), row),
                  pl.BlockSpec((tb, LANES), row),
                  pl.BlockSpec((tb, tb), lambda i, e, f: (0, 0)),
                  pl.BlockSpec((1, d, tf), lambda i, e, f: (e, 0, f)),
                  pl.BlockSpec((1, d, tf), lambda i, e, f: (e, 0, f)),
                  pl.BlockSpec((1, tf, d), lambda i, e, f: (e, f, 0))],
        out_specs=pl.BlockSpec((tb, d), row),
        out_shape=jax.ShapeDtypeStruct((m, d), F32),
        scratch_shapes=[pltpu.VMEM((tb, d), BF16),
                        pltpu.VMEM((tb, d), F32),
                        pltpu.VMEM((tb, LANES), F32),
                        pltpu.VMEM((LANES, tb), F32),
                        pltpu.SMEM((N_EXPERTS,), I32)],
        compiler_params=_params("parallel", "arbitrary", "arbitrary"),
    )(x, gates, tri, w1, w3, w2)


def _res_ln_body(res_ref, y_ref, g_ref, b_ref, of_ref, ob_ref, *, alpha):
    out = _layer_norm_rows(alpha * res_ref[...] + y_ref[...], g_ref[...], b_ref[...])
    of_ref[...] = out
    ob_ref[...] = out.astype(BF16)


def _residual_ln(res, y, g, b, alpha, tm=512):
    m, d = res.shape
    tm = min(tm, m)
    row = pl.BlockSpec((tm, d), lambda i: (i, 0))
    vec = pl.BlockSpec((1, d), lambda i: (0, 0))
    return pl.pallas_call(
        functools.partial(_res_ln_body, alpha=alpha),
        grid=(m // tm,),
        in_specs=[row, row, vec, vec],
        out_specs=[row, row],
        out_shape=[jax.ShapeDtypeStruct((m, d), F32), jax.ShapeDtypeStruct((m, d), BF16)],
        compiler_params=_params("parallel"),
    )(res, y, g.reshape(1, d), b.reshape(1, d))


def _band_bias(table):
    band = (LEFT_CHUNKS + 1) * CHUNK
    left = LEFT_CHUNKS * CHUNK
    rev = table.astype(F32)[:, ::-1]
    pad = left - REL_MAX + CHUNK - 1
    n_rev = band - 1 - (left - REL_MAX) + 1
    assert n_rev <= 2 * REL_MAX + 1
    ext = jnp.concatenate([jnp.broadcast_to(rev[:, :1], (rev.shape[0], pad)), rev[:, :n_rev]], axis=1)
    n_ext = ext.shape[1]
    period = jnp.concatenate([ext[:, CHUNK - 1:], jnp.zeros((ext.shape[0], 1), F32), ext[:, :CHUNK - 1]], axis=1)
    skew = jnp.tile(period, (1, CHUNK))[:, :CHUNK * n_ext].reshape(-1, CHUNK, n_ext)
    return skew[:, :, :band]


def kernel(x, mem, w_in, lam_q1, lam_k1, lam_q2, lam_k2, diff_norm_g, rel_bias, w_out,
           ln1_g, ln1_b, xattn_wq, xattn_wk, xattn_wv, xattn_wo, ln2_g, ln2_b,
           ffn_w1, ffn_w3, ffn_w2, moe_router, moe_w1, moe_w3, moe_w2, ln3_g, ln3_b):
    b, s, d = x.shape
    depth = w_in.shape[0]
    t = b * s
    alpha = (2.0 * depth) ** 0.25
    slopes_a, slopes_c = _alibi_slopes()
    n_main = 3 * H_A * DV_A + 3 * H_B * DH_B + 3 * H_C * DH_C + H_IDX * D_IDX
    n_tail = D_IDX + H_IDX
    wa, wb, wc = H_A * DV_A, H_B * DH_B, H_C * DH_C
    left = LEFT_CHUNKS * CHUNK

    h = x.reshape(t, d)
    hb = h.astype(BF16)
    memb = mem.reshape(-1, d).astype(BF16)
    for l in range(depth):
        wl = w_in[l]
        o_b, o_c, o_i = 3 * wa, 3 * wa + 3 * wb, 3 * wa + 3 * wb + 3 * wc
        w_main = jnp.concatenate(
            [wl[:, :wa] * (DQK_A ** -0.5 * LOG2E), wl[:, wa:2 * wa], wl[:, o_i:o_i + H_IDX * D_IDX],
             wl[:, o_b:o_b + wb] * (DH_B ** -0.5 * LOG2E), wl[:, o_b + wb:o_b + 2 * wb],
             wl[:, o_c:o_c + wc] * (DH_C ** -0.5 * LOG2E), wl[:, o_c + wc:o_c + 2 * wc]], axis=1).astype(BF16)
        w_tail = jnp.zeros((d, LANES), BF16).at[:, :n_tail].set(wl[:, n_main:].astype(BF16))
        proj = _matmul(hb, w_main, BF16, tn=w_main.shape[1] // 2).reshape(b, s, -1)
        kiw = _matmul(hb, w_tail, F32, tn=LANES).reshape(b, s, LANES)
        w_v = jnp.concatenate([wl[:, 2 * wa:3 * wa], wl[:, o_c + 2 * wc:o_c + 3 * wc]], axis=1).astype(BF16)
        tile = min(ATT_TILE, s)
        vt = _matmul_t_tiles(hb, w_v, tile).reshape(b, s // tile, wa + wc, tile)
        lam_init = 0.8 - 0.6 * math.exp(-0.3 * l)
        ya = _diff_attention(proj, vt, lam_q1[l], lam_k1[l], lam_q2[l], lam_k2[l], diff_norm_g[l],
                             slopes_a, lam_init)
        kpad = jnp.pad(proj[:, :, 3 * wa + wb:3 * wa + 2 * wb], ((0, 0), (left, 0), (0, 0)))
        vbt = _matmul_t_tiles(hb, wl[:, o_b + 2 * wb:o_b + 3 * wb].astype(BF16), LANES)
        vbt = jnp.pad(vbt.reshape(b, s // LANES, wb, LANES), ((0, 0), (left // LANES, 0), (0, 0), (0, 0)))
        yb = _band_attention(proj, kpad, vbt, _band_bias_table(rel_bias[l]))
        yc = _dsa_attention(proj, vt, kiw, slopes_c)
        wo = w_out[l].astype(BF16)
        h, hb = _matmul_ln(
            [ya.reshape(t, wa), yb.reshape(t, wb), yc.reshape(t, -1)],
            [wo[:wa], wo[wa:wa + wb], wo[wa + wb:]],
            h, ln1_g[l], ln1_b[l], alpha)
        q = _matmul(hb, xattn_wq[l].astype(BF16), BF16, tn=d).reshape(b, s, d)
        km = _matmul(memb, xattn_wk[l].astype(BF16), BF16).reshape(b, -1, d)
        vm = _matmul(memb, xattn_wv[l].astype(BF16), BF16).reshape(b, -1, d)
        xo = _xattn_core(q, km, vm).reshape(t, d)
        h, hb = _matmul_ln([xo], [xattn_wo[l].astype(BF16)], h, ln2_g[l], ln2_b[l], alpha)
        if l % 2 == 0:
            hid = _swiglu_hidden(hb, ffn_w1[l // 2].astype(BF16), ffn_w3[l // 2].astype(BF16))
            h, hb = _matmul_ln([hid], [ffn_w2[l // 2].astype(BF16)], h, ln3_g[l], ln3_b[l], alpha)
        else:
            gates = _router_gates(hb, moe_router[l // 2])
            y = _moe_experts(hb, gates, moe_w1[l // 2].astype(BF16), moe_w3[l // 2].astype(BF16),
                             moe_w2[l // 2].astype(BF16))
            h, hb = _residual_ln(h, y, ln3_g[l], ln3_b[l], alpha)
    return h.reshape(b, s, d)
```

```python
import functools
import math

import numpy as np
import jax
import jax.numpy as jnp
from jax import lax
from jax.experimental import pallas as pl
from jax.experimental.pallas import tpu as pltpu

F32 = jnp.float32
BF16 = jnp.bfloat16
I32 = jnp.int32
I16 = jnp.int16

CHUNK = 64
H_A, DQK_A = 4, 64
DV_A = 2 * DQK_A
H_B, DH_B = 4, 64
LEFT_CHUNKS = 8
REL_MAX = 128
H_C, DH_C = 4, 64
H_IDX, D_IDX = 8, 64
TOPK_MAX = 256
H_X = 4
N_EXPERTS = 8
LN_EPS = 1e-5
RMS_EPS = 1e-6
NEG_INF = -1e30
MASKED = -2e30

LANES = 128
SUBLANES = 8
VMEM_LIMIT = 56 * 1024 * 1024
ATT_TILE = 256

_NT = (((1,), (1,)), ((), ()))


def _alibi_slopes():
    n = H_A + H_C
    s = (2.0 ** (-8.0 * np.arange(1, n + 1) / n)).astype(np.float32)
    return [float(v) for v in s[0::2]], [float(v) for v in s[1::2]]


def _params(*sem):
    return pltpu.CompilerParams(dimension_semantics=sem, vmem_limit_bytes=VMEM_LIMIT)


def _dot(a, b):
    return jnp.dot(a, b, preferred_element_type=F32)


SCORE_ROW_CHUNK = 128


def _dot_row_chunks(a, b):
    rows = a.shape[0]
    step = min(SCORE_ROW_CHUNK, rows)
    return jnp.concatenate([_dot(a[r:r + step], b) for r in range(0, rows, step)], axis=0)


def _dot_nt(a, b):
    return lax.dot_general(a, b, _NT, preferred_element_type=F32)


def _layer_norm_rows(y, g, b):
    mu = jnp.mean(y, axis=-1, keepdims=True)
    d = y - mu
    var = jnp.mean(d * d, axis=-1, keepdims=True)
    return d * lax.rsqrt(var + LN_EPS) * g + b


def _silu(a):
    return a / (1.0 + jnp.exp(-a))


def _mm_body(x_ref, w_ref, o_ref):
    o_ref[...] = _dot(x_ref[...], w_ref[...]).astype(o_ref.dtype)


def _matmul(x, w, out_dtype, tm=1024, tn=512):
    m, k = x.shape
    n = w.shape[1]
    tm, tn = min(tm, m), min(tn, n)
    return pl.pallas_call(
        _mm_body,
        grid=(m // tm, n // tn),
        in_specs=[pl.BlockSpec((tm, k), lambda i, j: (i, 0)),
                  pl.BlockSpec((k, tn), lambda i, j: (0, j))],
        out_specs=pl.BlockSpec((tm, tn), lambda i, j: (i, j)),
        out_shape=jax.ShapeDtypeStruct((m, n), out_dtype),
        compiler_params=_params("parallel", "parallel"),
    )(x, w)


def _mm_t_body(x_ref, w_ref, o_ref, *, tile):
    y = _dot(x_ref[...], w_ref[...])
    for r in range(o_ref.shape[0]):
        o_ref[r] = y[r * tile:(r + 1) * tile, :].T.astype(o_ref.dtype)


def _matmul_t_tiles(x, w, tile, tm=1024):
    m, k = x.shape
    n = w.shape[1]
    tm = min(tm, m)
    return pl.pallas_call(
        functools.partial(_mm_t_body, tile=tile),
        grid=(m // tm,),
        in_specs=[pl.BlockSpec((tm, k), lambda i: (i, 0)),
                  pl.BlockSpec((k, n), lambda i: (0, 0))],
        out_specs=pl.BlockSpec((tm // tile, n, tile), lambda i: (i, 0, 0)),
        out_shape=jax.ShapeDtypeStruct((m // tile, n, tile), BF16),
        compiler_params=_params("parallel"),
    )(x, w)


def _mm_ln_body(*refs, n_in, alpha):
    xs, ws = refs[:n_in], refs[n_in:2 * n_in]
    res_ref, g_ref, b_ref, of_ref, ob_ref = refs[2 * n_in:]
    acc = _dot(xs[0][...], ws[0][...])
    for x_ref, w_ref in zip(xs[1:], ws[1:]):
        acc = acc + _dot(x_ref[...], w_ref[...])
    out = _layer_norm_rows(alpha * res_ref[...] + acc, g_ref[...], b_ref[...])
    of_ref[...] = out
    ob_ref[...] = out.astype(BF16)


def _matmul_ln(xs, ws, res, g, b, alpha):
    m, d = res.shape
    k_total = sum(x.shape[1] for x in xs)
    tm = min(1024 if k_total <= 1024 else 512, m)
    n_in = len(xs)
    in_specs = ([pl.BlockSpec((tm, x.shape[1]), lambda i: (i, 0)) for x in xs]
                + [pl.BlockSpec(w.shape, lambda i: (0, 0)) for w in ws]
                + [pl.BlockSpec((tm, d), lambda i: (i, 0)),
                   pl.BlockSpec((1, d), lambda i: (0, 0)),
                   pl.BlockSpec((1, d), lambda i: (0, 0))])
    return pl.pallas_call(
        functools.partial(_mm_ln_body, n_in=n_in, alpha=alpha),
        grid=(m // tm,),
        in_specs=in_specs,
        out_specs=[pl.BlockSpec((tm, d), lambda i: (i, 0)),
                   pl.BlockSpec((tm, d), lambda i: (i, 0))],
        out_shape=[jax.ShapeDtypeStruct((m, d), F32), jax.ShapeDtypeStruct((m, d), BF16)],
        compiler_params=_params("parallel"),
    )(*xs, *ws, res, g.reshape(1, d), b.reshape(1, d))


def _swiglu_body(x_ref, w1_ref, w3_ref, o_ref):
    x = x_ref[...]
    a = _dot(x, w1_ref[...])
    b = _dot(x, w3_ref[...])
    o_ref[...] = (_silu(a) * b).astype(o_ref.dtype)


def _swiglu_hidden(x, w1, w3, tm=512, tn=1408):
    m, k = x.shape
    n = w1.shape[1]
    tm = min(tm, m)
    if n % tn:
        tn = n
    return pl.pallas_call(
        _swiglu_body,
        grid=(m // tm, n // tn),
        in_specs=[pl.BlockSpec((tm, k), lambda i, j: (i, 0)),
                  pl.BlockSpec((k, tn), lambda i, j: (0, j)),
                  pl.BlockSpec((k, tn), lambda i, j: (0, j))],
        out_specs=pl.BlockSpec((tm, tn), lambda i, j: (i, j)),
        out_shape=jax.ShapeDtypeStruct((m, n), BF16),
        compiler_params=_params("parallel", "parallel"),
    )(x, w1, w3)


def _key_query_iotas(tile):
    kpos = lax.broadcasted_iota(I32, (tile, tile), 0)
    qpos = lax.broadcasted_iota(I32, (tile, tile), 1)
    return kpos, qpos


def _scaled_transpose(x, scale):
    return (x.astype(F32) * scale).T.astype(BF16)


LOG2E = math.log2(math.e)
ALIBI_PARTS = 3
ALIBI_ROWS = 16


def _bf16_parts(x):
    parts, r = [], np.float32(x)
    for _ in range(ALIBI_PARTS):
        p = np.float32(r.astype(jnp.bfloat16))
        parts.append(float(p))
        r = np.float32(r - p)
    return parts


def _alibi_key_cols(tile, parts):
    lane = lax.broadcasted_iota(I32, (tile, LANES), 1)
    kr = lax.broadcasted_iota(I32, (tile, LANES), 0).astype(F32)
    v = jnp.where((lane >= ALIBI_PARTS) & (lane < 2 * ALIBI_PARTS), kr, 0.0)
    for n, c in enumerate(parts):
        v = jnp.where(lane == n, c, v)
    return v.astype(BF16)


def _alibi_query_rows(tile, parts):
    row = lax.broadcasted_iota(I32, (ALIBI_ROWS, tile), 0)
    qr = lax.broadcasted_iota(I32, (ALIBI_ROWS, tile), 1).astype(F32)
    v = jnp.where(row < ALIBI_PARTS, -qr, 0.0)
    for n, c in enumerate(parts):
        v = jnp.where(row == ALIBI_PARTS + n, c, v)
    return v.astype(BF16)


def _init_alibi(slopes, dneg_scr, kx_scr, qa_scr, tile, maps_per_head):
    kpos, qpos = _key_query_iotas(tile)
    rel = (qpos - kpos).astype(F32)
    qa_scr[...] = jnp.zeros(qa_scr.shape, BF16)
    for h, slope in enumerate(slopes):
        parts = _bf16_parts(slope * LOG2E)
        dneg_scr[h] = 2.0 * jnp.minimum(np.float32(slope * LOG2E) * rel, 0.0)
        kx_scr[h] = _alibi_key_cols(tile, parts)
        for mm in range(maps_per_head):
            qa_scr[h * maps_per_head + mm, LANES:LANES + ALIBI_ROWS, :] = _alibi_query_rows(tile, parts)


def _store_query_maps(qa_scr, qt, n_maps, dh):
    for idx in range(n_maps):
        r0 = (idx % 2) * dh
        qa_scr[idx, r0:r0 + dh, :] = qt[idx * dh:(idx + 1) * dh, :]


def _softmax_probs(c, t_ref, tmax_ref, p_ref, a_ref, m_ref, l_ref):
    m_old = m_ref[...]
    m_new = jnp.maximum(m_old, tmax_ref[...] - c)
    p = jnp.exp2(t_ref[...] - (m_new + c))
    a = jnp.exp2(m_old - m_new)
    l_ref[...] = a * l_ref[...] + jnp.sum(p, axis=0, keepdims=True)
    m_ref[...] = m_new
    a_ref[...] = a
    p_ref[...] = p.astype(BF16)


def _accumulate(vt, p_ref, a_ref, acc_ref):
    acc_ref[...] = a_ref[...] * acc_ref[...] + _dot(vt, p_ref[...])


N_T_SLOTS, N_P_SLOTS = 3, 2


def _stage_scratch(tile):
    return ([pltpu.VMEM((tile, tile), F32)] * N_T_SLOTS + [pltpu.VMEM((1, tile), F32)] * N_T_SLOTS
            + [pltpu.VMEM((tile, tile), BF16)] * N_P_SLOTS + [pltpu.VMEM((1, tile), F32)] * N_P_SLOTS)


def _pipelined_maps(n_maps, scores, probs, accumulate, stage):
    nt, npb = N_T_SLOTS, N_P_SLOTS
    t_slots, x_slots = stage[:nt], stage[nt:2 * nt]
    p_slots, a_slots = stage[2 * nt:2 * nt + npb], stage[2 * nt + npb:]

    def stage_scores(k):
        s = scores(k)
        t_slots[k % nt][...] = s
        x_slots[k % nt][...] = jnp.max(s, axis=0, keepdims=True)

    def stage_probs(k):
        probs(k, t_slots[k % nt], x_slots[k % nt], p_slots[k % npb], a_slots[k % npb])

    stage_scores(0)
    if n_maps > 1:
        stage_scores(1)
    stage_probs(0)
    for k in range(n_maps):
        if k + 2 < n_maps:
            stage_scores(k + 2)
        if k + 1 < n_maps:
            stage_probs(k + 1)
        accumulate(k, p_slots[k % npb], a_slots[k % npb])


def _diff_body(q_ref, k_ref, vt_ref, lq1_ref, lk1_ref, lq2_ref, lk2_ref, gain_ref, o_ref,
               qa_scr, kx_scr, dneg_scr, m_scr, l_scr, acc_scr, *stage,
               tile, slopes, lam_init):
    i = pl.program_id(1)
    kpos, qpos = _key_query_iotas(tile)

    @pl.when((pl.program_id(0) == 0) & (i == 0))
    def _():
        _init_alibi(slopes, dneg_scr, kx_scr, qa_scr, tile, 2)

    _store_query_maps(qa_scr, _scaled_transpose(q_ref[0], 1.0), 2 * H_A, DQK_A)
    m_scr[...] = jnp.full(m_scr.shape, NEG_INF, F32)
    l_scr[...] = jnp.zeros(l_scr.shape, F32)
    acc_scr[...] = jnp.zeros(acc_scr.shape, F32)

    def block(j, diag):
        k0 = pl.multiple_of(j * tile, tile)
        kb = k_ref[0, pl.ds(k0, tile), :]
        vtb = vt_ref[0, j]
        off = ((i - j) * tile).astype(F32)

        def scores(idx):
            h, pair = idx // 2, idx // 2
            keys = jnp.concatenate([kb[:, pair * LANES:(pair + 1) * LANES], kx_scr[h]], axis=1)
            s = _dot_row_chunks(keys, qa_scr[idx])
            if diag:
                allowed = (kpos // CHUNK) <= (qpos // CHUNK)
                return jnp.where(allowed, s + dneg_scr[h], MASKED)
            return s

        def probs(idx, t_ref, tmax_ref, p_ref, a_ref):
            shift = 0.0 if diag else (slopes[idx // 2] * LOG2E) * off
            _softmax_probs(shift, t_ref, tmax_ref, p_ref, a_ref, m_scr.at[idx], l_scr.at[idx])

        def accumulate(idx, p_ref, a_ref):
            h = idx // 2
            _accumulate(vtb[h * DV_A:(h + 1) * DV_A, :], p_ref, a_ref, acc_scr.at[idx])

        _pipelined_maps(2 * H_A, scores, probs, accumulate, stage)

    def full_block(j, carry):
        block(j, False)
        return carry

    lax.fori_loop(0, i, full_block, 0)
    block(i, True)

    lam = (jnp.exp(jnp.sum(lq1_ref[...] * lk1_ref[...], axis=-1, keepdims=True))
           - jnp.exp(jnp.sum(lq2_ref[...] * lk2_ref[...], axis=-1, keepdims=True)) + lam_init)
    gain = gain_ref[...] * (1.0 - lam_init)
    outs = []
    for h in range(H_A):
        o = acc_scr[2 * h] / l_scr[2 * h] - lam * (acc_scr[2 * h + 1] / l_scr[2 * h + 1])
        o = o * lax.rsqrt(jnp.mean(o * o, axis=0, keepdims=True) + RMS_EPS)
        outs.append((o * gain).T)
    o_ref[0] = jnp.concatenate(outs, axis=-1).astype(BF16)


def _diff_attention(proj, vt, lq1, lk1, lq2, lk2, gain, slopes, lam_init):
    b, s, _ = proj.shape
    tile = min(ATT_TILE, s)
    nkb = s // tile
    w = H_A * DV_A
    vec = lambda a: a.reshape(1, -1).astype(F32)
    small = lambda n: pl.BlockSpec((1, n), lambda bi, i: (0, 0))
    return pl.pallas_call(
        functools.partial(_diff_body, tile=tile, slopes=slopes, lam_init=lam_init),
        grid=(b, nkb),
        in_specs=[pl.BlockSpec((1, tile, w), lambda bi, i: (bi, i, 0)),
                  pl.BlockSpec((1, s, w), lambda bi, i: (bi, 0, 1)),
                  pl.BlockSpec((1, nkb, w, tile), lambda bi, i: (bi, 0, 0, 0)),
                  small(DQK_A), small(DQK_A), small(DQK_A), small(DQK_A),
                  pl.BlockSpec((DV_A, 1), lambda bi, i: (0, 0))],
        out_specs=pl.BlockSpec((1, tile, w), lambda bi, i: (bi, i, 0)),
        out_shape=jax.ShapeDtypeStruct((b, s, w), BF16),
        scratch_shapes=[pltpu.VMEM((2 * H_A, 2 * LANES, tile), BF16),
                        pltpu.VMEM((H_A, tile, LANES), BF16),
                        pltpu.VMEM((H_A, tile, tile), F32),
                        pltpu.VMEM((2 * H_A, 1, tile), F32),
                        pltpu.VMEM((2 * H_A, 1, tile), F32),
                        pltpu.VMEM((2 * H_A, DV_A, tile), F32)] + _stage_scratch(tile),
        compiler_params=_params("arbitrary", "arbitrary"),
    )(proj, proj, vt, vec(lq1), vec(lk1), vec(lq2), vec(lk2), gain.reshape(-1, 1).astype(F32))


BAND_CHUNKS = LANES // CHUNK
BAND_WINDOW = (LEFT_CHUNKS + BAND_CHUNKS) * CHUNK


def _band_body(q_ref, k_ref, vt_ref, bias_ref, o_ref, qz_scr):
    i = pl.program_id(1)
    ntile = BAND_WINDOW // LANES

    @pl.when((pl.program_id(0) == 0) & (i == 0))
    def _():
        qz_scr[...] = jnp.zeros(qz_scr.shape, BF16)

    qt = _scaled_transpose(q_ref[0], 1.0)
    for h in range(H_B):
        r0 = (h % 2) * DH_B
        qz_scr[h, r0:r0 + DH_B, :] = qt[h * DH_B:(h + 1) * DH_B, :]

    start = pl.multiple_of(i * LANES, LANES)
    kw = k_ref[0, pl.ds(start, BAND_WINDOW), :]
    vt = vt_ref[0, pl.ds(i, ntile)]
    kk = lax.broadcasted_iota(I32, (BAND_WINDOW, LANES), 0)
    valid = kk >= LEFT_CHUNKS * CHUNK - start
    outs = []
    for h in range(H_B):
        pair = h // 2
        s = _dot_row_chunks(kw[:, pair * LANES:(pair + 1) * LANES], qz_scr[h])
        t = jnp.where(valid, s + bias_ref[h], MASKED)
        p = jnp.exp2(t - jnp.max(t, axis=0, keepdims=True))
        l = jnp.sum(p, axis=0, keepdims=True)
        vth = jnp.concatenate([vt[n, h * DH_B:(h + 1) * DH_B, :] for n in range(ntile)], axis=1)
        outs.append(_dot(vth, p.astype(BF16)) / l)
    o_ref[0] = jnp.concatenate(outs, axis=0).T.astype(BF16)


def _band_bias_table(table):
    bias = _band_bias(table).transpose(0, 2, 1) * LOG2E
    cols = [jnp.pad(bias, ((0, 0), (c * CHUNK, (BAND_CHUNKS - 1 - c) * CHUNK), (0, 0)), constant_values=MASKED)
            for c in range(BAND_CHUNKS)]
    return jnp.concatenate(cols, axis=2)


def _band_attention(proj, kpad, vt_pad, bias):
    b, s, _ = proj.shape
    w = H_B * DH_B
    sp = kpad.shape[1]
    nvt = vt_pad.shape[1]
    return pl.pallas_call(
        _band_body,
        grid=(b, s // LANES),
        in_specs=[pl.BlockSpec((1, LANES, w), lambda bi, i: (bi, i, 6)),
                  pl.BlockSpec((1, sp, w), lambda bi, i: (bi, 0, 0)),
                  pl.BlockSpec((1, nvt, w, LANES), lambda bi, i: (bi, 0, 0, 0)),
                  pl.BlockSpec((H_B, BAND_WINDOW, LANES), lambda bi, i: (0, 0, 0))],
        out_specs=pl.BlockSpec((1, LANES, w), lambda bi, i: (bi, i, 0)),
        out_shape=jax.ShapeDtypeStruct((b, s, w), BF16),
        scratch_shapes=[pltpu.VMEM((H_B, LANES, LANES), BF16)],
        compiler_params=_params("arbitrary", "arbitrary"),
    )(proj, kpad, vt_pad, bias)


def _sortable(x):
    bits = pltpu.bitcast(x + 0.0, I32)
    return bits ^ ((bits >> 31) & 0x7FFFFFFF)


_INT_MIN = -2 ** 31
_VALID_KEY = int(np.array(0.5 * NEG_INF, np.float32).view(np.int32))
_VALID_KEY = _VALID_KEY ^ ((_VALID_KEY >> 31) & 0x7FFFFFFF)


def _dsa_body(qc_ref, kc_ref, vct_ref, qi_ref, kiw_ref, wq_ref, o_ref,
              qa_scr, kx_scr, dneg_scr, qit_scr, key_scr, hi_scr, lo_scr, jstar_scr,
              m_scr, l_scr, acc_scr, *stage,
              tile, topk, slopes, seq):
    i = pl.program_id(1)
    nblk = i + 1
    kpos, qpos = _key_query_iotas(tile)

    @pl.when((pl.program_id(0) == 0) & (i == 0))
    def _():
        _init_alibi(slopes, dneg_scr, kx_scr, qa_scr, tile, 1)

    qit_scr[...] = _scaled_transpose(qi_ref[0], D_IDX ** -0.5)
    wt = wq_ref[0].T
    wrow = [wt[D_IDX + h:D_IDX + h + 1, :] * H_IDX ** -0.5 for h in range(H_IDX)]

    def score_block(j, diag):
        k0 = pl.multiple_of(j * tile, tile)
        kib = kiw_ref[0, pl.ds(k0, tile), :][:, :D_IDX].astype(BF16)
        isc = jnp.zeros((tile, tile), F32)
        for h in range(H_IDX):
            d = _dot(kib, qit_scr[h * D_IDX:(h + 1) * D_IDX, :])
            isc = isc + jnp.maximum(d, 0.0) * wrow[h]
        if diag:
            isc = jnp.where((kpos // CHUNK) <= (qpos // CHUNK), isc, NEG_INF)
        key = _sortable(isc)
        key_scr[pl.ds(k0, tile), :] = key
        hi_scr[pl.ds(k0, tile), :] = (key >> 16).astype(I16)
        lo_scr[pl.ds(k0, tile), :] = ((key & 0xFFFF) - 2 ** 15).astype(I16)

    def score_full(j, carry):
        score_block(j, False)
        return carry

    lax.fori_loop(0, i, score_full, 0)
    score_block(i, True)

    def count(pred):
        def body(j, part):
            k0 = pl.multiple_of(j * tile, tile)
            hit = jnp.where(pred(key_scr[pl.ds(k0, tile), :], kpos + k0), 1.0, 0.0)
            return part + jnp.sum(hit.reshape(tile // SUBLANES, SUBLANES, tile), axis=0)
        part = lax.fori_loop(0, nblk, body, jnp.zeros((SUBLANES, tile), F32))
        return jnp.sum(part, axis=0, keepdims=True)

    kf = float(topk)
    rows16 = 2 * SUBLANES

    pair = 2 if (seq // tile) % 2 == 0 else 1
    span = pair * tile
    lowest16 = jnp.full((tile, tile), -2 ** 15, I16)

    @pl.when(nblk % pair == 1)
    def _():
        pad_rows = pl.ds(pl.multiple_of(nblk * tile, tile), tile)
        hi_scr[pad_rows, :] = lowest16
        lo_scr[pad_rows, :] = lowest16

    def count16(src_ref, pred):
        def body(j, part):
            k0 = pl.multiple_of(j * span, span)
            hit = jnp.where(pred(src_ref[pl.ds(k0, span), :]), jnp.int16(1), jnp.int16(0))
            for g in range(span // rows16):
                part = part + hit[g * rows16:(g + 1) * rows16, :]
            return part
        part = lax.fori_loop(0, (nblk + pair - 1) // pair, body, jnp.zeros((rows16, tile), I16))
        return jnp.sum(part.astype(I32), axis=0, keepdims=True).astype(F32)

    def search16(src_ref, base):
        def bit(p, u):
            cand_u = u | jnp.left_shift(jnp.int32(1), 15 - p)
            cand = (cand_u - 2 ** 15).astype(I16)
            cnt = base + count16(src_ref, lambda v: v >= cand)
            return jnp.where(cnt >= kf, cand_u, u)
        return lax.fori_loop(0, 16, bit, jnp.zeros((1, tile), I32))

    thr_hi = search16(hi_scr, 0.0) - 2 ** 15
    thr_hi16 = thr_hi.astype(I16)
    above = count16(hi_scr, lambda v: v > thr_hi16)

    def mask_low(j, carry):
        k0 = pl.multiple_of(j * tile, tile)
        rows = pl.ds(k0, tile)
        lo_scr[rows, :] = jnp.where(hi_scr[rows, :] == thr_hi16, lo_scr[rows, :], jnp.int16(-2 ** 15))
        return carry

    lax.fori_loop(0, nblk, mask_low, 0)
    thr = jnp.left_shift(thr_hi, 16) + search16(lo_scr, above)

    thr_b = thr
    c_gt = count(lambda key, idx: key > thr_b)
    c_ge = count(lambda key, idx: key >= thr_b)
    tied = (c_ge > kf) & (thr > _VALID_KEY)
    need = kf - c_gt
    jstar_scr[...] = jnp.full((1, tile), seq, I32)

    @pl.when(jnp.max(jnp.where(tied, 1.0, 0.0)) > 0.0)
    def _():
        nbits = max(1, int(math.ceil(math.log2(seq))))

        def index_bit(p, lo):
            cand = lo | jnp.left_shift(jnp.int32(1), nbits - 1 - p)
            cnt = count(lambda key, idx: (key == thr_b) & (idx < cand))
            return jnp.where(cnt < need, cand, lo)

        lo = lax.fori_loop(0, nbits, index_bit, jnp.zeros((1, tile), I32))
        jstar_scr[...] = jnp.where(tied, lo, seq)

    jstar = jstar_scr[...]
    thr2 = jnp.maximum(thr, _VALID_KEY + 1)

    _store_query_maps(qa_scr, _scaled_transpose(qc_ref[0], 1.0), H_C, DH_C)
    m_scr[...] = jnp.full(m_scr.shape, NEG_INF, F32)
    l_scr[...] = jnp.zeros(l_scr.shape, F32)
    acc_scr[...] = jnp.zeros(acc_scr.shape, F32)

    def attend(j, diag):
        k0 = pl.multiple_of(j * tile, tile)
        key = key_scr[pl.ds(k0, tile), :]
        sel = (key > thr2) | ((key == thr2) & (kpos + k0 <= jstar))
        kb = kc_ref[0, pl.ds(k0, tile), :]
        vtb = vct_ref[0, j]
        off = ((i - j) * tile).astype(F32)

        def scores(h):
            pair = h // 2
            keys = jnp.concatenate([kb[:, pair * LANES:(pair + 1) * LANES], kx_scr[h]], axis=1)
            s = _dot_row_chunks(keys, qa_scr[h])
            if diag:
                s = s + dneg_scr[h]
            return jnp.where(sel, s, MASKED)

        def probs(h, t_ref, tmax_ref, p_ref, a_ref):
            shift = 0.0 if diag else (slopes[h] * LOG2E) * off
            _softmax_probs(shift, t_ref, tmax_ref, p_ref, a_ref, m_scr.at[h], l_scr.at[h])

        def accumulate(h, p_ref, a_ref):
            _accumulate(vtb[h * DH_C:(h + 1) * DH_C, :], p_ref, a_ref, acc_scr.at[h])

        _pipelined_maps(H_C, scores, probs, accumulate, stage)

    def attend_full(j, carry):
        attend(j, False)
        return carry

    lax.fori_loop(0, i, attend_full, 0)
    attend(i, True)
    outs = [(acc_scr[h] / l_scr[h]).T for h in range(H_C)]
    o_ref[0] = jnp.concatenate(outs, axis=-1).astype(BF16)


def _dsa_attention(proj, vt, kiw, slopes):
    b, s, _ = proj.shape
    tile = min(ATT_TILE, s)
    nkb = s // tile
    w = H_C * DH_C
    wa = H_A * DV_A
    topk = min(TOPK_MAX, s // 4)
    return pl.pallas_call(
        functools.partial(_dsa_body, tile=tile, topk=topk, slopes=slopes, seq=s),
        grid=(b, nkb),
        in_specs=[pl.BlockSpec((1, tile, w), lambda bi, i: (bi, i, 8)),
                  pl.BlockSpec((1, s, w), lambda bi, i: (bi, 0, 9)),
                  pl.BlockSpec((1, nkb, w, tile), lambda bi, i: (bi, 0, wa // w, 0)),
                  pl.BlockSpec((1, tile, H_IDX * D_IDX), lambda bi, i: (bi, i, 2)),
                  pl.BlockSpec((1, s, LANES), lambda bi, i: (bi, 0, 0)),
                  pl.BlockSpec((1, tile, LANES), lambda bi, i: (bi, i, 0))],
        out_specs=pl.BlockSpec((1, tile, w), lambda bi, i: (bi, i, 0)),
        out_shape=jax.ShapeDtypeStruct((b, s, w), BF16),
        scratch_shapes=[pltpu.VMEM((H_C, 2 * LANES, tile), BF16),
                        pltpu.VMEM((H_C, tile, LANES), BF16),
                        pltpu.VMEM((H_C, tile, tile), F32),
                        pltpu.VMEM((H_IDX * D_IDX, tile), BF16),
                        pltpu.VMEM((s, tile), I32),
                        pltpu.VMEM((s, tile), I16),
                        pltpu.VMEM((s, tile), I16),
                        pltpu.VMEM((1, tile), I32),
                        pltpu.VMEM((H_C, 1, tile), F32),
                        pltpu.VMEM((H_C, 1, tile), F32),
                        pltpu.VMEM((H_C, DH_C, tile), F32)] + _stage_scratch(tile),
        compiler_params=_params("arbitrary", "arbitrary"),
    )(proj, proj, vt, proj, kiw, kiw)


def _xattn_body(q_ref, k_ref, v_ref, o_ref, *, dh):
    outs = []
    for h in range(H_X):
        c = h * dh
        q = q_ref[0, :, c:c + dh] * dh ** -0.5
        s = _dot_nt(q, k_ref[0, :, c:c + dh])
        p = jnp.exp(s - jnp.max(s, axis=-1, keepdims=True))
        l = jnp.sum(p, axis=-1, keepdims=True)
        outs.append((_dot(p.astype(BF16), v_ref[0, :, c:c + dh]) / l).astype(BF16))
    o_ref[0] = jnp.concatenate(outs, axis=-1)


def _xattn_core(q, k, v, tq=512):
    b, s, d = q.shape
    n_mem = k.shape[1]
    tq = min(tq, s)
    return pl.pallas_call(
        functools.partial(_xattn_body, dh=d // H_X),
        grid=(b, s // tq),
        in_specs=[pl.BlockSpec((1, tq, d), lambda bi, i: (bi, i, 0)),
                  pl.BlockSpec((1, n_mem, d), lambda bi, i: (bi, 0, 0)),
                  pl.BlockSpec((1, n_mem, d), lambda bi, i: (bi, 0, 0))],
        out_specs=pl.BlockSpec((1, tq, d), lambda bi, i: (bi, i, 0)),
        out_shape=jax.ShapeDtypeStruct((b, s, d), BF16),
        compiler_params=_params("parallel", "parallel"),
    )(q, k, v)


def _router_body(x_ref, r_ref, g_ref):
    logits = _dot(x_ref[...], r_ref[...])
    lane = lax.broadcasted_iota(I32, logits.shape, 1).astype(F32)
    big = float(LANES)
    lg = jnp.where(lane < N_EXPERTS, logits, -jnp.inf)
    m1 = jnp.max(lg, axis=-1, keepdims=True)
    i1 = jnp.min(jnp.where(lg == m1, lane, big), axis=-1, keepdims=True)
    lg2 = jnp.where(lane == i1, -jnp.inf, lg)
    m2 = jnp.max(lg2, axis=-1, keepdims=True)
    i2 = jnp.min(jnp.where(lg2 == m2, lane, big), axis=-1, keepdims=True)
    e = jnp.exp(m2 - m1)
    den = 1.0 + e
    g_ref[...] = jnp.where(lane == i1, 1.0 / den, 0.0) + jnp.where(lane == i2, e / den, 0.0)


def _router_gates(x, router, tm=1024):
    m, d = x.shape
    tm = min(tm, m)
    rpad = jnp.zeros((d, LANES), BF16).at[:, :N_EXPERTS].set(router.astype(BF16))
    return pl.pallas_call(
        _router_body,
        grid=(m // tm,),
        in_specs=[pl.BlockSpec((tm, d), lambda i: (i, 0)),
                  pl.BlockSpec((d, LANES), lambda i: (0, 0))],
        out_specs=pl.BlockSpec((tm, LANES), lambda i: (i, 0)),
        out_shape=jax.ShapeDtypeStruct((m, LANES), F32),
        compiler_params=_params("parallel"),
    )(x, rpad)


MOE_BLOCK = 896
MOE_CHUNK = 128
MOE_GROUP = 2
MOE_FF_TILE = 1792


def _moe_body(x_ref, gate_ref, tri_ref, w1_ref, w3_ref, w2_ref, o_ref,
              xs_scr, acc_scr, rcol_scr, rrow_scr, cnt_smem, *, n_tokens):
    e = pl.program_id(1)
    f = pl.program_id(2)
    nf = pl.num_programs(2)
    tb = x_ref.shape[0]
    ch = MOE_CHUNK
    gw = MOE_GROUP * ch

    tok0 = pl.program_id(0) * tb
    valid_col = lax.broadcasted_iota(I32, (tb, 1), 0) + tok0 < n_tokens

    @pl.when((e == 0) & (f == 0))
    def _():
        o_ref[...] = jnp.zeros(o_ref.shape, F32)
        member = jnp.where((gate_ref[...] > 0.0) & valid_col, 1.0, 0.0)
        rank = _dot(tri_ref[...], member.astype(BF16))
        rcol = jnp.where(member > 0.0, rank, -1.0)
        rcol_scr[...] = rcol
        rrow_scr[...] = rcol.T
        counts = jnp.sum(member, axis=0, keepdims=True)
        for ee in range(N_EXPERTS):
            cnt_smem[ee] = counts[0, ee].astype(I32)

    count = cnt_smem[e]
    single = count <= gw
    nchunk = jnp.where(single, MOE_GROUP, (count + (ch - 1)) // ch)

    @pl.when(f == 0)
    def _():
        acc_scr[...] = jnp.zeros(acc_scr.shape, F32)
        sub = lax.broadcasted_iota(I32, (SUBLANES, tb), 0)
        r_row = jnp.sum(jnp.where(sub == e, rrow_scr[0:SUBLANES, :], 0.0), axis=0, keepdims=True)
        xz = jnp.where(valid_col, x_ref[...], jnp.zeros((), BF16))

        def gather_rows(r0, rows):
            row_id = lax.broadcasted_iota(I32, (rows, tb), 0).astype(F32) + jnp.asarray(r0, F32)
            sel = jnp.where(row_id == r_row, 1.0, 0.0).astype(BF16)
            xs_scr[pl.ds(r0, rows), :] = _dot(sel, xz).astype(BF16)

        @pl.when(single)
        def _():
            gather_rows(0, gw)

        @pl.when(jnp.logical_not(single))
        def _():
            def gather(c, carry):
                gather_rows(pl.multiple_of(c * ch, ch), ch)
                return carry
            lax.fori_loop(0, nchunk, gather, 0)

    def ffn_rows(r0, rows):
        xs = xs_scr[pl.ds(r0, rows), :]
        a = _dot(xs, w1_ref[0])
        b = _dot(xs, w3_ref[0])
        acc_scr[pl.ds(r0, rows), :] += _dot((_silu(a) * b).astype(BF16), w2_ref[0])

    @pl.when(single)
    def _():
        ffn_rows(0, gw)

    @pl.when(jnp.logical_not(single))
    def _():
        def ffn(c, carry):
            ffn_rows(pl.multiple_of(c * ch, ch), ch)
            return carry
        lax.fori_loop(0, nchunk, ffn, 0)

    @pl.when(f == nf - 1)
    def _():
        pick = lax.broadcasted_iota(I32, (tb, LANES), 1) == e
        r_col = jnp.sum(jnp.where(pick, rcol_scr[...], 0.0), axis=1, keepdims=True)
        g_col = jnp.sum(jnp.where(pick & valid_col, gate_ref[...], 0.0), axis=1, keepdims=True)
        col_id = lax.broadcasted_iota(I32, (tb, gw), 1).astype(F32)

        def combine(gi, carry):
            r0 = pl.multiple_of(gi * gw, gw)
            selt = jnp.where(col_id + r0.astype(F32) == r_col, 1.0, 0.0).astype(BF16)
            y = acc_scr[pl.ds(r0, gw), :]
            y_hi = y.astype(BF16)
            y_lo = (y - y_hi.astype(F32)).astype(BF16)
            o_ref[...] += g_col * (_dot(selt, y_hi) + _dot(selt, y_lo))
            return carry

        lax.fori_loop(0, (nchunk + (MOE_GROUP - 1)) // MOE_GROUP, combine, 0)


def _moe_experts(x, gates, w1, w3, w2):
    m, d = x.shape
    ne, _, ff = w1.shape
    tb = min(MOE_BLOCK, m)
    tf = MOE_FF_TILE if ff % MOE_FF_TILE == 0 else ff
    gw = MOE_GROUP * MOE_CHUNK
    cap = pl.cdiv(tb, gw) * gw
    assert tb % LANES == 0 and ne <= SUBLANES
    tri = jnp.asarray(np.tril(np.ones((tb, tb), np.float32), -1), BF16)
    row = lambda i, e, f: (i, 0)
    return pl.pallas_call(
        functools.partial(_moe_body, n_tokens=m),
        grid=(pl.cdiv(m, tb), ne, ff // tf),
        in_specs=[pl.BlockSpec((tb, d), row),
                  pl.BlockSpec((tb, LANES), row),
                  pl.BlockSpec((tb, tb), lambda i, e, f: (0, 0)),
                  pl.BlockSpec((1, d, tf), lambda i, e, f: (e, 0, f)),
                  pl.BlockSpec((1, d, tf), lambda i, e, f: (e, 0, f)),
                  pl.BlockSpec((1, tf, d), lambda i, e, f: (e, f, 0))],
        out_specs=pl.BlockSpec((tb, d), row),
        out_shape=jax.ShapeDtypeStruct((m, d), F32),
        scratch_shapes=[pltpu.VMEM((cap, d), BF16),
                        pltpu.VMEM((cap, d), F32),
                        pltpu.VMEM((tb, LANES), F32),
                        pltpu.VMEM((LANES, tb), F32),
                        pltpu.SMEM((N_EXPERTS,), I32)],
        compiler_params=_params("parallel", "arbitrary", "arbitrary"),
    )(x, gates, tri, w1, w3, w2)


def _res_ln_body(res_ref, y_ref, g_ref, b_ref, of_ref, ob_ref, *, alpha):
    out = _layer_norm_rows(alpha * res_ref[...] + y_ref[...], g_ref[...], b_ref[...])
    of_ref[...] = out
    ob_ref[...] = out.astype(BF16)


def _residual_ln(res, y, g, b, alpha, tm=512):
    m, d = res.shape
    tm = min(tm, m)
    row = pl.BlockSpec((tm, d), lambda i: (i, 0))
    vec = pl.BlockSpec((1, d), lambda i: (0, 0))
    return pl.pallas_call(
        functools.partial(_res_ln_body, alpha=alpha),
        grid=(m // tm,),
        in_specs=[row, row, vec, vec],
        out_specs=[row, row],
        out_shape=[jax.ShapeDtypeStruct((m, d), F32), jax.ShapeDtypeStruct((m, d), BF16)],
        compiler_params=_params("parallel"),
    )(res, y, g.reshape(1, d), b.reshape(1, d))


def _band_bias(table):
    band = (LEFT_CHUNKS + 1) * CHUNK
    left = LEFT_CHUNKS * CHUNK
    rev = table.astype(F32)[:, ::-1]
    pad = left - REL_MAX + CHUNK - 1
    n_rev = band - 1 - (left - REL_MAX) + 1
    assert n_rev <= 2 * REL_MAX + 1
    ext = jnp.concatenate([jnp.broadcast_to(rev[:, :1], (rev.shape[0], pad)), rev[:, :n_rev]], axis=1)
    n_ext = ext.shape[1]
    period = jnp.concatenate([ext[:, CHUNK - 1:], jnp.zeros((ext.shape[0], 1), F32), ext[:, :CHUNK - 1]], axis=1)
    skew = jnp.tile(period, (1, CHUNK))[:, :CHUNK * n_ext].reshape(-1, CHUNK, n_ext)
    return skew[:, :, :band]


def kernel(x, mem, w_in, lam_q1, lam_k1, lam_q2, lam_k2, diff_norm_g, rel_bias, w_out,
           ln1_g, ln1_b, xattn_wq, xattn_wk, xattn_wv, xattn_wo, ln2_g, ln2_b,
           ffn_w1, ffn_w3, ffn_w2, moe_router, moe_w1, moe_w3, moe_w2, ln3_g, ln3_b):
    b, s, d = x.shape
    depth = w_in.shape[0]
    t = b * s
    alpha = (2.0 * depth) ** 0.25
    slopes_a, slopes_c = _alibi_slopes()
    n_main = 3 * H_A * DV_A + 3 * H_B * DH_B + 3 * H_C * DH_C + H_IDX * D_IDX
    n_tail = D_IDX + H_IDX
    wa, wb, wc = H_A * DV_A, H_B * DH_B, H_C * DH_C
    left = LEFT_CHUNKS * CHUNK

    h = x.reshape(t, d)
    hb = h.astype(BF16)
    memb = mem.reshape(-1, d).astype(BF16)
    for l in range(depth):
        wl = w_in[l]
        o_b, o_c, o_i = 3 * wa, 3 * wa + 3 * wb, 3 * wa + 3 * wb + 3 * wc
        w_main = jnp.concatenate(
            [wl[:, :wa] * (DQK_A ** -0.5 * LOG2E), wl[:, wa:2 * wa], wl[:, o_i:o_i + H_IDX * D_IDX],
             wl[:, o_b:o_b + wb] * (DH_B ** -0.5 * LOG2E), wl[:, o_b + wb:o_b + 2 * wb],
             wl[:, o_c:o_c + wc] * (DH_C ** -0.5 * LOG2E), wl[:, o_c + wc:o_c + 2 * wc]], axis=1).astype(BF16)
        w_tail = jnp.zeros((d, LANES), BF16).at[:, :n_tail].set(wl[:, n_main:].astype(BF16))
        proj = _matmul(hb, w_main, BF16, tn=w_main.shape[1] // 2).reshape(b, s, -1)
        kiw = _matmul(hb, w_tail, F32, tn=LANES).reshape(b, s, LANES)
        w_v = jnp.concatenate([wl[:, 2 * wa:3 * wa], wl[:, o_c + 2 * wc:o_c + 3 * wc]], axis=1).astype(BF16)
        tile = min(ATT_TILE, s)
        vt = _matmul_t_tiles(hb, w_v, tile).reshape(b, s // tile, wa + wc, tile)
        lam_init = 0.8 - 0.6 * math.exp(-0.3 * l)
        ya = _diff_attention(proj, vt, lam_q1[l], lam_k1[l], lam_q2[l], lam_k2[l], diff_norm_g[l],
                             slopes_a, lam_init)
        kpad = jnp.pad(proj[:, :, 3 * wa + wb:3 * wa + 2 * wb], ((0, 0), (left, 0), (0, 0)))
        vbt = _matmul_t_tiles(hb, wl[:, o_b + 2 * wb:o_b + 3 * wb].astype(BF16), LANES)
        vbt = jnp.pad(vbt.reshape(b, s // LANES, wb, LANES), ((0, 0), (left // LANES, 0), (0, 0), (0, 0)))
        yb = _band_attention(proj, kpad, vbt, _band_bias_table(rel_bias[l]))
        yc = _dsa_attention(proj, vt, kiw, slopes_c)
        wo = w_out[l].astype(BF16)
        h, hb = _matmul_ln(
            [ya.reshape(t, wa), yb.reshape(t, wb), yc.reshape(t, -1)],
            [wo[:wa], wo[wa:wa + wb], wo[wa + wb:]],
            h, ln1_g[l], ln1_b[l], alpha)
        q = _matmul(hb, xattn_wq[l].astype(BF16), BF16, tn=d).reshape(b, s, d)
        km = _matmul(memb, xattn_wk[l].astype(BF16), BF16).reshape(b, -1, d)
        vm = _matmul(memb, xattn_wv[l].astype(BF16), BF16).reshape(b, -1, d)
        xo = _xattn_core(q, km, vm).reshape(t, d)
        h, hb = _matmul_ln([xo], [xattn_wo[l].astype(BF16)], h, ln2_g[l], ln2_b[l], alpha)
        if l % 2 == 0:
            hid = _swiglu_hidden(hb, ffn_w1[l // 2].astype(BF16), ffn_w3[l // 2].astype(BF16))
            h, hb = _matmul_ln([hid], [ffn_w2[l // 2].astype(BF16)], h, ln3_g[l], ln3_b[l], alpha)
        else:
            gates = _router_gates(hb, moe_router[l // 2])
            y = _moe_experts(hb, gates, moe_w1[l // 2].astype(BF16), moe_w3[l // 2].astype(BF16),
                             moe_w2[l // 2].astype(BF16))
            h, hb = _residual_ln(h, y, ln3_g[l], ln3_b[l], alpha)
    return h.reshape(b, s, d)
```

```python
import functools
import math

import numpy as np
import jax
import jax.numpy as jnp
from jax import lax
from jax.experimental import pallas as pl
from jax.experimental.pallas import tpu as pltpu

F32 = jnp.float32
BF16 = jnp.bfloat16
I32 = jnp.int32
I16 = jnp.int16

CHUNK = 64
H_A, DQK_A = 4, 64
DV_A = 2 * DQK_A
H_B, DH_B = 4, 64
LEFT_CHUNKS = 8
REL_MAX = 128
H_C, DH_C = 4, 64
H_IDX, D_IDX = 8, 64
TOPK_MAX = 256
H_X = 4
N_EXPERTS = 8
LN_EPS = 1e-5
RMS_EPS = 1e-6
NEG_INF = -1e30
MASKED = -2e30

LANES = 128
SUBLANES = 8
VMEM_LIMIT = 56 * 1024 * 1024
ATT_TILE = 256

_NT = (((1,), (1,)), ((), ()))


def _alibi_slopes():
    n = H_A + H_C
    s = (2.0 ** (-8.0 * np.arange(1, n + 1) / n)).astype(np.float32)
    return [float(v) for v in s[0::2]], [float(v) for v in s[1::2]]


def _params(*sem):
    return pltpu.CompilerParams(dimension_semantics=sem, vmem_limit_bytes=VMEM_LIMIT)


def _dot(a, b):
    return jnp.dot(a, b, preferred_element_type=F32)


SCORE_ROW_CHUNK = 128


def _dot_row_chunks(a, b):
    rows = a.shape[0]
    step = min(SCORE_ROW_CHUNK, rows)
    return jnp.concatenate([_dot(a[r:r + step], b) for r in range(0, rows, step)], axis=0)


def _dot_nt(a, b):
    return lax.dot_general(a, b, _NT, preferred_element_type=F32)


def _layer_norm_rows(y, g, b):
    mu = jnp.mean(y, axis=-1, keepdims=True)
    d = y - mu
    var = jnp.mean(d * d, axis=-1, keepdims=True)
    return d * lax.rsqrt(var + LN_EPS) * g + b


def _silu(a):
    return a / (1.0 + jnp.exp(-a))


def _mm_body(x_ref, w_ref, o_ref):
    o_ref[...] = _dot(x_ref[...], w_ref[...]).astype(o_ref.dtype)


def _matmul(x, w, out_dtype, tm=1024, tn=512):
    m, k = x.shape
    n = w.shape[1]
    tm, tn = min(tm, m), min(tn, n)
    return pl.pallas_call(
        _mm_body,
        grid=(m // tm, n // tn),
        in_specs=[pl.BlockSpec((tm, k), lambda i, j: (i, 0)),
                  pl.BlockSpec((k, tn), lambda i, j: (0, j))],
        out_specs=pl.BlockSpec((tm, tn), lambda i, j: (i, j)),
        out_shape=jax.ShapeDtypeStruct((m, n), out_dtype),
        compiler_params=_params("parallel", "parallel"),
    )(x, w)


def _side_proj_body(x_ref, w_ref, vt_ref, vbt_ref, tail_ref):
    y = _dot(x_ref[...], w_ref[...])
    n_v, n_vb = vt_ref.shape[1], vbt_ref.shape[1]
    tile, tile_b = vt_ref.shape[2], vbt_ref.shape[2]
    for r in range(vt_ref.shape[0]):
        vt_ref[r] = y[r * tile:(r + 1) * tile, :n_v].T.astype(BF16)
    for r in range(vbt_ref.shape[0]):
        vbt_ref[r] = y[r * tile_b:(r + 1) * tile_b, n_v:n_v + n_vb].T.astype(BF16)
    tail_ref[...] = y[:, n_v + n_vb:]


def _side_projections(x, w, n_v, tile, n_vb, tile_b, tm=1024):
    m, k = x.shape
    n = w.shape[1]
    n_tail = n - n_v - n_vb
    tm = min(tm, m)
    return pl.pallas_call(
        _side_proj_body,
        grid=(m // tm,),
        in_specs=[pl.BlockSpec((tm, k), lambda i: (i, 0)),
                  pl.BlockSpec((k, n), lambda i: (0, 0))],
        out_specs=[pl.BlockSpec((tm // tile, n_v, tile), lambda i: (i, 0, 0)),
                   pl.BlockSpec((tm // tile_b, n_vb, tile_b), lambda i: (i, 0, 0)),
                   pl.BlockSpec((tm, n_tail), lambda i: (i, 0))],
        out_shape=[jax.ShapeDtypeStruct((m // tile, n_v, tile), BF16),
                   jax.ShapeDtypeStruct((m // tile_b, n_vb, tile_b), BF16),
                   jax.ShapeDtypeStruct((m, n_tail), F32)],
        compiler_params=_params("parallel"),
    )(x, w)


def _mm_ln_body(*refs, n_in, alpha):
    xs, ws = refs[:n_in], refs[n_in:2 * n_in]
    res_ref, g_ref, b_ref, of_ref, ob_ref = refs[2 * n_in:]
    acc = _dot(xs[0][...], ws[0][...])
    for x_ref, w_ref in zip(xs[1:], ws[1:]):
        acc = acc + _dot(x_ref[...], w_ref[...])
    out = _layer_norm_rows(alpha * res_ref[...] + acc, g_ref[...], b_ref[...])
    of_ref[...] = out
    ob_ref[...] = out.astype(BF16)


def _matmul_ln(xs, ws, res, g, b, alpha):
    m, d = res.shape
    k_total = sum(x.shape[1] for x in xs)
    tm = min(1024 if k_total <= 1024 else 512, m)
    n_in = len(xs)
    in_specs = ([pl.BlockSpec((tm, x.shape[1]), lambda i: (i, 0)) for x in xs]
                + [pl.BlockSpec(w.shape, lambda i: (0, 0)) for w in ws]
                + [pl.BlockSpec((tm, d), lambda i: (i, 0)),
                   pl.BlockSpec((1, d), lambda i: (0, 0)),
                   pl.BlockSpec((1, d), lambda i: (0, 0))])
    return pl.pallas_call(
        functools.partial(_mm_ln_body, n_in=n_in, alpha=alpha),
        grid=(m // tm,),
        in_specs=in_specs,
        out_specs=[pl.BlockSpec((tm, d), lambda i: (i, 0)),
                   pl.BlockSpec((tm, d), lambda i: (i, 0))],
        out_shape=[jax.ShapeDtypeStruct((m, d), F32), jax.ShapeDtypeStruct((m, d), BF16)],
        compiler_params=_params("parallel"),
    )(*xs, *ws, res, g.reshape(1, d), b.reshape(1, d))


def _ffn_ln_body(x_ref, w1_ref, w3_ref, w2_ref, res_ref, g_ref, b_ref, of_ref, ob_ref, acc_ref, *, alpha):
    f = pl.program_id(1)
    x = x_ref[...]
    a = _dot(x, w1_ref[...])
    b = _dot(x, w3_ref[...])
    y = _dot((_silu(a) * b).astype(BF16), w2_ref[...])

    @pl.when(f == 0)
    def _():
        acc_ref[...] = y

    @pl.when(f > 0)
    def _():
        acc_ref[...] += y

    @pl.when(f == pl.num_programs(1) - 1)
    def _():
        out = _layer_norm_rows(alpha * res_ref[...] + acc_ref[...], g_ref[...], b_ref[...])
        of_ref[...] = out
        ob_ref[...] = out.astype(BF16)


def _ffn_ln(x, w1, w3, w2, res, g, b, alpha, tm=512, tf=1408):
    m, d = x.shape
    ff = w1.shape[1]
    tm = min(tm, m)
    if ff % tf:
        tf = ff
    row = pl.BlockSpec((tm, d), lambda i, f: (i, 0))
    vec = pl.BlockSpec((1, d), lambda i, f: (0, 0))
    return pl.pallas_call(
        functools.partial(_ffn_ln_body, alpha=alpha),
        grid=(m // tm, ff // tf),
        in_specs=[row,
                  pl.BlockSpec((d, tf), lambda i, f: (0, f)),
                  pl.BlockSpec((d, tf), lambda i, f: (0, f)),
                  pl.BlockSpec((tf, d), lambda i, f: (f, 0)),
                  row, vec, vec],
        out_specs=[row, row],
        out_shape=[jax.ShapeDtypeStruct((m, d), F32), jax.ShapeDtypeStruct((m, d), BF16)],
        scratch_shapes=[pltpu.VMEM((tm, d), F32)],
        compiler_params=_params("parallel", "arbitrary"),
    )(x, w1, w3, w2, res, g.reshape(1, d), b.reshape(1, d))


def _key_query_iotas(tile):
    kpos = lax.broadcasted_iota(I32, (tile, tile), 0)
    qpos = lax.broadcasted_iota(I32, (tile, tile), 1)
    return kpos, qpos


def _scaled_transpose(x, scale):
    return (x.astype(F32) * scale).T.astype(BF16)


LOG2E = math.log2(math.e)
ALIBI_PARTS = 3
ALIBI_ROWS = 16


def _bf16_parts(x):
    parts, r = [], np.float32(x)
    for _ in range(ALIBI_PARTS):
        p = np.float32(r.astype(jnp.bfloat16))
        parts.append(float(p))
        r = np.float32(r - p)
    return parts


def _alibi_key_cols(tile, parts):
    lane = lax.broadcasted_iota(I32, (tile, LANES), 1)
    kr = lax.broadcasted_iota(I32, (tile, LANES), 0).astype(F32)
    v = jnp.where((lane >= ALIBI_PARTS) & (lane < 2 * ALIBI_PARTS), kr, 0.0)
    for n, c in enumerate(parts):
        v = jnp.where(lane == n, c, v)
    return v.astype(BF16)


def _alibi_query_rows(tile, parts):
    row = lax.broadcasted_iota(I32, (ALIBI_ROWS, tile), 0)
    qr = lax.broadcasted_iota(I32, (ALIBI_ROWS, tile), 1).astype(F32)
    v = jnp.where(row < ALIBI_PARTS, -qr, 0.0)
    for n, c in enumerate(parts):
        v = jnp.where(row == ALIBI_PARTS + n, c, v)
    return v.astype(BF16)


def _init_alibi(slopes, dneg_scr, kx_scr, qa_scr, tile, maps_per_head):
    kpos, qpos = _key_query_iotas(tile)
    rel = (qpos - kpos).astype(F32)
    qa_scr[...] = jnp.zeros(qa_scr.shape, BF16)
    for h, slope in enumerate(slopes):
        parts = _bf16_parts(slope * LOG2E)
        dneg_scr[h] = 2.0 * jnp.minimum(np.float32(slope * LOG2E) * rel, 0.0)
        kx_scr[h] = _alibi_key_cols(tile, parts)
        for mm in range(maps_per_head):
            qa_scr[h * maps_per_head + mm, LANES:LANES + ALIBI_ROWS, :] = _alibi_query_rows(tile, parts)


def _store_query_maps(qa_scr, qt, n_maps, dh):
    for idx in range(n_maps):
        r0 = (idx % 2) * dh
        qa_scr[idx, r0:r0 + dh, :] = qt[idx * dh:(idx + 1) * dh, :]


def _softmax_probs(c, t_ref, tmax_ref, p_ref, a_ref, m_ref, l_ref):
    m_old = m_ref[...]
    m_new = jnp.maximum(m_old, tmax_ref[...] - c)
    p = jnp.exp2(t_ref[...] - (m_new + c))
    a = jnp.exp2(m_old - m_new)
    l_ref[...] = a * l_ref[...] + jnp.sum(p, axis=0, keepdims=True)
    m_ref[...] = m_new
    a_ref[...] = a
    p_ref[...] = p.astype(BF16)


def _accumulate(vt, p_ref, a_ref, acc_ref):
    acc_ref[...] = a_ref[...] * acc_ref[...] + _dot(vt, p_ref[...])


N_T_SLOTS, N_P_SLOTS = 3, 2


def _stage_scratch(tile):
    return ([pltpu.VMEM((tile, tile), F32)] * N_T_SLOTS + [pltpu.VMEM((1, tile), F32)] * N_T_SLOTS
            + [pltpu.VMEM((tile, tile), BF16)] * N_P_SLOTS + [pltpu.VMEM((1, tile), F32)] * N_P_SLOTS)


def _pipelined_maps(n_maps, scores, probs, accumulate, stage):
    nt, npb = N_T_SLOTS, N_P_SLOTS
    t_slots, x_slots = stage[:nt], stage[nt:2 * nt]
    p_slots, a_slots = stage[2 * nt:2 * nt + npb], stage[2 * nt + npb:]

    def stage_scores(k):
        s = scores(k)
        t_slots[k % nt][...] = s
        x_slots[k % nt][...] = jnp.max(s, axis=0, keepdims=True)

    def stage_probs(k):
        probs(k, t_slots[k % nt], x_slots[k % nt], p_slots[k % npb], a_slots[k % npb])

    stage_scores(0)
    if n_maps > 1:
        stage_scores(1)
    stage_probs(0)
    for k in range(n_maps):
        if k + 2 < n_maps:
            stage_scores(k + 2)
        if k + 1 < n_maps:
            stage_probs(k + 1)
        accumulate(k, p_slots[k % npb], a_slots[k % npb])


def _diff_body(q_ref, k_ref, vt_ref, lq1_ref, lk1_ref, lq2_ref, lk2_ref, gain_ref, o_ref,
               qa_scr, kx_scr, dneg_scr, m_scr, l_scr, acc_scr, *stage,
               tile, slopes, lam_init):
    i = pl.program_id(1)
    kpos, qpos = _key_query_iotas(tile)

    @pl.when((pl.program_id(0) == 0) & (i == 0))
    def _():
        _init_alibi(slopes, dneg_scr, kx_scr, qa_scr, tile, 2)

    _store_query_maps(qa_scr, _scaled_transpose(q_ref[0], 1.0), 2 * H_A, DQK_A)
    m_scr[...] = jnp.full(m_scr.shape, NEG_INF, F32)
    l_scr[...] = jnp.zeros(l_scr.shape, F32)
    acc_scr[...] = jnp.zeros(acc_scr.shape, F32)

    def block(j, diag):
        k0 = pl.multiple_of(j * tile, tile)
        kb = k_ref[0, pl.ds(k0, tile), :]
        vtb = vt_ref[0, j]
        off = ((i - j) * tile).astype(F32)

        def scores(idx):
            h, pair = idx // 2, idx // 2
            keys = jnp.concatenate([kb[:, pair * LANES:(pair + 1) * LANES], kx_scr[h]], axis=1)
            s = _dot_row_chunks(keys, qa_scr[idx])
            if diag:
                allowed = (kpos // CHUNK) <= (qpos // CHUNK)
                return jnp.where(allowed, s + dneg_scr[h], MASKED)
            return s

        def probs(idx, t_ref, tmax_ref, p_ref, a_ref):
            shift = 0.0 if diag else (slopes[idx // 2] * LOG2E) * off
            _softmax_probs(shift, t_ref, tmax_ref, p_ref, a_ref, m_scr.at[idx], l_scr.at[idx])

        def accumulate(idx, p_ref, a_ref):
            h = idx // 2
            _accumulate(vtb[h * DV_A:(h + 1) * DV_A, :], p_ref, a_ref, acc_scr.at[idx])

        _pipelined_maps(2 * H_A, scores, probs, accumulate, stage)

    def full_block(j, carry):
        block(j, False)
        return carry

    lax.fori_loop(0, i, full_block, 0)
    block(i, True)

    lam = (jnp.exp(jnp.sum(lq1_ref[...] * lk1_ref[...], axis=-1, keepdims=True))
           - jnp.exp(jnp.sum(lq2_ref[...] * lk2_ref[...], axis=-1, keepdims=True)) + lam_init)
    gain = gain_ref[...] * (1.0 - lam_init)
    outs = []
    for h in range(H_A):
        o = acc_scr[2 * h] / l_scr[2 * h] - lam * (acc_scr[2 * h + 1] / l_scr[2 * h + 1])
        o = o * lax.rsqrt(jnp.mean(o * o, axis=0, keepdims=True) + RMS_EPS)
        outs.append((o * gain).T)
    o_ref[0] = jnp.concatenate(outs, axis=-1).astype(BF16)


def _diff_attention(proj, vt, lq1, lk1, lq2, lk2, gain, slopes, lam_init):
    b, s, _ = proj.shape
    tile = min(ATT_TILE, s)
    nkb = s // tile
    w = H_A * DV_A
    vec = lambda a: a.reshape(1, -1).astype(F32)
    small = lambda n: pl.BlockSpec((1, n), lambda bi, i: (0, 0))
    return pl.pallas_call(
        functools.partial(_diff_body, tile=tile, slopes=slopes, lam_init=lam_init),
        grid=(b, nkb),
        in_specs=[pl.BlockSpec((1, tile, w), lambda bi, i: (bi, i, 0)),
                  pl.BlockSpec((1, s, w), lambda bi, i: (bi, 0, 1)),
                  pl.BlockSpec((1, nkb, w, tile), lambda bi, i: (bi, 0, 0, 0)),
                  small(DQK_A), small(DQK_A), small(DQK_A), small(DQK_A),
                  pl.BlockSpec((DV_A, 1), lambda bi, i: (0, 0))],
        out_specs=pl.BlockSpec((1, tile, w), lambda bi, i: (bi, i, 0)),
        out_shape=jax.ShapeDtypeStruct((b, s, w), BF16),
        scratch_shapes=[pltpu.VMEM((2 * H_A, 2 * LANES, tile), BF16),
                        pltpu.VMEM((H_A, tile, LANES), BF16),
                        pltpu.VMEM((H_A, tile, tile), F32),
                        pltpu.VMEM((2 * H_A, 1, tile), F32),
                        pltpu.VMEM((2 * H_A, 1, tile), F32),
                        pltpu.VMEM((2 * H_A, DV_A, tile), F32)] + _stage_scratch(tile),
        compiler_params=_params("arbitrary", "arbitrary"),
    )(proj, proj, vt, vec(lq1), vec(lk1), vec(lq2), vec(lk2), gain.reshape(-1, 1).astype(F32))


BAND_CHUNKS = LANES // CHUNK
BAND_WINDOW = (LEFT_CHUNKS + BAND_CHUNKS) * CHUNK


def _band_body(q_ref, k_ref, vt_ref, bias_ref, o_ref, qz_scr):
    i = pl.program_id(1)
    ntile = BAND_WINDOW // LANES

    @pl.when((pl.program_id(0) == 0) & (i == 0))
    def _():
        qz_scr[...] = jnp.zeros(qz_scr.shape, BF16)

    qt = _scaled_transpose(q_ref[0], 1.0)
    for h in range(H_B):
        r0 = (h % 2) * DH_B
        qz_scr[h, r0:r0 + DH_B, :] = qt[h * DH_B:(h + 1) * DH_B, :]

    start = pl.multiple_of(i * LANES, LANES)
    kw = k_ref[0, pl.ds(start, BAND_WINDOW), :]
    vt = vt_ref[0, pl.ds(i, ntile)]
    kk = lax.broadcasted_iota(I32, (BAND_WINDOW, LANES), 0)
    valid = kk >= LEFT_CHUNKS * CHUNK - start
    outs = []
    for h in range(H_B):
        pair = h // 2
        s = _dot_row_chunks(kw[:, pair * LANES:(pair + 1) * LANES], qz_scr[h])
        t = jnp.where(valid, s + bias_ref[h], MASKED)
        p = jnp.exp2(t - jnp.max(t, axis=0, keepdims=True))
        l = jnp.sum(p, axis=0, keepdims=True)
        vth = jnp.concatenate([vt[n, h * DH_B:(h + 1) * DH_B, :] for n in range(ntile)], axis=1)
        outs.append(_dot(vth, p.astype(BF16)) / l)
    o_ref[0] = jnp.concatenate(outs, axis=0).T.astype(BF16)


def _band_bias_table(table):
    bias = _band_bias(table).transpose(0, 2, 1) * LOG2E
    cols = [jnp.pad(bias, ((0, 0), (c * CHUNK, (BAND_CHUNKS - 1 - c) * CHUNK), (0, 0)), constant_values=MASKED)
            for c in range(BAND_CHUNKS)]
    return jnp.concatenate(cols, axis=2)


def _band_attention(proj, kpad, vt_pad, bias):
    b, s, _ = proj.shape
    w = H_B * DH_B
    sp = kpad.shape[1]
    nvt = vt_pad.shape[1]
    return pl.pallas_call(
        _band_body,
        grid=(b, s // LANES),
        in_specs=[pl.BlockSpec((1, LANES, w), lambda bi, i: (bi, i, 6)),
                  pl.BlockSpec((1, sp, w), lambda bi, i: (bi, 0, 0)),
                  pl.BlockSpec((1, nvt, w, LANES), lambda bi, i: (bi, 0, 0, 0)),
                  pl.BlockSpec((H_B, BAND_WINDOW, LANES), lambda bi, i: (0, 0, 0))],
        out_specs=pl.BlockSpec((1, LANES, w), lambda bi, i: (bi, i, 0)),
        out_shape=jax.ShapeDtypeStruct((b, s, w), BF16),
        scratch_shapes=[pltpu.VMEM((H_B, LANES, LANES), BF16)],
        compiler_params=_params("arbitrary", "arbitrary"),
    )(proj, kpad, vt_pad, bias)


def _sortable(x):
    bits = pltpu.bitcast(x + 0.0, I32)
    return bits ^ ((bits >> 31) & 0x7FFFFFFF)


_INT_MIN = -2 ** 31
_VALID_KEY = int(np.array(0.5 * NEG_INF, np.float32).view(np.int32))
_VALID_KEY = _VALID_KEY ^ ((_VALID_KEY >> 31) & 0x7FFFFFFF)


def _dsa_body(qc_ref, kc_ref, vct_ref, qi_ref, kiw_ref, wq_ref, o_ref,
              qa_scr, kx_scr, dneg_scr, qit_scr, key_scr, hi_scr, lo_scr, jstar_scr,
              m_scr, l_scr, acc_scr, *stage,
              tile, topk, slopes, seq):
    i = pl.program_id(1)
    nblk = i + 1
    kpos, qpos = _key_query_iotas(tile)

    @pl.when((pl.program_id(0) == 0) & (i == 0))
    def _():
        _init_alibi(slopes, dneg_scr, kx_scr, qa_scr, tile, 1)

    qit_scr[...] = _scaled_transpose(qi_ref[0], D_IDX ** -0.5)
    wt = wq_ref[0].T
    wrow = [wt[D_IDX + h:D_IDX + h + 1, :] * H_IDX ** -0.5 for h in range(H_IDX)]

    def score_block(j, diag):
        k0 = pl.multiple_of(j * tile, tile)
        kib = kiw_ref[0, pl.ds(k0, tile), :][:, :D_IDX].astype(BF16)
        isc = jnp.zeros((tile, tile), F32)
        for h in range(H_IDX):
            d = _dot(kib, qit_scr[h * D_IDX:(h + 1) * D_IDX, :])
            isc = isc + jnp.maximum(d, 0.0) * wrow[h]
        if diag:
            isc = jnp.where((kpos // CHUNK) <= (qpos // CHUNK), isc, NEG_INF)
        key = _sortable(isc)
        key_scr[pl.ds(k0, tile), :] = key
        hi_scr[pl.ds(k0, tile), :] = (key >> 16).astype(I16)
        lo_scr[pl.ds(k0, tile), :] = ((key & 0xFFFF) - 2 ** 15).astype(I16)

    def score_full(j, carry):
        score_block(j, False)
        return carry

    lax.fori_loop(0, i, score_full, 0)
    score_block(i, True)

    def count(pred):
        def body(j, part):
            k0 = pl.multiple_of(j * tile, tile)
            hit = jnp.where(pred(key_scr[pl.ds(k0, tile), :], kpos + k0), 1.0, 0.0)
            return part + jnp.sum(hit.reshape(tile // SUBLANES, SUBLANES, tile), axis=0)
        part = lax.fori_loop(0, nblk, body, jnp.zeros((SUBLANES, tile), F32))
        return jnp.sum(part, axis=0, keepdims=True)

    kf = float(topk)
    rows16 = 2 * SUBLANES

    pair = 2 if (seq // tile) % 2 == 0 else 1
    span = pair * tile
    lowest16 = jnp.full((tile, tile), -2 ** 15, I16)

    @pl.when(nblk % pair == 1)
    def _():
        pad_rows = pl.ds(pl.multiple_of(nblk * tile, tile), tile)
        hi_scr[pad_rows, :] = lowest16
        lo_scr[pad_rows, :] = lowest16

    def count16(src_ref, pred):
        def body(j, part):
            k0 = pl.multiple_of(j * span, span)
            hit = jnp.where(pred(src_ref[pl.ds(k0, span), :]), jnp.int16(1), jnp.int16(0))
            for g in range(span // rows16):
                part = part + hit[g * rows16:(g + 1) * rows16, :]
            return part
        part = lax.fori_loop(0, (nblk + pair - 1) // pair, body, jnp.zeros((rows16, tile), I16))
        return jnp.sum(part.astype(I32), axis=0, keepdims=True).astype(F32)

    def search16(src_ref, base):
        def bit(p, u):
            cand_u = u | jnp.left_shift(jnp.int32(1), 15 - p)
            cand = (cand_u - 2 ** 15).astype(I16)
            cnt = base + count16(src_ref, lambda v: v >= cand)
            return jnp.where(cnt >= kf, cand_u, u)
        return lax.fori_loop(0, 16, bit, jnp.zeros((1, tile), I32))

    thr_hi = search16(hi_scr, 0.0) - 2 ** 15
    thr_hi16 = thr_hi.astype(I16)
    above = count16(hi_scr, lambda v: v > thr_hi16)

    def mask_low(j, carry):
        k0 = pl.multiple_of(j * tile, tile)
        rows = pl.ds(k0, tile)
        lo_scr[rows, :] = jnp.where(hi_scr[rows, :] == thr_hi16, lo_scr[rows, :], jnp.int16(-2 ** 15))
        return carry

    lax.fori_loop(0, nblk, mask_low, 0)
    thr = jnp.left_shift(thr_hi, 16) + search16(lo_scr, above)

    thr_b = thr
    c_gt = count(lambda key, idx: key > thr_b)
    c_ge = count(lambda key, idx: key >= thr_b)
    tied = (c_ge > kf) & (thr > _VALID_KEY)
    need = kf - c_gt
    jstar_scr[...] = jnp.full((1, tile), seq, I32)

    @pl.when(jnp.max(jnp.where(tied, 1.0, 0.0)) > 0.0)
    def _():
        nbits = max(1, int(math.ceil(math.log2(seq))))

        def index_bit(p, lo):
            cand = lo | jnp.left_shift(jnp.int32(1), nbits - 1 - p)
            cnt = count(lambda key, idx: (key == thr_b) & (idx < cand))
            return jnp.where(cnt < need, cand, lo)

        lo = lax.fori_loop(0, nbits, index_bit, jnp.zeros((1, tile), I32))
        jstar_scr[...] = jnp.where(tied, lo, seq)

    jstar = jstar_scr[...]
    thr2 = jnp.maximum(thr, _VALID_KEY + 1)

    _store_query_maps(qa_scr, _scaled_transpose(qc_ref[0], 1.0), H_C, DH_C)
    m_scr[...] = jnp.full(m_scr.shape, NEG_INF, F32)
    l_scr[...] = jnp.zeros(l_scr.shape, F32)
    acc_scr[...] = jnp.zeros(acc_scr.shape, F32)

    def attend(j, diag):
        k0 = pl.multiple_of(j * tile, tile)
        key = key_scr[pl.ds(k0, tile), :]
        sel = (key > thr2) | ((key == thr2) & (kpos + k0 <= jstar))
        kb = kc_ref[0, pl.ds(k0, tile), :]
        vtb = vct_ref[0, j]
        off = ((i - j) * tile).astype(F32)

        def scores(h):
            pair = h // 2
            keys = jnp.concatenate([kb[:, pair * LANES:(pair + 1) * LANES], kx_scr[h]], axis=1)
            s = _dot_row_chunks(keys, qa_scr[h])
            if diag:
                s = s + dneg_scr[h]
            return jnp.where(sel, s, MASKED)

        def probs(h, t_ref, tmax_ref, p_ref, a_ref):
            shift = 0.0 if diag else (slopes[h] * LOG2E) * off
            _softmax_probs(shift, t_ref, tmax_ref, p_ref, a_ref, m_scr.at[h], l_scr.at[h])

        def accumulate(h, p_ref, a_ref):
            _accumulate(vtb[h * DH_C:(h + 1) * DH_C, :], p_ref, a_ref, acc_scr.at[h])

        _pipelined_maps(H_C, scores, probs, accumulate, stage)

    def attend_full(j, carry):
        attend(j, False)
        return carry

    lax.fori_loop(0, i, attend_full, 0)
    attend(i, True)
    outs = [(acc_scr[h] / l_scr[h]).T for h in range(H_C)]
    o_ref[0] = jnp.concatenate(outs, axis=-1).astype(BF16)


def _dsa_attention(proj, vt, kiw, slopes):
    b, s, _ = proj.shape
    tile = min(ATT_TILE, s)
    nkb = s // tile
    w = H_C * DH_C
    wa = H_A * DV_A
    topk = min(TOPK_MAX, s // 4)
    return pl.pallas_call(
        functools.partial(_dsa_body, tile=tile, topk=topk, slopes=slopes, seq=s),
        grid=(b, nkb),
        in_specs=[pl.BlockSpec((1, tile, w), lambda bi, i: (bi, i, 8)),
                  pl.BlockSpec((1, s, w), lambda bi, i: (bi, 0, 9)),
                  pl.BlockSpec((1, nkb, w, tile), lambda bi, i: (bi, 0, wa // w, 0)),
                  pl.BlockSpec((1, tile, H_IDX * D_IDX), lambda bi, i: (bi, i, 2)),
                  pl.BlockSpec((1, s, LANES), lambda bi, i: (bi, 0, 0)),
                  pl.BlockSpec((1, tile, LANES), lambda bi, i: (bi, i, 0))],
        out_specs=pl.BlockSpec((1, tile, w), lambda bi, i: (bi, i, 0)),
        out_shape=jax.ShapeDtypeStruct((b, s, w), BF16),
        scratch_shapes=[pltpu.VMEM((H_C, 2 * LANES, tile), BF16),
                        pltpu.VMEM((H_C, tile, LANES), BF16),
                        pltpu.VMEM((H_C, tile, tile), F32),
                        pltpu.VMEM((H_IDX * D_IDX, tile), BF16),
                        pltpu.VMEM((s, tile), I32),
                        pltpu.VMEM((s, tile), I16),
                        pltpu.VMEM((s, tile), I16),
                        pltpu.VMEM((1, tile), I32),
                        pltpu.VMEM((H_C, 1, tile), F32),
                        pltpu.VMEM((H_C, 1, tile), F32),
                        pltpu.VMEM((H_C, DH_C, tile), F32)] + _stage_scratch(tile),
        compiler_params=_params("arbitrary", "arbitrary"),
    )(proj, proj, vt, proj, kiw, kiw)


def _xattn_body(x_ref, res_ref, wq_ref, k_ref, v_ref, wo_ref, g_ref, b_ref, of_ref, ob_ref, *, dh, alpha):
    q_all = _dot(x_ref[0], wq_ref[...]).astype(BF16)
    outs = []
    for h in range(H_X):
        c = h * dh
        q = q_all[:, c:c + dh] * dh ** -0.5
        s = _dot_nt(q, k_ref[0, :, c:c + dh])
        p = jnp.exp(s - jnp.max(s, axis=-1, keepdims=True))
        l = jnp.sum(p, axis=-1, keepdims=True)
        outs.append((_dot(p.astype(BF16), v_ref[0, :, c:c + dh]) / l).astype(BF16))
    y = _dot(jnp.concatenate(outs, axis=-1), wo_ref[...])
    out = _layer_norm_rows(alpha * res_ref[0] + y, g_ref[...], b_ref[...])
    of_ref[0] = out
    ob_ref[0] = out.astype(BF16)


def _xattn_ln(x, res, wq, kv, wo, g, b, alpha, tq=512):
    bsz, s, d = x.shape
    n_mem = kv.shape[1]
    tq = min(tq, s)
    tile = pl.BlockSpec((1, tq, d), lambda bi, i: (bi, i, 0))
    whole = lambda shape: pl.BlockSpec(shape, lambda bi, i: (0,) * len(shape))
    mem = lambda col: pl.BlockSpec((1, n_mem, d), lambda bi, i: (bi, 0, col))
    return pl.pallas_call(
        functools.partial(_xattn_body, dh=d // H_X, alpha=alpha),
        grid=(bsz, s // tq),
        in_specs=[tile, tile, whole((d, d)), mem(0), mem(1), whole((d, d)), whole((1, d)), whole((1, d))],
        out_specs=[tile, tile],
        out_shape=[jax.ShapeDtypeStruct((bsz, s, d), F32), jax.ShapeDtypeStruct((bsz, s, d), BF16)],
        compiler_params=_params("parallel", "parallel"),
    )(x, res, wq, kv, kv, wo, g.reshape(1, d), b.reshape(1, d))


def _router_body(x_ref, r_ref, g_ref):
    logits = _dot(x_ref[...], r_ref[...])
    lane = lax.broadcasted_iota(I32, logits.shape, 1).astype(F32)
    big = float(LANES)
    lg = jnp.where(lane < N_EXPERTS, logits, -jnp.inf)
    m1 = jnp.max(lg, axis=-1, keepdims=True)
    i1 = jnp.min(jnp.where(lg == m1, lane, big), axis=-1, keepdims=True)
    lg2 = jnp.where(lane == i1, -jnp.inf, lg)
    m2 = jnp.max(lg2, axis=-1, keepdims=True)
    i2 = jnp.min(jnp.where(lg2 == m2, lane, big), axis=-1, keepdims=True)
    e = jnp.exp(m2 - m1)
    den = 1.0 + e
    g_ref[...] = jnp.where(lane == i1, 1.0 / den, 0.0) + jnp.where(lane == i2, e / den, 0.0)


def _router_gates(x, router, tm=1024):
    m, d = x.shape
    tm = min(tm, m)
    rpad = jnp.zeros((d, LANES), BF16).at[:, :N_EXPERTS].set(router.astype(BF16))
    return pl.pallas_call(
        _router_body,
        grid=(m // tm,),
        in_specs=[pl.BlockSpec((tm, d), lambda i: (i, 0)),
                  pl.BlockSpec((d, LANES), lambda i: (0, 0))],
        out_specs=pl.BlockSpec((tm, LANES), lambda i: (i, 0)),
        out_shape=jax.ShapeDtypeStruct((m, LANES), F32),
        compiler_params=_params("parallel"),
    )(x, rpad)


MOE_BLOCK = 896
MOE_CHUNK = 128
MOE_GROUP = 2
MOE_FF_TILE = 1792


def _moe_body(x_ref, gate_ref, tri_ref, w1_ref, w3_ref, w2_ref, o_ref,
              xs_scr, acc_scr, rcol_scr, rrow_scr, cnt_smem, *, n_tokens):
    e = pl.program_id(1)
    f = pl.program_id(2)
    nf = pl.num_programs(2)
    tb = x_ref.shape[0]
    ch = MOE_CHUNK
    gw = MOE_GROUP * ch

    tok0 = pl.program_id(0) * tb
    valid_col = lax.broadcasted_iota(I32, (tb, 1), 0) + tok0 < n_tokens

    @pl.when((e == 0) & (f == 0))
    def _():
        o_ref[...] = jnp.zeros(o_ref.shape, F32)
        member = jnp.where((gate_ref[...] > 0.0) & valid_col, 1.0, 0.0)
        rank = _dot(tri_ref[...], member.astype(BF16))
        rcol = jnp.where(member > 0.0, rank, -1.0)
        rcol_scr[...] = rcol
        rrow_scr[...] = rcol.T
        counts = jnp.sum(member, axis=0, keepdims=True)
        for ee in range(N_EXPERTS):
            cnt_smem[ee] = counts[0, ee].astype(I32)

    count = cnt_smem[e]
    single = count <= gw
    nchunk = jnp.where(single, MOE_GROUP, (count + (ch - 1)) // ch)

    @pl.when(f == 0)
    def _():
        acc_scr[...] = jnp.zeros(acc_scr.shape, F32)
        sub = lax.broadcasted_iota(I32, (SUBLANES, tb), 0)
        r_row = jnp.sum(jnp.where(sub == e, rrow_scr[0:SUBLANES, :], 0.0), axis=0, keepdims=True)
        xz = jnp.where(valid_col, x_ref[...], jnp.zeros((), BF16))

        def gather_rows(r0, rows):
            row_id = lax.broadcasted_iota(I32, (rows, tb), 0).astype(F32) + jnp.asarray(r0, F32)
            sel = jnp.where(row_id == r_row, 1.0, 0.0).astype(BF16)
            xs_scr[pl.ds(r0, rows), :] = _dot(sel, xz).astype(BF16)

        @pl.when(single)
        def _():
            gather_rows(0, gw)

        @pl.when(jnp.logical_not(single))
        def _():
            def gather(c, carry):
                gather_rows(pl.multiple_of(c * ch, ch), ch)
                return carry
            lax.fori_loop(0, nchunk, gather, 0)

    def ffn_rows(r0, rows):
        xs = xs_scr[pl.ds(r0, rows), :]
        a = _dot(xs, w1_ref[0])
        b = _dot(xs, w3_ref[0])
        acc_scr[pl.ds(r0, rows), :] += _dot((_silu(a) * b).astype(BF16), w2_ref[0])

    @pl.when(single)
    def _():
        ffn_rows(0, gw)

    @pl.when(jnp.logical_not(single))
    def _():
        def ffn(c, carry):
            ffn_rows(pl.multiple_of(c * ch, ch), ch)
            return carry
        lax.fori_loop(0, nchunk, ffn, 0)

    @pl.when(f == nf - 1)
    def _():
        pick = lax.broadcasted_iota(I32, (tb, LANES), 1) == e
        r_col = jnp.sum(jnp.where(pick, rcol_scr[...], 0.0), axis=1, keepdims=True)
        g_col = jnp.sum(jnp.where(pick & valid_col, gate_ref[...], 0.0), axis=1, keepdims=True)
        col_id = lax.broadcasted_iota(I32, (tb, gw), 1).astype(F32)

        def combine(gi, carry):
            r0 = pl.multiple_of(gi * gw, gw)
            selt = jnp.where(col_id + r0.astype(F32) == r_col, 1.0, 0.0).astype(BF16)
            y = acc_scr[pl.ds(r0, gw), :]
            y_hi = y.astype(BF16)
            y_lo = (y - y_hi.astype(F32)).astype(BF16)
            o_ref[...] += g_col * (_dot(selt, y_hi) + _dot(selt, y_lo))
            return carry

        lax.fori_loop(0, (nchunk + (MOE_GROUP - 1)) // MOE_GROUP, combine, 0)


def _moe_experts(x, gates, w1, w3, w2):
    m, d = x.shape
    ne, _, ff = w1.shape
    tb = min(MOE_BLOCK, m)
    tf = MOE_FF_TILE if ff % MOE_FF_TILE == 0 else ff
    gw = MOE_GROUP * MOE_CHUNK
    cap = pl.cdiv(tb, gw) * gw
    assert tb % LANES == 0 and ne <= SUBLANES
    tri = jnp.asarray(np.tril(np.ones((tb, tb), np.float32), -1), BF16)
    row = lambda i, e, f: (i, 0)
    return pl.pallas_call(
        functools.partial(_moe_body, n_tokens=m),
        grid=(pl.cdiv(m, tb), ne, ff // tf),
        in_specs=[pl.BlockSpec((tb, d), row),
                  pl.BlockSpec((tb, LANES), row),
                  pl.BlockSpec((tb, tb), lambda i, e, f: (0, 0)),
                  pl.BlockSpec((1, d, tf), lambda i, e, f: (e, 0, f)),
                  pl.BlockSpec((1, d, tf), lambda i, e, f: (e, 0, f)),
                  pl.BlockSpec((1, tf, d), lambda i, e, f: (e, f, 0))],
        out_specs=pl.BlockSpec((tb, d), row),
        out_shape=jax.ShapeDtypeStruct((m, d), F32),
        scratch_shapes=[pltpu.VMEM((cap, d), BF16),
                        pltpu.VMEM((cap, d), F32),
                        pltpu.VMEM((tb, LANES), F32),
                        pltpu.VMEM((LANES, tb), F32),
                        pltpu.SMEM((N_EXPERTS,), I32)],
        compiler_params=_params("parallel", "arbitrary", "arbitrary"),
    )(x, gates, tri, w1, w3, w2)


def _res_ln_body(res_ref, y_ref, g_ref, b_ref, of_ref, ob_ref, *, alpha):
    out = _layer_norm_rows(alpha * res_ref[...] + y_ref[...], g_ref[...], b_ref[...])
    of_ref[...] = out
    ob_ref[...] = out.astype(BF16)


def _residual_ln(res, y, g, b, alpha, tm=512):
    m, d = res.shape
    tm = min(tm, m)
    row = pl.BlockSpec((tm, d), lambda i: (i, 0))
    vec = pl.BlockSpec((1, d), lambda i: (0, 0))
    return pl.pallas_call(
        functools.partial(_res_ln_body, alpha=alpha),
        grid=(m // tm,),
        in_specs=[row, row, vec, vec],
        out_specs=[row, row],
        out_shape=[jax.ShapeDtypeStruct((m, d), F32), jax.ShapeDtypeStruct((m, d), BF16)],
        compiler_params=_params("parallel"),
    )(res, y, g.reshape(1, d), b.reshape(1, d))


def _band_bias(table):
    band = (LEFT_CHUNKS + 1) * CHUNK
    left = LEFT_CHUNKS * CHUNK
    rev = table.astype(F32)[:, ::-1]
    pad = left - REL_MAX + CHUNK - 1
    n_rev = band - 1 - (left - REL_MAX) + 1
    assert n_rev <= 2 * REL_MAX + 1
    ext = jnp.concatenate([jnp.broadcast_to(rev[:, :1], (rev.shape[0], pad)), rev[:, :n_rev]], axis=1)
    n_ext = ext.shape[1]
    period = jnp.concatenate([ext[:, CHUNK - 1:], jnp.zeros((ext.shape[0], 1), F32), ext[:, :CHUNK - 1]], axis=1)
    skew = jnp.tile(period, (1, CHUNK))[:, :CHUNK * n_ext].reshape(-1, CHUNK, n_ext)
    return skew[:, :, :band]


def kernel(x, mem, w_in, lam_q1, lam_k1, lam_q2, lam_k2, diff_norm_g, rel_bias, w_out,
           ln1_g, ln1_b, xattn_wq, xattn_wk, xattn_wv, xattn_wo, ln2_g, ln2_b,
           ffn_w1, ffn_w3, ffn_w2, moe_router, moe_w1, moe_w3, moe_w2, ln3_g, ln3_b):
    b, s, d = x.shape
    depth = w_in.shape[0]
    t = b * s
    alpha = (2.0 * depth) ** 0.25
    slopes_a, slopes_c = _alibi_slopes()
    n_main = 3 * H_A * DV_A + 3 * H_B * DH_B + 3 * H_C * DH_C + H_IDX * D_IDX
    n_tail = D_IDX + H_IDX
    wa, wb, wc = H_A * DV_A, H_B * DH_B, H_C * DH_C
    left = LEFT_CHUNKS * CHUNK

    h = x.reshape(t, d)
    hb = h.astype(BF16)
    memb = mem.reshape(-1, d).astype(BF16)
    for l in range(depth):
        wl = w_in[l]
        o_b, o_c, o_i = 3 * wa, 3 * wa + 3 * wb, 3 * wa + 3 * wb + 3 * wc
        w_main = jnp.concatenate(
            [wl[:, :wa] * (DQK_A ** -0.5 * LOG2E), wl[:, wa:2 * wa], wl[:, o_i:o_i + H_IDX * D_IDX],
             wl[:, o_b:o_b + wb] * (DH_B ** -0.5 * LOG2E), wl[:, o_b + wb:o_b + 2 * wb],
             wl[:, o_c:o_c + wc] * (DH_C ** -0.5 * LOG2E), wl[:, o_c + wc:o_c + 2 * wc]], axis=1).astype(BF16)
        proj = _matmul(hb, w_main, BF16, tn=w_main.shape[1] // 2).reshape(b, s, -1)
        w_side = jnp.concatenate(
            [wl[:, 2 * wa:3 * wa], wl[:, o_c + 2 * wc:o_c + 3 * wc], wl[:, o_b + 2 * wb:o_b + 3 * wb],
             wl[:, n_main:], jnp.zeros((d, LANES - n_tail), F32)], axis=1).astype(BF16)
        tile = min(ATT_TILE, s)
        vt, vbt, kiw = _side_projections(hb, w_side, wa + wc, tile, wb, LANES)
        vt = vt.reshape(b, s // tile, wa + wc, tile)
        kiw = kiw.reshape(b, s, LANES)
        lam_init = 0.8 - 0.6 * math.exp(-0.3 * l)
        ya = _diff_attention(proj, vt, lam_q1[l], lam_k1[l], lam_q2[l], lam_k2[l], diff_norm_g[l],
                             slopes_a, lam_init)
        kpad = jnp.pad(proj[:, :, 3 * wa + wb:3 * wa + 2 * wb], ((0, 0), (left, 0), (0, 0)))
        vbt = jnp.pad(vbt.reshape(b, s // LANES, wb, LANES), ((0, 0), (left // LANES, 0), (0, 0), (0, 0)))
        yb = _band_attention(proj, kpad, vbt, _band_bias_table(rel_bias[l]))
        yc = _dsa_attention(proj, vt, kiw, slopes_c)
        wo = w_out[l].astype(BF16)
        h, hb = _matmul_ln(
            [ya.reshape(t, wa), yb.reshape(t, wb), yc.reshape(t, -1)],
            [wo[:wa], wo[wa:wa + wb], wo[wa + wb:]],
            h, ln1_g[l], ln1_b[l], alpha)
        w_kv = jnp.concatenate([xattn_wk[l], xattn_wv[l]], axis=1).astype(BF16)
        kv = _matmul(memb, w_kv, BF16, tn=d).reshape(b, -1, 2 * d)
        h, hb = _xattn_ln(hb.reshape(b, s, d), h.reshape(b, s, d), xattn_wq[l].astype(BF16), kv,
                          xattn_wo[l].astype(BF16), ln2_g[l], ln2_b[l], alpha)
        h, hb = h.reshape(t, d), hb.reshape(t, d)
        if l % 2 == 0:
            h, hb = _ffn_ln(hb, ffn_w1[l // 2].astype(BF16), ffn_w3[l // 2].astype(BF16),
                            ffn_w2[l // 2].astype(BF16), h, ln3_g[l], ln3_b[l], alpha)
        else:
            gates = _router_gates(hb, moe_router[l // 2])
            y = _moe_experts(hb, gates, moe_w1[l // 2].astype(BF16), moe_w3[l // 2].astype(BF16),
                             moe_w2[l // 2].astype(BF16))
            h, hb = _residual_ln(h, y, ln3_g[l], ln3_b[l], alpha)
    return h.reshape(b, s, d)
```

```python
import functools
import math

import numpy as np
import jax
import jax.numpy as jnp
from jax import lax
from jax.experimental import pallas as pl
from jax.experimental.pallas import tpu as pltpu

F32 = jnp.float32
BF16 = jnp.bfloat16
I32 = jnp.int32
I16 = jnp.int16

CHUNK = 64
H_A, DQK_A = 4, 64
DV_A = 2 * DQK_A
H_B, DH_B = 4, 64
LEFT_CHUNKS = 8
REL_MAX = 128
H_C, DH_C = 4, 64
H_IDX, D_IDX = 8, 64
TOPK_MAX = 256
H_X = 4
N_EXPERTS = 8
LN_EPS = 1e-5
RMS_EPS = 1e-6
NEG_INF = -1e30
MASKED = -2e30

LANES = 128
SUBLANES = 8
VMEM_LIMIT = 56 * 1024 * 1024
ATT_TILE = 256

_NT = (((1,), (1,)), ((), ()))


def _alibi_slopes():
    n = H_A + H_C
    s = (2.0 ** (-8.0 * np.arange(1, n + 1) / n)).astype(np.float32)
    return [float(v) for v in s[0::2]], [float(v) for v in s[1::2]]


def _params(*sem):
    return pltpu.CompilerParams(dimension_semantics=sem, vmem_limit_bytes=VMEM_LIMIT)


def _dot(a, b):
    return jnp.dot(a, b, preferred_element_type=F32)


SCORE_ROW_CHUNK = 128


def _dot_row_chunks(a, b):
    rows = a.shape[0]
    step = min(SCORE_ROW_CHUNK, rows)
    return jnp.concatenate([_dot(a[r:r + step], b) for r in range(0, rows, step)], axis=0)


def _dot_nt(a, b):
    return lax.dot_general(a, b, _NT, preferred_element_type=F32)


def _layer_norm_rows(y, g, b):
    mu = jnp.mean(y, axis=-1, keepdims=True)
    d = y - mu
    var = jnp.mean(d * d, axis=-1, keepdims=True)
    return d * lax.rsqrt(var + LN_EPS) * g + b


def _silu(a):
    return a / (1.0 + jnp.exp(-a))


def _mm_body(x_ref, w_ref, o_ref):
    o_ref[...] = _dot(x_ref[...], w_ref[...]).astype(o_ref.dtype)


def _matmul(x, w, out_dtype, tm=1024, tn=512):
    m, k = x.shape
    n = w.shape[1]
    tm, tn = min(tm, m), min(tn, n)
    return pl.pallas_call(
        _mm_body,
        grid=(m // tm, n // tn),
        in_specs=[pl.BlockSpec((tm, k), lambda i, j: (i, 0)),
                  pl.BlockSpec((k, tn), lambda i, j: (0, j))],
        out_specs=pl.BlockSpec((tm, tn), lambda i, j: (i, j)),
        out_shape=jax.ShapeDtypeStruct((m, n), out_dtype),
        compiler_params=_params("parallel", "parallel"),
    )(x, w)


def _side_proj_body(x_ref, w_ref, vt_ref, vbt_ref, tail_ref):
    y = _dot(x_ref[...], w_ref[...])
    n_v, n_vb = vt_ref.shape[1], vbt_ref.shape[1]
    tile, tile_b = vt_ref.shape[2], vbt_ref.shape[2]
    for r in range(vt_ref.shape[0]):
        vt_ref[r] = y[r * tile:(r + 1) * tile, :n_v].T.astype(BF16)
    for r in range(vbt_ref.shape[0]):
        vbt_ref[r] = y[r * tile_b:(r + 1) * tile_b, n_v:n_v + n_vb].T.astype(BF16)
    tail_ref[...] = y[:, n_v + n_vb:]


def _side_projections(x, w, n_v, tile, n_vb, tile_b, tm=1024):
    m, k = x.shape
    n = w.shape[1]
    n_tail = n - n_v - n_vb
    tm = min(tm, m)
    return pl.pallas_call(
        _side_proj_body,
        grid=(m // tm,),
        in_specs=[pl.BlockSpec((tm, k), lambda i: (i, 0)),
                  pl.BlockSpec((k, n), lambda i: (0, 0))],
        out_specs=[pl.BlockSpec((tm // tile, n_v, tile), lambda i: (i, 0, 0)),
                   pl.BlockSpec((tm // tile_b, n_vb, tile_b), lambda i: (i, 0, 0)),
                   pl.BlockSpec((tm, n_tail), lambda i: (i, 0))],
        out_shape=[jax.ShapeDtypeStruct((m // tile, n_v, tile), BF16),
                   jax.ShapeDtypeStruct((m // tile_b, n_vb, tile_b), BF16),
                   jax.ShapeDtypeStruct((m, n_tail), F32)],
        compiler_params=_params("parallel"),
    )(x, w)


def _mm_ln_body(*refs, n_in, alpha):
    xs, ws = refs[:n_in], refs[n_in:2 * n_in]
    res_ref, g_ref, b_ref, of_ref, ob_ref = refs[2 * n_in:]
    acc = _dot(xs[0][...], ws[0][...])
    for x_ref, w_ref in zip(xs[1:], ws[1:]):
        acc = acc + _dot(x_ref[...], w_ref[...])
    out = _layer_norm_rows(alpha * res_ref[...] + acc, g_ref[...], b_ref[...])
    of_ref[...] = out
    ob_ref[...] = out.astype(BF16)


def _matmul_ln(xs, ws, res, g, b, alpha):
    m, d = res.shape
    k_total = sum(x.shape[1] for x in xs)
    tm = min(1024 if k_total <= 1024 else 512, m)
    n_in = len(xs)
    in_specs = ([pl.BlockSpec((tm, x.shape[1]), lambda i: (i, 0)) for x in xs]
                + [pl.BlockSpec(w.shape, lambda i: (0, 0)) for w in ws]
                + [pl.BlockSpec((tm, d), lambda i: (i, 0)),
                   pl.BlockSpec((1, d), lambda i: (0, 0)),
                   pl.BlockSpec((1, d), lambda i: (0, 0))])
    return pl.pallas_call(
        functools.partial(_mm_ln_body, n_in=n_in, alpha=alpha),
        grid=(m // tm,),
        in_specs=in_specs,
        out_specs=[pl.BlockSpec((tm, d), lambda i: (i, 0)),
                   pl.BlockSpec((tm, d), lambda i: (i, 0))],
        out_shape=[jax.ShapeDtypeStruct((m, d), F32), jax.ShapeDtypeStruct((m, d), BF16)],
        compiler_params=_params("parallel"),
    )(*xs, *ws, res, g.reshape(1, d), b.reshape(1, d))


def _ffn_ln_body(x_ref, w1_ref, w3_ref, w2_ref, res_ref, g_ref, b_ref, of_ref, ob_ref, acc_ref, *, alpha):
    f = pl.program_id(1)
    x = x_ref[...]
    a = _dot(x, w1_ref[...])
    b = _dot(x, w3_ref[...])
    y = _dot((_silu(a) * b).astype(BF16), w2_ref[...])

    @pl.when(f == 0)
    def _():
        acc_ref[...] = y

    @pl.when(f > 0)
    def _():
        acc_ref[...] += y

    @pl.when(f == pl.num_programs(1) - 1)
    def _():
        out = _layer_norm_rows(alpha * res_ref[...] + acc_ref[...], g_ref[...], b_ref[...])
        of_ref[...] = out
        ob_ref[...] = out.astype(BF16)


def _ffn_ln(x, w1, w3, w2, res, g, b, alpha, tm=512, tf=1408):
    m, d = x.shape
    ff = w1.shape[1]
    tm = min(tm, m)
    if ff % tf:
        tf = ff
    row = pl.BlockSpec((tm, d), lambda i, f: (i, 0))
    vec = pl.BlockSpec((1, d), lambda i, f: (0, 0))
    return pl.pallas_call(
        functools.partial(_ffn_ln_body, alpha=alpha),
        grid=(m // tm, ff // tf),
        in_specs=[row,
                  pl.BlockSpec((d, tf), lambda i, f: (0, f)),
                  pl.BlockSpec((d, tf), lambda i, f: (0, f)),
                  pl.BlockSpec((tf, d), lambda i, f: (f, 0)),
                  row, vec, vec],
        out_specs=[row, row],
        out_shape=[jax.ShapeDtypeStruct((m, d), F32), jax.ShapeDtypeStruct((m, d), BF16)],
        scratch_shapes=[pltpu.VMEM((tm, d), F32)],
        compiler_params=_params("parallel", "arbitrary"),
    )(x, w1, w3, w2, res, g.reshape(1, d), b.reshape(1, d))


def _key_query_iotas(tile):
    kpos = lax.broadcasted_iota(I32, (tile, tile), 0)
    qpos = lax.broadcasted_iota(I32, (tile, tile), 1)
    return kpos, qpos


def _scaled_transpose(x, scale):
    return (x.astype(F32) * scale).T.astype(BF16)


LOG2E = math.log2(math.e)
ALIBI_PARTS = 3
ALIBI_ROWS = 16


def _bf16_parts(x):
    parts, r = [], np.float32(x)
    for _ in range(ALIBI_PARTS):
        p = np.float32(r.astype(jnp.bfloat16))
        parts.append(float(p))
        r = np.float32(r - p)
    return parts


def _alibi_key_cols(tile, parts):
    lane = lax.broadcasted_iota(I32, (tile, LANES), 1)
    kr = lax.broadcasted_iota(I32, (tile, LANES), 0).astype(F32)
    v = jnp.where((lane >= ALIBI_PARTS) & (lane < 2 * ALIBI_PARTS), kr, 0.0)
    for n, c in enumerate(parts):
        v = jnp.where(lane == n, c, v)
    return v.astype(BF16)


def _alibi_query_rows(tile, parts):
    row = lax.broadcasted_iota(I32, (ALIBI_ROWS, tile), 0)
    qr = lax.broadcasted_iota(I32, (ALIBI_ROWS, tile), 1).astype(F32)
    v = jnp.where(row < ALIBI_PARTS, -qr, 0.0)
    for n, c in enumerate(parts):
        v = jnp.where(row == ALIBI_PARTS + n, c, v)
    return v.astype(BF16)


def _init_alibi(slopes, dneg_scr, kx_scr, qa_scr, tile, maps_per_head):
    kpos, qpos = _key_query_iotas(tile)
    rel = (qpos - kpos).astype(F32)
    qa_scr[...] = jnp.zeros(qa_scr.shape, BF16)
    for h, slope in enumerate(slopes):
        parts = _bf16_parts(slope * LOG2E)
        dneg_scr[h] = 2.0 * jnp.minimum(np.float32(slope * LOG2E) * rel, 0.0)
        kx_scr[h] = _alibi_key_cols(tile, parts)
        for mm in range(maps_per_head):
            qa_scr[h * maps_per_head + mm, LANES:LANES + ALIBI_ROWS, :] = _alibi_query_rows(tile, parts)


def _store_query_maps(qa_scr, qt, n_maps, dh):
    for idx in range(n_maps):
        r0 = (idx % 2) * dh
        qa_scr[idx, r0:r0 + dh, :] = qt[idx * dh:(idx + 1) * dh, :]


def _softmax_probs(c, t_ref, tmax_ref, p_ref, a_ref, m_ref, l_ref):
    m_old = m_ref[...]
    m_new = jnp.maximum(m_old, tmax_ref[...] - c)
    p = jnp.exp2(t_ref[...] - (m_new + c))
    a = jnp.exp2(m_old - m_new)
    l_ref[...] = a * l_ref[...] + jnp.sum(p, axis=0, keepdims=True)
    m_ref[...] = m_new
    a_ref[...] = a
    p_ref[...] = p.astype(BF16)


def _accumulate(vt, p_ref, a_ref, acc_ref):
    acc_ref[...] = a_ref[...] * acc_ref[...] + _dot(vt, p_ref[...])


SCORE_LEAD, PROB_LEAD = 4, 2
N_T_SLOTS, N_P_SLOTS = SCORE_LEAD - PROB_LEAD + 2, PROB_LEAD + 1


def _stage_scratch(tile):
    return ([pltpu.VMEM((tile, tile), F32)] * N_T_SLOTS + [pltpu.VMEM((1, tile), F32)] * N_T_SLOTS
            + [pltpu.VMEM((tile, tile), BF16)] * N_P_SLOTS + [pltpu.VMEM((1, tile), F32)] * N_P_SLOTS)


def _pipelined_maps(n_maps, scores, probs, accumulate, stage):
    nt, npb = N_T_SLOTS, N_P_SLOTS
    t_slots, x_slots = stage[:nt], stage[nt:2 * nt]
    p_slots, a_slots = stage[2 * nt:2 * nt + npb], stage[2 * nt + npb:]

    def stage_scores(k):
        s = scores(k)
        t_slots[k % nt][...] = s
        x_slots[k % nt][...] = jnp.max(s, axis=0, keepdims=True)

    def stage_probs(k):
        probs(k, t_slots[k % nt], x_slots[k % nt], p_slots[k % npb], a_slots[k % npb])

    for step in range(-SCORE_LEAD, n_maps):
        if 0 <= step + SCORE_LEAD < n_maps:
            stage_scores(step + SCORE_LEAD)
        if 0 <= step + PROB_LEAD < n_maps:
            stage_probs(step + PROB_LEAD)
        if step >= 0:
            accumulate(step, p_slots[step % npb], a_slots[step % npb])


def _diff_body(q_ref, k_ref, vt_ref, lq1_ref, lk1_ref, lq2_ref, lk2_ref, gain_ref, o_ref,
               qa_scr, kx_scr, dneg_scr, m_scr, l_scr, acc_scr, *stage,
               tile, slopes, lam_init):
    i = pl.program_id(1)
    kpos, qpos = _key_query_iotas(tile)

    @pl.when((pl.program_id(0) == 0) & (i == 0))
    def _():
        _init_alibi(slopes, dneg_scr, kx_scr, qa_scr, tile, 2)

    _store_query_maps(qa_scr, _scaled_transpose(q_ref[0], 1.0), 2 * H_A, DQK_A)
    m_scr[...] = jnp.full(m_scr.shape, NEG_INF, F32)
    l_scr[...] = jnp.zeros(l_scr.shape, F32)
    acc_scr[...] = jnp.zeros(acc_scr.shape, F32)

    def block(j, diag):
        k0 = pl.multiple_of(j * tile, tile)
        kb = k_ref[0, pl.ds(k0, tile), :]
        vtb = vt_ref[0, j]
        off = ((i - j) * tile).astype(F32)

        def scores(idx):
            h, pair = idx // 2, idx // 2
            keys = jnp.concatenate([kb[:, pair * LANES:(pair + 1) * LANES], kx_scr[h]], axis=1)
            s = _dot_row_chunks(keys, qa_scr[idx])
            if diag:
                allowed = (kpos // CHUNK) <= (qpos // CHUNK)
                return jnp.where(allowed, s + dneg_scr[h], MASKED)
            return s

        def probs(idx, t_ref, tmax_ref, p_ref, a_ref):
            shift = 0.0 if diag else (slopes[idx // 2] * LOG2E) * off
            _softmax_probs(shift, t_ref, tmax_ref, p_ref, a_ref, m_scr.at[idx], l_scr.at[idx])

        def accumulate(idx, p_ref, a_ref):
            h = idx // 2
            _accumulate(vtb[h * DV_A:(h + 1) * DV_A, :], p_ref, a_ref, acc_scr.at[idx])

        _pipelined_maps(2 * H_A, scores, probs, accumulate, stage)

    def full_block(j, carry):
        block(j, False)
        return carry

    lax.fori_loop(0, i, full_block, 0)
    block(i, True)

    lam = (jnp.exp(jnp.sum(lq1_ref[...] * lk1_ref[...], axis=-1, keepdims=True))
           - jnp.exp(jnp.sum(lq2_ref[...] * lk2_ref[...], axis=-1, keepdims=True)) + lam_init)
    gain = gain_ref[...] * (1.0 - lam_init)
    outs = []
    for h in range(H_A):
        o = acc_scr[2 * h] / l_scr[2 * h] - lam * (acc_scr[2 * h + 1] / l_scr[2 * h + 1])
        o = o * lax.rsqrt(jnp.mean(o * o, axis=0, keepdims=True) + RMS_EPS)
        outs.append((o * gain).T)
    o_ref[0] = jnp.concatenate(outs, axis=-1).astype(BF16)


def _diff_attention(proj, vt, lq1, lk1, lq2, lk2, gain, slopes, lam_init):
    b, s, _ = proj.shape
    tile = min(ATT_TILE, s)
    nkb = s // tile
    w = H_A * DV_A
    vec = lambda a: a.reshape(1, -1).astype(F32)
    small = lambda n: pl.BlockSpec((1, n), lambda bi, i: (0, 0))
    return pl.pallas_call(
        functools.partial(_diff_body, tile=tile, slopes=slopes, lam_init=lam_init),
        grid=(b, nkb),
        in_specs=[pl.BlockSpec((1, tile, w), lambda bi, i: (bi, i, 0)),
                  pl.BlockSpec((1, s, w), lambda bi, i: (bi, 0, 1)),
                  pl.BlockSpec((1, nkb, w, tile), lambda bi, i: (bi, 0, 0, 0)),
                  small(DQK_A), small(DQK_A), small(DQK_A), small(DQK_A),
                  pl.BlockSpec((DV_A, 1), lambda bi, i: (0, 0))],
        out_specs=pl.BlockSpec((1, tile, w), lambda bi, i: (bi, i, 0)),
        out_shape=jax.ShapeDtypeStruct((b, s, w), BF16),
        scratch_shapes=[pltpu.VMEM((2 * H_A, 2 * LANES, tile), BF16),
                        pltpu.VMEM((H_A, tile, LANES), BF16),
                        pltpu.VMEM((H_A, tile, tile), F32),
                        pltpu.VMEM((2 * H_A, 1, tile), F32),
                        pltpu.VMEM((2 * H_A, 1, tile), F32),
                        pltpu.VMEM((2 * H_A, DV_A, tile), F32)] + _stage_scratch(tile),
        compiler_params=_params("arbitrary", "arbitrary"),
    )(proj, proj, vt, vec(lq1), vec(lk1), vec(lq2), vec(lk2), gain.reshape(-1, 1).astype(F32))


BAND_CHUNKS = LANES // CHUNK
BAND_WINDOW = (LEFT_CHUNKS + BAND_CHUNKS) * CHUNK


def _band_body(q_ref, k_ref, vt_ref, bias_ref, o_ref, qz_scr):
    i = pl.program_id(1)
    ntile = BAND_WINDOW // LANES

    @pl.when((pl.program_id(0) == 0) & (i == 0))
    def _():
        qz_scr[...] = jnp.zeros(qz_scr.shape, BF16)

    qt = _scaled_transpose(q_ref[0], 1.0)
    for h in range(H_B):
        r0 = (h % 2) * DH_B
        qz_scr[h, r0:r0 + DH_B, :] = qt[h * DH_B:(h + 1) * DH_B, :]

    start = pl.multiple_of(i * LANES, LANES)
    kw = k_ref[0, pl.ds(start, BAND_WINDOW), :]
    vt = vt_ref[0, pl.ds(i, ntile)]
    kk = lax.broadcasted_iota(I32, (BAND_WINDOW, LANES), 0)
    valid = kk >= LEFT_CHUNKS * CHUNK - start
    outs = []
    for h in range(H_B):
        pair = h // 2
        s = _dot_row_chunks(kw[:, pair * LANES:(pair + 1) * LANES], qz_scr[h])
        t = jnp.where(valid, s + bias_ref[h], MASKED)
        p = jnp.exp2(t - jnp.max(t, axis=0, keepdims=True))
        l = jnp.sum(p, axis=0, keepdims=True)
        vth = jnp.concatenate([vt[n, h * DH_B:(h + 1) * DH_B, :] for n in range(ntile)], axis=1)
        outs.append(_dot(vth, p.astype(BF16)) / l)
    o_ref[0] = jnp.concatenate(outs, axis=0).T.astype(BF16)


def _band_bias_table(table):
    bias = _band_bias(table).transpose(0, 2, 1) * LOG2E
    cols = [jnp.pad(bias, ((0, 0), (c * CHUNK, (BAND_CHUNKS - 1 - c) * CHUNK), (0, 0)), constant_values=MASKED)
            for c in range(BAND_CHUNKS)]
    return jnp.concatenate(cols, axis=2)


def _band_attention(proj, kpad, vt_pad, bias):
    b, s, _ = proj.shape
    w = H_B * DH_B
    sp = kpad.shape[1]
    nvt = vt_pad.shape[1]
    return pl.pallas_call(
        _band_body,
        grid=(b, s // LANES),
        in_specs=[pl.BlockSpec((1, LANES, w), lambda bi, i: (bi, i, 6)),
                  pl.BlockSpec((1, sp, w), lambda bi, i: (bi, 0, 0)),
                  pl.BlockSpec((1, nvt, w, LANES), lambda bi, i: (bi, 0, 0, 0)),
                  pl.BlockSpec((H_B, BAND_WINDOW, LANES), lambda bi, i: (0, 0, 0))],
        out_specs=pl.BlockSpec((1, LANES, w), lambda bi, i: (bi, i, 0)),
        out_shape=jax.ShapeDtypeStruct((b, s, w), BF16),
        scratch_shapes=[pltpu.VMEM((H_B, LANES, LANES), BF16)],
        compiler_params=_params("arbitrary", "arbitrary"),
    )(proj, kpad, vt_pad, bias)


def _sortable(x):
    bits = pltpu.bitcast(x + 0.0, I32)
    return bits ^ ((bits >> 31) & 0x7FFFFFFF)


_INT_MIN = -2 ** 31
_VALID_KEY = int(np.array(0.5 * NEG_INF, np.float32).view(np.int32))
_VALID_KEY = _VALID_KEY ^ ((_VALID_KEY >> 31) & 0x7FFFFFFF)


def _dsa_body(qc_ref, kc_ref, vct_ref, qi_ref, kiw_ref, wq_ref, o_ref,
              qa_scr, kx_scr, dneg_scr, qit_scr, key_scr, hi_scr, lo_scr, jstar_scr,
              m_scr, l_scr, acc_scr, *stage,
              tile, topk, slopes, seq):
    i = pl.program_id(1)
    nblk = i + 1
    kpos, qpos = _key_query_iotas(tile)

    @pl.when((pl.program_id(0) == 0) & (i == 0))
    def _():
        _init_alibi(slopes, dneg_scr, kx_scr, qa_scr, tile, 1)

    qit_scr[...] = _scaled_transpose(qi_ref[0], D_IDX ** -0.5)
    wt = wq_ref[0].T
    wrow = [wt[D_IDX + h:D_IDX + h + 1, :] * H_IDX ** -0.5 for h in range(H_IDX)]

    def score_block(j, diag):
        k0 = pl.multiple_of(j * tile, tile)
        kib = kiw_ref[0, pl.ds(k0, tile), :][:, :D_IDX].astype(BF16)
        isc = jnp.zeros((tile, tile), F32)
        for h in range(H_IDX):
            d = _dot(kib, qit_scr[h * D_IDX:(h + 1) * D_IDX, :])
            isc = isc + jnp.maximum(d, 0.0) * wrow[h]
        if diag:
            isc = jnp.where((kpos // CHUNK) <= (qpos // CHUNK), isc, NEG_INF)
        key = _sortable(isc)
        key_scr[pl.ds(k0, tile), :] = key
        hi_scr[pl.ds(k0, tile), :] = (key >> 16).astype(I16)
        lo_scr[pl.ds(k0, tile), :] = ((key & 0xFFFF) - 2 ** 15).astype(I16)

    def score_full(j, carry):
        score_block(j, False)
        return carry

    lax.fori_loop(0, i, score_full, 0)
    score_block(i, True)

    def count(pred):
        def body(j, part):
            k0 = pl.multiple_of(j * tile, tile)
            hit = jnp.where(pred(key_scr[pl.ds(k0, tile), :], kpos + k0), 1.0, 0.0)
            return part + jnp.sum(hit.reshape(tile // SUBLANES, SUBLANES, tile), axis=0)
        part = lax.fori_loop(0, nblk, body, jnp.zeros((SUBLANES, tile), F32))
        return jnp.sum(part, axis=0, keepdims=True)

    kf = float(topk)
    rows16 = 2 * SUBLANES

    pair = 2 if (seq // tile) % 2 == 0 else 1
    span = pair * tile
    lowest16 = jnp.full((tile, tile), -2 ** 15, I16)

    @pl.when(nblk % pair == 1)
    def _():
        pad_rows = pl.ds(pl.multiple_of(nblk * tile, tile), tile)
        hi_scr[pad_rows, :] = lowest16
        lo_scr[pad_rows, :] = lowest16

    def count16(src_ref, pred):
        def body(j, part):
            k0 = pl.multiple_of(j * span, span)
            hit = jnp.where(pred(src_ref[pl.ds(k0, span), :]), jnp.int16(1), jnp.int16(0))
            for g in range(span // rows16):
                part = part + hit[g * rows16:(g + 1) * rows16, :]
            return part
        part = lax.fori_loop(0, (nblk + pair - 1) // pair, body, jnp.zeros((rows16, tile), I16))
        return jnp.sum(part.astype(I32), axis=0, keepdims=True).astype(F32)

    def search16(src_ref, base):
        def bit(p, u):
            cand_u = u | jnp.left_shift(jnp.int32(1), 15 - p)
            cand = (cand_u - 2 ** 15).astype(I16)
            cnt = base + count16(src_ref, lambda v: v >= cand)
            return jnp.where(cnt >= kf, cand_u, u)
        return lax.fori_loop(0, 16, bit, jnp.zeros((1, tile), I32))

    thr_hi = search16(hi_scr, 0.0) - 2 ** 15
    thr_hi16 = thr_hi.astype(I16)
    above = count16(hi_scr, lambda v: v > thr_hi16)

    def mask_low(j, carry):
        k0 = pl.multiple_of(j * tile, tile)
        rows = pl.ds(k0, tile)
        lo_scr[rows, :] = jnp.where(hi_scr[rows, :] == thr_hi16, lo_scr[rows, :], jnp.int16(-2 ** 15))
        return carry

    lax.fori_loop(0, nblk, mask_low, 0)
    thr = jnp.left_shift(thr_hi, 16) + search16(lo_scr, above)

    thr_b = thr
    c_gt = count(lambda key, idx: key > thr_b)
    c_ge = count(lambda key, idx: key >= thr_b)
    tied = (c_ge > kf) & (thr > _VALID_KEY)
    need = kf - c_gt
    jstar_scr[...] = jnp.full((1, tile), seq, I32)

    @pl.when(jnp.max(jnp.where(tied, 1.0, 0.0)) > 0.0)
    def _():
        nbits = max(1, int(math.ceil(math.log2(seq))))

        def index_bit(p, lo):
            cand = lo | jnp.left_shift(jnp.int32(1), nbits - 1 - p)
            cnt = count(lambda key, idx: (key == thr_b) & (idx < cand))
            return jnp.where(cnt < need, cand, lo)

        lo = lax.fori_loop(0, nbits, index_bit, jnp.zeros((1, tile), I32))
        jstar_scr[...] = jnp.where(tied, lo, seq)

    jstar = jstar_scr[...]
    thr2 = jnp.maximum(thr, _VALID_KEY + 1)

    _store_query_maps(qa_scr, _scaled_transpose(qc_ref[0], 1.0), H_C, DH_C)
    m_scr[...] = jnp.full(m_scr.shape, NEG_INF, F32)
    l_scr[...] = jnp.zeros(l_scr.shape, F32)
    acc_scr[...] = jnp.zeros(acc_scr.shape, F32)

    def attend(j, diag):
        k0 = pl.multiple_of(j * tile, tile)
        key = key_scr[pl.ds(k0, tile), :]
        sel = (key > thr2) | ((key == thr2) & (kpos + k0 <= jstar))
        kb = kc_ref[0, pl.ds(k0, tile), :]
        vtb = vct_ref[0, j]
        off = ((i - j) * tile).astype(F32)

        def scores(h):
            pair = h // 2
            keys = jnp.concatenate([kb[:, pair * LANES:(pair + 1) * LANES], kx_scr[h]], axis=1)
            s = _dot_row_chunks(keys, qa_scr[h])
            if diag:
                s = s + dneg_scr[h]
            return jnp.where(sel, s, MASKED)

        def probs(h, t_ref, tmax_ref, p_ref, a_ref):
            shift = 0.0 if diag else (slopes[h] * LOG2E) * off
            _softmax_probs(shift, t_ref, tmax_ref, p_ref, a_ref, m_scr.at[h], l_scr.at[h])

        def accumulate(h, p_ref, a_ref):
            _accumulate(vtb[h * DH_C:(h + 1) * DH_C, :], p_ref, a_ref, acc_scr.at[h])

        _pipelined_maps(H_C, scores, probs, accumulate, stage)

    def attend_full(j, carry):
        attend(j, False)
        return carry

    lax.fori_loop(0, i, attend_full, 0)
    attend(i, True)
    outs = [(acc_scr[h] / l_scr[h]).T for h in range(H_C)]
    o_ref[0] = jnp.concatenate(outs, axis=-1).astype(BF16)


def _dsa_attention(proj, vt, kiw, slopes):
    b, s, _ = proj.shape
    tile = min(ATT_TILE, s)
    nkb = s // tile
    w = H_C * DH_C
    wa = H_A * DV_A
    topk = min(TOPK_MAX, s // 4)
    return pl.pallas_call(
        functools.partial(_dsa_body, tile=tile, topk=topk, slopes=slopes, seq=s),
        grid=(b, nkb),
        in_specs=[pl.BlockSpec((1, tile, w), lambda bi, i: (bi, i, 8)),
                  pl.BlockSpec((1, s, w), lambda bi, i: (bi, 0, 9)),
                  pl.BlockSpec((1, nkb, w, tile), lambda bi, i: (bi, 0, wa // w, 0)),
                  pl.BlockSpec((1, tile, H_IDX * D_IDX), lambda bi, i: (bi, i, 2)),
                  pl.BlockSpec((1, s, LANES), lambda bi, i: (bi, 0, 0)),
                  pl.BlockSpec((1, tile, LANES), lambda bi, i: (bi, i, 0))],
        out_specs=pl.BlockSpec((1, tile, w), lambda bi, i: (bi, i, 0)),
        out_shape=jax.ShapeDtypeStruct((b, s, w), BF16),
        scratch_shapes=[pltpu.VMEM((H_C, 2 * LANES, tile), BF16),
                        pltpu.VMEM((H_C, tile, LANES), BF16),
                        pltpu.VMEM((H_C, tile, tile), F32),
                        pltpu.VMEM((H_IDX * D_IDX, tile), BF16),
                        pltpu.VMEM((s, tile), I32),
                        pltpu.VMEM((s, tile), I16),
                        pltpu.VMEM((s, tile), I16),
                        pltpu.VMEM((1, tile), I32),
                        pltpu.VMEM((H_C, 1, tile), F32),
                        pltpu.VMEM((H_C, 1, tile), F32),
                        pltpu.VMEM((H_C, DH_C, tile), F32)] + _stage_scratch(tile),
        compiler_params=_params("arbitrary", "arbitrary"),
    )(proj, proj, vt, proj, kiw, kiw)


def _xattn_body(x_ref, res_ref, wq_ref, k_ref, v_ref, wo_ref, g_ref, b_ref, of_ref, ob_ref, *, dh, alpha):
    q_all = _dot(x_ref[0], wq_ref[...]).astype(BF16)
    outs = []
    for h in range(H_X):
        c = h * dh
        q = q_all[:, c:c + dh] * dh ** -0.5
        s = _dot_nt(q, k_ref[0, :, c:c + dh])
        p = jnp.exp(s - jnp.max(s, axis=-1, keepdims=True))
        l = jnp.sum(p, axis=-1, keepdims=True)
        outs.append((_dot(p.astype(BF16), v_ref[0, :, c:c + dh]) / l).astype(BF16))
    y = _dot(jnp.concatenate(outs, axis=-1), wo_ref[...])
    out = _layer_norm_rows(alpha * res_ref[0] + y, g_ref[...], b_ref[...])
    of_ref[0] = out
    ob_ref[0] = out.astype(BF16)


def _xattn_ln(x, res, wq, kv, wo, g, b, alpha, tq=512):
    bsz, s, d = x.shape
    n_mem = kv.shape[1]
    tq = min(tq, s)
    tile = pl.BlockSpec((1, tq, d), lambda bi, i: (bi, i, 0))
    whole = lambda shape: pl.BlockSpec(shape, lambda bi, i: (0,) * len(shape))
    mem = lambda col: pl.BlockSpec((1, n_mem, d), lambda bi, i: (bi, 0, col))
    return pl.pallas_call(
        functools.partial(_xattn_body, dh=d // H_X, alpha=alpha),
        grid=(bsz, s // tq),
        in_specs=[tile, tile, whole((d, d)), mem(0), mem(1), whole((d, d)), whole((1, d)), whole((1, d))],
        out_specs=[tile, tile],
        out_shape=[jax.ShapeDtypeStruct((bsz, s, d), F32), jax.ShapeDtypeStruct((bsz, s, d), BF16)],
        compiler_params=_params("parallel", "parallel"),
    )(x, res, wq, kv, kv, wo, g.reshape(1, d), b.reshape(1, d))


def _router_body(x_ref, r_ref, g_ref):
    logits = _dot(x_ref[...], r_ref[...])
    lane = lax.broadcasted_iota(I32, logits.shape, 1).astype(F32)
    big = float(LANES)
    lg = jnp.where(lane < N_EXPERTS, logits, -jnp.inf)
    m1 = jnp.max(lg, axis=-1, keepdims=True)
    i1 = jnp.min(jnp.where(lg == m1, lane, big), axis=-1, keepdims=True)
    lg2 = jnp.where(lane == i1, -jnp.inf, lg)
    m2 = jnp.max(lg2, axis=-1, keepdims=True)
    i2 = jnp.min(jnp.where(lg2 == m2, lane, big), axis=-1, keepdims=True)
    e = jnp.exp(m2 - m1)
    den = 1.0 + e
    g_ref[...] = jnp.where(lane == i1, 1.0 / den, 0.0) + jnp.where(lane == i2, e / den, 0.0)


def _router_gates(x, router, tm=1024):
    m, d = x.shape
    tm = min(tm, m)
    rpad = jnp.zeros((d, LANES), BF16).at[:, :N_EXPERTS].set(router.astype(BF16))
    return pl.pallas_call(
        _router_body,
        grid=(m // tm,),
        in_specs=[pl.BlockSpec((tm, d), lambda i: (i, 0)),
                  pl.BlockSpec((d, LANES), lambda i: (0, 0))],
        out_specs=pl.BlockSpec((tm, LANES), lambda i: (i, 0)),
        out_shape=jax.ShapeDtypeStruct((m, LANES), F32),
        compiler_params=_params("parallel"),
    )(x, rpad)


MOE_BLOCK = 896
MOE_CHUNK = 128
MOE_GROUP = 2
MOE_FF_TILE = 1792


def _moe_body(x_ref, gate_ref, tri_ref, w1_ref, w3_ref, w2_ref, o_ref,
              xs_scr, acc_scr, rcol_scr, rrow_scr, cnt_smem, *, n_tokens):
    e = pl.program_id(1)
    f = pl.program_id(2)
    nf = pl.num_programs(2)
    tb = x_ref.shape[0]
    ch = MOE_CHUNK
    gw = MOE_GROUP * ch

    tok0 = pl.program_id(0) * tb
    valid_col = lax.broadcasted_iota(I32, (tb, 1), 0) + tok0 < n_tokens

    @pl.when((e == 0) & (f == 0))
    def _():
        o_ref[...] = jnp.zeros(o_ref.shape, F32)
        member = jnp.where((gate_ref[...] > 0.0) & valid_col, 1.0, 0.0)
        rank = _dot(tri_ref[...], member.astype(BF16))
        rcol = jnp.where(member > 0.0, rank, -1.0)
        rcol_scr[...] = rcol
        rrow_scr[...] = rcol.T
        counts = jnp.sum(member, axis=0, keepdims=True)
        for ee in range(N_EXPERTS):
            cnt_smem[ee] = counts[0, ee].astype(I32)

    count = cnt_smem[e]
    single = count <= gw
    nchunk = jnp.where(single, MOE_GROUP, (count + (ch - 1)) // ch)

    @pl.when(f == 0)
    def _():
        acc_scr[...] = jnp.zeros(acc_scr.shape, F32)
        sub = lax.broadcasted_iota(I32, (SUBLANES, tb), 0)
        r_row = jnp.sum(jnp.where(sub == e, rrow_scr[0:SUBLANES, :], 0.0), axis=0, keepdims=True)
        xz = jnp.where(valid_col, x_ref[...], jnp.zeros((), BF16))

        def gather_rows(r0, rows):
            row_id = lax.broadcasted_iota(I32, (rows, tb), 0).astype(F32) + jnp.asarray(r0, F32)
            sel = jnp.where(row_id == r_row, 1.0, 0.0).astype(BF16)
            xs_scr[pl.ds(r0, rows), :] = _dot(sel, xz).astype(BF16)

        @pl.when(single)
        def _():
            gather_rows(0, gw)

        @pl.when(jnp.logical_not(single))
        def _():
            def gather(c, carry):
                gather_rows(pl.multiple_of(c * ch, ch), ch)
                return carry
            lax.fori_loop(0, nchunk, gather, 0)

    def ffn_rows(r0, rows):
        xs = xs_scr[pl.ds(r0, rows), :]
        a = _dot(xs, w1_ref[0])
        b = _dot(xs, w3_ref[0])
        acc_scr[pl.ds(r0, rows), :] += _dot((_silu(a) * b).astype(BF16), w2_ref[0])

    @pl.when(single)
    def _():
        ffn_rows(0, gw)

    @pl.when(jnp.logical_not(single))
    def _():
        def ffn(c, carry):
            ffn_rows(pl.multiple_of(c * ch, ch), ch)
            return carry
        lax.fori_loop(0, nchunk, ffn, 0)

    @pl.when(f == nf - 1)
    def _():
        pick = lax.broadcasted_iota(I32, (tb, LANES), 1) == e
        r_col = jnp.sum(jnp.where(pick, rcol_scr[...], 0.0), axis=1, keepdims=True)
        g_col = jnp.sum(jnp.where(pick & valid_col, gate_ref[...], 0.0), axis=1, keepdims=True)
        col_id = lax.broadcasted_iota(I32, (tb, gw), 1).astype(F32)

        def combine(gi, carry):
            r0 = pl.multiple_of(gi * gw, gw)
            selt = jnp.where(col_id + r0.astype(F32) == r_col, 1.0, 0.0).astype(BF16)
            y = acc_scr[pl.ds(r0, gw), :]
            y_hi = y.astype(BF16)
            y_lo = (y - y_hi.astype(F32)).astype(BF16)
            o_ref[...] += g_col * (_dot(selt, y_hi) + _dot(selt, y_lo))
            return carry

        lax.fori_loop(0, (nchunk + (MOE_GROUP - 1)) // MOE_GROUP, combine, 0)


def _moe_experts(x, gates, w1, w3, w2):
    m, d = x.shape
    ne, _, ff = w1.shape
    tb = min(MOE_BLOCK, m)
    tf = MOE_FF_TILE if ff % MOE_FF_TILE == 0 else ff
    gw = MOE_GROUP * MOE_CHUNK
    cap = pl.cdiv(tb, gw) * gw
    assert tb % LANES == 0 and ne <= SUBLANES
    tri = jnp.asarray(np.tril(np.ones((tb, tb), np.float32), -1), BF16)
    row = lambda i, e, f: (i, 0)
    return pl.pallas_call(
        functools.partial(_moe_body, n_tokens=m),
        grid=(pl.cdiv(m, tb), ne, ff // tf),
        in_specs=[pl.BlockSpec((tb, d), row),
                  pl.BlockSpec((tb, LANES), row),
                  pl.BlockSpec((tb, tb), lambda i, e, f: (0, 0)),
                  pl.BlockSpec((1, d, tf), lambda i, e, f: (e, 0, f)),
                  pl.BlockSpec((1, d, tf), lambda i, e, f: (e, 0, f)),
                  pl.BlockSpec((1, tf, d), lambda i, e, f: (e, f, 0))],
        out_specs=pl.BlockSpec((tb, d), row),
        out_shape=jax.ShapeDtypeStruct((m, d), F32),
        scratch_shapes=[pltpu.VMEM((cap, d), BF16),
                        pltpu.VMEM((cap, d), F32),
                        pltpu.VMEM((tb, LANES), F32),
                        pltpu.VMEM((LANES, tb), F32),
                        pltpu.SMEM((N_EXPERTS,), I32)],
        compiler_params=_params("parallel", "arbitrary", "arbitrary"),
    )(x, gates, tri, w1, w3, w2)


def _res_ln_body(res_ref, y_ref, g_ref, b_ref, of_ref, ob_ref, *, alpha):
    out = _layer_norm_rows(alpha * res_ref[...] + y_ref[...], g_ref[...], b_ref[...])
    of_ref[...] = out
    ob_ref[...] = out.astype(BF16)


def _residual_ln(res, y, g, b, alpha, tm=512):
    m, d = res.shape
    tm = min(tm, m)
    row = pl.BlockSpec((tm, d), lambda i: (i, 0))
    vec = pl.BlockSpec((1, d), lambda i: (0, 0))
    return pl.pallas_call(
        functools.partial(_res_ln_body, alpha=alpha),
        grid=(m // tm,),
        in_specs=[row, row, vec, vec],
        out_specs=[row, row],
        out_shape=[jax.ShapeDtypeStruct((m, d), F32), jax.ShapeDtypeStruct((m, d), BF16)],
        compiler_params=_params("parallel"),
    )(res, y, g.reshape(1, d), b.reshape(1, d))


def _band_bias(table):
    band = (LEFT_CHUNKS + 1) * CHUNK
    left = LEFT_CHUNKS * CHUNK
    rev = table.astype(F32)[:, ::-1]
    pad = left - REL_MAX + CHUNK - 1
    n_rev = band - 1 - (left - REL_MAX) + 1
    assert n_rev <= 2 * REL_MAX + 1
    ext = jnp.concatenate([jnp.broadcast_to(rev[:, :1], (rev.shape[0], pad)), rev[:, :n_rev]], axis=1)
    n_ext = ext.shape[1]
    period = jnp.concatenate([ext[:, CHUNK - 1:], jnp.zeros((ext.shape[0], 1), F32), ext[:, :CHUNK - 1]], axis=1)
    skew = jnp.tile(period, (1, CHUNK))[:, :CHUNK * n_ext].reshape(-1, CHUNK, n_ext)
    return skew[:, :, :band]


def kernel(x, mem, w_in, lam_q1, lam_k1, lam_q2, lam_k2, diff_norm_g, rel_bias, w_out,
           ln1_g, ln1_b, xattn_wq, xattn_wk, xattn_wv, xattn_wo, ln2_g, ln2_b,
           ffn_w1, ffn_w3, ffn_w2, moe_router, moe_w1, moe_w3, moe_w2, ln3_g, ln3_b):
    b, s, d = x.shape
    depth = w_in.shape[0]
    t = b * s
    alpha = (2.0 * depth) ** 0.25
    slopes_a, slopes_c = _alibi_slopes()
    n_main = 3 * H_A * DV_A + 3 * H_B * DH_B + 3 * H_C * DH_C + H_IDX * D_IDX
    n_tail = D_IDX + H_IDX
    wa, wb, wc = H_A * DV_A, H_B * DH_B, H_C * DH_C
    left = LEFT_CHUNKS * CHUNK

    h = x.reshape(t, d)
    hb = h.astype(BF16)
    memb = mem.reshape(-1, d).astype(BF16)
    for l in range(depth):
        wl = w_in[l]
        o_b, o_c, o_i = 3 * wa, 3 * wa + 3 * wb, 3 * wa + 3 * wb + 3 * wc
        w_main = jnp.concatenate(
            [wl[:, :wa] * (DQK_A ** -0.5 * LOG2E), wl[:, wa:2 * wa], wl[:, o_i:o_i + H_IDX * D_IDX],
             wl[:, o_b:o_b + wb] * (DH_B ** -0.5 * LOG2E), wl[:, o_b + wb:o_b + 2 * wb],
             wl[:, o_c:o_c + wc] * (DH_C ** -0.5 * LOG2E), wl[:, o_c + wc:o_c + 2 * wc]], axis=1).astype(BF16)
        proj = _matmul(hb, w_main, BF16, tn=w_main.shape[1] // 2).reshape(b, s, -1)
        w_side = jnp.concatenate(
            [wl[:, 2 * wa:3 * wa], wl[:, o_c + 2 * wc:o_c + 3 * wc], wl[:, o_b + 2 * wb:o_b + 3 * wb],
             wl[:, n_main:], jnp.zeros((d, LANES - n_tail), F32)], axis=1).astype(BF16)
        tile = min(ATT_TILE, s)
        vt, vbt, kiw = _side_projections(hb, w_side, wa + wc, tile, wb, LANES)
        vt = vt.reshape(b, s // tile, wa + wc, tile)
        kiw = kiw.reshape(b, s, LANES)
        lam_init = 0.8 - 0.6 * math.exp(-0.3 * l)
        ya = _diff_attention(proj, vt, lam_q1[l], lam_k1[l], lam_q2[l], lam_k2[l], diff_norm_g[l],
                             slopes_a, lam_init)
        kpad = jnp.pad(proj[:, :, 3 * wa + wb:3 * wa + 2 * wb], ((0, 0), (left, 0), (0, 0)))
        vbt = jnp.pad(vbt.reshape(b, s // LANES, wb, LANES), ((0, 0), (left // LANES, 0), (0, 0), (0, 0)))
        yb = _band_attention(proj, kpad, vbt, _band_bias_table(rel_bias[l]))
        yc = _dsa_attention(proj, vt, kiw, slopes_c)
        wo = w_out[l].astype(BF16)
        h, hb = _matmul_ln(
            [ya.reshape(t, wa), yb.reshape(t, wb), yc.reshape(t, -1)],
            [wo[:wa], wo[wa:wa + wb], wo[wa + wb:]],
            h, ln1_g[l], ln1_b[l], alpha)
        w_kv = jnp.concatenate([xattn_wk[l], xattn_wv[l]], axis=1).astype(BF16)
        kv = _matmul(memb, w_kv, BF16, tn=d).reshape(b, -1, 2 * d)
        h, hb = _xattn_ln(hb.reshape(b, s, d), h.reshape(b, s, d), xattn_wq[l].astype(BF16), kv,
                          xattn_wo[l].astype(BF16), ln2_g[l], ln2_b[l], alpha)
        h, hb = h.reshape(t, d), hb.reshape(t, d)
        if l % 2 == 0:
            h, hb = _ffn_ln(hb, ffn_w1[l // 2].astype(BF16), ffn_w3[l // 2].astype(BF16),
                            ffn_w2[l // 2].astype(BF16), h, ln3_g[l], ln3_b[l], alpha)
        else:
            gates = _router_gates(hb, moe_router[l // 2])
            y = _moe_experts(hb, gates, moe_w1[l // 2].astype(BF16), moe_w3[l // 2].astype(BF16),
                             moe_w2[l // 2].astype(BF16))
            h, hb = _residual_ln(h, y, ln3_g[l], ln3_b[l], alpha)
    return h.reshape(b, s, d)
```

```python
import functools
import math

import numpy as np
import jax
import jax.numpy as jnp
from jax import lax
from jax.experimental import pallas as pl
from jax.experimental.pallas import tpu as pltpu

F32 = jnp.float32
BF16 = jnp.bfloat16
I32 = jnp.int32
I16 = jnp.int16

CHUNK = 64
H_A, DQK_A = 4, 64
DV_A = 2 * DQK_A
H_B, DH_B = 4, 64
LEFT_CHUNKS = 8
REL_MAX = 128
H_C, DH_C = 4, 64
H_IDX, D_IDX = 8, 64
TOPK_MAX = 256
H_X = 4
N_EXPERTS = 8
LN_EPS = 1e-5
RMS_EPS = 1e-6
NEG_INF = -1e30
MASKED = -2e30

LANES = 128
SUBLANES = 8
VMEM_LIMIT = 56 * 1024 * 1024
ATT_TILE = 256

_NT = (((1,), (1,)), ((), ()))


def _alibi_slopes():
    n = H_A + H_C
    s = (2.0 ** (-8.0 * np.arange(1, n + 1) / n)).astype(np.float32)
    return [float(v) for v in s[0::2]], [float(v) for v in s[1::2]]


def _params(*sem):
    return pltpu.CompilerParams(dimension_semantics=sem, vmem_limit_bytes=VMEM_LIMIT)


def _dot(a, b):
    return jnp.dot(a, b, preferred_element_type=F32)


SCORE_ROW_CHUNK = 128


def _dot_row_chunks(a, b):
    rows = a.shape[0]
    step = min(SCORE_ROW_CHUNK, rows)
    return jnp.concatenate([_dot(a[r:r + step], b) for r in range(0, rows, step)], axis=0)


def _dot_nt(a, b):
    return lax.dot_general(a, b, _NT, preferred_element_type=F32)


def _layer_norm_rows(y, g, b):
    mu = jnp.mean(y, axis=-1, keepdims=True)
    d = y - mu
    var = jnp.mean(d * d, axis=-1, keepdims=True)
    return d * lax.rsqrt(var + LN_EPS) * g + b


def _silu(a):
    return a / (1.0 + jnp.exp(-a))


def _mm_body(x_ref, w_ref, o_ref):
    o_ref[...] = _dot(x_ref[...], w_ref[...]).astype(o_ref.dtype)


def _matmul(x, w, out_dtype, tm=1024, tn=512):
    m, k = x.shape
    n = w.shape[1]
    tm, tn = min(tm, m), min(tn, n)
    return pl.pallas_call(
        _mm_body,
        grid=(m // tm, n // tn),
        in_specs=[pl.BlockSpec((tm, k), lambda i, j: (i, 0)),
                  pl.BlockSpec((k, tn), lambda i, j: (0, j))],
        out_specs=pl.BlockSpec((tm, tn), lambda i, j: (i, j)),
        out_shape=jax.ShapeDtypeStruct((m, n), out_dtype),
        compiler_params=_params("parallel", "parallel"),
    )(x, w)


def _side_proj_body(x_ref, w_ref, vt_ref, vbt_ref, tail_ref):
    y = _dot(x_ref[...], w_ref[...])
    n_v, n_vb = vt_ref.shape[1], vbt_ref.shape[1]
    tile, tile_b = vt_ref.shape[2], vbt_ref.shape[2]
    for r in range(vt_ref.shape[0]):
        vt_ref[r] = y[r * tile:(r + 1) * tile, :n_v].T.astype(BF16)
    for r in range(vbt_ref.shape[0]):
        vbt_ref[r] = y[r * tile_b:(r + 1) * tile_b, n_v:n_v + n_vb].T.astype(BF16)
    tail_ref[...] = y[:, n_v + n_vb:]


def _side_projections(x, w, n_v, tile, n_vb, tile_b, tm=1024):
    m, k = x.shape
    n = w.shape[1]
    n_tail = n - n_v - n_vb
    tm = min(tm, m)
    return pl.pallas_call(
        _side_proj_body,
        grid=(m // tm,),
        in_specs=[pl.BlockSpec((tm, k), lambda i: (i, 0)),
                  pl.BlockSpec((k, n), lambda i: (0, 0))],
        out_specs=[pl.BlockSpec((tm // tile, n_v, tile), lambda i: (i, 0, 0)),
                   pl.BlockSpec((tm // tile_b, n_vb, tile_b), lambda i: (i, 0, 0)),
                   pl.BlockSpec((tm, n_tail), lambda i: (i, 0))],
        out_shape=[jax.ShapeDtypeStruct((m // tile, n_v, tile), BF16),
                   jax.ShapeDtypeStruct((m // tile_b, n_vb, tile_b), BF16),
                   jax.ShapeDtypeStruct((m, n_tail), F32)],
        compiler_params=_params("parallel"),
    )(x, w)


def _mm_ln_body(*refs, n_in, alpha):
    xs, ws = refs[:n_in], refs[n_in:2 * n_in]
    res_ref, g_ref, b_ref, of_ref, ob_ref = refs[2 * n_in:]
    acc = _dot(xs[0][...], ws[0][...])
    for x_ref, w_ref in zip(xs[1:], ws[1:]):
        acc = acc + _dot(x_ref[...], w_ref[...])
    out = _layer_norm_rows(alpha * res_ref[...] + acc, g_ref[...], b_ref[...])
    of_ref[...] = out
    ob_ref[...] = out.astype(BF16)


def _matmul_ln(xs, ws, res, g, b, alpha):
    m, d = res.shape
    k_total = sum(x.shape[1] for x in xs)
    tm = min(1024 if k_total <= 1024 else 512, m)
    n_in = len(xs)
    in_specs = ([pl.BlockSpec((tm, x.shape[1]), lambda i: (i, 0)) for x in xs]
                + [pl.BlockSpec(w.shape, lambda i: (0, 0)) for w in ws]
                + [pl.BlockSpec((tm, d), lambda i: (i, 0)),
                   pl.BlockSpec((1, d), lambda i: (0, 0)),
                   pl.BlockSpec((1, d), lambda i: (0, 0))])
    return pl.pallas_call(
        functools.partial(_mm_ln_body, n_in=n_in, alpha=alpha),
        grid=(m // tm,),
        in_specs=in_specs,
        out_specs=[pl.BlockSpec((tm, d), lambda i: (i, 0)),
                   pl.BlockSpec((tm, d), lambda i: (i, 0))],
        out_shape=[jax.ShapeDtypeStruct((m, d), F32), jax.ShapeDtypeStruct((m, d), BF16)],
        compiler_params=_params("parallel"),
    )(*xs, *ws, res, g.reshape(1, d), b.reshape(1, d))


def _ffn_ln_body(x_ref, w1_ref, w3_ref, w2_ref, res_ref, g_ref, b_ref, of_ref, ob_ref, acc_ref, *, alpha):
    f = pl.program_id(1)
    x = x_ref[...]
    a = _dot(x, w1_ref[...])
    b = _dot(x, w3_ref[...])
    y = _dot((_silu(a) * b).astype(BF16), w2_ref[...])

    @pl.when(f == 0)
    def _():
        acc_ref[...] = y

    @pl.when(f > 0)
    def _():
        acc_ref[...] += y

    @pl.when(f == pl.num_programs(1) - 1)
    def _():
        out = _layer_norm_rows(alpha * res_ref[...] + acc_ref[...], g_ref[...], b_ref[...])
        of_ref[...] = out
        ob_ref[...] = out.astype(BF16)


def _ffn_ln(x, w1, w3, w2, res, g, b, alpha, tm=512, tf=1408):
    m, d = x.shape
    ff = w1.shape[1]
    tm = min(tm, m)
    if ff % tf:
        tf = ff
    row = pl.BlockSpec((tm, d), lambda i, f: (i, 0))
    vec = pl.BlockSpec((1, d), lambda i, f: (0, 0))
    return pl.pallas_call(
        functools.partial(_ffn_ln_body, alpha=alpha),
        grid=(m // tm, ff // tf),
        in_specs=[row,
                  pl.BlockSpec((d, tf), lambda i, f: (0, f)),
                  pl.BlockSpec((d, tf), lambda i, f: (0, f)),
                  pl.BlockSpec((tf, d), lambda i, f: (f, 0)),
                  row, vec, vec],
        out_specs=[row, row],
        out_shape=[jax.ShapeDtypeStruct((m, d), F32), jax.ShapeDtypeStruct((m, d), BF16)],
        scratch_shapes=[pltpu.VMEM((tm, d), F32)],
        compiler_params=_params("parallel", "arbitrary"),
    )(x, w1, w3, w2, res, g.reshape(1, d), b.reshape(1, d))


def _key_query_iotas(tile):
    kpos = lax.broadcasted_iota(I32, (tile, tile), 0)
    qpos = lax.broadcasted_iota(I32, (tile, tile), 1)
    return kpos, qpos


def _scaled_transpose(x, scale):
    return (x.astype(F32) * scale).T.astype(BF16)


LOG2E = math.log2(math.e)
ALIBI_PARTS = 3
ALIBI_ROWS = 16


def _bf16_parts(x):
    parts, r = [], np.float32(x)
    for _ in range(ALIBI_PARTS):
        p = np.float32(r.astype(jnp.bfloat16))
        parts.append(float(p))
        r = np.float32(r - p)
    return parts


def _alibi_key_cols(tile, parts):
    lane = lax.broadcasted_iota(I32, (tile, LANES), 1)
    kr = lax.broadcasted_iota(I32, (tile, LANES), 0).astype(F32)
    v = jnp.where((lane >= ALIBI_PARTS) & (lane < 2 * ALIBI_PARTS), kr, 0.0)
    for n, c in enumerate(parts):
        v = jnp.where(lane == n, c, v)
    return v.astype(BF16)


def _alibi_query_rows(tile, parts):
    row = lax.broadcasted_iota(I32, (ALIBI_ROWS, tile), 0)
    qr = lax.broadcasted_iota(I32, (ALIBI_ROWS, tile), 1).astype(F32)
    v = jnp.where(row < ALIBI_PARTS, -qr, 0.0)
    for n, c in enumerate(parts):
        v = jnp.where(row == ALIBI_PARTS + n, c, v)
    return v.astype(BF16)


def _init_alibi(slopes, dneg_scr, kx_scr, qa_scr, tile, maps_per_head):
    kpos, qpos = _key_query_iotas(tile)
    rel = (qpos - kpos).astype(F32)
    qa_scr[...] = jnp.zeros(qa_scr.shape, BF16)
    for h, slope in enumerate(slopes):
        parts = _bf16_parts(slope * LOG2E)
        dneg_scr[h] = 2.0 * jnp.minimum(np.float32(slope * LOG2E) * rel, 0.0)
        kx_scr[h] = _alibi_key_cols(tile, parts)
        for mm in range(maps_per_head):
            qa_scr[h * maps_per_head + mm, LANES:LANES + ALIBI_ROWS, :] = _alibi_query_rows(tile, parts)


def _store_query_maps(qa_scr, qt, n_maps, dh):
    for idx in range(n_maps):
        r0 = (idx % 2) * dh
        qa_scr[idx, r0:r0 + dh, :] = qt[idx * dh:(idx + 1) * dh, :]


def _softmax_probs(c, t_ref, tmax_ref, p_ref, a_ref, m_ref, l_ref):
    m_old = m_ref[...]
    m_new = jnp.maximum(m_old, tmax_ref[...] - c)
    p = jnp.exp2(t_ref[...] - (m_new + c))
    a = jnp.exp2(m_old - m_new)
    l_ref[...] = a * l_ref[...] + jnp.sum(p, axis=0, keepdims=True)
    m_ref[...] = m_new
    a_ref[...] = a
    p_ref[...] = p.astype(BF16)


def _accumulate(vt, p_ref, a_ref, acc_ref):
    acc_ref[...] = a_ref[...] * acc_ref[...] + _dot(vt, p_ref[...])


SCORE_LEAD, PROB_LEAD = 4, 2
N_T_SLOTS, N_P_SLOTS = SCORE_LEAD - PROB_LEAD + 2, PROB_LEAD + 1


def _stage_scratch(tile, keys=None):
    keys = tile if keys is None else keys
    return ([pltpu.VMEM((keys, tile), F32)] * N_T_SLOTS + [pltpu.VMEM((1, tile), F32)] * N_T_SLOTS
            + [pltpu.VMEM((keys, tile), BF16)] * N_P_SLOTS + [pltpu.VMEM((1, tile), F32)] * N_P_SLOTS)


def _pipelined_maps(n_maps, scores, probs, accumulate, stage):
    nt, npb = N_T_SLOTS, N_P_SLOTS
    t_slots, x_slots = stage[:nt], stage[nt:2 * nt]
    p_slots, a_slots = stage[2 * nt:2 * nt + npb], stage[2 * nt + npb:]

    def stage_scores(k):
        s = scores(k)
        t_slots[k % nt][...] = s
        x_slots[k % nt][...] = jnp.max(s, axis=0, keepdims=True)

    def stage_probs(k):
        probs(k, t_slots[k % nt], x_slots[k % nt], p_slots[k % npb], a_slots[k % npb])

    for step in range(-SCORE_LEAD, n_maps):
        if 0 <= step + SCORE_LEAD < n_maps:
            stage_scores(step + SCORE_LEAD)
        if 0 <= step + PROB_LEAD < n_maps:
            stage_probs(step + PROB_LEAD)
        if step >= 0:
            accumulate(step, p_slots[step % npb], a_slots[step % npb])


def _diff_body(q_ref, k_ref, vt_ref, lq1_ref, lk1_ref, lq2_ref, lk2_ref, gain_ref, o_ref,
               qa_scr, kx_scr, dneg_scr, m_scr, l_scr, acc_scr, *stage,
               tile, slopes, lam_init):
    i = pl.program_id(1)
    kpos, qpos = _key_query_iotas(tile)

    @pl.when((pl.program_id(0) == 0) & (i == 0))
    def _():
        _init_alibi(slopes, dneg_scr, kx_scr, qa_scr, tile, 2)

    _store_query_maps(qa_scr, _scaled_transpose(q_ref[0], 1.0), 2 * H_A, DQK_A)
    m_scr[...] = jnp.full(m_scr.shape, NEG_INF, F32)
    l_scr[...] = jnp.zeros(l_scr.shape, F32)
    acc_scr[...] = jnp.zeros(acc_scr.shape, F32)

    def block(j, diag):
        k0 = pl.multiple_of(j * tile, tile)
        kb = k_ref[0, pl.ds(k0, tile), :]
        vtb = vt_ref[0, j]
        off = ((i - j) * tile).astype(F32)

        def scores(idx):
            h, pair = idx // 2, idx // 2
            keys = jnp.concatenate([kb[:, pair * LANES:(pair + 1) * LANES], kx_scr[h]], axis=1)
            s = _dot_row_chunks(keys, qa_scr[idx])
            if diag:
                allowed = (kpos // CHUNK) <= (qpos // CHUNK)
                return jnp.where(allowed, s + dneg_scr[h], MASKED)
            return s

        def probs(idx, t_ref, tmax_ref, p_ref, a_ref):
            shift = 0.0 if diag else (slopes[idx // 2] * LOG2E) * off
            _softmax_probs(shift, t_ref, tmax_ref, p_ref, a_ref, m_scr.at[idx], l_scr.at[idx])

        def accumulate(idx, p_ref, a_ref):
            h = idx // 2
            _accumulate(vtb[h * DV_A:(h + 1) * DV_A, :], p_ref, a_ref, acc_scr.at[idx])

        _pipelined_maps(2 * H_A, scores, probs, accumulate, stage)

    def full_block(j, carry):
        block(j, False)
        return carry

    lax.fori_loop(0, i, full_block, 0)
    block(i, True)

    lam = (jnp.exp(jnp.sum(lq1_ref[...] * lk1_ref[...], axis=-1, keepdims=True))
           - jnp.exp(jnp.sum(lq2_ref[...] * lk2_ref[...], axis=-1, keepdims=True)) + lam_init)
    gain = gain_ref[...] * (1.0 - lam_init)
    outs = []
    for h in range(H_A):
        o = acc_scr[2 * h] / l_scr[2 * h] - lam * (acc_scr[2 * h + 1] / l_scr[2 * h + 1])
        o = o * lax.rsqrt(jnp.mean(o * o, axis=0, keepdims=True) + RMS_EPS)
        outs.append((o * gain).T)
    o_ref[0] = jnp.concatenate(outs, axis=-1).astype(BF16)


def _diff_attention(proj, vt, lq1, lk1, lq2, lk2, gain, slopes, lam_init):
    b, s, _ = proj.shape
    tile = min(ATT_TILE, s)
    nkb = s // tile
    w = H_A * DV_A
    vec = lambda a: a.reshape(1, -1).astype(F32)
    small = lambda n: pl.BlockSpec((1, n), lambda bi, i: (0, 0))
    return pl.pallas_call(
        functools.partial(_diff_body, tile=tile, slopes=slopes, lam_init=lam_init),
        grid=(b, nkb),
        in_specs=[pl.BlockSpec((1, tile, w), lambda bi, i: (bi, i, 0)),
                  pl.BlockSpec((1, s, w), lambda bi, i: (bi, 0, 1)),
                  pl.BlockSpec((1, nkb, w, tile), lambda bi, i: (bi, 0, 0, 0)),
                  small(DQK_A), small(DQK_A), small(DQK_A), small(DQK_A),
                  pl.BlockSpec((DV_A, 1), lambda bi, i: (0, 0))],
        out_specs=pl.BlockSpec((1, tile, w), lambda bi, i: (bi, i, 0)),
        out_shape=jax.ShapeDtypeStruct((b, s, w), BF16),
        scratch_shapes=[pltpu.VMEM((2 * H_A, 2 * LANES, tile), BF16),
                        pltpu.VMEM((H_A, tile, LANES), BF16),
                        pltpu.VMEM((H_A, tile, tile), F32),
                        pltpu.VMEM((2 * H_A, 1, tile), F32),
                        pltpu.VMEM((2 * H_A, 1, tile), F32),
                        pltpu.VMEM((2 * H_A, DV_A, tile), F32)] + _stage_scratch(tile),
        compiler_params=_params("arbitrary", "arbitrary"),
    )(proj, proj, vt, vec(lq1), vec(lk1), vec(lq2), vec(lk2), gain.reshape(-1, 1).astype(F32))


BAND_CHUNKS = LANES // CHUNK
BAND_WINDOW = (LEFT_CHUNKS + BAND_CHUNKS) * CHUNK


def _band_body(q_ref, k_ref, vt_ref, bias_ref, o_ref, qz_scr, ot_scr, *stage):
    i = pl.program_id(1)
    ntile = BAND_WINDOW // LANES

    @pl.when((pl.program_id(0) == 0) & (i == 0))
    def _():
        qz_scr[...] = jnp.zeros(qz_scr.shape, BF16)

    qt = _scaled_transpose(q_ref[0], 1.0)
    for h in range(H_B):
        r0 = (h % 2) * DH_B
        qz_scr[h, r0:r0 + DH_B, :] = qt[h * DH_B:(h + 1) * DH_B, :]

    start = pl.multiple_of(i * LANES, LANES)
    kw = k_ref[0, pl.ds(start, BAND_WINDOW), :]
    vt = vt_ref[0, pl.ds(i, ntile)]
    kk = lax.broadcasted_iota(I32, (BAND_WINDOW, LANES), 0)
    valid = kk >= LEFT_CHUNKS * CHUNK - start

    def scores(h):
        pair = h // 2
        s = _dot_row_chunks(kw[:, pair * LANES:(pair + 1) * LANES], qz_scr[h])
        return jnp.where(valid, s + bias_ref[h], MASKED)

    def probs(h, t_ref, tmax_ref, p_ref, l_ref):
        p = jnp.exp2(t_ref[...] - tmax_ref[...])
        l_ref[...] = jnp.sum(p, axis=0, keepdims=True)
        p_ref[...] = p.astype(BF16)

    def accumulate(h, p_ref, l_ref):
        vth = jnp.concatenate([vt[n, h * DH_B:(h + 1) * DH_B, :] for n in range(ntile)], axis=1)
        ot_scr[h * DH_B:(h + 1) * DH_B, :] = _dot(vth, p_ref[...]) / l_ref[...]

    _pipelined_maps(H_B, scores, probs, accumulate, stage)
    o_ref[0] = ot_scr[...].T.astype(BF16)


def _band_bias_table(table):
    bias = _band_bias(table).transpose(0, 2, 1) * LOG2E
    cols = [jnp.pad(bias, ((0, 0), (c * CHUNK, (BAND_CHUNKS - 1 - c) * CHUNK), (0, 0)), constant_values=MASKED)
            for c in range(BAND_CHUNKS)]
    return jnp.concatenate(cols, axis=2)


def _band_attention(proj, kpad, vt_pad, bias):
    b, s, _ = proj.shape
    w = H_B * DH_B
    sp = kpad.shape[1]
    nvt = vt_pad.shape[1]
    return pl.pallas_call(
        _band_body,
        grid=(b, s // LANES),
        in_specs=[pl.BlockSpec((1, LANES, w), lambda bi, i: (bi, i, 6)),
                  pl.BlockSpec((1, sp, w), lambda bi, i: (bi, 0, 0)),
                  pl.BlockSpec((1, nvt, w, LANES), lambda bi, i: (bi, 0, 0, 0)),
                  pl.BlockSpec((H_B, BAND_WINDOW, LANES), lambda bi, i: (0, 0, 0))],
        out_specs=pl.BlockSpec((1, LANES, w), lambda bi, i: (bi, i, 0)),
        out_shape=jax.ShapeDtypeStruct((b, s, w), BF16),
        scratch_shapes=[pltpu.VMEM((H_B, LANES, LANES), BF16),
                        pltpu.VMEM((w, LANES), F32)] + _stage_scratch(LANES, BAND_WINDOW),
        compiler_params=_params("arbitrary", "arbitrary"),
    )(proj, kpad, vt_pad, bias)


def _sortable(x):
    bits = pltpu.bitcast(x + 0.0, I32)
    return bits ^ ((bits >> 31) & 0x7FFFFFFF)


_INT_MIN = -2 ** 31
_VALID_KEY = int(np.array(0.5 * NEG_INF, np.float32).view(np.int32))
_VALID_KEY = _VALID_KEY ^ ((_VALID_KEY >> 31) & 0x7FFFFFFF)


def _dsa_body(qc_ref, kc_ref, vct_ref, qi_ref, kiw_ref, wq_ref, o_ref,
              qa_scr, kx_scr, dneg_scr, qit_scr, key_scr, hi_scr, lo_scr, jstar_scr,
              m_scr, l_scr, acc_scr, *stage,
              tile, topk, slopes, seq):
    i = pl.program_id(1)
    nblk = i + 1
    kpos, qpos = _key_query_iotas(tile)

    @pl.when((pl.program_id(0) == 0) & (i == 0))
    def _():
        _init_alibi(slopes, dneg_scr, kx_scr, qa_scr, tile, 1)

    qit_scr[...] = _scaled_transpose(qi_ref[0], D_IDX ** -0.5)
    wt = wq_ref[0].T
    wrow = [wt[D_IDX + h:D_IDX + h + 1, :] * H_IDX ** -0.5 for h in range(H_IDX)]

    def score_block(j, diag):
        k0 = pl.multiple_of(j * tile, tile)
        kib = kiw_ref[0, pl.ds(k0, tile), :][:, :D_IDX].astype(BF16)
        isc = jnp.zeros((tile, tile), F32)
        for h in range(H_IDX):
            d = _dot(kib, qit_scr[h * D_IDX:(h + 1) * D_IDX, :])
            isc = isc + jnp.maximum(d, 0.0) * wrow[h]
        if diag:
            isc = jnp.where((kpos // CHUNK) <= (qpos // CHUNK), isc, NEG_INF)
        key = _sortable(isc)
        key_scr[pl.ds(k0, tile), :] = key
        hi_scr[pl.ds(k0, tile), :] = (key >> 16).astype(I16)
        lo_scr[pl.ds(k0, tile), :] = ((key & 0xFFFF) - 2 ** 15).astype(I16)

    def score_full(j, carry):
        score_block(j, False)
        return carry

    lax.fori_loop(0, i, score_full, 0)
    score_block(i, True)

    def count(pred):
        def body(j, part):
            k0 = pl.multiple_of(j * tile, tile)
            hit = jnp.where(pred(key_scr[pl.ds(k0, tile), :], kpos + k0), 1.0, 0.0)
            return part + jnp.sum(hit.reshape(tile // SUBLANES, SUBLANES, tile), axis=0)
        part = lax.fori_loop(0, nblk, body, jnp.zeros((SUBLANES, tile), F32))
        return jnp.sum(part, axis=0, keepdims=True)

    kf = float(topk)
    rows16 = 2 * SUBLANES

    pair = 2 if (seq // tile) % 2 == 0 else 1
    span = pair * tile
    lowest16 = jnp.full((tile, tile), -2 ** 15, I16)

    @pl.when(nblk % pair == 1)
    def _():
        pad_rows = pl.ds(pl.multiple_of(nblk * tile, tile), tile)
        hi_scr[pad_rows, :] = lowest16
        lo_scr[pad_rows, :] = lowest16

    def count16(src_ref, pred):
        def body(j, part):
            k0 = pl.multiple_of(j * span, span)
            hit = jnp.where(pred(src_ref[pl.ds(k0, span), :]), jnp.int16(1), jnp.int16(0))
            for g in range(span // rows16):
                part = part + hit[g * rows16:(g + 1) * rows16, :]
            return part
        part = lax.fori_loop(0, (nblk + pair - 1) // pair, body, jnp.zeros((rows16, tile), I16))
        return jnp.sum(part.astype(I32), axis=0, keepdims=True).astype(F32)

    def search16(src_ref, base):
        def bit(p, u):
            cand_u = u | jnp.left_shift(jnp.int32(1), 15 - p)
            cand = (cand_u - 2 ** 15).astype(I16)
            cnt = base + count16(src_ref, lambda v: v >= cand)
            return jnp.where(cnt >= kf, cand_u, u)
        return lax.fori_loop(0, 16, bit, jnp.zeros((1, tile), I32))

    thr_hi = search16(hi_scr, 0.0) - 2 ** 15
    thr_hi16 = thr_hi.astype(I16)
    above = count16(hi_scr, lambda v: v > thr_hi16)

    def mask_low(j, carry):
        k0 = pl.multiple_of(j * tile, tile)
        rows = pl.ds(k0, tile)
        lo_scr[rows, :] = jnp.where(hi_scr[rows, :] == thr_hi16, lo_scr[rows, :], jnp.int16(-2 ** 15))
        return carry

    lax.fori_loop(0, nblk, mask_low, 0)
    thr = jnp.left_shift(thr_hi, 16) + search16(lo_scr, above)

    thr_b = thr
    c_gt = count(lambda key, idx: key > thr_b)
    c_ge = count(lambda key, idx: key >= thr_b)
    tied = (c_ge > kf) & (thr > _VALID_KEY)
    need = kf - c_gt
    jstar_scr[...] = jnp.full((1, tile), seq, I32)

    @pl.when(jnp.max(jnp.where(tied, 1.0, 0.0)) > 0.0)
    def _():
        nbits = max(1, int(math.ceil(math.log2(seq))))

        def index_bit(p, lo):
            cand = lo | jnp.left_shift(jnp.int32(1), nbits - 1 - p)
            cnt = count(lambda key, idx: (key == thr_b) & (idx < cand))
            return jnp.where(cnt < need, cand, lo)

        lo = lax.fori_loop(0, nbits, index_bit, jnp.zeros((1, tile), I32))
        jstar_scr[...] = jnp.where(tied, lo, seq)

    jstar = jstar_scr[...]
    thr2 = jnp.maximum(thr, _VALID_KEY + 1)

    _store_query_maps(qa_scr, _scaled_transpose(qc_ref[0], 1.0), H_C, DH_C)
    m_scr[...] = jnp.full(m_scr.shape, NEG_INF, F32)
    l_scr[...] = jnp.zeros(l_scr.shape, F32)
    acc_scr[...] = jnp.zeros(acc_scr.shape, F32)

    def attend(j, diag):
        k0 = pl.multiple_of(j * tile, tile)
        key = key_scr[pl.ds(k0, tile), :]
        sel = (key > thr2) | ((key == thr2) & (kpos + k0 <= jstar))
        kb = kc_ref[0, pl.ds(k0, tile), :]
        vtb = vct_ref[0, j]
        off = ((i - j) * tile).astype(F32)

        def scores(h):
            pair = h // 2
            keys = jnp.concatenate([kb[:, pair * LANES:(pair + 1) * LANES], kx_scr[h]], axis=1)
            s = _dot_row_chunks(keys, qa_scr[h])
            if diag:
                s = s + dneg_scr[h]
            return jnp.where(sel, s, MASKED)

        def probs(h, t_ref, tmax_ref, p_ref, a_ref):
            shift = 0.0 if diag else (slopes[h] * LOG2E) * off
            _softmax_probs(shift, t_ref, tmax_ref, p_ref, a_ref, m_scr.at[h], l_scr.at[h])

        def accumulate(h, p_ref, a_ref):
            _accumulate(vtb[h * DH_C:(h + 1) * DH_C, :], p_ref, a_ref, acc_scr.at[h])

        _pipelined_maps(H_C, scores, probs, accumulate, stage)

    def attend_full(j, carry):
        attend(j, False)
        return carry

    lax.fori_loop(0, i, attend_full, 0)
    attend(i, True)
    outs = [(acc_scr[h] / l_scr[h]).T for h in range(H_C)]
    o_ref[0] = jnp.concatenate(outs, axis=-1).astype(BF16)


def _dsa_attention(proj, vt, kiw, slopes):
    b, s, _ = proj.shape
    tile = min(ATT_TILE, s)
    nkb = s // tile
    w = H_C * DH_C
    wa = H_A * DV_A
    topk = min(TOPK_MAX, s // 4)
    return pl.pallas_call(
        functools.partial(_dsa_body, tile=tile, topk=topk, slopes=slopes, seq=s),
        grid=(b, nkb),
        in_specs=[pl.BlockSpec((1, tile, w), lambda bi, i: (bi, i, 8)),
                  pl.BlockSpec((1, s, w), lambda bi, i: (bi, 0, 9)),
                  pl.BlockSpec((1, nkb, w, tile), lambda bi, i: (bi, 0, wa // w, 0)),
                  pl.BlockSpec((1, tile, H_IDX * D_IDX), lambda bi, i: (bi, i, 2)),
                  pl.BlockSpec((1, s, LANES), lambda bi, i: (bi, 0, 0)),
                  pl.BlockSpec((1, tile, LANES), lambda bi, i: (bi, i, 0))],
        out_specs=pl.BlockSpec((1, tile, w), lambda bi, i: (bi, i, 0)),
        out_shape=jax.ShapeDtypeStruct((b, s, w), BF16),
        scratch_shapes=[pltpu.VMEM((H_C, 2 * LANES, tile), BF16),
                        pltpu.VMEM((H_C, tile, LANES), BF16),
                        pltpu.VMEM((H_C, tile, tile), F32),
                        pltpu.VMEM((H_IDX * D_IDX, tile), BF16),
                        pltpu.VMEM((s, tile), I32),
                        pltpu.VMEM((s, tile), I16),
                        pltpu.VMEM((s, tile), I16),
                        pltpu.VMEM((1, tile), I32),
                        pltpu.VMEM((H_C, 1, tile), F32),
                        pltpu.VMEM((H_C, 1, tile), F32),
                        pltpu.VMEM((H_C, DH_C, tile), F32)] + _stage_scratch(tile),
        compiler_params=_params("arbitrary", "arbitrary"),
    )(proj, proj, vt, proj, kiw, kiw)


def _xattn_body(x_ref, res_ref, wq_ref, k_ref, v_ref, wo_ref, g_ref, b_ref, of_ref, ob_ref, *, dh, alpha):
    q_all = _dot(x_ref[0], wq_ref[...]).astype(BF16)
    outs = []
    for h in range(H_X):
        c = h * dh
        q = q_all[:, c:c + dh] * dh ** -0.5
        s = _dot_nt(q, k_ref[0, :, c:c + dh])
        p = jnp.exp(s - jnp.max(s, axis=-1, keepdims=True))
        l = jnp.sum(p, axis=-1, keepdims=True)
        outs.append((_dot(p.astype(BF16), v_ref[0, :, c:c + dh]) / l).astype(BF16))
    y = _dot(jnp.concatenate(outs, axis=-1), wo_ref[...])
    out = _layer_norm_rows(alpha * res_ref[0] + y, g_ref[...], b_ref[...])
    of_ref[0] = out
    ob_ref[0] = out.astype(BF16)


def _xattn_ln(x, res, wq, kv, wo, g, b, alpha, tq=512):
    bsz, s, d = x.shape
    n_mem = kv.shape[1]
    tq = min(tq, s)
    tile = pl.BlockSpec((1, tq, d), lambda bi, i: (bi, i, 0))
    whole = lambda shape: pl.BlockSpec(shape, lambda bi, i: (0,) * len(shape))
    mem = lambda col: pl.BlockSpec((1, n_mem, d), lambda bi, i: (bi, 0, col))
    return pl.pallas_call(
        functools.partial(_xattn_body, dh=d // H_X, alpha=alpha),
        grid=(bsz, s // tq),
        in_specs=[tile, tile, whole((d, d)), mem(0), mem(1), whole((d, d)), whole((1, d)), whole((1, d))],
        out_specs=[tile, tile],
        out_shape=[jax.ShapeDtypeStruct((bsz, s, d), F32), jax.ShapeDtypeStruct((bsz, s, d), BF16)],
        compiler_params=_params("parallel", "parallel"),
    )(x, res, wq, kv, kv, wo, g.reshape(1, d), b.reshape(1, d))


def _router_body(x_ref, r_ref, g_ref):
    logits = _dot(x_ref[...], r_ref[...])
    lane = lax.broadcasted_iota(I32, logits.shape, 1).astype(F32)
    big = float(LANES)
    lg = jnp.where(lane < N_EXPERTS, logits, -jnp.inf)
    m1 = jnp.max(lg, axis=-1, keepdims=True)
    i1 = jnp.min(jnp.where(lg == m1, lane, big), axis=-1, keepdims=True)
    lg2 = jnp.where(lane == i1, -jnp.inf, lg)
    m2 = jnp.max(lg2, axis=-1, keepdims=True)
    i2 = jnp.min(jnp.where(lg2 == m2, lane, big), axis=-1, keepdims=True)
    e = jnp.exp(m2 - m1)
    den = 1.0 + e
    g_ref[...] = jnp.where(lane == i1, 1.0 / den, 0.0) + jnp.where(lane == i2, e / den, 0.0)


def _router_gates(x, router, tm=1024):
    m, d = x.shape
    tm = min(tm, m)
    rpad = jnp.zeros((d, LANES), BF16).at[:, :N_EXPERTS].set(router.astype(BF16))
    return pl.pallas_call(
        _router_body,
        grid=(m // tm,),
        in_specs=[pl.BlockSpec((tm, d), lambda i: (i, 0)),
                  pl.BlockSpec((d, LANES), lambda i: (0, 0))],
        out_specs=pl.BlockSpec((tm, LANES), lambda i: (i, 0)),
        out_shape=jax.ShapeDtypeStruct((m, LANES), F32),
        compiler_params=_params("parallel"),
    )(x, rpad)


MOE_BLOCK = 896
MOE_CHUNK = 128
MOE_GROUP = 2
MOE_FF_TILE = 1792


def _moe_body(x_ref, gate_ref, tri_ref, w1_ref, w3_ref, w2_ref, o_ref,
              xs_scr, acc_scr, rcol_scr, rrow_scr, cnt_smem, *, n_tokens):
    e = pl.program_id(1)
    f = pl.program_id(2)
    nf = pl.num_programs(2)
    tb = x_ref.shape[0]
    ch = MOE_CHUNK
    gw = MOE_GROUP * ch

    tok0 = pl.program_id(0) * tb
    valid_col = lax.broadcasted_iota(I32, (tb, 1), 0) + tok0 < n_tokens

    @pl.when((e == 0) & (f == 0))
    def _():
        o_ref[...] = jnp.zeros(o_ref.shape, F32)
        member = jnp.where((gate_ref[...] > 0.0) & valid_col, 1.0, 0.0)
        rank = _dot(tri_ref[...], member.astype(BF16))
        rcol = jnp.where(member > 0.0, rank, -1.0)
        rcol_scr[...] = rcol
        rrow_scr[...] = rcol.T
        counts = jnp.sum(member, axis=0, keepdims=True)
        for ee in range(N_EXPERTS):
            cnt_smem[ee] = counts[0, ee].astype(I32)

    count = cnt_smem[e]
    single = count <= gw
    nchunk = jnp.where(single, MOE_GROUP, (count + (ch - 1)) // ch)

    @pl.when(f == 0)
    def _():
        acc_scr[...] = jnp.zeros(acc_scr.shape, F32)
        sub = lax.broadcasted_iota(I32, (SUBLANES, tb), 0)
        r_row = jnp.sum(jnp.where(sub == e, rrow_scr[0:SUBLANES, :], 0.0), axis=0, keepdims=True)
        xz = jnp.where(valid_col, x_ref[...], jnp.zeros((), BF16))

        def gather_rows(r0, rows):
            row_id = lax.broadcasted_iota(I32, (rows, tb), 0).astype(F32) + jnp.asarray(r0, F32)
            sel = jnp.where(row_id == r_row, 1.0, 0.0).astype(BF16)
            xs_scr[pl.ds(r0, rows), :] = _dot(sel, xz).astype(BF16)

        @pl.when(single)
        def _():
            gather_rows(0, gw)

        @pl.when(jnp.logical_not(single))
        def _():
            def gather(c, carry):
                gather_rows(pl.multiple_of(c * ch, ch), ch)
                return carry
            lax.fori_loop(0, nchunk, gather, 0)

    def ffn_rows(r0, rows):
        xs = xs_scr[pl.ds(r0, rows), :]
        a = _dot(xs, w1_ref[0])
        b = _dot(xs, w3_ref[0])
        acc_scr[pl.ds(r0, rows), :] += _dot((_silu(a) * b).astype(BF16), w2_ref[0])

    @pl.when(single)
    def _():
        ffn_rows(0, gw)

    @pl.when(jnp.logical_not(single))
    def _():
        def ffn(c, carry):
            ffn_rows(pl.multiple_of(c * ch, ch), ch)
            return carry
        lax.fori_loop(0, nchunk, ffn, 0)

    @pl.when(f == nf - 1)
    def _():
        pick = lax.broadcasted_iota(I32, (tb, LANES), 1) == e
        r_col = jnp.sum(jnp.where(pick, rcol_scr[...], 0.0), axis=1, keepdims=True)
        g_col = jnp.sum(jnp.where(pick & valid_col, gate_ref[...], 0.0), axis=1, keepdims=True)
        col_id = lax.broadcasted_iota(I32, (tb, gw), 1).astype(F32)

        def combine(gi, carry):
            r0 = pl.multiple_of(gi * gw, gw)
            selt = jnp.where(col_id + r0.astype(F32) == r_col, 1.0, 0.0).astype(BF16)
            y = acc_scr[pl.ds(r0, gw), :]
            y_hi = y.astype(BF16)
            y_lo = (y - y_hi.astype(F32)).astype(BF16)
            o_ref[...] += g_col * (_dot(selt, y_hi) + _dot(selt, y_lo))
            return carry

        lax.fori_loop(0, (nchunk + (MOE_GROUP - 1)) // MOE_GROUP, combine, 0)


def _moe_experts(x, gates, w1, w3, w2):
    m, d = x.shape
    ne, _, ff = w1.shape
    tb = min(MOE_BLOCK, m)
    tf = MOE_FF_TILE if ff % MOE_FF_TILE == 0 else ff
    gw = MOE_GROUP * MOE_CHUNK
    cap = pl.cdiv(tb, gw) * gw
    assert tb % LANES == 0 and ne <= SUBLANES
    tri = jnp.asarray(np.tril(np.ones((tb, tb), np.float32), -1), BF16)
    row = lambda i, e, f: (i, 0)
    return pl.pallas_call(
        functools.partial(_moe_body, n_tokens=m),
        grid=(pl.cdiv(m, tb), ne, ff // tf),
        in_specs=[pl.BlockSpec((tb, d), row),
                  pl.BlockSpec((tb, LANES), row),
                  pl.BlockSpec((tb, tb), lambda i, e, f: (0, 0)),
                  pl.BlockSpec((1, d, tf), lambda i, e, f: (e, 0, f)),
                  pl.BlockSpec((1, d, tf), lambda i, e, f: (e, 0, f)),
                  pl.BlockSpec((1, tf, d), lambda i, e, f: (e, f, 0))],
        out_specs=pl.BlockSpec((tb, d), row),
        out_shape=jax.ShapeDtypeStruct((m, d), F32),
        scratch_shapes=[pltpu.VMEM((cap, d), BF16),
                        pltpu.VMEM((cap, d), F32),
                        pltpu.VMEM((tb, LANES), F32),
                        pltpu.VMEM((LANES, tb), F32),
                        pltpu.SMEM((N_EXPERTS,), I32)],
        compiler_params=_params("parallel", "arbitrary", "arbitrary"),
    )(x, gates, tri, w1, w3, w2)


def _res_ln_body(res_ref, y_ref, g_ref, b_ref, of_ref, ob_ref, *, alpha):
    out = _layer_norm_rows(alpha * res_ref[...] + y_ref[...], g_ref[...], b_ref[...])
    of_ref[...] = out
    ob_ref[...] = out.astype(BF16)


def _residual_ln(res, y, g, b, alpha, tm=512):
    m, d = res.shape
    tm = min(tm, m)
    row = pl.BlockSpec((tm, d), lambda i: (i, 0))
    vec = pl.BlockSpec((1, d), lambda i: (0, 0))
    return pl.pallas_call(
        functools.partial(_res_ln_body, alpha=alpha),
        grid=(m // tm,),
        in_specs=[row, row, vec, vec],
        out_specs=[row, row],
        out_shape=[jax.ShapeDtypeStruct((m, d), F32), jax.ShapeDtypeStruct((m, d), BF16)],
        compiler_params=_params("parallel"),
    )(res, y, g.reshape(1, d), b.reshape(1, d))


def _band_bias(table):
    band = (LEFT_CHUNKS + 1) * CHUNK
    left = LEFT_CHUNKS * CHUNK
    rev = table.astype(F32)[:, ::-1]
    pad = left - REL_MAX + CHUNK - 1
    n_rev = band - 1 - (left - REL_MAX) + 1
    assert n_rev <= 2 * REL_MAX + 1
    ext = jnp.concatenate([jnp.broadcast_to(rev[:, :1], (rev.shape[0], pad)), rev[:, :n_rev]], axis=1)
    n_ext = ext.shape[1]
    period = jnp.concatenate([ext[:, CHUNK - 1:], jnp.zeros((ext.shape[0], 1), F32), ext[:, :CHUNK - 1]], axis=1)
    skew = jnp.tile(period, (1, CHUNK))[:, :CHUNK * n_ext].reshape(-1, CHUNK, n_ext)
    return skew[:, :, :band]


def kernel(x, mem, w_in, lam_q1, lam_k1, lam_q2, lam_k2, diff_norm_g, rel_bias, w_out,
           ln1_g, ln1_b, xattn_wq, xattn_wk, xattn_wv, xattn_wo, ln2_g, ln2_b,
           ffn_w1, ffn_w3, ffn_w2, moe_router, moe_w1, moe_w3, moe_w2, ln3_g, ln3_b):
    b, s, d = x.shape
    depth = w_in.shape[0]
    t = b * s
    alpha = (2.0 * depth) ** 0.25
    slopes_a, slopes_c = _alibi_slopes()
    n_main = 3 * H_A * DV_A + 3 * H_B * DH_B + 3 * H_C * DH_C + H_IDX * D_IDX
    n_tail = D_IDX + H_IDX
    wa, wb, wc = H_A * DV_A, H_B * DH_B, H_C * DH_C
    left = LEFT_CHUNKS * CHUNK

    h = x.reshape(t, d)
    hb = h.astype(BF16)
    memb = mem.reshape(-1, d).astype(BF16)
    for l in range(depth):
        wl = w_in[l]
        o_b, o_c, o_i = 3 * wa, 3 * wa + 3 * wb, 3 * wa + 3 * wb + 3 * wc
        w_main = jnp.concatenate(
            [wl[:, :wa] * (DQK_A ** -0.5 * LOG2E), wl[:, wa:2 * wa], wl[:, o_i:o_i + H_IDX * D_IDX],
             wl[:, o_b:o_b + wb] * (DH_B ** -0.5 * LOG2E), wl[:, o_b + wb:o_b + 2 * wb],
             wl[:, o_c:o_c + wc] * (DH_C ** -0.5 * LOG2E), wl[:, o_c + wc:o_c + 2 * wc]], axis=1).astype(BF16)
        proj = _matmul(hb, w_main, BF16, tn=w_main.shape[1] // 2).reshape(b, s, -1)
        w_side = jnp.concatenate(
            [wl[:, 2 * wa:3 * wa], wl[:, o_c + 2 * wc:o_c + 3 * wc], wl[:, o_b + 2 * wb:o_b + 3 * wb],
             wl[:, n_main:], jnp.zeros((d, LANES - n_tail), F32)], axis=1).astype(BF16)
        tile = min(ATT_TILE, s)
        vt, vbt, kiw = _side_projections(hb, w_side, wa + wc, tile, wb, LANES)
        vt = vt.reshape(b, s // tile, wa + wc, tile)
        kiw = kiw.reshape(b, s, LANES)
        lam_init = 0.8 - 0.6 * math.exp(-0.3 * l)
        ya = _diff_attention(proj, vt, lam_q1[l], lam_k1[l], lam_q2[l], lam_k2[l], diff_norm_g[l],
                             slopes_a, lam_init)
        kpad = jnp.pad(proj[:, :, 3 * wa + wb:3 * wa + 2 * wb], ((0, 0), (left, 0), (0, 0)))
        vbt = jnp.pad(vbt.reshape(b, s // LANES, wb, LANES), ((0, 0), (left // LANES, 0), (0, 0), (0, 0)))
        yb = _band_attention(proj, kpad, vbt, _band_bias_table(rel_bias[l]))
        yc = _dsa_attention(proj, vt, kiw, slopes_c)
        wo = w_out[l].astype(BF16)
        h, hb = _matmul_ln(
            [ya.reshape(t, wa), yb.reshape(t, wb), yc.reshape(t, -1)],
            [wo[:wa], wo[wa:wa + wb], wo[wa + wb:]],
            h, ln1_g[l], ln1_b[l], alpha)
        w_kv = jnp.concatenate([xattn_wk[l], xattn_wv[l]], axis=1).astype(BF16)
        kv = _matmul(memb, w_kv, BF16, tn=d).reshape(b, -1, 2 * d)
        h, hb = _xattn_ln(hb.reshape(b, s, d), h.reshape(b, s, d), xattn_wq[l].astype(BF16), kv,
                          xattn_wo[l].astype(BF16), ln2_g[l], ln2_b[l], alpha)
        h, hb = h.reshape(t, d), hb.reshape(t, d)
        if l % 2 == 0:
            h, hb = _ffn_ln(hb, ffn_w1[l // 2].astype(BF16), ffn_w3[l // 2].astype(BF16),
                            ffn_w2[l // 2].astype(BF16), h, ln3_g[l], ln3_b[l], alpha)
        else:
            gates = _router_gates(hb, moe_router[l // 2])
            y = _moe_experts(hb, gates, moe_w1[l // 2].astype(BF16), moe_w3[l // 2].astype(BF16),
                             moe_w2[l // 2].astype(BF16))
            h, hb = _residual_ln(h, y, ln3_g[l], ln3_b[l], alpha)
    return h.reshape(b, s, d)
```

```python
import functools
import math

import numpy as np
import jax
import jax.numpy as jnp
from jax import lax
from jax.experimental import pallas as pl
from jax.experimental.pallas import tpu as pltpu

F32 = jnp.float32
BF16 = jnp.bfloat16
I32 = jnp.int32
I16 = jnp.int16

CHUNK = 64
H_A, DQK_A = 4, 64
DV_A = 2 * DQK_A
H_B, DH_B = 4, 64
LEFT_CHUNKS = 8
REL_MAX = 128
H_C, DH_C = 4, 64
H_IDX, D_IDX = 8, 64
TOPK_MAX = 256
H_X = 4
N_EXPERTS = 8
LN_EPS = 1e-5
RMS_EPS = 1e-6
NEG_INF = -1e30
MASKED = -2e30

LANES = 128
SUBLANES = 8
VMEM_LIMIT = 56 * 1024 * 1024
ATT_TILE = 256

_NT = (((1,), (1,)), ((), ()))


def _alibi_slopes():
    n = H_A + H_C
    s = (2.0 ** (-8.0 * np.arange(1, n + 1) / n)).astype(np.float32)
    return [float(v) for v in s[0::2]], [float(v) for v in s[1::2]]


def _params(*sem):
    return pltpu.CompilerParams(dimension_semantics=sem, vmem_limit_bytes=VMEM_LIMIT)


def _dot(a, b):
    return jnp.dot(a, b, preferred_element_type=F32)


SCORE_ROW_CHUNK = 128


def _dot_row_chunks(a, b):
    rows = a.shape[0]
    step = min(SCORE_ROW_CHUNK, rows)
    return jnp.concatenate([_dot(a[r:r + step], b) for r in range(0, rows, step)], axis=0)


def _dot_nt(a, b):
    return lax.dot_general(a, b, _NT, preferred_element_type=F32)


def _layer_norm_rows(y, g, b):
    mu = jnp.mean(y, axis=-1, keepdims=True)
    d = y - mu
    var = jnp.mean(d * d, axis=-1, keepdims=True)
    return d * lax.rsqrt(var + LN_EPS) * g + b


def _silu(a):
    return a / (1.0 + jnp.exp(-a))


def _mm_body(x_ref, w_ref, o_ref):
    o_ref[...] = _dot(x_ref[...], w_ref[...]).astype(o_ref.dtype)


def _matmul(x, w, out_dtype, tm=1024, tn=512):
    m, k = x.shape
    n = w.shape[1]
    tm, tn = min(tm, m), min(tn, n)
    return pl.pallas_call(
        _mm_body,
        grid=(m // tm, n // tn),
        in_specs=[pl.BlockSpec((tm, k), lambda i, j: (i, 0)),
                  pl.BlockSpec((k, tn), lambda i, j: (0, j))],
        out_specs=pl.BlockSpec((tm, tn), lambda i, j: (i, j)),
        out_shape=jax.ShapeDtypeStruct((m, n), out_dtype),
        compiler_params=_params("parallel", "parallel"),
    )(x, w)


def _side_proj_body(x_ref, w_ref, vt_ref, vbt_ref, tail_ref):
    y = _dot(x_ref[...], w_ref[...])
    n_v, n_vb = vt_ref.shape[1], vbt_ref.shape[1]
    tile, tile_b = vt_ref.shape[2], vbt_ref.shape[2]
    for r in range(vt_ref.shape[0]):
        vt_ref[r] = y[r * tile:(r + 1) * tile, :n_v].T.astype(BF16)
    for r in range(vbt_ref.shape[0]):
        vbt_ref[r] = y[r * tile_b:(r + 1) * tile_b, n_v:n_v + n_vb].T.astype(BF16)
    tail_ref[...] = y[:, n_v + n_vb:]


def _side_projections(x, w, n_v, tile, n_vb, tile_b, tm=1024):
    m, k = x.shape
    n = w.shape[1]
    n_tail = n - n_v - n_vb
    tm = min(tm, m)
    return pl.pallas_call(
        _side_proj_body,
        grid=(m // tm,),
        in_specs=[pl.BlockSpec((tm, k), lambda i: (i, 0)),
                  pl.BlockSpec((k, n), lambda i: (0, 0))],
        out_specs=[pl.BlockSpec((tm // tile, n_v, tile), lambda i: (i, 0, 0)),
                   pl.BlockSpec((tm // tile_b, n_vb, tile_b), lambda i: (i, 0, 0)),
                   pl.BlockSpec((tm, n_tail), lambda i: (i, 0))],
        out_shape=[jax.ShapeDtypeStruct((m // tile, n_v, tile), BF16),
                   jax.ShapeDtypeStruct((m // tile_b, n_vb, tile_b), BF16),
                   jax.ShapeDtypeStruct((m, n_tail), F32)],
        compiler_params=_params("parallel"),
    )(x, w)


def _mm_ln_body(*refs, n_in, alpha):
    xs, ws = refs[:n_in], refs[n_in:2 * n_in]
    res_ref, g_ref, b_ref, of_ref, ob_ref = refs[2 * n_in:]
    acc = _dot(xs[0][...], ws[0][...])
    for x_ref, w_ref in zip(xs[1:], ws[1:]):
        acc = acc + _dot(x_ref[...], w_ref[...])
    out = _layer_norm_rows(alpha * res_ref[...] + acc, g_ref[...], b_ref[...])
    of_ref[...] = out
    ob_ref[...] = out.astype(BF16)


def _matmul_ln(xs, ws, res, g, b, alpha):
    m, d = res.shape
    k_total = sum(x.shape[1] for x in xs)
    tm = min(1024 if k_total <= 1024 else 512, m)
    n_in = len(xs)
    in_specs = ([pl.BlockSpec((tm, x.shape[1]), lambda i: (i, 0)) for x in xs]
                + [pl.BlockSpec(w.shape, lambda i: (0, 0)) for w in ws]
                + [pl.BlockSpec((tm, d), lambda i: (i, 0)),
                   pl.BlockSpec((1, d), lambda i: (0, 0)),
                   pl.BlockSpec((1, d), lambda i: (0, 0))])
    return pl.pallas_call(
        functools.partial(_mm_ln_body, n_in=n_in, alpha=alpha),
        grid=(m // tm,),
        in_specs=in_specs,
        out_specs=[pl.BlockSpec((tm, d), lambda i: (i, 0)),
                   pl.BlockSpec((tm, d), lambda i: (i, 0))],
        out_shape=[jax.ShapeDtypeStruct((m, d), F32), jax.ShapeDtypeStruct((m, d), BF16)],
        compiler_params=_params("parallel"),
    )(*xs, *ws, res, g.reshape(1, d), b.reshape(1, d))


def _ffn_ln_body(x_ref, w1_ref, w3_ref, w2_ref, res_ref, g_ref, b_ref, of_ref, ob_ref, acc_ref, *, alpha):
    f = pl.program_id(1)
    x = x_ref[...]
    a = _dot(x, w1_ref[...])
    b = _dot(x, w3_ref[...])
    y = _dot((_silu(a) * b).astype(BF16), w2_ref[...])

    @pl.when(f == 0)
    def _():
        acc_ref[...] = y

    @pl.when(f > 0)
    def _():
        acc_ref[...] += y

    @pl.when(f == pl.num_programs(1) - 1)
    def _():
        out = _layer_norm_rows(alpha * res_ref[...] + acc_ref[...], g_ref[...], b_ref[...])
        of_ref[...] = out
        ob_ref[...] = out.astype(BF16)


def _ffn_ln(x, w1, w3, w2, res, g, b, alpha, tm=512, tf=1408):
    m, d = x.shape
    ff = w1.shape[1]
    tm = min(tm, m)
    if ff % tf:
        tf = ff
    row = pl.BlockSpec((tm, d), lambda i, f: (i, 0))
    vec = pl.BlockSpec((1, d), lambda i, f: (0, 0))
    return pl.pallas_call(
        functools.partial(_ffn_ln_body, alpha=alpha),
        grid=(m // tm, ff // tf),
        in_specs=[row,
                  pl.BlockSpec((d, tf), lambda i, f: (0, f)),
                  pl.BlockSpec((d, tf), lambda i, f: (0, f)),
                  pl.BlockSpec((tf, d), lambda i, f: (f, 0)),
                  row, vec, vec],
        out_specs=[row, row],
        out_shape=[jax.ShapeDtypeStruct((m, d), F32), jax.ShapeDtypeStruct((m, d), BF16)],
        scratch_shapes=[pltpu.VMEM((tm, d), F32)],
        compiler_params=_params("parallel", "arbitrary"),
    )(x, w1, w3, w2, res, g.reshape(1, d), b.reshape(1, d))


def _key_query_iotas(tile):
    kpos = lax.broadcasted_iota(I32, (tile, tile), 0)
    qpos = lax.broadcasted_iota(I32, (tile, tile), 1)
    return kpos, qpos


def _scaled_transpose(x, scale):
    return (x.astype(F32) * scale).T.astype(BF16)


LOG2E = math.log2(math.e)
ALIBI_PARTS = 3
ALIBI_ROWS = 16


def _bf16_parts(x):
    parts, r = [], np.float32(x)
    for _ in range(ALIBI_PARTS):
        p = np.float32(r.astype(jnp.bfloat16))
        parts.append(float(p))
        r = np.float32(r - p)
    return parts


def _alibi_key_cols(tile, parts):
    lane = lax.broadcasted_iota(I32, (tile, LANES), 1)
    kr = lax.broadcasted_iota(I32, (tile, LANES), 0).astype(F32)
    v = jnp.where((lane >= ALIBI_PARTS) & (lane < 2 * ALIBI_PARTS), kr, 0.0)
    for n, c in enumerate(parts):
        v = jnp.where(lane == n, c, v)
    return v.astype(BF16)


def _alibi_query_rows(tile, parts):
    row = lax.broadcasted_iota(I32, (ALIBI_ROWS, tile), 0)
    qr = lax.broadcasted_iota(I32, (ALIBI_ROWS, tile), 1).astype(F32)
    v = jnp.where(row < ALIBI_PARTS, -qr, 0.0)
    for n, c in enumerate(parts):
        v = jnp.where(row == ALIBI_PARTS + n, c, v)
    return v.astype(BF16)


def _init_alibi(slopes, dneg_scr, kx_scr, qa_scr, tile, maps_per_head):
    kpos, qpos = _key_query_iotas(tile)
    rel = (qpos - kpos).astype(F32)
    qa_scr[...] = jnp.zeros(qa_scr.shape, BF16)
    for h, slope in enumerate(slopes):
        parts = _bf16_parts(slope * LOG2E)
        dneg_scr[h] = 2.0 * jnp.minimum(np.float32(slope * LOG2E) * rel, 0.0)
        kx_scr[h] = _alibi_key_cols(tile, parts)
        for mm in range(maps_per_head):
            qa_scr[h * maps_per_head + mm, LANES:LANES + ALIBI_ROWS, :] = _alibi_query_rows(tile, parts)


def _store_query_maps(qa_scr, qt, n_maps, dh):
    for idx in range(n_maps):
        r0 = (idx % 2) * dh
        qa_scr[idx, r0:r0 + dh, :] = qt[idx * dh:(idx + 1) * dh, :]


def _softmax_probs(c, t_ref, tmax_ref, p_ref, a_ref, m_ref, l_ref):
    m_old = m_ref[...]
    m_new = jnp.maximum(m_old, tmax_ref[...] - c)
    p = jnp.exp2(t_ref[...] - (m_new + c))
    a = jnp.exp2(m_old - m_new)
    l_ref[...] = a * l_ref[...] + jnp.sum(p, axis=0, keepdims=True)
    m_ref[...] = m_new
    a_ref[...] = a
    p_ref[...] = p.astype(BF16)


def _accumulate(vt, p_ref, a_ref, acc_ref):
    acc_ref[...] = a_ref[...] * acc_ref[...] + _dot(vt, p_ref[...])


SCORE_LEAD, PROB_LEAD = 4, 2
N_T_SLOTS, N_P_SLOTS = SCORE_LEAD - PROB_LEAD + 2, PROB_LEAD + 1


def _stage_scratch(tile, keys=None):
    keys = tile if keys is None else keys
    return ([pltpu.VMEM((keys, tile), F32)] * N_T_SLOTS + [pltpu.VMEM((1, tile), F32)] * N_T_SLOTS
            + [pltpu.VMEM((keys, tile), BF16)] * N_P_SLOTS + [pltpu.VMEM((1, tile), F32)] * N_P_SLOTS)


def _pipelined_maps(n_maps, scores, probs, accumulate, stage):
    nt, npb = N_T_SLOTS, N_P_SLOTS
    t_slots, x_slots = stage[:nt], stage[nt:2 * nt]
    p_slots, a_slots = stage[2 * nt:2 * nt + npb], stage[2 * nt + npb:]

    def stage_scores(k):
        s = scores(k)
        t_slots[k % nt][...] = s
        x_slots[k % nt][...] = jnp.max(s, axis=0, keepdims=True)

    def stage_probs(k):
        probs(k, t_slots[k % nt], x_slots[k % nt], p_slots[k % npb], a_slots[k % npb])

    for step in range(-SCORE_LEAD, n_maps):
        if 0 <= step + SCORE_LEAD < n_maps:
            stage_scores(step + SCORE_LEAD)
        if 0 <= step + PROB_LEAD < n_maps:
            stage_probs(step + PROB_LEAD)
        if step >= 0:
            accumulate(step, p_slots[step % npb], a_slots[step % npb])


def _diff_body(q_ref, k_ref, vt_ref, lq1_ref, lk1_ref, lq2_ref, lk2_ref, gain_ref, o_ref,
               qa_scr, kx_scr, dneg_scr, m_scr, l_scr, acc_scr, *stage,
               tile, slopes, lam_init):
    i = pl.program_id(1)
    kpos, qpos = _key_query_iotas(tile)

    @pl.when((pl.program_id(0) == 0) & (i == 0))
    def _():
        _init_alibi(slopes, dneg_scr, kx_scr, qa_scr, tile, 2)

    _store_query_maps(qa_scr, _scaled_transpose(q_ref[0], 1.0), 2 * H_A, DQK_A)
    m_scr[...] = jnp.full(m_scr.shape, NEG_INF, F32)
    l_scr[...] = jnp.zeros(l_scr.shape, F32)
    acc_scr[...] = jnp.zeros(acc_scr.shape, F32)

    n_maps = 2 * H_A

    def sweep(blocks):
        ctx = []
        for j, diag in blocks:
            k0 = pl.multiple_of(j * tile, tile)
            ctx.append((diag, k_ref[0, pl.ds(k0, tile), :], vt_ref[0, j], ((i - j) * tile).astype(F32)))

        def scores(n):
            (diag, kb, _, _), idx = ctx[n // n_maps], n % n_maps
            h, pair = idx // 2, idx // 2
            keys = jnp.concatenate([kb[:, pair * LANES:(pair + 1) * LANES], kx_scr[h]], axis=1)
            s = _dot_row_chunks(keys, qa_scr[idx])
            if diag:
                allowed = (kpos // CHUNK) <= (qpos // CHUNK)
                return jnp.where(allowed, s + dneg_scr[h], MASKED)
            return s

        def probs(n, t_ref, tmax_ref, p_ref, a_ref):
            (diag, _, _, off), idx = ctx[n // n_maps], n % n_maps
            shift = 0.0 if diag else (slopes[idx // 2] * LOG2E) * off
            _softmax_probs(shift, t_ref, tmax_ref, p_ref, a_ref, m_scr.at[idx], l_scr.at[idx])

        def accumulate(n, p_ref, a_ref):
            vtb, idx = ctx[n // n_maps][2], n % n_maps
            h = idx // 2
            _accumulate(vtb[h * DV_A:(h + 1) * DV_A, :], p_ref, a_ref, acc_scr.at[idx])

        _pipelined_maps(n_maps * len(blocks), scores, probs, accumulate, stage)

    def sweep_pair(pp, carry):
        sweep([(2 * pp, False), (2 * pp + 1, False)])
        return carry

    lax.fori_loop(0, i // 2, sweep_pair, 0)

    @pl.when(i % 2 == 1)
    def _():
        sweep([(i - 1, False), (i, True)])

    @pl.when(i % 2 == 0)
    def _():
        sweep([(i, True)])

    lam = (jnp.exp(jnp.sum(lq1_ref[...] * lk1_ref[...], axis=-1, keepdims=True))
           - jnp.exp(jnp.sum(lq2_ref[...] * lk2_ref[...], axis=-1, keepdims=True)) + lam_init)
    gain = gain_ref[...] * (1.0 - lam_init)
    outs = []
    for h in range(H_A):
        o = acc_scr[2 * h] / l_scr[2 * h] - lam * (acc_scr[2 * h + 1] / l_scr[2 * h + 1])
        o = o * lax.rsqrt(jnp.mean(o * o, axis=0, keepdims=True) + RMS_EPS)
        outs.append((o * gain).T)
    o_ref[0] = jnp.concatenate(outs, axis=-1).astype(BF16)


def _diff_attention(proj, vt, lq1, lk1, lq2, lk2, gain, slopes, lam_init):
    b, s, _ = proj.shape
    tile = min(ATT_TILE, s)
    nkb = s // tile
    w = H_A * DV_A
    vec = lambda a: a.reshape(1, -1).astype(F32)
    small = lambda n: pl.BlockSpec((1, n), lambda bi, i: (0, 0))
    return pl.pallas_call(
        functools.partial(_diff_body, tile=tile, slopes=slopes, lam_init=lam_init),
        grid=(b, nkb),
        in_specs=[pl.BlockSpec((1, tile, w), lambda bi, i: (bi, i, 0)),
                  pl.BlockSpec((1, s, w), lambda bi, i: (bi, 0, 1)),
                  pl.BlockSpec((1, nkb, w, tile), lambda bi, i: (bi, 0, 0, 0)),
                  small(DQK_A), small(DQK_A), small(DQK_A), small(DQK_A),
                  pl.BlockSpec((DV_A, 1), lambda bi, i: (0, 0))],
        out_specs=pl.BlockSpec((1, tile, w), lambda bi, i: (bi, i, 0)),
        out_shape=jax.ShapeDtypeStruct((b, s, w), BF16),
        scratch_shapes=[pltpu.VMEM((2 * H_A, 2 * LANES, tile), BF16),
                        pltpu.VMEM((H_A, tile, LANES), BF16),
                        pltpu.VMEM((H_A, tile, tile), F32),
                        pltpu.VMEM((2 * H_A, 1, tile), F32),
                        pltpu.VMEM((2 * H_A, 1, tile), F32),
                        pltpu.VMEM((2 * H_A, DV_A, tile), F32)] + _stage_scratch(tile),
        compiler_params=_params("arbitrary", "arbitrary"),
    )(proj, proj, vt, vec(lq1), vec(lk1), vec(lq2), vec(lk2), gain.reshape(-1, 1).astype(F32))


BAND_CHUNKS = LANES // CHUNK
BAND_WINDOW = (LEFT_CHUNKS + BAND_CHUNKS) * CHUNK


def _band_body(q_ref, k_ref, vt_ref, bias_ref, o_ref, qz_scr, ot_scr, *stage):
    i = pl.program_id(1)
    ntile = BAND_WINDOW // LANES

    @pl.when((pl.program_id(0) == 0) & (i == 0))
    def _():
        qz_scr[...] = jnp.zeros(qz_scr.shape, BF16)

    qt = _scaled_transpose(q_ref[0], 1.0)
    for h in range(H_B):
        r0 = (h % 2) * DH_B
        qz_scr[h, r0:r0 + DH_B, :] = qt[h * DH_B:(h + 1) * DH_B, :]

    start = pl.multiple_of(i * LANES, LANES)
    kw = k_ref[0, pl.ds(start, BAND_WINDOW), :]
    vt = vt_ref[0, pl.ds(i, ntile)]
    kk = lax.broadcasted_iota(I32, (BAND_WINDOW, LANES), 0)
    valid = kk >= LEFT_CHUNKS * CHUNK - start

    def scores(h):
        pair = h // 2
        s = _dot_row_chunks(kw[:, pair * LANES:(pair + 1) * LANES], qz_scr[h])
        return jnp.where(valid, s + bias_ref[h], MASKED)

    def probs(h, t_ref, tmax_ref, p_ref, l_ref):
        p = jnp.exp2(t_ref[...] - tmax_ref[...])
        l_ref[...] = jnp.sum(p, axis=0, keepdims=True)
        p_ref[...] = p.astype(BF16)

    def accumulate(h, p_ref, l_ref):
        vth = jnp.concatenate([vt[n, h * DH_B:(h + 1) * DH_B, :] for n in range(ntile)], axis=1)
        ot_scr[h * DH_B:(h + 1) * DH_B, :] = _dot(vth, p_ref[...]) / l_ref[...]

    _pipelined_maps(H_B, scores, probs, accumulate, stage)
    o_ref[0] = ot_scr[...].T.astype(BF16)


def _band_bias_table(table):
    bias = _band_bias(table).transpose(0, 2, 1) * LOG2E
    cols = [jnp.pad(bias, ((0, 0), (c * CHUNK, (BAND_CHUNKS - 1 - c) * CHUNK), (0, 0)), constant_values=MASKED)
            for c in range(BAND_CHUNKS)]
    return jnp.concatenate(cols, axis=2)


def _band_attention(proj, kpad, vt_pad, bias):
    b, s, _ = proj.shape
    w = H_B * DH_B
    sp = kpad.shape[1]
    nvt = vt_pad.shape[1]
    return pl.pallas_call(
        _band_body,
        grid=(b, s // LANES),
        in_specs=[pl.BlockSpec((1, LANES, w), lambda bi, i: (bi, i, 6)),
                  pl.BlockSpec((1, sp, w), lambda bi, i: (bi, 0, 0)),
                  pl.BlockSpec((1, nvt, w, LANES), lambda bi, i: (bi, 0, 0, 0)),
                  pl.BlockSpec((H_B, BAND_WINDOW, LANES), lambda bi, i: (0, 0, 0))],
        out_specs=pl.BlockSpec((1, LANES, w), lambda bi, i: (bi, i, 0)),
        out_shape=jax.ShapeDtypeStruct((b, s, w), BF16),
        scratch_shapes=[pltpu.VMEM((H_B, LANES, LANES), BF16),
                        pltpu.VMEM((w, LANES), F32)] + _stage_scratch(LANES, BAND_WINDOW),
        compiler_params=_params("arbitrary", "arbitrary"),
    )(proj, kpad, vt_pad, bias)


def _sortable(x):
    bits = pltpu.bitcast(x + 0.0, I32)
    return bits ^ ((bits >> 31) & 0x7FFFFFFF)


_INT_MIN = -2 ** 31
_VALID_KEY = int(np.array(0.5 * NEG_INF, np.float32).view(np.int32))
_VALID_KEY = _VALID_KEY ^ ((_VALID_KEY >> 31) & 0x7FFFFFFF)


def _dsa_body(qc_ref, kc_ref, vct_ref, qi_ref, kiw_ref, wq_ref, o_ref,
              qa_scr, kx_scr, dneg_scr, qit_scr, key_scr, hi_scr, lo_scr, jstar_scr,
              m_scr, l_scr, acc_scr, *stage,
              tile, topk, slopes, seq):
    i = pl.program_id(1)
    nblk = i + 1
    kpos, qpos = _key_query_iotas(tile)

    @pl.when((pl.program_id(0) == 0) & (i == 0))
    def _():
        _init_alibi(slopes, dneg_scr, kx_scr, qa_scr, tile, 1)

    qit_scr[...] = _scaled_transpose(qi_ref[0], D_IDX ** -0.5)
    wt = wq_ref[0].T
    wrow = [wt[D_IDX + h:D_IDX + h + 1, :] * H_IDX ** -0.5 for h in range(H_IDX)]

    def score_block(j, diag):
        k0 = pl.multiple_of(j * tile, tile)
        kib = kiw_ref[0, pl.ds(k0, tile), :][:, :D_IDX].astype(BF16)
        isc = jnp.zeros((tile, tile), F32)
        for h in range(H_IDX):
            d = _dot(kib, qit_scr[h * D_IDX:(h + 1) * D_IDX, :])
            isc = isc + jnp.maximum(d, 0.0) * wrow[h]
        if diag:
            isc = jnp.where((kpos // CHUNK) <= (qpos // CHUNK), isc, NEG_INF)
        key = _sortable(isc)
        key_scr[pl.ds(k0, tile), :] = key
        hi_scr[pl.ds(k0, tile), :] = (key >> 16).astype(I16)
        lo_scr[pl.ds(k0, tile), :] = ((key & 0xFFFF) - 2 ** 15).astype(I16)

    def score_full(j, carry):
        score_block(j, False)
        return carry

    lax.fori_loop(0, i, score_full, 0)
    score_block(i, True)

    def count(pred):
        def body(j, part):
            k0 = pl.multiple_of(j * tile, tile)
            hit = jnp.where(pred(key_scr[pl.ds(k0, tile), :], kpos + k0), 1.0, 0.0)
            return part + jnp.sum(hit.reshape(tile // SUBLANES, SUBLANES, tile), axis=0)
        part = lax.fori_loop(0, nblk, body, jnp.zeros((SUBLANES, tile), F32))
        return jnp.sum(part, axis=0, keepdims=True)

    kf = float(topk)
    rows16 = 2 * SUBLANES

    pair = 2 if (seq // tile) % 2 == 0 else 1
    span = pair * tile
    lowest16 = jnp.full((tile, tile), -2 ** 15, I16)

    @pl.when(nblk % pair == 1)
    def _():
        pad_rows = pl.ds(pl.multiple_of(nblk * tile, tile), tile)
        hi_scr[pad_rows, :] = lowest16
        lo_scr[pad_rows, :] = lowest16

    def count16(src_ref, pred):
        def body(j, part):
            k0 = pl.multiple_of(j * span, span)
            hit = jnp.where(pred(src_ref[pl.ds(k0, span), :]), jnp.int16(1), jnp.int16(0))
            for g in range(span // rows16):
                part = part + hit[g * rows16:(g + 1) * rows16, :]
            return part
        part = lax.fori_loop(0, (nblk + pair - 1) // pair, body, jnp.zeros((rows16, tile), I16))
        return jnp.sum(part.astype(I32), axis=0, keepdims=True).astype(F32)

    def search16(src_ref, base):
        def bit(p, u):
            cand_u = u | jnp.left_shift(jnp.int32(1), 15 - p)
            cand = (cand_u - 2 ** 15).astype(I16)
            cnt = base + count16(src_ref, lambda v: v >= cand)
            return jnp.where(cnt >= kf, cand_u, u)
        return lax.fori_loop(0, 16, bit, jnp.zeros((1, tile), I32))

    thr_hi = search16(hi_scr, 0.0) - 2 ** 15
    thr_hi16 = thr_hi.astype(I16)
    above = count16(hi_scr, lambda v: v > thr_hi16)

    def mask_low(j, carry):
        k0 = pl.multiple_of(j * tile, tile)
        rows = pl.ds(k0, tile)
        lo_scr[rows, :] = jnp.where(hi_scr[rows, :] == thr_hi16, lo_scr[rows, :], jnp.int16(-2 ** 15))
        return carry

    lax.fori_loop(0, nblk, mask_low, 0)
    thr = jnp.left_shift(thr_hi, 16) + search16(lo_scr, above)

    thr_b = thr
    c_gt = count(lambda key, idx: key > thr_b)
    c_ge = count(lambda key, idx: key >= thr_b)
    tied = (c_ge > kf) & (thr > _VALID_KEY)
    need = kf - c_gt
    jstar_scr[...] = jnp.full((1, tile), seq, I32)

    @pl.when(jnp.max(jnp.where(tied, 1.0, 0.0)) > 0.0)
    def _():
        nbits = max(1, int(math.ceil(math.log2(seq))))

        def index_bit(p, lo):
            cand = lo | jnp.left_shift(jnp.int32(1), nbits - 1 - p)
            cnt = count(lambda key, idx: (key == thr_b) & (idx < cand))
            return jnp.where(cnt < need, cand, lo)

        lo = lax.fori_loop(0, nbits, index_bit, jnp.zeros((1, tile), I32))
        jstar_scr[...] = jnp.where(tied, lo, seq)

    jstar = jstar_scr[...]
    thr2 = jnp.maximum(thr, _VALID_KEY + 1)

    _store_query_maps(qa_scr, _scaled_transpose(qc_ref[0], 1.0), H_C, DH_C)
    m_scr[...] = jnp.full(m_scr.shape, NEG_INF, F32)
    l_scr[...] = jnp.zeros(l_scr.shape, F32)
    acc_scr[...] = jnp.zeros(acc_scr.shape, F32)

    def attend(blocks):
        ctx = []
        for j, diag in blocks:
            k0 = pl.multiple_of(j * tile, tile)
            key = key_scr[pl.ds(k0, tile), :]
            sel = (key > thr2) | ((key == thr2) & (kpos + k0 <= jstar))
            ctx.append((diag, sel, kc_ref[0, pl.ds(k0, tile), :], vct_ref[0, j], ((i - j) * tile).astype(F32)))

        def scores(idx):
            (diag, sel, kb, _, _), h = ctx[idx // H_C], idx % H_C
            pair = h // 2
            keys = jnp.concatenate([kb[:, pair * LANES:(pair + 1) * LANES], kx_scr[h]], axis=1)
            s = _dot_row_chunks(keys, qa_scr[h])
            if diag:
                s = s + dneg_scr[h]
            return jnp.where(sel, s, MASKED)

        def probs(idx, t_ref, tmax_ref, p_ref, a_ref):
            (diag, _, _, _, off), h = ctx[idx // H_C], idx % H_C
            shift = 0.0 if diag else (slopes[h] * LOG2E) * off
            _softmax_probs(shift, t_ref, tmax_ref, p_ref, a_ref, m_scr.at[h], l_scr.at[h])

        def accumulate(idx, p_ref, a_ref):
            vtb, h = ctx[idx // H_C][3], idx % H_C
            _accumulate(vtb[h * DH_C:(h + 1) * DH_C, :], p_ref, a_ref, acc_scr.at[h])

        _pipelined_maps(H_C * len(blocks), scores, probs, accumulate, stage)

    def attend_pair(pp, carry):
        attend([(2 * pp, False), (2 * pp + 1, False)])
        return carry

    lax.fori_loop(0, i // 2, attend_pair, 0)

    @pl.when(i % 2 == 1)
    def _():
        attend([(i - 1, False), (i, True)])

    @pl.when(i % 2 == 0)
    def _():
        attend([(i, True)])
    outs = [(acc_scr[h] / l_scr[h]).T for h in range(H_C)]
    o_ref[0] = jnp.concatenate(outs, axis=-1).astype(BF16)


def _dsa_attention(proj, vt, kiw, slopes):
    b, s, _ = proj.shape
    tile = min(ATT_TILE, s)
    nkb = s // tile
    w = H_C * DH_C
    wa = H_A * DV_A
    topk = min(TOPK_MAX, s // 4)
    return pl.pallas_call(
        functools.partial(_dsa_body, tile=tile, topk=topk, slopes=slopes, seq=s),
        grid=(b, nkb),
        in_specs=[pl.BlockSpec((1, tile, w), lambda bi, i: (bi, i, 8)),
                  pl.BlockSpec((1, s, w), lambda bi, i: (bi, 0, 9)),
                  pl.BlockSpec((1, nkb, w, tile), lambda bi, i: (bi, 0, wa // w, 0)),
                  pl.BlockSpec((1, tile, H_IDX * D_IDX), lambda bi, i: (bi, i, 2)),
                  pl.BlockSpec((1, s, LANES), lambda bi, i: (bi, 0, 0)),
                  pl.BlockSpec((1, tile, LANES), lambda bi, i: (bi, i, 0))],
        out_specs=pl.BlockSpec((1, tile, w), lambda bi, i: (bi, i, 0)),
        out_shape=jax.ShapeDtypeStruct((b, s, w), BF16),
        scratch_shapes=[pltpu.VMEM((H_C, 2 * LANES, tile), BF16),
                        pltpu.VMEM((H_C, tile, LANES), BF16),
                        pltpu.VMEM((H_C, tile, tile), F32),
                        pltpu.VMEM((H_IDX * D_IDX, tile), BF16),
                        pltpu.VMEM((s, tile), I32),
                        pltpu.VMEM((s, tile), I16),
                        pltpu.VMEM((s, tile), I16),
                        pltpu.VMEM((1, tile), I32),
                        pltpu.VMEM((H_C, 1, tile), F32),
                        pltpu.VMEM((H_C, 1, tile), F32),
                        pltpu.VMEM((H_C, DH_C, tile), F32)] + _stage_scratch(tile),
        compiler_params=_params("arbitrary", "arbitrary"),
    )(proj, proj, vt, proj, kiw, kiw)


def _xattn_body(x_ref, res_ref, wq_ref, k_ref, v_ref, wo_ref, g_ref, b_ref, of_ref, ob_ref, *, dh, alpha):
    q_all = _dot(x_ref[0], wq_ref[...]).astype(BF16)
    outs = []
    for h in range(H_X):
        c = h * dh
        q = q_all[:, c:c + dh] * dh ** -0.5
        s = _dot_nt(q, k_ref[0, :, c:c + dh])
        p = jnp.exp(s - jnp.max(s, axis=-1, keepdims=True))
        l = jnp.sum(p, axis=-1, keepdims=True)
        outs.append((_dot(p.astype(BF16), v_ref[0, :, c:c + dh]) / l).astype(BF16))
    y = _dot(jnp.concatenate(outs, axis=-1), wo_ref[...])
    out = _layer_norm_rows(alpha * res_ref[0] + y, g_ref[...], b_ref[...])
    of_ref[0] = out
    ob_ref[0] = out.astype(BF16)


def _xattn_ln(x, res, wq, kv, wo, g, b, alpha, tq=512):
    bsz, s, d = x.shape
    n_mem = kv.shape[1]
    tq = min(tq, s)
    tile = pl.BlockSpec((1, tq, d), lambda bi, i: (bi, i, 0))
    whole = lambda shape: pl.BlockSpec(shape, lambda bi, i: (0,) * len(shape))
    mem = lambda col: pl.BlockSpec((1, n_mem, d), lambda bi, i: (bi, 0, col))
    return pl.pallas_call(
        functools.partial(_xattn_body, dh=d // H_X, alpha=alpha),
        grid=(bsz, s // tq),
        in_specs=[tile, tile, whole((d, d)), mem(0), mem(1), whole((d, d)), whole((1, d)), whole((1, d))],
        out_specs=[tile, tile],
        out_shape=[jax.ShapeDtypeStruct((bsz, s, d), F32), jax.ShapeDtypeStruct((bsz, s, d), BF16)],
        compiler_params=_params("parallel", "parallel"),
    )(x, res, wq, kv, kv, wo, g.reshape(1, d), b.reshape(1, d))


def _router_body(x_ref, r_ref, g_ref):
    logits = _dot(x_ref[...], r_ref[...])
    lane = lax.broadcasted_iota(I32, logits.shape, 1).astype(F32)
    big = float(LANES)
    lg = jnp.where(lane < N_EXPERTS, logits, -jnp.inf)
    m1 = jnp.max(lg, axis=-1, keepdims=True)
    i1 = jnp.min(jnp.where(lg == m1, lane, big), axis=-1, keepdims=True)
    lg2 = jnp.where(lane == i1, -jnp.inf, lg)
    m2 = jnp.max(lg2, axis=-1, keepdims=True)
    i2 = jnp.min(jnp.where(lg2 == m2, lane, big), axis=-1, keepdims=True)
    e = jnp.exp(m2 - m1)
    den = 1.0 + e
    g_ref[...] = jnp.where(lane == i1, 1.0 / den, 0.0) + jnp.where(lane == i2, e / den, 0.0)


def _router_gates(x, router, tm=1024):
    m, d = x.shape
    tm = min(tm, m)
    rpad = jnp.zeros((d, LANES), BF16).at[:, :N_EXPERTS].set(router.astype(BF16))
    return pl.pallas_call(
        _router_body,
        grid=(m // tm,),
        in_specs=[pl.BlockSpec((tm, d), lambda i: (i, 0)),
                  pl.BlockSpec((d, LANES), lambda i: (0, 0))],
        out_specs=pl.BlockSpec((tm, LANES), lambda i: (i, 0)),
        out_shape=jax.ShapeDtypeStruct((m, LANES), F32),
        compiler_params=_params("parallel"),
    )(x, rpad)


MOE_BLOCK = 896
MOE_CHUNK = 128
MOE_GROUP = 2
MOE_FF_TILE = 1792


def _moe_body(x_ref, gate_ref, tri_ref, w1_ref, w3_ref, w2_ref, o_ref,
              xs_scr, acc_scr, rcol_scr, rrow_scr, cnt_smem, *, n_tokens):
    e = pl.program_id(1)
    f = pl.program_id(2)
    nf = pl.num_programs(2)
    tb = x_ref.shape[0]
    ch = MOE_CHUNK
    gw = MOE_GROUP * ch

    tok0 = pl.program_id(0) * tb
    valid_col = lax.broadcasted_iota(I32, (tb, 1), 0) + tok0 < n_tokens

    @pl.when((e == 0) & (f == 0))
    def _():
        o_ref[...] = jnp.zeros(o_ref.shape, F32)
        member = jnp.where((gate_ref[...] > 0.0) & valid_col, 1.0, 0.0)
        rank = _dot(tri_ref[...], member.astype(BF16))
        rcol = jnp.where(member > 0.0, rank, -1.0)
        rcol_scr[...] = rcol
        rrow_scr[...] = rcol.T
        counts = jnp.sum(member, axis=0, keepdims=True)
        for ee in range(N_EXPERTS):
            cnt_smem[ee] = counts[0, ee].astype(I32)

    count = cnt_smem[e]
    single = count <= gw
    nchunk = jnp.where(single, MOE_GROUP, (count + (ch - 1)) // ch)

    @pl.when(f == 0)
    def _():
        acc_scr[...] = jnp.zeros(acc_scr.shape, F32)
        sub = lax.broadcasted_iota(I32, (SUBLANES, tb), 0)
        r_row = jnp.sum(jnp.where(sub == e, rrow_scr[0:SUBLANES, :], 0.0), axis=0, keepdims=True)
        xz = jnp.where(valid_col, x_ref[...], jnp.zeros((), BF16))

        def gather_rows(r0, rows):
            row_id = lax.broadcasted_iota(I32, (rows, tb), 0).astype(F32) + jnp.asarray(r0, F32)
            sel = jnp.where(row_id == r_row, 1.0, 0.0).astype(BF16)
            xs_scr[pl.ds(r0, rows), :] = _dot(sel, xz).astype(BF16)

        @pl.when(single)
        def _():
            gather_rows(0, gw)

        @pl.when(jnp.logical_not(single))
        def _():
            def gather(c, carry):
                gather_rows(pl.multiple_of(c * ch, ch), ch)
                return carry
            lax.fori_loop(0, nchunk, gather, 0)

    def ffn_rows(r0, rows):
        xs = xs_scr[pl.ds(r0, rows), :]
        a = _dot(xs, w1_ref[0])
        b = _dot(xs, w3_ref[0])
        acc_scr[pl.ds(r0, rows), :] += _dot((_silu(a) * b).astype(BF16), w2_ref[0])

    @pl.when(single)
    def _():
        ffn_rows(0, gw)

    @pl.when(jnp.logical_not(single))
    def _():
        def ffn(c, carry):
            ffn_rows(pl.multiple_of(c * ch, ch), ch)
            return carry
        lax.fori_loop(0, nchunk, ffn, 0)

    @pl.when(f == nf - 1)
    def _():
        pick = lax.broadcasted_iota(I32, (tb, LANES), 1) == e
        r_col = jnp.sum(jnp.where(pick, rcol_scr[...], 0.0), axis=1, keepdims=True)
        g_col = jnp.sum(jnp.where(pick & valid_col, gate_ref[...], 0.0), axis=1, keepdims=True)
        col_id = lax.broadcasted_iota(I32, (tb, gw), 1).astype(F32)

        def combine(gi, carry):
            r0 = pl.multiple_of(gi * gw, gw)
            selt = jnp.where(col_id + r0.astype(F32) == r_col, 1.0, 0.0).astype(BF16)
            y = acc_scr[pl.ds(r0, gw), :]
            y_hi = y.astype(BF16)
            y_lo = (y - y_hi.astype(F32)).astype(BF16)
            o_ref[...] += g_col * (_dot(selt, y_hi) + _dot(selt, y_lo))
            return carry

        lax.fori_loop(0, (nchunk + (MOE_GROUP - 1)) // MOE_GROUP, combine, 0)


def _moe_experts(x, gates, w1, w3, w2):
    m, d = x.shape
    ne, _, ff = w1.shape
    tb = min(MOE_BLOCK, m)
    tf = MOE_FF_TILE if ff % MOE_FF_TILE == 0 else ff
    gw = MOE_GROUP * MOE_CHUNK
    cap = pl.cdiv(tb, gw) * gw
    assert tb % LANES == 0 and ne <= SUBLANES
    tri = jnp.asarray(np.tril(np.ones((tb, tb), np.float32), -1), BF16)
    row = lambda i, e, f: (i, 0)
    return pl.pallas_call(
        functools.partial(_moe_body, n_tokens=m),
        grid=(pl.cdiv(m, tb), ne, ff // tf),
        in_specs=[pl.BlockSpec((tb, d), row),
                  pl.BlockSpec((tb, LANES), row),
                  pl.BlockSpec((tb, tb), lambda i, e, f: (0, 0)),
                  pl.BlockSpec((1, d, tf), lambda i, e, f: (e, 0, f)),
                  pl.BlockSpec((1, d, tf), lambda i, e, f: (e, 0, f)),
                  pl.BlockSpec((1, tf, d), lambda i, e, f: (e, f, 0))],
        out_specs=pl.BlockSpec((tb, d), row),
        out_shape=jax.ShapeDtypeStruct((m, d), F32),
        scratch_shapes=[pltpu.VMEM((cap, d), BF16),
                        pltpu.VMEM((cap, d), F32),
                        pltpu.VMEM((tb, LANES), F32),
                        pltpu.VMEM((LANES, tb), F32),
                        pltpu.SMEM((N_EXPERTS,), I32)],
        compiler_params=_params("parallel", "arbitrary", "arbitrary"),
    )(x, gates, tri, w1, w3, w2)


def _res_ln_body(res_ref, y_ref, g_ref, b_ref, of_ref, ob_ref, *, alpha):
    out = _layer_norm_rows(alpha * res_ref[...] + y_ref[...], g_ref[...], b_ref[...])
    of_ref[...] = out
    ob_ref[...] = out.astype(BF16)


def _residual_ln(res, y, g, b, alpha, tm=512):
    m, d = res.shape
    tm = min(tm, m)
    row = pl.BlockSpec((tm, d), lambda i: (i, 0))
    vec = pl.BlockSpec((1, d), lambda i: (0, 0))
    return pl.pallas_call(
        functools.partial(_res_ln_body, alpha=alpha),
        grid=(m // tm,),
        in_specs=[row, row, vec, vec],
        out_specs=[row, row],
        out_shape=[jax.ShapeDtypeStruct((m, d), F32), jax.ShapeDtypeStruct((m, d), BF16)],
        compiler_params=_params("parallel"),
    )(res, y, g.reshape(1, d), b.reshape(1, d))


def _band_bias(table):
    band = (LEFT_CHUNKS + 1) * CHUNK
    left = LEFT_CHUNKS * CHUNK
    rev = table.astype(F32)[:, ::-1]
    pad = left - REL_MAX + CHUNK - 1
    n_rev = band - 1 - (left - REL_MAX) + 1
    assert n_rev <= 2 * REL_MAX + 1
    ext = jnp.concatenate([jnp.broadcast_to(rev[:, :1], (rev.shape[0], pad)), rev[:, :n_rev]], axis=1)
    n_ext = ext.shape[1]
    period = jnp.concatenate([ext[:, CHUNK - 1:], jnp.zeros((ext.shape[0], 1), F32), ext[:, :CHUNK - 1]], axis=1)
    skew = jnp.tile(period, (1, CHUNK))[:, :CHUNK * n_ext].reshape(-1, CHUNK, n_ext)
    return skew[:, :, :band]


def kernel(x, mem, w_in, lam_q1, lam_k1, lam_q2, lam_k2, diff_norm_g, rel_bias, w_out,
           ln1_g, ln1_b, xattn_wq, xattn_wk, xattn_wv, xattn_wo, ln2_g, ln2_b,
           ffn_w1, ffn_w3, ffn_w2, moe_router, moe_w1, moe_w3, moe_w2, ln3_g, ln3_b):
    b, s, d = x.shape
    depth = w_in.shape[0]
    t = b * s
    alpha = (2.0 * depth) ** 0.25
    slopes_a, slopes_c = _alibi_slopes()
    n_main = 3 * H_A * DV_A + 3 * H_B * DH_B + 3 * H_C * DH_C + H_IDX * D_IDX
    n_tail = D_IDX + H_IDX
    wa, wb, wc = H_A * DV_A, H_B * DH_B, H_C * DH_C
    left = LEFT_CHUNKS * CHUNK

    h = x.reshape(t, d)
    hb = h.astype(BF16)
    memb = mem.reshape(-1, d).astype(BF16)
    for l in range(depth):
        wl = w_in[l]
        o_b, o_c, o_i = 3 * wa, 3 * wa + 3 * wb, 3 * wa + 3 * wb + 3 * wc
        w_main = jnp.concatenate(
            [wl[:, :wa] * (DQK_A ** -0.5 * LOG2E), wl[:, wa:2 * wa], wl[:, o_i:o_i + H_IDX * D_IDX],
             wl[:, o_b:o_b + wb] * (DH_B ** -0.5 * LOG2E), wl[:, o_b + wb:o_b + 2 * wb],
             wl[:, o_c:o_c + wc] * (DH_C ** -0.5 * LOG2E), wl[:, o_c + wc:o_c + 2 * wc]], axis=1).astype(BF16)
        proj = _matmul(hb, w_main, BF16, tn=w_main.shape[1] // 2).reshape(b, s, -1)
        w_side = jnp.concatenate(
            [wl[:, 2 * wa:3 * wa], wl[:, o_c + 2 * wc:o_c + 3 * wc], wl[:, o_b + 2 * wb:o_b + 3 * wb],
             wl[:, n_main:], jnp.zeros((d, LANES - n_tail), F32)], axis=1).astype(BF16)
        tile = min(ATT_TILE, s)
        vt, vbt, kiw = _side_projections(hb, w_side, wa + wc, tile, wb, LANES)
        vt = vt.reshape(b, s // tile, wa + wc, tile)
        kiw = kiw.reshape(b, s, LANES)
        lam_init = 0.8 - 0.6 * math.exp(-0.3 * l)
        ya = _diff_attention(proj, vt, lam_q1[l], lam_k1[l], lam_q2[l], lam_k2[l], diff_norm_g[l],
                             slopes_a, lam_init)
        kpad = jnp.pad(proj[:, :, 3 * wa + wb:3 * wa + 2 * wb], ((0, 0), (left, 0), (0, 0)))
        vbt = jnp.pad(vbt.reshape(b, s // LANES, wb, LANES), ((0, 0), (left // LANES, 0), (0, 0), (0, 0)))
        yb = _band_attention(proj, kpad, vbt, _band_bias_table(rel_bias[l]))
        yc = _dsa_attention(proj, vt, kiw, slopes_c)
        wo = w_out[l].astype(BF16)
        h, hb = _matmul_ln(
            [ya.reshape(t, wa), yb.reshape(t, wb), yc.reshape(t, -1)],
            [wo[:wa], wo[wa:wa + wb], wo[wa + wb:]],
            h, ln1_g[l], ln1_b[l], alpha)
        w_kv = jnp.concatenate([xattn_wk[l], xattn_wv[l]], axis=1).astype(BF16)
        kv = _matmul(memb, w_kv, BF16, tn=d).reshape(b, -1, 2 * d)
        h, hb = _xattn_ln(hb.reshape(b, s, d), h.reshape(b, s, d), xattn_wq[l].astype(BF16), kv,
                          xattn_wo[l].astype(BF16), ln2_g[l], ln2_b[l], alpha)
        h, hb = h.reshape(t, d), hb.reshape(t, d)
        if l % 2 == 0:
            h, hb = _ffn_ln(hb, ffn_w1[l // 2].astype(BF16), ffn_w3[l // 2].astype(BF16),
                            ffn_w2[l // 2].astype(BF16), h, ln3_g[l], ln3_b[l], alpha)
        else:
            gates = _router_gates(hb, moe_router[l // 2])
            y = _moe_experts(hb, gates, moe_w1[l // 2].astype(BF16), moe_w3[l // 2].astype(BF16),
                             moe_w2[l // 2].astype(BF16))
            h, hb = _residual_ln(h, y, ln3_g[l], ln3_b[l], alpha)
    return h.reshape(b, s, d)
```

```python
import functools
import math

import numpy as np
import jax
import jax.numpy as jnp
from jax import lax
from jax.experimental import pallas as pl
from jax.experimental.pallas import tpu as pltpu

F32 = jnp.float32
BF16 = jnp.bfloat16
I32 = jnp.int32
I16 = jnp.int16

CHUNK = 64
H_A, DQK_A = 4, 64
DV_A = 2 * DQK_A
H_B, DH_B = 4, 64
LEFT_CHUNKS = 8
REL_MAX = 128
H_C, DH_C = 4, 64
H_IDX, D_IDX = 8, 64
TOPK_MAX = 256
H_X = 4
N_EXPERTS = 8
LN_EPS = 1e-5
RMS_EPS = 1e-6
NEG_INF = -1e30
MASKED = -2e30

LANES = 128
SUBLANES = 8
VMEM_LIMIT = 56 * 1024 * 1024
ATT_TILE = 256

_NT = (((1,), (1,)), ((), ()))


def _alibi_slopes():
    n = H_A + H_C
    s = (2.0 ** (-8.0 * np.arange(1, n + 1) / n)).astype(np.float32)
    return [float(v) for v in s[0::2]], [float(v) for v in s[1::2]]


def _params(*sem):
    return pltpu.CompilerParams(dimension_semantics=sem, vmem_limit_bytes=VMEM_LIMIT)


def _dot(a, b):
    return jnp.dot(a, b, preferred_element_type=F32)


SCORE_ROW_CHUNK = 128


def _dot_row_chunks(a, b):
    rows = a.shape[0]
    step = min(SCORE_ROW_CHUNK, rows)
    return jnp.concatenate([_dot(a[r:r + step], b) for r in range(0, rows, step)], axis=0)


def _dot_nt(a, b):
    return lax.dot_general(a, b, _NT, preferred_element_type=F32)


def _layer_norm_rows(y, g, b):
    mu = jnp.mean(y, axis=-1, keepdims=True)
    d = y - mu
    var = jnp.mean(d * d, axis=-1, keepdims=True)
    return d * lax.rsqrt(var + LN_EPS) * g + b


def _silu(a):
    return a / (1.0 + jnp.exp(-a))


def _mm_body(x_ref, w_ref, o_ref):
    o_ref[...] = _dot(x_ref[...], w_ref[...]).astype(o_ref.dtype)


def _matmul(x, w, out_dtype, tm=1024, tn=512):
    m, k = x.shape
    n = w.shape[1]
    tm, tn = min(tm, m), min(tn, n)
    return pl.pallas_call(
        _mm_body,
        grid=(m // tm, n // tn),
        in_specs=[pl.BlockSpec((tm, k), lambda i, j: (i, 0)),
                  pl.BlockSpec((k, tn), lambda i, j: (0, j))],
        out_specs=pl.BlockSpec((tm, tn), lambda i, j: (i, j)),
        out_shape=jax.ShapeDtypeStruct((m, n), out_dtype),
        compiler_params=_params("parallel", "parallel"),
    )(x, w)


def _side_proj_body(x_ref, w_ref, vt_ref, vbt_ref, tail_ref):
    y = _dot(x_ref[...], w_ref[...])
    n_v, n_vb = vt_ref.shape[1], vbt_ref.shape[1]
    tile, tile_b = vt_ref.shape[2], vbt_ref.shape[2]
    for r in range(vt_ref.shape[0]):
        vt_ref[r] = y[r * tile:(r + 1) * tile, :n_v].T.astype(BF16)
    for r in range(vbt_ref.shape[0]):
        vbt_ref[r] = y[r * tile_b:(r + 1) * tile_b, n_v:n_v + n_vb].T.astype(BF16)
    tail_ref[...] = y[:, n_v + n_vb:]


def _side_projections(x, w, n_v, tile, n_vb, tile_b, tm=1024):
    m, k = x.shape
    n = w.shape[1]
    n_tail = n - n_v - n_vb
    tm = min(tm, m)
    return pl.pallas_call(
        _side_proj_body,
        grid=(m // tm,),
        in_specs=[pl.BlockSpec((tm, k), lambda i: (i, 0)),
                  pl.BlockSpec((k, n), lambda i: (0, 0))],
        out_specs=[pl.BlockSpec((tm // tile, n_v, tile), lambda i: (i, 0, 0)),
                   pl.BlockSpec((tm // tile_b, n_vb, tile_b), lambda i: (i, 0, 0)),
                   pl.BlockSpec((tm, n_tail), lambda i: (i, 0))],
        out_shape=[jax.ShapeDtypeStruct((m // tile, n_v, tile), BF16),
                   jax.ShapeDtypeStruct((m // tile_b, n_vb, tile_b), BF16),
                   jax.ShapeDtypeStruct((m, n_tail), F32)],
        compiler_params=_params("parallel"),
    )(x, w)


def _mm_ln_body(*refs, n_in, alpha):
    xs, ws = refs[:n_in], refs[n_in:2 * n_in]
    res_ref, g_ref, b_ref, of_ref, ob_ref = refs[2 * n_in:]
    acc = _dot(xs[0][...], ws[0][...])
    for x_ref, w_ref in zip(xs[1:], ws[1:]):
        acc = acc + _dot(x_ref[...], w_ref[...])
    out = _layer_norm_rows(alpha * res_ref[...] + acc, g_ref[...], b_ref[...])
    of_ref[...] = out
    ob_ref[...] = out.astype(BF16)


def _matmul_ln(xs, ws, res, g, b, alpha):
    m, d = res.shape
    k_total = sum(x.shape[1] for x in xs)
    tm = min(1024 if k_total <= 1024 else 512, m)
    n_in = len(xs)
    in_specs = ([pl.BlockSpec((tm, x.shape[1]), lambda i: (i, 0)) for x in xs]
                + [pl.BlockSpec(w.shape, lambda i: (0, 0)) for w in ws]
                + [pl.BlockSpec((tm, d), lambda i: (i, 0)),
                   pl.BlockSpec((1, d), lambda i: (0, 0)),
                   pl.BlockSpec((1, d), lambda i: (0, 0))])
    return pl.pallas_call(
        functools.partial(_mm_ln_body, n_in=n_in, alpha=alpha),
        grid=(m // tm,),
        in_specs=in_specs,
        out_specs=[pl.BlockSpec((tm, d), lambda i: (i, 0)),
                   pl.BlockSpec((tm, d), lambda i: (i, 0))],
        out_shape=[jax.ShapeDtypeStruct((m, d), F32), jax.ShapeDtypeStruct((m, d), BF16)],
        compiler_params=_params("parallel"),
    )(*xs, *ws, res, g.reshape(1, d), b.reshape(1, d))


def _ffn_ln_body(x_ref, w1_ref, w3_ref, w2_ref, res_ref, g_ref, b_ref, of_ref, ob_ref, acc_ref, *, alpha):
    f = pl.program_id(1)
    x = x_ref[...]
    a = _dot(x, w1_ref[...])
    b = _dot(x, w3_ref[...])
    y = _dot((_silu(a) * b).astype(BF16), w2_ref[...])

    @pl.when(f == 0)
    def _():
        acc_ref[...] = y

    @pl.when(f > 0)
    def _():
        acc_ref[...] += y

    @pl.when(f == pl.num_programs(1) - 1)
    def _():
        out = _layer_norm_rows(alpha * res_ref[...] + acc_ref[...], g_ref[...], b_ref[...])
        of_ref[...] = out
        ob_ref[...] = out.astype(BF16)


def _ffn_ln(x, w1, w3, w2, res, g, b, alpha, tm=512, tf=1408):
    m, d = x.shape
    ff = w1.shape[1]
    tm = min(tm, m)
    if ff % tf:
        tf = ff
    row = pl.BlockSpec((tm, d), lambda i, f: (i, 0))
    vec = pl.BlockSpec((1, d), lambda i, f: (0, 0))
    return pl.pallas_call(
        functools.partial(_ffn_ln_body, alpha=alpha),
        grid=(m // tm, ff // tf),
        in_specs=[row,
                  pl.BlockSpec((d, tf), lambda i, f: (0, f)),
                  pl.BlockSpec((d, tf), lambda i, f: (0, f)),
                  pl.BlockSpec((tf, d), lambda i, f: (f, 0)),
                  row, vec, vec],
        out_specs=[row, row],
        out_shape=[jax.ShapeDtypeStruct((m, d), F32), jax.ShapeDtypeStruct((m, d), BF16)],
        scratch_shapes=[pltpu.VMEM((tm, d), F32)],
        compiler_params=_params("parallel", "arbitrary"),
    )(x, w1, w3, w2, res, g.reshape(1, d), b.reshape(1, d))


def _key_query_iotas(tile):
    kpos = lax.broadcasted_iota(I32, (tile, tile), 0)
    qpos = lax.broadcasted_iota(I32, (tile, tile), 1)
    return kpos, qpos


def _scaled_transpose(x, scale):
    return (x.astype(F32) * scale).T.astype(BF16)


LOG2E = math.log2(math.e)
ALIBI_PARTS = 3
ALIBI_ROWS = 16


def _bf16_parts(x):
    parts, r = [], np.float32(x)
    for _ in range(ALIBI_PARTS):
        p = np.float32(r.astype(jnp.bfloat16))
        parts.append(float(p))
        r = np.float32(r - p)
    return parts


def _alibi_key_cols(tile, parts):
    lane = lax.broadcasted_iota(I32, (tile, LANES), 1)
    kr = lax.broadcasted_iota(I32, (tile, LANES), 0).astype(F32)
    v = jnp.where((lane >= ALIBI_PARTS) & (lane < 2 * ALIBI_PARTS), kr, 0.0)
    for n, c in enumerate(parts):
        v = jnp.where(lane == n, c, v)
    return v.astype(BF16)


def _alibi_query_rows(tile, parts):
    row = lax.broadcasted_iota(I32, (ALIBI_ROWS, tile), 0)
    qr = lax.broadcasted_iota(I32, (ALIBI_ROWS, tile), 1).astype(F32)
    v = jnp.where(row < ALIBI_PARTS, -qr, 0.0)
    for n, c in enumerate(parts):
        v = jnp.where(row == ALIBI_PARTS + n, c, v)
    return v.astype(BF16)


def _init_alibi(slopes, dneg_scr, kx_scr, qa_scr, tile, maps_per_head):
    kpos, qpos = _key_query_iotas(tile)
    rel = (qpos - kpos).astype(F32)
    qa_scr[...] = jnp.zeros(qa_scr.shape, BF16)
    for h, slope in enumerate(slopes):
        parts = _bf16_parts(slope * LOG2E)
        dneg_scr[h] = 2.0 * jnp.minimum(np.float32(slope * LOG2E) * rel, 0.0)
        kx_scr[h] = _alibi_key_cols(tile, parts)
        for mm in range(maps_per_head):
            qa_scr[h * maps_per_head + mm, LANES:LANES + ALIBI_ROWS, :] = _alibi_query_rows(tile, parts)


def _store_query_maps(qa_scr, qt, n_maps, dh):
    for idx in range(n_maps):
        r0 = (idx % 2) * dh
        qa_scr[idx, r0:r0 + dh, :] = qt[idx * dh:(idx + 1) * dh, :]


def _softmax_probs(c, t_ref, tmax_ref, p_ref, a_ref, m_ref, l_ref):
    m_old = m_ref[...]
    m_new = jnp.maximum(m_old, tmax_ref[...] - c)
    p = jnp.exp2(t_ref[...] - (m_new + c))
    a = jnp.exp2(m_old - m_new)
    l_ref[...] = a * l_ref[...] + jnp.sum(p, axis=0, keepdims=True)
    m_ref[...] = m_new
    a_ref[...] = a
    p_ref[...] = p.astype(BF16)


def _accumulate(vt, p_ref, a_ref, acc_ref):
    acc_ref[...] = a_ref[...] * acc_ref[...] + _dot(vt, p_ref[...])


SCORE_LEAD, PROB_LEAD = 4, 2
N_T_SLOTS, N_P_SLOTS = SCORE_LEAD - PROB_LEAD + 2, PROB_LEAD + 1


def _stage_scratch(tile, keys=None):
    keys = tile if keys is None else keys
    return ([pltpu.VMEM((keys, tile), F32)] * N_T_SLOTS + [pltpu.VMEM((1, tile), F32)] * N_T_SLOTS
            + [pltpu.VMEM((keys, tile), BF16)] * N_P_SLOTS + [pltpu.VMEM((1, tile), F32)] * N_P_SLOTS)


def _pipelined_maps(n_maps, scores, probs, accumulate, stage):
    nt, npb = N_T_SLOTS, N_P_SLOTS
    t_slots, x_slots = stage[:nt], stage[nt:2 * nt]
    p_slots, a_slots = stage[2 * nt:2 * nt + npb], stage[2 * nt + npb:]

    def stage_scores(k):
        s = scores(k)
        t_slots[k % nt][...] = s
        x_slots[k % nt][...] = jnp.max(s, axis=0, keepdims=True)

    def stage_probs(k):
        probs(k, t_slots[k % nt], x_slots[k % nt], p_slots[k % npb], a_slots[k % npb])

    for step in range(-SCORE_LEAD, n_maps):
        if 0 <= step + SCORE_LEAD < n_maps:
            stage_scores(step + SCORE_LEAD)
        if 0 <= step + PROB_LEAD < n_maps:
            stage_probs(step + PROB_LEAD)
        if step >= 0:
            accumulate(step, p_slots[step % npb], a_slots[step % npb])


def _diff_body(q_ref, k_ref, vt_ref, lq1_ref, lk1_ref, lq2_ref, lk2_ref, gain_ref, o_ref,
               qa_scr, kx_scr, dneg_scr, m_scr, l_scr, acc_scr, *stage,
               tile, slopes, lam_init):
    i = pl.program_id(1)
    kpos, qpos = _key_query_iotas(tile)

    @pl.when((pl.program_id(0) == 0) & (i == 0))
    def _():
        _init_alibi(slopes, dneg_scr, kx_scr, qa_scr, tile, 2)

    _store_query_maps(qa_scr, _scaled_transpose(q_ref[0], 1.0), 2 * H_A, DQK_A)
    m_scr[...] = jnp.full(m_scr.shape, NEG_INF, F32)
    l_scr[...] = jnp.zeros(l_scr.shape, F32)
    acc_scr[...] = jnp.zeros(acc_scr.shape, F32)

    n_maps = 2 * H_A

    def sweep(blocks):
        ctx = []
        for j, diag in blocks:
            k0 = pl.multiple_of(j * tile, tile)
            ctx.append((diag, k_ref[0, pl.ds(k0, tile), :], vt_ref[0, j], ((i - j) * tile).astype(F32)))

        def scores(n):
            (diag, kb, _, _), idx = ctx[n // n_maps], n % n_maps
            h, pair = idx // 2, idx // 2
            keys = jnp.concatenate([kb[:, pair * LANES:(pair + 1) * LANES], kx_scr[h]], axis=1)
            s = _dot_row_chunks(keys, qa_scr[idx])
            if diag:
                allowed = (kpos // CHUNK) <= (qpos // CHUNK)
                return jnp.where(allowed, s + dneg_scr[h], MASKED)
            return s

        def probs(n, t_ref, tmax_ref, p_ref, a_ref):
            (diag, _, _, off), idx = ctx[n // n_maps], n % n_maps
            shift = 0.0 if diag else (slopes[idx // 2] * LOG2E) * off
            _softmax_probs(shift, t_ref, tmax_ref, p_ref, a_ref, m_scr.at[idx], l_scr.at[idx])

        def accumulate(n, p_ref, a_ref):
            vtb, idx = ctx[n // n_maps][2], n % n_maps
            h = idx // 2
            _accumulate(vtb[h * DV_A:(h + 1) * DV_A, :], p_ref, a_ref, acc_scr.at[idx])

        _pipelined_maps(n_maps * len(blocks), scores, probs, accumulate, stage)

    def sweep_pair(pp, carry):
        sweep([(2 * pp, False), (2 * pp + 1, False)])
        return carry

    lax.fori_loop(0, i // 2, sweep_pair, 0)

    @pl.when(i % 2 == 1)
    def _():
        sweep([(i - 1, False), (i, True)])

    @pl.when(i % 2 == 0)
    def _():
        sweep([(i, True)])

    lam = (jnp.exp(jnp.sum(lq1_ref[...] * lk1_ref[...], axis=-1, keepdims=True))
           - jnp.exp(jnp.sum(lq2_ref[...] * lk2_ref[...], axis=-1, keepdims=True)) + lam_init)
    gain = gain_ref[...] * (1.0 - lam_init)
    outs = []
    for h in range(H_A):
        o = acc_scr[2 * h] / l_scr[2 * h] - lam * (acc_scr[2 * h + 1] / l_scr[2 * h + 1])
        o = o * lax.rsqrt(jnp.mean(o * o, axis=0, keepdims=True) + RMS_EPS)
        outs.append((o * gain).T)
    o_ref[0] = jnp.concatenate(outs, axis=-1).astype(BF16)


def _diff_attention(proj, vt, lq1, lk1, lq2, lk2, gain, slopes, lam_init):
    b, s, _ = proj.shape
    tile = min(ATT_TILE, s)
    nkb = s // tile
    w = H_A * DV_A
    vec = lambda a: a.reshape(1, -1).astype(F32)
    small = lambda n: pl.BlockSpec((1, n), lambda bi, i: (0, 0))
    return pl.pallas_call(
        functools.partial(_diff_body, tile=tile, slopes=slopes, lam_init=lam_init),
        grid=(b, nkb),
        in_specs=[pl.BlockSpec((1, tile, w), lambda bi, i: (bi, i, 0)),
                  pl.BlockSpec((1, s, w), lambda bi, i: (bi, 0, 1)),
                  pl.BlockSpec((1, nkb, w, tile), lambda bi, i: (bi, 0, 0, 0)),
                  small(DQK_A), small(DQK_A), small(DQK_A), small(DQK_A),
                  pl.BlockSpec((DV_A, 1), lambda bi, i: (0, 0))],
        out_specs=pl.BlockSpec((1, tile, w), lambda bi, i: (bi, i, 0)),
        out_shape=jax.ShapeDtypeStruct((b, s, w), BF16),
        scratch_shapes=[pltpu.VMEM((2 * H_A, 2 * LANES, tile), BF16),
                        pltpu.VMEM((H_A, tile, LANES), BF16),
                        pltpu.VMEM((H_A, tile, tile), F32),
                        pltpu.VMEM((2 * H_A, 1, tile), F32),
                        pltpu.VMEM((2 * H_A, 1, tile), F32),
                        pltpu.VMEM((2 * H_A, DV_A, tile), F32)] + _stage_scratch(tile),
        compiler_params=_params("arbitrary", "arbitrary"),
    )(proj, proj, vt, vec(lq1), vec(lk1), vec(lq2), vec(lk2), gain.reshape(-1, 1).astype(F32))


BAND_CHUNKS = LANES // CHUNK
BAND_WINDOW = (LEFT_CHUNKS + BAND_CHUNKS) * CHUNK


def _band_body(q_ref, k_ref, vt_ref, bias_ref, o_ref, qz_scr, ot_scr, *stage):
    i = pl.program_id(1)
    ntile = BAND_WINDOW // LANES

    @pl.when((pl.program_id(0) == 0) & (i == 0))
    def _():
        qz_scr[...] = jnp.zeros(qz_scr.shape, BF16)

    qt = _scaled_transpose(q_ref[0], 1.0)
    for h in range(H_B):
        r0 = (h % 2) * DH_B
        qz_scr[h, r0:r0 + DH_B, :] = qt[h * DH_B:(h + 1) * DH_B, :]

    start = pl.multiple_of(i * LANES, LANES)
    kw = k_ref[0, pl.ds(start, BAND_WINDOW), :]
    vt = vt_ref[0, pl.ds(i, ntile)]
    kk = lax.broadcasted_iota(I32, (BAND_WINDOW, LANES), 0)
    valid = kk >= LEFT_CHUNKS * CHUNK - start

    def scores(h):
        pair = h // 2
        s = _dot_row_chunks(kw[:, pair * LANES:(pair + 1) * LANES], qz_scr[h])
        return jnp.where(valid, s + bias_ref[h], MASKED)

    def probs(h, t_ref, tmax_ref, p_ref, l_ref):
        p = jnp.exp2(t_ref[...] - tmax_ref[...])
        l_ref[...] = jnp.sum(p, axis=0, keepdims=True)
        p_ref[...] = p.astype(BF16)

    def accumulate(h, p_ref, l_ref):
        vth = jnp.concatenate([vt[n, h * DH_B:(h + 1) * DH_B, :] for n in range(ntile)], axis=1)
        ot_scr[h * DH_B:(h + 1) * DH_B, :] = _dot(vth, p_ref[...]) / l_ref[...]

    _pipelined_maps(H_B, scores, probs, accumulate, stage)
    o_ref[0] = ot_scr[...].T.astype(BF16)


def _band_bias_table(table):
    bias = _band_bias(table).transpose(0, 2, 1) * LOG2E
    cols = [jnp.pad(bias, ((0, 0), (c * CHUNK, (BAND_CHUNKS - 1 - c) * CHUNK), (0, 0)), constant_values=MASKED)
            for c in range(BAND_CHUNKS)]
    return jnp.concatenate(cols, axis=2)


def _band_attention(proj, kpad, vt_pad, bias):
    b, s, _ = proj.shape
    w = H_B * DH_B
    sp = kpad.shape[1]
    nvt = vt_pad.shape[1]
    return pl.pallas_call(
        _band_body,
        grid=(b, s // LANES),
        in_specs=[pl.BlockSpec((1, LANES, w), lambda bi, i: (bi, i, 6)),
                  pl.BlockSpec((1, sp, w), lambda bi, i: (bi, 0, 0)),
                  pl.BlockSpec((1, nvt, w, LANES), lambda bi, i: (bi, 0, 0, 0)),
                  pl.BlockSpec((H_B, BAND_WINDOW, LANES), lambda bi, i: (0, 0, 0))],
        out_specs=pl.BlockSpec((1, LANES, w), lambda bi, i: (bi, i, 0)),
        out_shape=jax.ShapeDtypeStruct((b, s, w), BF16),
        scratch_shapes=[pltpu.VMEM((H_B, LANES, LANES), BF16),
                        pltpu.VMEM((w, LANES), F32)] + _stage_scratch(LANES, BAND_WINDOW),
        compiler_params=_params("arbitrary", "arbitrary"),
    )(proj, kpad, vt_pad, bias)


def _sortable(x):
    bits = pltpu.bitcast(x + 0.0, I32)
    return bits ^ ((bits >> 31) & 0x7FFFFFFF)


_INT_MIN = -2 ** 31
_VALID_KEY = int(np.array(0.5 * NEG_INF, np.float32).view(np.int32))
_VALID_KEY = _VALID_KEY ^ ((_VALID_KEY >> 31) & 0x7FFFFFFF)


def _dsa_body(qc_ref, kc_ref, vct_ref, qi_ref, kiw_ref, wq_ref, o_ref,
              qa_scr, kx_scr, dneg_scr, qit_scr, key_scr, hi_scr, lo_scr, jstar_scr,
              m_scr, l_scr, acc_scr, *stage,
              tile, topk, slopes, seq):
    i = pl.program_id(1)
    nblk = i + 1
    kpos, qpos = _key_query_iotas(tile)

    @pl.when((pl.program_id(0) == 0) & (i == 0))
    def _():
        _init_alibi(slopes, dneg_scr, kx_scr, qa_scr, tile, 1)

    qit_scr[...] = _scaled_transpose(qi_ref[0], D_IDX ** -0.5)
    wt = wq_ref[0].T
    wrow = [wt[D_IDX + h:D_IDX + h + 1, :] * H_IDX ** -0.5 for h in range(H_IDX)]

    def score_block(j, diag):
        k0 = pl.multiple_of(j * tile, tile)
        kib = kiw_ref[0, pl.ds(k0, tile), :][:, :D_IDX].astype(BF16)
        isc = jnp.zeros((tile, tile), F32)
        for h in range(H_IDX):
            d = _dot(kib, qit_scr[h * D_IDX:(h + 1) * D_IDX, :])
            isc = isc + jnp.maximum(d, 0.0) * wrow[h]
        if diag:
            isc = jnp.where((kpos // CHUNK) <= (qpos // CHUNK), isc, NEG_INF)
        key = _sortable(isc)
        key_scr[pl.ds(k0, tile), :] = key
        hi_scr[pl.ds(k0, tile), :] = (key >> 16).astype(I16)
        lo_scr[pl.ds(k0, tile), :] = ((key & 0xFFFF) - 2 ** 15).astype(I16)

    def score_pair(pp, carry):
        score_block(2 * pp, False)
        score_block(2 * pp + 1, False)
        return carry

    lax.fori_loop(0, i // 2, score_pair, 0)

    @pl.when(i % 2 == 1)
    def _():
        score_block(i - 1, False)
        score_block(i, True)

    @pl.when(i % 2 == 0)
    def _():
        score_block(i, True)

    def count(pred):
        def body(j, part):
            k0 = pl.multiple_of(j * tile, tile)
            hit = jnp.where(pred(key_scr[pl.ds(k0, tile), :], kpos + k0), 1.0, 0.0)
            return part + jnp.sum(hit.reshape(tile // SUBLANES, SUBLANES, tile), axis=0)
        part = lax.fori_loop(0, nblk, body, jnp.zeros((SUBLANES, tile), F32))
        return jnp.sum(part, axis=0, keepdims=True)

    kf = float(topk)
    rows16 = 2 * SUBLANES

    pair = 2 if (seq // tile) % 2 == 0 else 1
    span = pair * tile
    lowest16 = jnp.full((tile, tile), -2 ** 15, I16)

    @pl.when(nblk % pair == 1)
    def _():
        pad_rows = pl.ds(pl.multiple_of(nblk * tile, tile), tile)
        hi_scr[pad_rows, :] = lowest16
        lo_scr[pad_rows, :] = lowest16

    def count16(src_ref, pred):
        def body(j, part):
            k0 = pl.multiple_of(j * span, span)
            hit = jnp.where(pred(src_ref[pl.ds(k0, span), :]), jnp.int16(1), jnp.int16(0))
            for g in range(span // rows16):
                part = part + hit[g * rows16:(g + 1) * rows16, :]
            return part
        part = lax.fori_loop(0, (nblk + pair - 1) // pair, body, jnp.zeros((rows16, tile), I16))
        return jnp.sum(part.astype(I32), axis=0, keepdims=True).astype(F32)

    def search16(src_ref, base):
        def bit(p, u):
            cand_u = u | jnp.left_shift(jnp.int32(1), 15 - p)
            cand = (cand_u - 2 ** 15).astype(I16)
            cnt = base + count16(src_ref, lambda v: v >= cand)
            return jnp.where(cnt >= kf, cand_u, u)
        return lax.fori_loop(0, 16, bit, jnp.zeros((1, tile), I32))

    thr_hi = search16(hi_scr, 0.0) - 2 ** 15
    thr_hi16 = thr_hi.astype(I16)
    above = count16(hi_scr, lambda v: v > thr_hi16)

    def mask_low(j, carry):
        k0 = pl.multiple_of(j * tile, tile)
        rows = pl.ds(k0, tile)
        lo_scr[rows, :] = jnp.where(hi_scr[rows, :] == thr_hi16, lo_scr[rows, :], jnp.int16(-2 ** 15))
        return carry

    lax.fori_loop(0, nblk, mask_low, 0)
    thr = jnp.left_shift(thr_hi, 16) + search16(lo_scr, above)

    thr_b = thr
    c_gt = count(lambda key, idx: key > thr_b)
    c_ge = count(lambda key, idx: key >= thr_b)
    tied = (c_ge > kf) & (thr > _VALID_KEY)
    need = kf - c_gt
    jstar_scr[...] = jnp.full((1, tile), seq, I32)

    @pl.when(jnp.max(jnp.where(tied, 1.0, 0.0)) > 0.0)
    def _():
        nbits = max(1, int(math.ceil(math.log2(seq))))

        def index_bit(p, lo):
            cand = lo | jnp.left_shift(jnp.int32(1), nbits - 1 - p)
            cnt = count(lambda key, idx: (key == thr_b) & (idx < cand))
            return jnp.where(cnt < need, cand, lo)

        lo = lax.fori_loop(0, nbits, index_bit, jnp.zeros((1, tile), I32))
        jstar_scr[...] = jnp.where(tied, lo, seq)

    jstar = jstar_scr[...]
    thr2 = jnp.maximum(thr, _VALID_KEY + 1)

    _store_query_maps(qa_scr, _scaled_transpose(qc_ref[0], 1.0), H_C, DH_C)
    m_scr[...] = jnp.full(m_scr.shape, NEG_INF, F32)
    l_scr[...] = jnp.zeros(l_scr.shape, F32)
    acc_scr[...] = jnp.zeros(acc_scr.shape, F32)

    def attend(blocks):
        ctx = []
        for j, diag in blocks:
            k0 = pl.multiple_of(j * tile, tile)
            key = key_scr[pl.ds(k0, tile), :]
            sel = (key > thr2) | ((key == thr2) & (kpos + k0 <= jstar))
            ctx.append((diag, sel, kc_ref[0, pl.ds(k0, tile), :], vct_ref[0, j], ((i - j) * tile).astype(F32)))

        def scores(idx):
            (diag, sel, kb, _, _), h = ctx[idx // H_C], idx % H_C
            pair = h // 2
            keys = jnp.concatenate([kb[:, pair * LANES:(pair + 1) * LANES], kx_scr[h]], axis=1)
            s = _dot_row_chunks(keys, qa_scr[h])
            if diag:
                s = s + dneg_scr[h]
            return jnp.where(sel, s, MASKED)

        def probs(idx, t_ref, tmax_ref, p_ref, a_ref):
            (diag, _, _, _, off), h = ctx[idx // H_C], idx % H_C
            shift = 0.0 if diag else (slopes[h] * LOG2E) * off
            _softmax_probs(shift, t_ref, tmax_ref, p_ref, a_ref, m_scr.at[h], l_scr.at[h])

        def accumulate(idx, p_ref, a_ref):
            vtb, h = ctx[idx // H_C][3], idx % H_C
            _accumulate(vtb[h * DH_C:(h + 1) * DH_C, :], p_ref, a_ref, acc_scr.at[h])

        _pipelined_maps(H_C * len(blocks), scores, probs, accumulate, stage)

    def attend_pair(pp, carry):
        attend([(2 * pp, False), (2 * pp + 1, False)])
        return carry

    lax.fori_loop(0, i // 2, attend_pair, 0)

    @pl.when(i % 2 == 1)
    def _():
        attend([(i - 1, False), (i, True)])

    @pl.when(i % 2 == 0)
    def _():
        attend([(i, True)])
    outs = [(acc_scr[h] / l_scr[h]).T for h in range(H_C)]
    o_ref[0] = jnp.concatenate(outs, axis=-1).astype(BF16)


def _dsa_attention(proj, vt, kiw, slopes):
    b, s, _ = proj.shape
    tile = min(ATT_TILE, s)
    nkb = s // tile
    w = H_C * DH_C
    wa = H_A * DV_A
    topk = min(TOPK_MAX, s // 4)
    return pl.pallas_call(
        functools.partial(_dsa_body, tile=tile, topk=topk, slopes=slopes, seq=s),
        grid=(b, nkb),
        in_specs=[pl.BlockSpec((1, tile, w), lambda bi, i: (bi, i, 8)),
                  pl.BlockSpec((1, s, w), lambda bi, i: (bi, 0, 9)),
                  pl.BlockSpec((1, nkb, w, tile), lambda bi, i: (bi, 0, wa // w, 0)),
                  pl.BlockSpec((1, tile, H_IDX * D_IDX), lambda bi, i: (bi, i, 2)),
                  pl.BlockSpec((1, s, LANES), lambda bi, i: (bi, 0, 0)),
                  pl.BlockSpec((1, tile, LANES), lambda bi, i: (bi, i, 0))],
        out_specs=pl.BlockSpec((1, tile, w), lambda bi, i: (bi, i, 0)),
        out_shape=jax.ShapeDtypeStruct((b, s, w), BF16),
        scratch_shapes=[pltpu.VMEM((H_C, 2 * LANES, tile), BF16),
                        pltpu.VMEM((H_C, tile, LANES), BF16),
                        pltpu.VMEM((H_C, tile, tile), F32),
                        pltpu.VMEM((H_IDX * D_IDX, tile), BF16),
                        pltpu.VMEM((s, tile), I32),
                        pltpu.VMEM((s, tile), I16),
                        pltpu.VMEM((s, tile), I16),
                        pltpu.VMEM((1, tile), I32),
                        pltpu.VMEM((H_C, 1, tile), F32),
                        pltpu.VMEM((H_C, 1, tile), F32),
                        pltpu.VMEM((H_C, DH_C, tile), F32)] + _stage_scratch(tile),
        compiler_params=_params("arbitrary", "arbitrary"),
    )(proj, proj, vt, proj, kiw, kiw)


def _xattn_body(x_ref, res_ref, wq_ref, k_ref, v_ref, wo_ref, g_ref, b_ref, of_ref, ob_ref, *, dh, alpha):
    q_all = _dot(x_ref[0], wq_ref[...]).astype(BF16)
    outs = []
    for h in range(H_X):
        c = h * dh
        q = q_all[:, c:c + dh] * dh ** -0.5
        s = _dot_nt(q, k_ref[0, :, c:c + dh])
        p = jnp.exp(s - jnp.max(s, axis=-1, keepdims=True))
        l = jnp.sum(p, axis=-1, keepdims=True)
        outs.append((_dot(p.astype(BF16), v_ref[0, :, c:c + dh]) / l).astype(BF16))
    y = _dot(jnp.concatenate(outs, axis=-1), wo_ref[...])
    out = _layer_norm_rows(alpha * res_ref[0] + y, g_ref[...], b_ref[...])
    of_ref[0] = out
    ob_ref[0] = out.astype(BF16)


def _xattn_ln(x, res, wq, kv, wo, g, b, alpha, tq=512):
    bsz, s, d = x.shape
    n_mem = kv.shape[1]
    tq = min(tq, s)
    tile = pl.BlockSpec((1, tq, d), lambda bi, i: (bi, i, 0))
    whole = lambda shape: pl.BlockSpec(shape, lambda bi, i: (0,) * len(shape))
    mem = lambda col: pl.BlockSpec((1, n_mem, d), lambda bi, i: (bi, 0, col))
    return pl.pallas_call(
        functools.partial(_xattn_body, dh=d // H_X, alpha=alpha),
        grid=(bsz, s // tq),
        in_specs=[tile, tile, whole((d, d)), mem(0), mem(1), whole((d, d)), whole((1, d)), whole((1, d))],
        out_specs=[tile, tile],
        out_shape=[jax.ShapeDtypeStruct((bsz, s, d), F32), jax.ShapeDtypeStruct((bsz, s, d), BF16)],
        compiler_params=_params("parallel", "parallel"),
    )(x, res, wq, kv, kv, wo, g.reshape(1, d), b.reshape(1, d))


def _router_body(x_ref, r_ref, g_ref):
    logits = _dot(x_ref[...], r_ref[...])
    lane = lax.broadcasted_iota(I32, logits.shape, 1).astype(F32)
    big = float(LANES)
    lg = jnp.where(lane < N_EXPERTS, logits, -jnp.inf)
    m1 = jnp.max(lg, axis=-1, keepdims=True)
    i1 = jnp.min(jnp.where(lg == m1, lane, big), axis=-1, keepdims=True)
    lg2 = jnp.where(lane == i1, -jnp.inf, lg)
    m2 = jnp.max(lg2, axis=-1, keepdims=True)
    i2 = jnp.min(jnp.where(lg2 == m2, lane, big), axis=-1, keepdims=True)
    e = jnp.exp(m2 - m1)
    den = 1.0 + e
    g_ref[...] = jnp.where(lane == i1, 1.0 / den, 0.0) + jnp.where(lane == i2, e / den, 0.0)


def _router_gates(x, router, tm=1024):
    m, d = x.shape
    tm = min(tm, m)
    rpad = jnp.zeros((d, LANES), BF16).at[:, :N_EXPERTS].set(router.astype(BF16))
    return pl.pallas_call(
        _router_body,
        grid=(m // tm,),
        in_specs=[pl.BlockSpec((tm, d), lambda i: (i, 0)),
                  pl.BlockSpec((d, LANES), lambda i: (0, 0))],
        out_specs=pl.BlockSpec((tm, LANES), lambda i: (i, 0)),
        out_shape=jax.ShapeDtypeStruct((m, LANES), F32),
        compiler_params=_params("parallel"),
    )(x, rpad)


MOE_BLOCK = 896
MOE_CHUNK = 128
MOE_GROUP = 2
MOE_FF_TILE = 1792


def _moe_body(x_ref, gate_ref, tri_ref, w1_ref, w3_ref, w2_ref, o_ref,
              xs_scr, acc_scr, rcol_scr, rrow_scr, cnt_smem, *, n_tokens):
    e = pl.program_id(1)
    f = pl.program_id(2)
    nf = pl.num_programs(2)
    tb = x_ref.shape[0]
    ch = MOE_CHUNK
    gw = MOE_GROUP * ch

    tok0 = pl.program_id(0) * tb
    valid_col = lax.broadcasted_iota(I32, (tb, 1), 0) + tok0 < n_tokens

    @pl.when((e == 0) & (f == 0))
    def _():
        o_ref[...] = jnp.zeros(o_ref.shape, F32)
        member = jnp.where((gate_ref[...] > 0.0) & valid_col, 1.0, 0.0)
        rank = _dot(tri_ref[...], member.astype(BF16))
        rcol = jnp.where(member > 0.0, rank, -1.0)
        rcol_scr[...] = rcol
        rrow_scr[...] = rcol.T
        counts = jnp.sum(member, axis=0, keepdims=True)
        for ee in range(N_EXPERTS):
            cnt_smem[ee] = counts[0, ee].astype(I32)

    count = cnt_smem[e]
    single = count <= gw
    nchunk = jnp.where(single, MOE_GROUP, (count + (ch - 1)) // ch)

    @pl.when(f == 0)
    def _():
        acc_scr[...] = jnp.zeros(acc_scr.shape, F32)
        sub = lax.broadcasted_iota(I32, (SUBLANES, tb), 0)
        r_row = jnp.sum(jnp.where(sub == e, rrow_scr[0:SUBLANES, :], 0.0), axis=0, keepdims=True)
        xz = jnp.where(valid_col, x_ref[...], jnp.zeros((), BF16))

        def gather_rows(r0, rows):
            row_id = lax.broadcasted_iota(I32, (rows, tb), 0).astype(F32) + jnp.asarray(r0, F32)
            sel = jnp.where(row_id == r_row, 1.0, 0.0).astype(BF16)
            xs_scr[pl.ds(r0, rows), :] = _dot(sel, xz).astype(BF16)

        @pl.when(single)
        def _():
            gather_rows(0, gw)

        @pl.when(jnp.logical_not(single))
        def _():
            def gather(c, carry):
                gather_rows(pl.multiple_of(c * ch, ch), ch)
                return carry
            lax.fori_loop(0, nchunk, gather, 0)

    def ffn_rows(r0, rows):
        xs = xs_scr[pl.ds(r0, rows), :]
        a = _dot(xs, w1_ref[0])
        b = _dot(xs, w3_ref[0])
        acc_scr[pl.ds(r0, rows), :] += _dot((_silu(a) * b).astype(BF16), w2_ref[0])

    @pl.when(single)
    def _():
        ffn_rows(0, gw)

    @pl.when(jnp.logical_not(single))
    def _():
        def ffn(c, carry):
            ffn_rows(pl.multiple_of(c * ch, ch), ch)
            return carry
        lax.fori_loop(0, nchunk, ffn, 0)

    @pl.when(f == nf - 1)
    def _():
        pick = lax.broadcasted_iota(I32, (tb, LANES), 1) == e
        r_col = jnp.sum(jnp.where(pick, rcol_scr[...], 0.0), axis=1, keepdims=True)
        g_col = jnp.sum(jnp.where(pick & valid_col, gate_ref[...], 0.0), axis=1, keepdims=True)
        col_id = lax.broadcasted_iota(I32, (tb, gw), 1).astype(F32)

        def combine(gi, carry):
            r0 = pl.multiple_of(gi * gw, gw)
            selt = jnp.where(col_id + r0.astype(F32) == r_col, 1.0, 0.0).astype(BF16)
            y = acc_scr[pl.ds(r0, gw), :]
            y_hi = y.astype(BF16)
            y_lo = (y - y_hi.astype(F32)).astype(BF16)
            o_ref[...] += g_col * (_dot(selt, y_hi) + _dot(selt, y_lo))
            return carry

        lax.fori_loop(0, (nchunk + (MOE_GROUP - 1)) // MOE_GROUP, combine, 0)


def _moe_experts(x, gates, w1, w3, w2):
    m, d = x.shape
    ne, _, ff = w1.shape
    tb = min(MOE_BLOCK, m)
    tf = MOE_FF_TILE if ff % MOE_FF_TILE == 0 else ff
    gw = MOE_GROUP * MOE_CHUNK
    cap = pl.cdiv(tb, gw) * gw
    assert tb % LANES == 0 and ne <= SUBLANES
    tri = jnp.asarray(np.tril(np.ones((tb, tb), np.float32), -1), BF16)
    row = lambda i, e, f: (i, 0)
    return pl.pallas_call(
        functools.partial(_moe_body, n_tokens=m),
        grid=(pl.cdiv(m, tb), ne, ff // tf),
        in_specs=[pl.BlockSpec((tb, d), row),
                  pl.BlockSpec((tb, LANES), row),
                  pl.BlockSpec((tb, tb), lambda i, e, f: (0, 0)),
                  pl.BlockSpec((1, d, tf), lambda i, e, f: (e, 0, f)),
                  pl.BlockSpec((1, d, tf), lambda i, e, f: (e, 0, f)),
                  pl.BlockSpec((1, tf, d), lambda i, e, f: (e, f, 0))],
        out_specs=pl.BlockSpec((tb, d), row),
        out_shape=jax.ShapeDtypeStruct((m, d), F32),
        scratch_shapes=[pltpu.VMEM((cap, d), BF16),
                        pltpu.VMEM((cap, d), F32),
                        pltpu.VMEM((tb, LANES), F32),
                        pltpu.VMEM((LANES, tb), F32),
                        pltpu.SMEM((N_EXPERTS,), I32)],
        compiler_params=_params("parallel", "arbitrary", "arbitrary"),
    )(x, gates, tri, w1, w3, w2)


def _res_ln_body(res_ref, y_ref, g_ref, b_ref, of_ref, ob_ref, *, alpha):
    out = _layer_norm_rows(alpha * res_ref[...] + y_ref[...], g_ref[...], b_ref[...])
    of_ref[...] = out
    ob_ref[...] = out.astype(BF16)


def _residual_ln(res, y, g, b, alpha, tm=512):
    m, d = res.shape
    tm = min(tm, m)
    row = pl.BlockSpec((tm, d), lambda i: (i, 0))
    vec = pl.BlockSpec((1, d), lambda i: (0, 0))
    return pl.pallas_call(
        functools.partial(_res_ln_body, alpha=alpha),
        grid=(m // tm,),
        in_specs=[row, row, vec, vec],
        out_specs=[row, row],
        out_shape=[jax.ShapeDtypeStruct((m, d), F32), jax.ShapeDtypeStruct((m, d), BF16)],
        compiler_params=_params("parallel"),
    )(res, y, g.reshape(1, d), b.reshape(1, d))


def _band_bias(table):
    band = (LEFT_CHUNKS + 1) * CHUNK
    left = LEFT_CHUNKS * CHUNK
    rev = table.astype(F32)[:, ::-1]
    pad = left - REL_MAX + CHUNK - 1
    n_rev = band - 1 - (left - REL_MAX) + 1
    assert n_rev <= 2 * REL_MAX + 1
    ext = jnp.concatenate([jnp.broadcast_to(rev[:, :1], (rev.shape[0], pad)), rev[:, :n_rev]], axis=1)
    n_ext = ext.shape[1]
    period = jnp.concatenate([ext[:, CHUNK - 1:], jnp.zeros((ext.shape[0], 1), F32), ext[:, :CHUNK - 1]], axis=1)
    skew = jnp.tile(period, (1, CHUNK))[:, :CHUNK * n_ext].reshape(-1, CHUNK, n_ext)
    return skew[:, :, :band]


def kernel(x, mem, w_in, lam_q1, lam_k1, lam_q2, lam_k2, diff_norm_g, rel_bias, w_out,
           ln1_g, ln1_b, xattn_wq, xattn_wk, xattn_wv, xattn_wo, ln2_g, ln2_b,
           ffn_w1, ffn_w3, ffn_w2, moe_router, moe_w1, moe_w3, moe_w2, ln3_g, ln3_b):
    b, s, d = x.shape
    depth = w_in.shape[0]
    t = b * s
    alpha = (2.0 * depth) ** 0.25
    slopes_a, slopes_c = _alibi_slopes()
    n_main = 3 * H_A * DV_A + 3 * H_B * DH_B + 3 * H_C * DH_C + H_IDX * D_IDX
    n_tail = D_IDX + H_IDX
    wa, wb, wc = H_A * DV_A, H_B * DH_B, H_C * DH_C
    left = LEFT_CHUNKS * CHUNK

    h = x.reshape(t, d)
    hb = h.astype(BF16)
    memb = mem.reshape(-1, d).astype(BF16)
    for l in range(depth):
        wl = w_in[l]
        o_b, o_c, o_i = 3 * wa, 3 * wa + 3 * wb, 3 * wa + 3 * wb + 3 * wc
        w_main = jnp.concatenate(
            [wl[:, :wa] * (DQK_A ** -0.5 * LOG2E), wl[:, wa:2 * wa], wl[:, o_i:o_i + H_IDX * D_IDX],
             wl[:, o_b:o_b + wb] * (DH_B ** -0.5 * LOG2E), wl[:, o_b + wb:o_b + 2 * wb],
             wl[:, o_c:o_c + wc] * (DH_C ** -0.5 * LOG2E), wl[:, o_c + wc:o_c + 2 * wc]], axis=1).astype(BF16)
        proj = _matmul(hb, w_main, BF16, tn=w_main.shape[1] // 2).reshape(b, s, -1)
        w_side = jnp.concatenate(
            [wl[:, 2 * wa:3 * wa], wl[:, o_c + 2 * wc:o_c + 3 * wc], wl[:, o_b + 2 * wb:o_b + 3 * wb],
             wl[:, n_main:], jnp.zeros((d, LANES - n_tail), F32)], axis=1).astype(BF16)
        tile = min(ATT_TILE, s)
        vt, vbt, kiw = _side_projections(hb, w_side, wa + wc, tile, wb, LANES)
        vt = vt.reshape(b, s // tile, wa + wc, tile)
        kiw = kiw.reshape(b, s, LANES)
        lam_init = 0.8 - 0.6 * math.exp(-0.3 * l)
        ya = _diff_attention(proj, vt, lam_q1[l], lam_k1[l], lam_q2[l], lam_k2[l], diff_norm_g[l],
                             slopes_a, lam_init)
        kpad = jnp.pad(proj[:, :, 3 * wa + wb:3 * wa + 2 * wb], ((0, 0), (left, 0), (0, 0)))
        vbt = jnp.pad(vbt.reshape(b, s // LANES, wb, LANES), ((0, 0), (left // LANES, 0), (0, 0), (0, 0)))
        yb = _band_attention(proj, kpad, vbt, _band_bias_table(rel_bias[l]))
        yc = _dsa_attention(proj, vt, kiw, slopes_c)
        wo = w_out[l].astype(BF16)
        h, hb = _matmul_ln(
            [ya.reshape(t, wa), yb.reshape(t, wb), yc.reshape(t, -1)],
            [wo[:wa], wo[wa:wa + wb], wo[wa + wb:]],
            h, ln1_g[l], ln1_b[l], alpha)
        w_kv = jnp.concatenate([xattn_wk[l], xattn_wv[l]], axis=1).astype(BF16)
        kv = _matmul(memb, w_kv, BF16, tn=d).reshape(b, -1, 2 * d)
        h, hb = _xattn_ln(hb.reshape(b, s, d), h.reshape(b, s, d), xattn_wq[l].astype(BF16), kv,
                          xattn_wo[l].astype(BF16), ln2_g[l], ln2_b[l], alpha)
        h, hb = h.reshape(t, d), hb.reshape(t, d)
        if l % 2 == 0:
            h, hb = _ffn_ln(hb, ffn_w1[l // 2].astype(BF16), ffn_w3[l // 2].astype(BF16),
                            ffn_w2[l // 2].astype(BF16), h, ln3_g[l], ln3_b[l], alpha)
        else:
            gates = _router_gates(hb, moe_router[l // 2])
            y = _moe_experts(hb, gates, moe_w1[l // 2].astype(BF16), moe_w3[l // 2].astype(BF16),
                             moe_w2[l // 2].astype(BF16))
            h, hb = _residual_ln(h, y, ln3_g[l], ln3_b[l], alpha)
    return h.reshape(b, s, d)
```

```python
import functools
import math

import numpy as np
import jax
import jax.numpy as jnp
from jax import lax
from jax.experimental import pallas as pl
from jax.experimental.pallas import tpu as pltpu

F32 = jnp.float32
BF16 = jnp.bfloat16
I32 = jnp.int32
I16 = jnp.int16

CHUNK = 64
H_A, DQK_A = 4, 64
DV_A = 2 * DQK_A
H_B, DH_B = 4, 64
LEFT_CHUNKS = 8
REL_MAX = 128
H_C, DH_C = 4, 64
H_IDX, D_IDX = 8, 64
TOPK_MAX = 256
H_X = 4
N_EXPERTS = 8
LN_EPS = 1e-5
RMS_EPS = 1e-6
NEG_INF = -1e30
MASKED = -2e30

LANES = 128
SUBLANES = 8
VMEM_LIMIT = 56 * 1024 * 1024
ATT_TILE = 256

_NT = (((1,), (1,)), ((), ()))


def _alibi_slopes():
    n = H_A + H_C
    s = (2.0 ** (-8.0 * np.arange(1, n + 1) / n)).astype(np.float32)
    return [float(v) for v in s[0::2]], [float(v) for v in s[1::2]]


def _params(*sem):
    return pltpu.CompilerParams(dimension_semantics=sem, vmem_limit_bytes=VMEM_LIMIT)


def _dot(a, b):
    return jnp.dot(a, b, preferred_element_type=F32)


SCORE_ROW_CHUNK = 128


def _dot_row_chunks(a, b):
    rows = a.shape[0]
    step = min(SCORE_ROW_CHUNK, rows)
    return jnp.concatenate([_dot(a[r:r + step], b) for r in range(0, rows, step)], axis=0)


def _dot_nt(a, b):
    return lax.dot_general(a, b, _NT, preferred_element_type=F32)


def _layer_norm_rows(y, g, b):
    mu = jnp.mean(y, axis=-1, keepdims=True)
    d = y - mu
    var = jnp.mean(d * d, axis=-1, keepdims=True)
    return d * lax.rsqrt(var + LN_EPS) * g + b


def _silu(a):
    return a / (1.0 + jnp.exp(-a))


def _mm_body(x_ref, w_ref, o_ref):
    o_ref[...] = _dot(x_ref[...], w_ref[...]).astype(o_ref.dtype)


def _matmul(x, w, out_dtype, tm=1024, tn=512):
    m, k = x.shape
    n = w.shape[1]
    tm, tn = min(tm, m), min(tn, n)
    return pl.pallas_call(
        _mm_body,
        grid=(m // tm, n // tn),
        in_specs=[pl.BlockSpec((tm, k), lambda i, j: (i, 0)),
                  pl.BlockSpec((k, tn), lambda i, j: (0, j))],
        out_specs=pl.BlockSpec((tm, tn), lambda i, j: (i, j)),
        out_shape=jax.ShapeDtypeStruct((m, n), out_dtype),
        compiler_params=_params("parallel", "parallel"),
    )(x, w)


def _side_proj_body(x_ref, w_ref, vt_ref, vbt_ref, tail_ref):
    y = _dot(x_ref[...], w_ref[...])
    n_v, n_vb = vt_ref.shape[1], vbt_ref.shape[1]
    tile, tile_b = vt_ref.shape[2], vbt_ref.shape[2]
    for r in range(vt_ref.shape[0]):
        vt_ref[r] = y[r * tile:(r + 1) * tile, :n_v].T.astype(BF16)
    for r in range(vbt_ref.shape[0]):
        vbt_ref[r] = y[r * tile_b:(r + 1) * tile_b, n_v:n_v + n_vb].T.astype(BF16)
    tail_ref[...] = y[:, n_v + n_vb:]


def _side_projections(x, w, n_v, tile, n_vb, tile_b, tm=1024):
    m, k = x.shape
    n = w.shape[1]
    n_tail = n - n_v - n_vb
    tm = min(tm, m)
    return pl.pallas_call(
        _side_proj_body,
        grid=(m // tm,),
        in_specs=[pl.BlockSpec((tm, k), lambda i: (i, 0)),
                  pl.BlockSpec((k, n), lambda i: (0, 0))],
        out_specs=[pl.BlockSpec((tm // tile, n_v, tile), lambda i: (i, 0, 0)),
                   pl.BlockSpec((tm // tile_b, n_vb, tile_b), lambda i: (i, 0, 0)),
                   pl.BlockSpec((tm, n_tail), lambda i: (i, 0))],
        out_shape=[jax.ShapeDtypeStruct((m // tile, n_v, tile), BF16),
                   jax.ShapeDtypeStruct((m // tile_b, n_vb, tile_b), BF16),
                   jax.ShapeDtypeStruct((m, n_tail), F32)],
        compiler_params=_params("parallel"),
    )(x, w)


def _mm_ln_body(*refs, n_in, alpha):
    xs, ws = refs[:n_in], refs[n_in:2 * n_in]
    res_ref, g_ref, b_ref, of_ref, ob_ref = refs[2 * n_in:]
    acc = _dot(xs[0][...], ws[0][...])
    for x_ref, w_ref in zip(xs[1:], ws[1:]):
        acc = acc + _dot(x_ref[...], w_ref[...])
    out = _layer_norm_rows(alpha * res_ref[...] + acc, g_ref[...], b_ref[...])
    of_ref[...] = out
    ob_ref[...] = out.astype(BF16)


def _matmul_ln(xs, ws, res, g, b, alpha):
    m, d = res.shape
    k_total = sum(x.shape[1] for x in xs)
    tm = min(1024 if k_total <= 1024 else 512, m)
    n_in = len(xs)
    in_specs = ([pl.BlockSpec((tm, x.shape[1]), lambda i: (i, 0)) for x in xs]
                + [pl.BlockSpec(w.shape, lambda i: (0, 0)) for w in ws]
                + [pl.BlockSpec((tm, d), lambda i: (i, 0)),
                   pl.BlockSpec((1, d), lambda i: (0, 0)),
                   pl.BlockSpec((1, d), lambda i: (0, 0))])
    return pl.pallas_call(
        functools.partial(_mm_ln_body, n_in=n_in, alpha=alpha),
        grid=(m // tm,),
        in_specs=in_specs,
        out_specs=[pl.BlockSpec((tm, d), lambda i: (i, 0)),
                   pl.BlockSpec((tm, d), lambda i: (i, 0))],
        out_shape=[jax.ShapeDtypeStruct((m, d), F32), jax.ShapeDtypeStruct((m, d), BF16)],
        compiler_params=_params("parallel"),
    )(*xs, *ws, res, g.reshape(1, d), b.reshape(1, d))


def _ffn_ln_body(x_ref, w1_ref, w3_ref, w2_ref, res_ref, g_ref, b_ref, of_ref, ob_ref, acc_ref, *, alpha):
    f = pl.program_id(1)
    x = x_ref[...]
    a = _dot(x, w1_ref[...])
    b = _dot(x, w3_ref[...])
    y = _dot((_silu(a) * b).astype(BF16), w2_ref[...])

    @pl.when(f == 0)
    def _():
        acc_ref[...] = y

    @pl.when(f > 0)
    def _():
        acc_ref[...] += y

    @pl.when(f == pl.num_programs(1) - 1)
    def _():
        out = _layer_norm_rows(alpha * res_ref[...] + acc_ref[...], g_ref[...], b_ref[...])
        of_ref[...] = out
        ob_ref[...] = out.astype(BF16)


def _ffn_ln(x, w1, w3, w2, res, g, b, alpha, tm=512, tf=1408):
    m, d = x.shape
    ff = w1.shape[1]
    tm = min(tm, m)
    if ff % tf:
        tf = ff
    row = pl.BlockSpec((tm, d), lambda i, f: (i, 0))
    vec = pl.BlockSpec((1, d), lambda i, f: (0, 0))
    return pl.pallas_call(
        functools.partial(_ffn_ln_body, alpha=alpha),
        grid=(m // tm, ff // tf),
        in_specs=[row,
                  pl.BlockSpec((d, tf), lambda i, f: (0, f)),
                  pl.BlockSpec((d, tf), lambda i, f: (0, f)),
                  pl.BlockSpec((tf, d), lambda i, f: (f, 0)),
                  row, vec, vec],
        out_specs=[row, row],
        out_shape=[jax.ShapeDtypeStruct((m, d), F32), jax.ShapeDtypeStruct((m, d), BF16)],
        scratch_shapes=[pltpu.VMEM((tm, d), F32)],
        compiler_params=_params("parallel", "arbitrary"),
    )(x, w1, w3, w2, res, g.reshape(1, d), b.reshape(1, d))


def _key_query_iotas(tile):
    kpos = lax.broadcasted_iota(I32, (tile, tile), 0)
    qpos = lax.broadcasted_iota(I32, (tile, tile), 1)
    return kpos, qpos


def _scaled_transpose(x, scale):
    return (x.astype(F32) * scale).T.astype(BF16)


LOG2E = math.log2(math.e)
ALIBI_PARTS = 3
ALIBI_ROWS = 16


def _bf16_parts(x):
    parts, r = [], np.float32(x)
    for _ in range(ALIBI_PARTS):
        p = np.float32(r.astype(jnp.bfloat16))
        parts.append(float(p))
        r = np.float32(r - p)
    return parts


def _alibi_key_cols(tile, parts):
    lane = lax.broadcasted_iota(I32, (tile, LANES), 1)
    kr = lax.broadcasted_iota(I32, (tile, LANES), 0).astype(F32)
    v = jnp.where((lane >= ALIBI_PARTS) & (lane < 2 * ALIBI_PARTS), kr, 0.0)
    for n, c in enumerate(parts):
        v = jnp.where(lane == n, c, v)
    return v.astype(BF16)


def _alibi_query_rows(tile, parts):
    row = lax.broadcasted_iota(I32, (ALIBI_ROWS, tile), 0)
    qr = lax.broadcasted_iota(I32, (ALIBI_ROWS, tile), 1).astype(F32)
    v = jnp.where(row < ALIBI_PARTS, -qr, 0.0)
    for n, c in enumerate(parts):
        v = jnp.where(row == ALIBI_PARTS + n, c, v)
    return v.astype(BF16)


def _init_alibi(slopes, dneg_scr, kx_scr, qa_scr, tile, maps_per_head):
    kpos, qpos = _key_query_iotas(tile)
    rel = (qpos - kpos).astype(F32)
    qa_scr[...] = jnp.zeros(qa_scr.shape, BF16)
    for h, slope in enumerate(slopes):
        parts = _bf16_parts(slope * LOG2E)
        dneg_scr[h] = 2.0 * jnp.minimum(np.float32(slope * LOG2E) * rel, 0.0)
        kx_scr[h] = _alibi_key_cols(tile, parts)
        for mm in range(maps_per_head):
            qa_scr[h * maps_per_head + mm, LANES:LANES + ALIBI_ROWS, :] = _alibi_query_rows(tile, parts)


def _store_query_maps(qa_scr, qt, n_maps, dh):
    for idx in range(n_maps):
        r0 = (idx % 2) * dh
        qa_scr[idx, r0:r0 + dh, :] = qt[idx * dh:(idx + 1) * dh, :]


def _softmax_probs(c, t_ref, tmax_ref, p_ref, a_ref, m_ref, l_ref):
    m_old = m_ref[...]
    m_new = jnp.maximum(m_old, tmax_ref[...] - c)
    p = jnp.exp2(t_ref[...] - (m_new + c))
    a = jnp.exp2(m_old - m_new)
    l_ref[...] = a * l_ref[...] + jnp.sum(p, axis=0, keepdims=True)
    m_ref[...] = m_new
    a_ref[...] = a
    p_ref[...] = p.astype(BF16)


def _accumulate(vt, p_ref, a_ref, acc_ref):
    acc_ref[...] = a_ref[...] * acc_ref[...] + _dot(vt, p_ref[...])


SCORE_LEAD, PROB_LEAD = 4, 2
N_T_SLOTS, N_P_SLOTS = SCORE_LEAD - PROB_LEAD + 2, PROB_LEAD + 1


def _stage_scratch(tile, keys=None):
    keys = tile if keys is None else keys
    return ([pltpu.VMEM((keys, tile), F32)] * N_T_SLOTS + [pltpu.VMEM((1, tile), F32)] * N_T_SLOTS
            + [pltpu.VMEM((keys, tile), BF16)] * N_P_SLOTS + [pltpu.VMEM((1, tile), F32)] * N_P_SLOTS)


def _pipelined_maps(n_maps, scores, probs, accumulate, stage):
    nt, npb = N_T_SLOTS, N_P_SLOTS
    t_slots, x_slots = stage[:nt], stage[nt:2 * nt]
    p_slots, a_slots = stage[2 * nt:2 * nt + npb], stage[2 * nt + npb:]

    def stage_scores(k):
        s = scores(k)
        t_slots[k % nt][...] = s
        x_slots[k % nt][...] = jnp.max(s, axis=0, keepdims=True)

    def stage_probs(k):
        probs(k, t_slots[k % nt], x_slots[k % nt], p_slots[k % npb], a_slots[k % npb])

    for step in range(-SCORE_LEAD, n_maps):
        if 0 <= step + SCORE_LEAD < n_maps:
            stage_scores(step + SCORE_LEAD)
        if 0 <= step + PROB_LEAD < n_maps:
            stage_probs(step + PROB_LEAD)
        if step >= 0:
            accumulate(step, p_slots[step % npb], a_slots[step % npb])


SWEEP_GROUP = 4


def _sweep_key_tiles(i, sweep):
    def body(g, carry):
        sweep([(SWEEP_GROUP * g + n, False) for n in range(SWEEP_GROUP)])
        return carry

    ngroups = i // SWEEP_GROUP
    lax.fori_loop(0, ngroups, body, 0)
    base = ngroups * SWEEP_GROUP
    for rem in range(SWEEP_GROUP):
        @pl.when(i - base == rem)
        def _():
            tiles = [(base + n, False) for n in range(rem)] + [(i, True)]
            for a in range(0, len(tiles), 2):
                sweep(tiles[a:a + 2])


def _diff_body(q_ref, k_ref, vt_ref, lq1_ref, lk1_ref, lq2_ref, lk2_ref, gain_ref, o_ref,
               qa_scr, kx_scr, dneg_scr, m_scr, l_scr, acc_scr, *stage,
               tile, slopes, lam_init):
    i = pl.program_id(1)
    kpos, qpos = _key_query_iotas(tile)

    @pl.when((pl.program_id(0) == 0) & (i == 0))
    def _():
        _init_alibi(slopes, dneg_scr, kx_scr, qa_scr, tile, 2)

    _store_query_maps(qa_scr, _scaled_transpose(q_ref[0], 1.0), 2 * H_A, DQK_A)
    m_scr[...] = jnp.full(m_scr.shape, NEG_INF, F32)
    l_scr[...] = jnp.zeros(l_scr.shape, F32)
    acc_scr[...] = jnp.zeros(acc_scr.shape, F32)

    n_maps = 2 * H_A

    def sweep(blocks):
        ctx = []
        for j, diag in blocks:
            k0 = pl.multiple_of(j * tile, tile)
            ctx.append((diag, k_ref[0, pl.ds(k0, tile), :], vt_ref[0, j], ((i - j) * tile).astype(F32)))

        def scores(n):
            (diag, kb, _, _), idx = ctx[n // n_maps], n % n_maps
            h, pair = idx // 2, idx // 2
            keys = jnp.concatenate([kb[:, pair * LANES:(pair + 1) * LANES], kx_scr[h]], axis=1)
            s = _dot_row_chunks(keys, qa_scr[idx])
            if diag:
                allowed = (kpos // CHUNK) <= (qpos // CHUNK)
                return jnp.where(allowed, s + dneg_scr[h], MASKED)
            return s

        def probs(n, t_ref, tmax_ref, p_ref, a_ref):
            (diag, _, _, off), idx = ctx[n // n_maps], n % n_maps
            shift = 0.0 if diag else (slopes[idx // 2] * LOG2E) * off
            _softmax_probs(shift, t_ref, tmax_ref, p_ref, a_ref, m_scr.at[idx], l_scr.at[idx])

        def accumulate(n, p_ref, a_ref):
            vtb, idx = ctx[n // n_maps][2], n % n_maps
            h = idx // 2
            _accumulate(vtb[h * DV_A:(h + 1) * DV_A, :], p_ref, a_ref, acc_scr.at[idx])

        _pipelined_maps(n_maps * len(blocks), scores, probs, accumulate, stage)

    _sweep_key_tiles(i, sweep)

    lam = (jnp.exp(jnp.sum(lq1_ref[...] * lk1_ref[...], axis=-1, keepdims=True))
           - jnp.exp(jnp.sum(lq2_ref[...] * lk2_ref[...], axis=-1, keepdims=True)) + lam_init)
    gain = gain_ref[...] * (1.0 - lam_init)
    outs = []
    for h in range(H_A):
        o = acc_scr[2 * h] / l_scr[2 * h] - lam * (acc_scr[2 * h + 1] / l_scr[2 * h + 1])
        o = o * lax.rsqrt(jnp.mean(o * o, axis=0, keepdims=True) + RMS_EPS)
        outs.append((o * gain).T)
    o_ref[0] = jnp.concatenate(outs, axis=-1).astype(BF16)


def _diff_attention(proj, vt, lq1, lk1, lq2, lk2, gain, slopes, lam_init):
    b, s, _ = proj.shape
    tile = min(ATT_TILE, s)
    nkb = s // tile
    w = H_A * DV_A
    vec = lambda a: a.reshape(1, -1).astype(F32)
    small = lambda n: pl.BlockSpec((1, n), lambda bi, i: (0, 0))
    return pl.pallas_call(
        functools.partial(_diff_body, tile=tile, slopes=slopes, lam_init=lam_init),
        grid=(b, nkb),
        in_specs=[pl.BlockSpec((1, tile, w), lambda bi, i: (bi, i, 0)),
                  pl.BlockSpec((1, s, w), lambda bi, i: (bi, 0, 1)),
                  pl.BlockSpec((1, nkb, w, tile), lambda bi, i: (bi, 0, 0, 0)),
                  small(DQK_A), small(DQK_A), small(DQK_A), small(DQK_A),
                  pl.BlockSpec((DV_A, 1), lambda bi, i: (0, 0))],
        out_specs=pl.BlockSpec((1, tile, w), lambda bi, i: (bi, i, 0)),
        out_shape=jax.ShapeDtypeStruct((b, s, w), BF16),
        scratch_shapes=[pltpu.VMEM((2 * H_A, 2 * LANES, tile), BF16),
                        pltpu.VMEM((H_A, tile, LANES), BF16),
                        pltpu.VMEM((H_A, tile, tile), F32),
                        pltpu.VMEM((2 * H_A, 1, tile), F32),
                        pltpu.VMEM((2 * H_A, 1, tile), F32),
                        pltpu.VMEM((2 * H_A, DV_A, tile), F32)] + _stage_scratch(tile),
        compiler_params=_params("arbitrary", "arbitrary"),
    )(proj, proj, vt, vec(lq1), vec(lk1), vec(lq2), vec(lk2), gain.reshape(-1, 1).astype(F32))


BAND_CHUNKS = LANES // CHUNK
BAND_WINDOW = (LEFT_CHUNKS + BAND_CHUNKS) * CHUNK


def _band_body(q_ref, k_ref, vt_ref, bias_ref, o_ref, qz_scr, ot_scr, *stage):
    i = pl.program_id(1)
    ntile = BAND_WINDOW // LANES

    @pl.when((pl.program_id(0) == 0) & (i == 0))
    def _():
        qz_scr[...] = jnp.zeros(qz_scr.shape, BF16)

    qt = _scaled_transpose(q_ref[0], 1.0)
    for h in range(H_B):
        r0 = (h % 2) * DH_B
        qz_scr[h, r0:r0 + DH_B, :] = qt[h * DH_B:(h + 1) * DH_B, :]

    start = pl.multiple_of(i * LANES, LANES)
    kw = k_ref[0, pl.ds(start, BAND_WINDOW), :]
    vt = vt_ref[0, pl.ds(i, ntile)]
    kk = lax.broadcasted_iota(I32, (BAND_WINDOW, LANES), 0)
    valid = kk >= LEFT_CHUNKS * CHUNK - start

    def scores(h):
        pair = h // 2
        s = _dot_row_chunks(kw[:, pair * LANES:(pair + 1) * LANES], qz_scr[h])
        return jnp.where(valid, s + bias_ref[h], MASKED)

    def probs(h, t_ref, tmax_ref, p_ref, l_ref):
        p = jnp.exp2(t_ref[...] - tmax_ref[...])
        l_ref[...] = jnp.sum(p, axis=0, keepdims=True)
        p_ref[...] = p.astype(BF16)

    def accumulate(h, p_ref, l_ref):
        vth = jnp.concatenate([vt[n, h * DH_B:(h + 1) * DH_B, :] for n in range(ntile)], axis=1)
        ot_scr[h * DH_B:(h + 1) * DH_B, :] = _dot(vth, p_ref[...]) / l_ref[...]

    _pipelined_maps(H_B, scores, probs, accumulate, stage)
    o_ref[0] = ot_scr[...].T.astype(BF16)


def _band_bias_table(table):
    bias = _band_bias(table).transpose(0, 2, 1) * LOG2E
    cols = [jnp.pad(bias, ((0, 0), (c * CHUNK, (BAND_CHUNKS - 1 - c) * CHUNK), (0, 0)), constant_values=MASKED)
            for c in range(BAND_CHUNKS)]
    return jnp.concatenate(cols, axis=2)


def _band_attention(proj, kpad, vt_pad, bias):
    b, s, _ = proj.shape
    w = H_B * DH_B
    sp = kpad.shape[1]
    nvt = vt_pad.shape[1]
    return pl.pallas_call(
        _band_body,
        grid=(b, s // LANES),
        in_specs=[pl.BlockSpec((1, LANES, w), lambda bi, i: (bi, i, 6)),
                  pl.BlockSpec((1, sp, w), lambda bi, i: (bi, 0, 0)),
                  pl.BlockSpec((1, nvt, w, LANES), lambda bi, i: (bi, 0, 0, 0)),
                  pl.BlockSpec((H_B, BAND_WINDOW, LANES), lambda bi, i: (0, 0, 0))],
        out_specs=pl.BlockSpec((1, LANES, w), lambda bi, i: (bi, i, 0)),
        out_shape=jax.ShapeDtypeStruct((b, s, w), BF16),
        scratch_shapes=[pltpu.VMEM((H_B, LANES, LANES), BF16),
                        pltpu.VMEM((w, LANES), F32)] + _stage_scratch(LANES, BAND_WINDOW),
        compiler_params=_params("arbitrary", "arbitrary"),
    )(proj, kpad, vt_pad, bias)


def _sortable(x):
    bits = pltpu.bitcast(x + 0.0, I32)
    return bits ^ ((bits >> 31) & 0x7FFFFFFF)


_INT_MIN = -2 ** 31
_VALID_KEY = int(np.array(0.5 * NEG_INF, np.float32).view(np.int32))
_VALID_KEY = _VALID_KEY ^ ((_VALID_KEY >> 31) & 0x7FFFFFFF)


def _dsa_body(qc_ref, kc_ref, vct_ref, qi_ref, kiw_ref, wq_ref, o_ref,
              qa_scr, kx_scr, dneg_scr, qit_scr, key_scr, hi_scr, lo_scr, jstar_scr,
              m_scr, l_scr, acc_scr, *stage,
              tile, topk, slopes, seq):
    i = pl.program_id(1)
    nblk = i + 1
    kpos, qpos = _key_query_iotas(tile)

    @pl.when((pl.program_id(0) == 0) & (i == 0))
    def _():
        _init_alibi(slopes, dneg_scr, kx_scr, qa_scr, tile, 1)

    qit_scr[...] = _scaled_transpose(qi_ref[0], D_IDX ** -0.5)
    wt = wq_ref[0].T
    wrow = [wt[D_IDX + h:D_IDX + h + 1, :] * H_IDX ** -0.5 for h in range(H_IDX)]

    def score_block(j, diag):
        k0 = pl.multiple_of(j * tile, tile)
        kib = kiw_ref[0, pl.ds(k0, tile), :][:, :D_IDX].astype(BF16)
        isc = jnp.zeros((tile, tile), F32)
        for h in range(H_IDX):
            d = _dot(kib, qit_scr[h * D_IDX:(h + 1) * D_IDX, :])
            isc = isc + jnp.maximum(d, 0.0) * wrow[h]
        if diag:
            isc = jnp.where((kpos // CHUNK) <= (qpos // CHUNK), isc, NEG_INF)
        key = _sortable(isc)
        key_scr[pl.ds(k0, tile), :] = key
        hi_scr[pl.ds(k0, tile), :] = (key >> 16).astype(I16)
        lo_scr[pl.ds(k0, tile), :] = ((key & 0xFFFF) - 2 ** 15).astype(I16)

    def score_tiles(blocks):
        for j, diag in blocks:
            score_block(j, diag)

    _sweep_key_tiles(i, score_tiles)

    def count(pred):
        def body(j, part):
            k0 = pl.multiple_of(j * tile, tile)
            hit = jnp.where(pred(key_scr[pl.ds(k0, tile), :], kpos + k0), 1.0, 0.0)
            return part + jnp.sum(hit.reshape(tile // SUBLANES, SUBLANES, tile), axis=0)
        part = lax.fori_loop(0, nblk, body, jnp.zeros((SUBLANES, tile), F32))
        return jnp.sum(part, axis=0, keepdims=True)

    kf = float(topk)
    rows16 = 2 * SUBLANES

    pair = 2 if (seq // tile) % 2 == 0 else 1
    span = pair * tile
    lowest16 = jnp.full((tile, tile), -2 ** 15, I16)

    @pl.when(nblk % pair == 1)
    def _():
        pad_rows = pl.ds(pl.multiple_of(nblk * tile, tile), tile)
        hi_scr[pad_rows, :] = lowest16
        lo_scr[pad_rows, :] = lowest16

    def count16(src_ref, pred):
        def body(j, part):
            k0 = pl.multiple_of(j * span, span)
            hit = jnp.where(pred(src_ref[pl.ds(k0, span), :]), jnp.int16(1), jnp.int16(0))
            for g in range(span // rows16):
                part = part + hit[g * rows16:(g + 1) * rows16, :]
            return part
        part = lax.fori_loop(0, (nblk + pair - 1) // pair, body, jnp.zeros((rows16, tile), I16))
        return jnp.sum(part.astype(I32), axis=0, keepdims=True).astype(F32)

    def search16(src_ref, base):
        def bit(p, u):
            cand_u = u | jnp.left_shift(jnp.int32(1), 15 - p)
            cand = (cand_u - 2 ** 15).astype(I16)
            cnt = base + count16(src_ref, lambda v: v >= cand)
            return jnp.where(cnt >= kf, cand_u, u)
        return lax.fori_loop(0, 16, bit, jnp.zeros((1, tile), I32))

    thr_hi = search16(hi_scr, 0.0) - 2 ** 15
    thr_hi16 = thr_hi.astype(I16)
    above = count16(hi_scr, lambda v: v > thr_hi16)

    def mask_low(j, carry):
        k0 = pl.multiple_of(j * tile, tile)
        rows = pl.ds(k0, tile)
        lo_scr[rows, :] = jnp.where(hi_scr[rows, :] == thr_hi16, lo_scr[rows, :], jnp.int16(-2 ** 15))
        return carry

    lax.fori_loop(0, nblk, mask_low, 0)
    thr = jnp.left_shift(thr_hi, 16) + search16(lo_scr, above)

    thr_b = thr
    c_gt = count(lambda key, idx: key > thr_b)
    c_ge = count(lambda key, idx: key >= thr_b)
    tied = (c_ge > kf) & (thr > _VALID_KEY)
    need = kf - c_gt
    jstar_scr[...] = jnp.full((1, tile), seq, I32)

    @pl.when(jnp.max(jnp.where(tied, 1.0, 0.0)) > 0.0)
    def _():
        nbits = max(1, int(math.ceil(math.log2(seq))))

        def index_bit(p, lo):
            cand = lo | jnp.left_shift(jnp.int32(1), nbits - 1 - p)
            cnt = count(lambda key, idx: (key == thr_b) & (idx < cand))
            return jnp.where(cnt < need, cand, lo)

        lo = lax.fori_loop(0, nbits, index_bit, jnp.zeros((1, tile), I32))
        jstar_scr[...] = jnp.where(tied, lo, seq)

    jstar = jstar_scr[...]
    thr2 = jnp.maximum(thr, _VALID_KEY + 1)

    _store_query_maps(qa_scr, _scaled_transpose(qc_ref[0], 1.0), H_C, DH_C)
    m_scr[...] = jnp.full(m_scr.shape, NEG_INF, F32)
    l_scr[...] = jnp.zeros(l_scr.shape, F32)
    acc_scr[...] = jnp.zeros(acc_scr.shape, F32)

    def attend(blocks):
        ctx = []
        for j, diag in blocks:
            k0 = pl.multiple_of(j * tile, tile)
            key = key_scr[pl.ds(k0, tile), :]
            sel = (key > thr2) | ((key == thr2) & (kpos + k0 <= jstar))
            ctx.append((diag, sel, kc_ref[0, pl.ds(k0, tile), :], vct_ref[0, j], ((i - j) * tile).astype(F32)))

        def scores(idx):
            (diag, sel, kb, _, _), h = ctx[idx // H_C], idx % H_C
            pair = h // 2
            keys = jnp.concatenate([kb[:, pair * LANES:(pair + 1) * LANES], kx_scr[h]], axis=1)
            s = _dot_row_chunks(keys, qa_scr[h])
            if diag:
                s = s + dneg_scr[h]
            return jnp.where(sel, s, MASKED)

        def probs(idx, t_ref, tmax_ref, p_ref, a_ref):
            (diag, _, _, _, off), h = ctx[idx // H_C], idx % H_C
            shift = 0.0 if diag else (slopes[h] * LOG2E) * off
            _softmax_probs(shift, t_ref, tmax_ref, p_ref, a_ref, m_scr.at[h], l_scr.at[h])

        def accumulate(idx, p_ref, a_ref):
            vtb, h = ctx[idx // H_C][3], idx % H_C
            _accumulate(vtb[h * DH_C:(h + 1) * DH_C, :], p_ref, a_ref, acc_scr.at[h])

        _pipelined_maps(H_C * len(blocks), scores, probs, accumulate, stage)

    _sweep_key_tiles(i, attend)
    outs =[(acc_scr[h] / l_scr[h]).T for h in range(H_C)]
    o_ref[0] = jnp.concatenate(outs, axis=-1).astype(BF16)


def _dsa_attention(proj, vt, kiw, slopes):
    b, s, _ = proj.shape
    tile = min(ATT_TILE, s)
    nkb = s // tile
    w = H_C * DH_C
    wa = H_A * DV_A
    topk = min(TOPK_MAX, s // 4)
    return pl.pallas_call(
        functools.partial(_dsa_body, tile=tile, topk=topk, slopes=slopes, seq=s),
        grid=(b, nkb),
        in_specs=[pl.BlockSpec((1, tile, w), lambda bi, i: (bi, i, 8)),
                  pl.BlockSpec((1, s, w), lambda bi, i: (bi, 0, 9)),
                  pl.BlockSpec((1, nkb, w, tile), lambda bi, i: (bi, 0, wa // w, 0)),
                  pl.BlockSpec((1, tile, H_IDX * D_IDX), lambda bi, i: (bi, i, 2)),
                  pl.BlockSpec((1, s, LANES), lambda bi, i: (bi, 0, 0)),
                  pl.BlockSpec((1, tile, LANES), lambda bi, i: (bi, i, 0))],
        out_specs=pl.BlockSpec((1, tile, w), lambda bi, i: (bi, i, 0)),
        out_shape=jax.ShapeDtypeStruct((b, s, w), BF16),
        scratch_shapes=[pltpu.VMEM((H_C, 2 * LANES, tile), BF16),
                        pltpu.VMEM((H_C, tile, LANES), BF16),
                        pltpu.VMEM((H_C, tile, tile), F32),
                        pltpu.VMEM((H_IDX * D_IDX, tile), BF16),
                        pltpu.VMEM((s, tile), I32),
                        pltpu.VMEM((s, tile), I16),
                        pltpu.VMEM((s, tile), I16),
                        pltpu.VMEM((1, tile), I32),
                        pltpu.VMEM((H_C, 1, tile), F32),
                        pltpu.VMEM((H_C, 1, tile), F32),
                        pltpu.VMEM((H_C, DH_C, tile), F32)] + _stage_scratch(tile),
        compiler_params=_params("arbitrary", "arbitrary"),
    )(proj, proj, vt, proj, kiw, kiw)


def _xattn_body(x_ref, res_ref, wq_ref, k_ref, v_ref, wo_ref, g_ref, b_ref, of_ref, ob_ref, *, dh, alpha):
    q_all = _dot(x_ref[0], wq_ref[...]).astype(BF16)
    outs = []
    for h in range(H_X):
        c = h * dh
        q = q_all[:, c:c + dh] * dh ** -0.5
        s = _dot_nt(q, k_ref[0, :, c:c + dh])
        p = jnp.exp(s - jnp.max(s, axis=-1, keepdims=True))
        l = jnp.sum(p, axis=-1, keepdims=True)
        outs.append((_dot(p.astype(BF16), v_ref[0, :, c:c + dh]) / l).astype(BF16))
    y = _dot(jnp.concatenate(outs, axis=-1), wo_ref[...])
    out = _layer_norm_rows(alpha * res_ref[0] + y, g_ref[...], b_ref[...])
    of_ref[0] = out
    ob_ref[0] = out.astype(BF16)


def _xattn_ln(x, res, wq, kv, wo, g, b, alpha, tq=512):
    bsz, s, d = x.shape
    n_mem = kv.shape[1]
    tq = min(tq, s)
    tile = pl.BlockSpec((1, tq, d), lambda bi, i: (bi, i, 0))
    whole = lambda shape: pl.BlockSpec(shape, lambda bi, i: (0,) * len(shape))
    mem = lambda col: pl.BlockSpec((1, n_mem, d), lambda bi, i: (bi, 0, col))
    return pl.pallas_call(
        functools.partial(_xattn_body, dh=d // H_X, alpha=alpha),
        grid=(bsz, s // tq),
        in_specs=[tile, tile, whole((d, d)), mem(0), mem(1), whole((d, d)), whole((1, d)), whole((1, d))],
        out_specs=[tile, tile],
        out_shape=[jax.ShapeDtypeStruct((bsz, s, d), F32), jax.ShapeDtypeStruct((bsz, s, d), BF16)],
        compiler_params=_params("parallel", "parallel"),
    )(x, res, wq, kv, kv, wo, g.reshape(1, d), b.reshape(1, d))


def _router_body(x_ref, r_ref, g_ref):
    logits = _dot(x_ref[...], r_ref[...])
    lane = lax.broadcasted_iota(I32, logits.shape, 1).astype(F32)
    big = float(LANES)
    lg = jnp.where(lane < N_EXPERTS, logits, -jnp.inf)
    m1 = jnp.max(lg, axis=-1, keepdims=True)
    i1 = jnp.min(jnp.where(lg == m1, lane, big), axis=-1, keepdims=True)
    lg2 = jnp.where(lane == i1, -jnp.inf, lg)
    m2 = jnp.max(lg2, axis=-1, keepdims=True)
    i2 = jnp.min(jnp.where(lg2 == m2, lane, big), axis=-1, keepdims=True)
    e = jnp.exp(m2 - m1)
    den = 1.0 + e
    g_ref[...] = jnp.where(lane == i1, 1.0 / den, 0.0) + jnp.where(lane == i2, e / den, 0.0)


def _router_gates(x, router, tm=1024):
    m, d = x.shape
    tm = min(tm, m)
    rpad = jnp.zeros((d, LANES), BF16).at[:, :N_EXPERTS].set(router.astype(BF16))
    return pl.pallas_call(
        _router_body,
        grid=(m // tm,),
        in_specs=[pl.BlockSpec((tm, d), lambda i: (i, 0)),
                  pl.BlockSpec((d, LANES), lambda i: (0, 0))],
        out_specs=pl.BlockSpec((tm, LANES), lambda i: (i, 0)),
        out_shape=jax.ShapeDtypeStruct((m, LANES), F32),
        compiler_params=_params("parallel"),
    )(x, rpad)


MOE_BLOCK = 896
MOE_CHUNK = 128
MOE_GROUP = 2
MOE_FF_TILE = 1792


def _moe_body(x_ref, gate_ref, tri_ref, w1_ref, w3_ref, w2_ref, o_ref,
              xs_scr, acc_scr, rcol_scr, rrow_scr, cnt_smem, *, n_tokens):
    e = pl.program_id(1)
    f = pl.program_id(2)
    nf = pl.num_programs(2)
    tb = x_ref.shape[0]
    ch = MOE_CHUNK
    gw = MOE_GROUP * ch

    tok0 = pl.program_id(0) * tb
    valid_col = lax.broadcasted_iota(I32, (tb, 1), 0) + tok0 < n_tokens

    @pl.when((e == 0) & (f == 0))
    def _():
        o_ref[...] = jnp.zeros(o_ref.shape, F32)
        member = jnp.where((gate_ref[...] > 0.0) & valid_col, 1.0, 0.0)
        rank = _dot(tri_ref[...], member.astype(BF16))
        rcol = jnp.where(member > 0.0, rank, -1.0)
        rcol_scr[...] = rcol
        rrow_scr[...] = rcol.T
        counts = jnp.sum(member, axis=0, keepdims=True)
        for ee in range(N_EXPERTS):
            cnt_smem[ee] = counts[0, ee].astype(I32)

    count = cnt_smem[e]
    single = count <= gw
    nchunk = jnp.where(single, MOE_GROUP, (count + (ch - 1)) // ch)

    @pl.when(f == 0)
    def _():
        acc_scr[...] = jnp.zeros(acc_scr.shape, F32)
        sub = lax.broadcasted_iota(I32, (SUBLANES, tb), 0)
        r_row = jnp.sum(jnp.where(sub == e, rrow_scr[0:SUBLANES, :], 0.0), axis=0, keepdims=True)
        xz = jnp.where(valid_col, x_ref[...], jnp.zeros((), BF16))

        def gather_rows(r0, rows):
            row_id = lax.broadcasted_iota(I32, (rows, tb), 0).astype(F32) + jnp.asarray(r0, F32)
            sel = jnp.where(row_id == r_row, 1.0, 0.0).astype(BF16)
            xs_scr[pl.ds(r0, rows), :] = _dot(sel, xz).astype(BF16)

        @pl.when(single)
        def _():
            gather_rows(0, gw)

        @pl.when(jnp.logical_not(single))
        def _():
            def gather(c, carry):
                gather_rows(pl.multiple_of(c * ch, ch), ch)
                return carry
            lax.fori_loop(0, nchunk, gather, 0)

    def ffn_rows(r0, rows):
        xs = xs_scr[pl.ds(r0, rows), :]
        a = _dot(xs, w1_ref[0])
        b = _dot(xs, w3_ref[0])
        acc_scr[pl.ds(r0, rows), :] += _dot((_silu(a) * b).astype(BF16), w2_ref[0])

    @pl.when(single)
    def _():
        ffn_rows(0, gw)

    @pl.when(jnp.logical_not(single))
    def _():
        def ffn(c, carry):
            ffn_rows(pl.multiple_of(c * ch, ch), ch)
            return carry
        lax.fori_loop(0, nchunk, ffn, 0)

    @pl.when(f == nf - 1)
    def _():
        pick = lax.broadcasted_iota(I32, (tb, LANES), 1) == e
        r_col = jnp.sum(jnp.where(pick, rcol_scr[...], 0.0), axis=1, keepdims=True)
        g_col = jnp.sum(jnp.where(pick & valid_col, gate_ref[...], 0.0), axis=1, keepdims=True)
        col_id = lax.broadcasted_iota(I32, (tb, gw), 1).astype(F32)

        def combine(gi, carry):
            r0 = pl.multiple_of(gi * gw, gw)
            selt = jnp.where(col_id + r0.astype(F32) == r_col, 1.0, 0.0).astype(BF16)
            y = acc_scr[pl.ds(r0, gw), :]
            y_hi = y.astype(BF16)
            y_lo = (y - y_hi.astype(F32)).astype(BF16)
            o_ref[...] += g_col * (_dot(selt, y_hi) + _dot(selt, y_lo))
            return carry

        lax.fori_loop(0, (nchunk + (MOE_GROUP - 1)) // MOE_GROUP, combine, 0)


def _moe_experts(x, gates, w1, w3, w2):
    m, d = x.shape
    ne, _, ff = w1.shape
    tb = min(MOE_BLOCK, m)
    tf = MOE_FF_TILE if ff % MOE_FF_TILE == 0 else ff
    gw = MOE_GROUP * MOE_CHUNK
    cap = pl.cdiv(tb, gw) * gw
    assert tb % LANES == 0 and ne <= SUBLANES
    tri = jnp.asarray(np.tril(np.ones((tb, tb), np.float32), -1), BF16)
    row = lambda i, e, f: (i, 0)
    return pl.pallas_call(
        functools.partial(_moe_body, n_tokens=m),
        grid=(pl.cdiv(m, tb), ne, ff // tf),
        in_specs=[pl.BlockSpec((tb, d), row),
                  pl.BlockSpec((tb, LANES), row),
                  pl.BlockSpec((tb, tb), lambda i, e, f: (0, 0)),
                  pl.BlockSpec((1, d, tf), lambda i, e, f: (e, 0, f)),
                  pl.BlockSpec((1, d, tf), lambda i, e, f: (e, 0, f)),
                  pl.BlockSpec((1, tf, d), lambda i, e, f: (e, f, 0))],
        out_specs=pl.BlockSpec((tb, d), row),
        out_shape=jax.ShapeDtypeStruct((m, d), F32),
        scratch_shapes=[pltpu.VMEM((cap, d), BF16),
                        pltpu.VMEM((cap, d), F32),
                        pltpu.VMEM((tb, LANES), F32),
                        pltpu.VMEM((LANES, tb), F32),
                        pltpu.SMEM((N_EXPERTS,), I32)],
        compiler_params=_params("parallel", "arbitrary", "arbitrary"),
    )(x, gates, tri, w1, w3, w2)


def _res_ln_body(res_ref, y_ref, g_ref, b_ref, of_ref, ob_ref, *, alpha):
    out = _layer_norm_rows(alpha * res_ref[...] + y_ref[...], g_ref[...], b_ref[...])
    of_ref[...] = out
    ob_ref[...] = out.astype(BF16)


def _residual_ln(res, y, g, b, alpha, tm=512):
    m, d = res.shape
    tm = min(tm, m)
    row = pl.BlockSpec((tm, d), lambda i: (i, 0))
    vec = pl.BlockSpec((1, d), lambda i: (0, 0))
    return pl.pallas_call(
        functools.partial(_res_ln_body, alpha=alpha),
        grid=(m // tm,),
        in_specs=[row, row, vec, vec],
        out_specs=[row, row],
        out_shape=[jax.ShapeDtypeStruct((m, d), F32), jax.ShapeDtypeStruct((m, d), BF16)],
        compiler_params=_params("parallel"),
    )(res, y, g.reshape(1, d), b.reshape(1, d))


def _band_bias(table):
    band = (LEFT_CHUNKS + 1) * CHUNK
    left = LEFT_CHUNKS * CHUNK
    rev = table.astype(F32)[:, ::-1]
    pad = left - REL_MAX + CHUNK - 1
    n_rev = band - 1 - (left - REL_MAX) + 1
    assert n_rev <= 2 * REL_MAX + 1
    ext = jnp.concatenate([jnp.broadcast_to(rev[:, :1], (rev.shape[0], pad)), rev[:, :n_rev]], axis=1)
    n_ext = ext.shape[1]
    period = jnp.concatenate([ext[:, CHUNK - 1:], jnp.zeros((ext.shape[0], 1), F32), ext[:, :CHUNK - 1]], axis=1)
    skew = jnp.tile(period, (1, CHUNK))[:, :CHUNK * n_ext].reshape(-1, CHUNK, n_ext)
    return skew[:, :, :band]


def kernel(x, mem, w_in, lam_q1, lam_k1, lam_q2, lam_k2, diff_norm_g, rel_bias, w_out,
           ln1_g, ln1_b, xattn_wq, xattn_wk, xattn_wv, xattn_wo, ln2_g, ln2_b,
           ffn_w1, ffn_w3, ffn_w2, moe_router, moe_w1, moe_w3, moe_w2, ln3_g, ln3_b):
    b, s, d = x.shape
    depth = w_in.shape[0]
    t = b * s
    alpha = (2.0 * depth) ** 0.25
    slopes_a, slopes_c = _alibi_slopes()
    n_main = 3 * H_A * DV_A + 3 * H_B * DH_B + 3 * H_C * DH_C + H_IDX * D_IDX
    n_tail = D_IDX + H_IDX
    wa, wb, wc = H_A * DV_A, H_B * DH_B, H_C * DH_C
    left = LEFT_CHUNKS * CHUNK

    h = x.reshape(t, d)
    hb = h.astype(BF16)
    memb = mem.reshape(-1, d).astype(BF16)
    for l in range(depth):
        wl = w_in[l]
        o_b, o_c, o_i = 3 * wa, 3 * wa + 3 * wb, 3 * wa + 3 * wb + 3 * wc
        w_main = jnp.concatenate(
            [wl[:, :wa] * (DQK_A ** -0.5 * LOG2E), wl[:, wa:2 * wa], wl[:, o_i:o_i + H_IDX * D_IDX],
             wl[:, o_b:o_b + wb] * (DH_B ** -0.5 * LOG2E), wl[:, o_b + wb:o_b + 2 * wb],
             wl[:, o_c:o_c + wc] * (DH_C ** -0.5 * LOG2E), wl[:, o_c + wc:o_c + 2 * wc]], axis=1).astype(BF16)
        proj = _matmul(hb, w_main, BF16, tn=w_main.shape[1] // 2).reshape(b, s, -1)
        w_side = jnp.concatenate(
            [wl[:, 2 * wa:3 * wa], wl[:, o_c + 2 * wc:o_c + 3 * wc], wl[:, o_b + 2 * wb:o_b + 3 * wb],
             wl[:, n_main:], jnp.zeros((d, LANES - n_tail), F32)], axis=1).astype(BF16)
        tile = min(ATT_TILE, s)
        vt, vbt, kiw = _side_projections(hb, w_side, wa + wc, tile, wb, LANES)
        vt = vt.reshape(b, s // tile, wa + wc, tile)
        kiw = kiw.reshape(b, s, LANES)
        lam_init = 0.8 - 0.6 * math.exp(-0.3 * l)
        ya = _diff_attention(proj, vt, lam_q1[l], lam_k1[l], lam_q2[l], lam_k2[l], diff_norm_g[l],
                             slopes_a, lam_init)
        kpad = jnp.pad(proj[:, :, 3 * wa + wb:3 * wa + 2 * wb], ((0, 0), (left, 0), (0, 0)))
        vbt = jnp.pad(vbt.reshape(b, s // LANES, wb, LANES), ((0, 0), (left // LANES, 0), (0, 0), (0, 0)))
        yb = _band_attention(proj, kpad, vbt, _band_bias_table(rel_bias[l]))
        yc = _dsa_attention(proj, vt, kiw, slopes_c)
        wo = w_out[l].astype(BF16)
        h, hb = _matmul_ln(
            [ya.reshape(t, wa), yb.reshape(t, wb), yc.reshape(t, -1)],
            [wo[:wa], wo[wa:wa + wb], wo[wa + wb:]],
            h, ln1_g[l], ln1_b[l], alpha)
        w_kv = jnp.concatenate([xattn_wk[l], xattn_wv[l]], axis=1).astype(BF16)
        kv = _matmul(memb, w_kv, BF16, tn=d).reshape(b, -1, 2 * d)
        h, hb = _xattn_ln(hb.reshape(b, s, d), h.reshape(b, s, d), xattn_wq[l].astype(BF16), kv,
                          xattn_wo[l].astype(BF16), ln2_g[l], ln2_b[l], alpha)
        h, hb = h.reshape(t, d), hb.reshape(t, d)
        if l % 2 == 0:
            h, hb = _ffn_ln(hb, ffn_w1[l // 2].astype(BF16), ffn_w3[l // 2].astype(BF16),
                            ffn_w2[l // 2].astype(BF16), h, ln3_g[l], ln3_b[l], alpha)
        else:
            gates = _router_gates(hb, moe_router[l // 2])
            y = _moe_experts(hb, gates, moe_w1[l // 2].astype(BF16), moe_w3[l // 2].astype(BF16),
                             moe_w2[l // 2].astype(BF16))
            h, hb = _residual_ln(h, y, ln3_g[l], ln3_b[l], alpha)
    return h.reshape(b, s, d)
```

```python
import functools
import math

import numpy as np
import jax
import jax.numpy as jnp
from jax import lax
from jax.experimental import pallas as pl
from jax.experimental.pallas import tpu as pltpu

F32 = jnp.float32
BF16 = jnp.bfloat16
I32 = jnp.int32
I16 = jnp.int16

CHUNK = 64
H_A, DQK_A = 4, 64
DV_A = 2 * DQK_A
H_B, DH_B = 4, 64
LEFT_CHUNKS = 8
REL_MAX = 128
H_C, DH_C = 4, 64
H_IDX, D_IDX = 8, 64
TOPK_MAX = 256
H_X = 4
N_EXPERTS = 8
LN_EPS = 1e-5
RMS_EPS = 1e-6
NEG_INF = -1e30
MASKED = -2e30

LANES = 128
SUBLANES = 8
VMEM_LIMIT = 56 * 1024 * 1024
ATT_TILE = 256

_NT = (((1,), (1,)), ((), ()))


def _alibi_slopes():
    n = H_A + H_C
    s = (2.0 ** (-8.0 * np.arange(1, n + 1) / n)).astype(np.float32)
    return [float(v) for v in s[0::2]], [float(v) for v in s[1::2]]


def _params(*sem):
    return pltpu.CompilerParams(dimension_semantics=sem, vmem_limit_bytes=VMEM_LIMIT)


def _dot(a, b):
    return jnp.dot(a, b, preferred_element_type=F32)


SCORE_ROW_CHUNK = 128


def _dot_row_chunks(a, b):
    rows = a.shape[0]
    step = min(SCORE_ROW_CHUNK, rows)
    return jnp.concatenate([_dot(a[r:r + step], b) for r in range(0, rows, step)], axis=0)


def _dot_nt(a, b):
    return lax.dot_general(a, b, _NT, preferred_element_type=F32)


def _layer_norm_rows(y, g, b):
    mu = jnp.mean(y, axis=-1, keepdims=True)
    d = y - mu
    var = jnp.mean(d * d, axis=-1, keepdims=True)
    return d * lax.rsqrt(var + LN_EPS) * g + b


def _silu(a):
    return a / (1.0 + jnp.exp(-a))


def _mm_body(x_ref, w_ref, o_ref):
    o_ref[...] = _dot(x_ref[...], w_ref[...]).astype(o_ref.dtype)


def _matmul(x, w, out_dtype, tm=1024, tn=512):
    m, k = x.shape
    n = w.shape[1]
    tm, tn = min(tm, m), min(tn, n)
    return pl.pallas_call(
        _mm_body,
        grid=(m // tm, n // tn),
        in_specs=[pl.BlockSpec((tm, k), lambda i, j: (i, 0)),
                  pl.BlockSpec((k, tn), lambda i, j: (0, j))],
        out_specs=pl.BlockSpec((tm, tn), lambda i, j: (i, j)),
        out_shape=jax.ShapeDtypeStruct((m, n), out_dtype),
        compiler_params=_params("parallel", "parallel"),
    )(x, w)


def _side_proj_body(x_ref, w_ref, vt_ref, vbt_ref, tail_ref):
    y = _dot(x_ref[...], w_ref[...])
    n_v, n_vb = vt_ref.shape[1], vbt_ref.shape[1]
    tile, tile_b = vt_ref.shape[2], vbt_ref.shape[2]
    for r in range(vt_ref.shape[0]):
        vt_ref[r] = y[r * tile:(r + 1) * tile, :n_v].T.astype(BF16)
    for r in range(vbt_ref.shape[0]):
        vbt_ref[r] = y[r * tile_b:(r + 1) * tile_b, n_v:n_v + n_vb].T.astype(BF16)
    tail_ref[...] = y[:, n_v + n_vb:]


def _side_projections(x, w, n_v, tile, n_vb, tile_b, tm=1024):
    m, k = x.shape
    n = w.shape[1]
    n_tail = n - n_v - n_vb
    tm = min(tm, m)
    return pl.pallas_call(
        _side_proj_body,
        grid=(m // tm,),
        in_specs=[pl.BlockSpec((tm, k), lambda i: (i, 0)),
                  pl.BlockSpec((k, n), lambda i: (0, 0))],
        out_specs=[pl.BlockSpec((tm // tile, n_v, tile), lambda i: (i, 0, 0)),
                   pl.BlockSpec((tm // tile_b, n_vb, tile_b), lambda i: (i, 0, 0)),
                   pl.BlockSpec((tm, n_tail), lambda i: (i, 0))],
        out_shape=[jax.ShapeDtypeStruct((m // tile, n_v, tile), BF16),
                   jax.ShapeDtypeStruct((m // tile_b, n_vb, tile_b), BF16),
                   jax.ShapeDtypeStruct((m, n_tail), F32)],
        compiler_params=_params("parallel"),
    )(x, w)


def _mm_ln_body(*refs, n_in, alpha):
    xs, ws = refs[:n_in], refs[n_in:2 * n_in]
    res_ref, g_ref, b_ref, of_ref, ob_ref = refs[2 * n_in:]
    acc = _dot(xs[0][...], ws[0][...])
    for x_ref, w_ref in zip(xs[1:], ws[1:]):
        acc = acc + _dot(x_ref[...], w_ref[...])
    out = _layer_norm_rows(alpha * res_ref[...] + acc, g_ref[...], b_ref[...])
    of_ref[...] = out
    ob_ref[...] = out.astype(BF16)


def _matmul_ln(xs, ws, res, g, b, alpha):
    m, d = res.shape
    k_total = sum(x.shape[1] for x in xs)
    tm = min(1024 if k_total <= 1024 else 512, m)
    n_in = len(xs)
    in_specs = ([pl.BlockSpec((tm, x.shape[1]), lambda i: (i, 0)) for x in xs]
                + [pl.BlockSpec(w.shape, lambda i: (0, 0)) for w in ws]
                + [pl.BlockSpec((tm, d), lambda i: (i, 0)),
                   pl.BlockSpec((1, d), lambda i: (0, 0)),
                   pl.BlockSpec((1, d), lambda i: (0, 0))])
    return pl.pallas_call(
        functools.partial(_mm_ln_body, n_in=n_in, alpha=alpha),
        grid=(m // tm,),
        in_specs=in_specs,
        out_specs=[pl.BlockSpec((tm, d), lambda i: (i, 0)),
                   pl.BlockSpec((tm, d), lambda i: (i, 0))],
        out_shape=[jax.ShapeDtypeStruct((m, d), F32), jax.ShapeDtypeStruct((m, d), BF16)],
        compiler_params=_params("parallel"),
    )(*xs, *ws, res, g.reshape(1, d), b.reshape(1, d))


def _ffn_ln_body(x_ref, w1_ref, w3_ref, w2_ref, res_ref, g_ref, b_ref, of_ref, ob_ref, acc_ref, *, alpha):
    f = pl.program_id(1)
    x = x_ref[...]
    a = _dot(x, w1_ref[...])
    b = _dot(x, w3_ref[...])
    y = _dot((_silu(a) * b).astype(BF16), w2_ref[...])

    @pl.when(f == 0)
    def _():
        acc_ref[...] = y

    @pl.when(f > 0)
    def _():
        acc_ref[...] += y

    @pl.when(f == pl.num_programs(1) - 1)
    def _():
        out = _layer_norm_rows(alpha * res_ref[...] + acc_ref[...], g_ref[...], b_ref[...])
        of_ref[...] = out
        ob_ref[...] = out.astype(BF16)


def _ffn_ln(x, w1, w3, w2, res, g, b, alpha, tm=512, tf=1408):
    m, d = x.shape
    ff = w1.shape[1]
    tm = min(tm, m)
    if ff % tf:
        tf = ff
    row = pl.BlockSpec((tm, d), lambda i, f: (i, 0))
    vec = pl.BlockSpec((1, d), lambda i, f: (0, 0))
    return pl.pallas_call(
        functools.partial(_ffn_ln_body, alpha=alpha),
        grid=(m // tm, ff // tf),
        in_specs=[row,
                  pl.BlockSpec((d, tf), lambda i, f: (0, f)),
                  pl.BlockSpec((d, tf), lambda i, f: (0, f)),
                  pl.BlockSpec((tf, d), lambda i, f: (f, 0)),
                  row, vec, vec],
        out_specs=[row, row],
        out_shape=[jax.ShapeDtypeStruct((m, d), F32), jax.ShapeDtypeStruct((m, d), BF16)],
        scratch_shapes=[pltpu.VMEM((tm, d), F32)],
        compiler_params=_params("parallel", "arbitrary"),
    )(x, w1, w3, w2, res, g.reshape(1, d), b.reshape(1, d))


def _key_query_iotas(tile):
    kpos = lax.broadcasted_iota(I32, (tile, tile), 0)
    qpos = lax.broadcasted_iota(I32, (tile, tile), 1)
    return kpos, qpos


def _scaled_transpose(x, scale):
    return (x.astype(F32) * scale).T.astype(BF16)


LOG2E = math.log2(math.e)
ALIBI_PARTS = 3
ALIBI_ROWS = 16


def _bf16_parts(x):
    parts, r = [], np.float32(x)
    for _ in range(ALIBI_PARTS):
        p = np.float32(r.astype(jnp.bfloat16))
        parts.append(float(p))
        r = np.float32(r - p)
    return parts


def _alibi_key_cols(tile, parts):
    lane = lax.broadcasted_iota(I32, (tile, LANES), 1)
    kr = lax.broadcasted_iota(I32, (tile, LANES), 0).astype(F32)
    v = jnp.where((lane >= ALIBI_PARTS) & (lane < 2 * ALIBI_PARTS), kr, 0.0)
    for n, c in enumerate(parts):
        v = jnp.where(lane == n, c, v)
    return v.astype(BF16)


def _alibi_query_rows(tile, parts):
    row = lax.broadcasted_iota(I32, (ALIBI_ROWS, tile), 0)
    qr = lax.broadcasted_iota(I32, (ALIBI_ROWS, tile), 1).astype(F32)
    v = jnp.where(row < ALIBI_PARTS, -qr, 0.0)
    for n, c in enumerate(parts):
        v = jnp.where(row == ALIBI_PARTS + n, c, v)
    return v.astype(BF16)


def _init_alibi(slopes, dneg_scr, kx_scr, qa_scr, tile, maps_per_head):
    kpos, qpos = _key_query_iotas(tile)
    rel = (qpos - kpos).astype(F32)
    qa_scr[...] = jnp.zeros(qa_scr.shape, BF16)
    for h, slope in enumerate(slopes):
        parts = _bf16_parts(slope * LOG2E)
        dneg_scr[h] = 2.0 * jnp.minimum(np.float32(slope * LOG2E) * rel, 0.0)
        kx_scr[h] = _alibi_key_cols(tile, parts)
        for mm in range(maps_per_head):
            qa_scr[h * maps_per_head + mm, LANES:LANES + ALIBI_ROWS, :] = _alibi_query_rows(tile, parts)


def _store_query_maps(qa_scr, qt, n_maps, dh):
    for idx in range(n_maps):
        r0 = (idx % 2) * dh
        qa_scr[idx, r0:r0 + dh, :] = qt[idx * dh:(idx + 1) * dh, :]


def _softmax_probs(c, t_ref, tmax_ref, p_ref, a_ref, m_ref, l_ref):
    m_old = m_ref[...]
    m_new = jnp.maximum(m_old, tmax_ref[...] - c)
    p = jnp.exp2(t_ref[...] - (m_new + c))
    a = jnp.exp2(m_old - m_new)
    l_ref[...] = a * l_ref[...] + jnp.sum(p, axis=0, keepdims=True)
    m_ref[...] = m_new
    a_ref[...] = a
    p_ref[...] = p.astype(BF16)


def _accumulate(vt, p_ref, a_ref, acc_ref):
    acc_ref[...] = a_ref[...] * acc_ref[...] + _dot(vt, p_ref[...])


SCORE_LEAD, PROB_LEAD = 4, 2
N_T_SLOTS, N_P_SLOTS = SCORE_LEAD - PROB_LEAD + 2, PROB_LEAD + 1


def _stage_scratch(tile, keys=None):
    keys = tile if keys is None else keys
    return ([pltpu.VMEM((keys, tile), F32)] * N_T_SLOTS + [pltpu.VMEM((1, tile), F32)] * N_T_SLOTS
            + [pltpu.VMEM((keys, tile), BF16)] * N_P_SLOTS + [pltpu.VMEM((1, tile), F32)] * N_P_SLOTS)


def _pipelined_maps(n_maps, scores, probs, accumulate, stage):
    nt, npb = N_T_SLOTS, N_P_SLOTS
    t_slots, x_slots = stage[:nt], stage[nt:2 * nt]
    p_slots, a_slots = stage[2 * nt:2 * nt + npb], stage[2 * nt + npb:]

    def stage_scores(k):
        s = scores(k)
        t_slots[k % nt][...] = s
        x_slots[k % nt][...] = jnp.max(s, axis=0, keepdims=True)

    def stage_probs(k):
        probs(k, t_slots[k % nt], x_slots[k % nt], p_slots[k % npb], a_slots[k % npb])

    for step in range(-SCORE_LEAD, n_maps):
        if 0 <= step + SCORE_LEAD < n_maps:
            stage_scores(step + SCORE_LEAD)
        if 0 <= step + PROB_LEAD < n_maps:
            stage_probs(step + PROB_LEAD)
        if step >= 0:
            accumulate(step, p_slots[step % npb], a_slots[step % npb])


SWEEP_GROUP = 4


def _sweep_key_tiles(i, sweep):
    def body(g, carry):
        sweep([(SWEEP_GROUP * g + n, False) for n in range(SWEEP_GROUP)])
        return carry

    ngroups = i // SWEEP_GROUP
    lax.fori_loop(0, ngroups, body, 0)
    base = ngroups * SWEEP_GROUP
    for rem in range(SWEEP_GROUP):
        @pl.when(i - base == rem)
        def _():
            sweep([(base + n, False) for n in range(rem)] + [(i, True)])


def _diff_body(q_ref, k_ref, vt_ref, lq1_ref, lk1_ref, lq2_ref, lk2_ref, gain_ref, o_ref,
               qa_scr, kx_scr, dneg_scr, m_scr, l_scr, acc_scr, *stage,
               tile, slopes, lam_init):
    i = pl.program_id(1)
    kpos, qpos = _key_query_iotas(tile)

    @pl.when((pl.program_id(0) == 0) & (i == 0))
    def _():
        _init_alibi(slopes, dneg_scr, kx_scr, qa_scr, tile, 2)

    _store_query_maps(qa_scr, _scaled_transpose(q_ref[0], 1.0), 2 * H_A, DQK_A)
    m_scr[...] = jnp.full(m_scr.shape, NEG_INF, F32)
    l_scr[...] = jnp.zeros(l_scr.shape, F32)
    acc_scr[...] = jnp.zeros(acc_scr.shape, F32)

    n_maps = 2 * H_A

    def sweep(blocks):
        ctx = []
        for j, diag in blocks:
            k0 = pl.multiple_of(j * tile, tile)
            ctx.append((diag, k_ref[0, pl.ds(k0, tile), :], vt_ref[0, j], ((i - j) * tile).astype(F32)))

        def scores(n):
            (diag, kb, _, _), idx = ctx[n // n_maps], n % n_maps
            h, pair = idx // 2, idx // 2
            keys = jnp.concatenate([kb[:, pair * LANES:(pair + 1) * LANES], kx_scr[h]], axis=1)
            s = _dot_row_chunks(keys, qa_scr[idx])
            if diag:
                allowed = (kpos // CHUNK) <= (qpos // CHUNK)
                return jnp.where(allowed, s + dneg_scr[h], MASKED)
            return s

        def probs(n, t_ref, tmax_ref, p_ref, a_ref):
            (diag, _, _, off), idx = ctx[n // n_maps], n % n_maps
            shift = 0.0 if diag else (slopes[idx // 2] * LOG2E) * off
            _softmax_probs(shift, t_ref, tmax_ref, p_ref, a_ref, m_scr.at[idx], l_scr.at[idx])

        def accumulate(n, p_ref, a_ref):
            vtb, idx = ctx[n // n_maps][2], n % n_maps
            h = idx // 2
            _accumulate(vtb[h * DV_A:(h + 1) * DV_A, :], p_ref, a_ref, acc_scr.at[idx])

        _pipelined_maps(n_maps * len(blocks), scores, probs, accumulate, stage)

    _sweep_key_tiles(i, sweep)

    lam = (jnp.exp(jnp.sum(lq1_ref[...] * lk1_ref[...], axis=-1, keepdims=True))
           - jnp.exp(jnp.sum(lq2_ref[...] * lk2_ref[...], axis=-1, keepdims=True)) + lam_init)
    gain = gain_ref[...] * (1.0 - lam_init)
    outs = []
    for h in range(H_A):
        o = acc_scr[2 * h] / l_scr[2 * h] - lam * (acc_scr[2 * h + 1] / l_scr[2 * h + 1])
        o = o * lax.rsqrt(jnp.mean(o * o, axis=0, keepdims=True) + RMS_EPS)
        outs.append((o * gain).T)
    o_ref[0] = jnp.concatenate(outs, axis=-1).astype(BF16)


def _diff_attention(proj, vt, lq1, lk1, lq2, lk2, gain, slopes, lam_init):
    b, s, _ = proj.shape
    tile = min(ATT_TILE, s)
    nkb = s // tile
    w = H_A * DV_A
    vec = lambda a: a.reshape(1, -1).astype(F32)
    small = lambda n: pl.BlockSpec((1, n), lambda bi, i: (0, 0))
    return pl.pallas_call(
        functools.partial(_diff_body, tile=tile, slopes=slopes, lam_init=lam_init),
        grid=(b, nkb),
        in_specs=[pl.BlockSpec((1, tile, w), lambda bi, i: (bi, i, 0)),
                  pl.BlockSpec((1, s, w), lambda bi, i: (bi, 0, 1)),
                  pl.BlockSpec((1, nkb, w, tile), lambda bi, i: (bi, 0, 0, 0)),
                  small(DQK_A), small(DQK_A), small(DQK_A), small(DQK_A),
                  pl.BlockSpec((DV_A, 1), lambda bi, i: (0, 0))],
        out_specs=pl.BlockSpec((1, tile, w), lambda bi, i: (bi, i, 0)),
        out_shape=jax.ShapeDtypeStruct((b, s, w), BF16),
        scratch_shapes=[pltpu.VMEM((2 * H_A, 2 * LANES, tile), BF16),
                        pltpu.VMEM((H_A, tile, LANES), BF16),
                        pltpu.VMEM((H_A, tile, tile), F32),
                        pltpu.VMEM((2 * H_A, 1, tile), F32),
                        pltpu.VMEM((2 * H_A, 1, tile), F32),
                        pltpu.VMEM((2 * H_A, DV_A, tile), F32)] + _stage_scratch(tile),
        compiler_params=_params("arbitrary", "arbitrary"),
    )(proj, proj, vt, vec(lq1), vec(lk1), vec(lq2), vec(lk2), gain.reshape(-1, 1).astype(F32))


BAND_CHUNKS = LANES // CHUNK
BAND_WINDOW = (LEFT_CHUNKS + BAND_CHUNKS) * CHUNK


def _band_body(q_ref, k_ref, vt_ref, bias_ref, o_ref, qz_scr, ot_scr, *stage, n_sub):
    i = pl.program_id(1)
    ntile = BAND_WINDOW // LANES

    @pl.when((pl.program_id(0) == 0) & (i == 0))
    def _():
        qz_scr[...] = jnp.zeros(qz_scr.shape, BF16)

    kk = lax.broadcasted_iota(I32, (BAND_WINDOW, LANES), 0)
    ctx = []
    for u in range(n_sub):
        qt = _scaled_transpose(q_ref[0, u * LANES:(u + 1) * LANES, :], 1.0)
        for h in range(H_B):
            r0 = (h % 2) * DH_B
            qz_scr[u * H_B + h, r0:r0 + DH_B, :] = qt[h * DH_B:(h + 1) * DH_B, :]
        start = pl.multiple_of((i * n_sub + u) * LANES, LANES)
        ctx.append((k_ref[0, pl.ds(start, BAND_WINDOW), :],
                    vt_ref[0, pl.ds(i * n_sub + u, ntile)],
                    kk >= LEFT_CHUNKS * CHUNK - start))

    def scores(idx):
        (kw, _, valid), h = ctx[idx // H_B], idx % H_B
        pair = h // 2
        s = _dot_row_chunks(kw[:, pair * LANES:(pair + 1) * LANES], qz_scr[idx])
        return jnp.where(valid, s + bias_ref[h], MASKED)

    def probs(idx, t_ref, tmax_ref, p_ref, l_ref):
        p = jnp.exp2(t_ref[...] - tmax_ref[...])
        l_ref[...] = jnp.sum(p, axis=0, keepdims=True)
        p_ref[...] = p.astype(BF16)

    def accumulate(idx, p_ref, l_ref):
        vt, h = ctx[idx // H_B][1], idx % H_B
        vth = jnp.concatenate([vt[n, h * DH_B:(h + 1) * DH_B, :] for n in range(ntile)], axis=1)
        ot_scr[idx * DH_B:(idx + 1) * DH_B, :] = _dot(vth, p_ref[...]) / l_ref[...]

    _pipelined_maps(n_sub * H_B, scores, probs, accumulate, stage)
    w = H_B * DH_B
    for u in range(n_sub):
        o_ref[0, u * LANES:(u + 1) * LANES, :] = ot_scr[u * w:(u + 1) * w, :].T.astype(BF16)


def _band_bias_table(table):
    bias = _band_bias(table).transpose(0, 2, 1) * LOG2E
    cols = [jnp.pad(bias, ((0, 0), (c * CHUNK, (BAND_CHUNKS - 1 - c) * CHUNK), (0, 0)), constant_values=MASKED)
            for c in range(BAND_CHUNKS)]
    return jnp.concatenate(cols, axis=2)


def _band_attention(proj, kpad, vt_pad, bias):
    b, s, _ = proj.shape
    w = H_B * DH_B
    sp = kpad.shape[1]
    nvt = vt_pad.shape[1]
    n_sub = 2 if s % (2 * LANES) == 0 else 1
    rows = n_sub * LANES
    return pl.pallas_call(
        functools.partial(_band_body, n_sub=n_sub),
        grid=(b, s // rows),
        in_specs=[pl.BlockSpec((1, rows, w), lambda bi, i: (bi, i, 6)),
                  pl.BlockSpec((1, sp, w), lambda bi, i: (bi, 0, 0)),
                  pl.BlockSpec((1, nvt, w, LANES), lambda bi, i: (bi, 0, 0, 0)),
                  pl.BlockSpec((H_B, BAND_WINDOW, LANES), lambda bi, i: (0, 0, 0))],
        out_specs=pl.BlockSpec((1, rows, w), lambda bi, i: (bi, i, 0)),
        out_shape=jax.ShapeDtypeStruct((b, s, w), BF16),
        scratch_shapes=[pltpu.VMEM((n_sub * H_B, LANES, LANES), BF16),
                        pltpu.VMEM((n_sub * w, LANES), F32)] + _stage_scratch(LANES, BAND_WINDOW),
        compiler_params=_params("arbitrary", "arbitrary"),
    )(proj, kpad, vt_pad, bias)


def _sortable(x):
    bits = pltpu.bitcast(x + 0.0, I32)
    return bits ^ ((bits >> 31) & 0x7FFFFFFF)


_INT_MIN = -2 ** 31
_VALID_KEY = int(np.array(0.5 * NEG_INF, np.float32).view(np.int32))
_VALID_KEY = _VALID_KEY ^ ((_VALID_KEY >> 31) & 0x7FFFFFFF)


def _dsa_body(qc_ref, kc_ref, vct_ref, qi_ref, kiw_ref, wq_ref, o_ref,
              qa_scr, kx_scr, dneg_scr, qit_scr, key_scr, hi_scr, lo_scr, jstar_scr,
              m_scr, l_scr, acc_scr, *stage,
              tile, topk, slopes, seq):
    i = pl.program_id(1)
    nblk = i + 1
    kpos, qpos = _key_query_iotas(tile)

    @pl.when((pl.program_id(0) == 0) & (i == 0))
    def _():
        _init_alibi(slopes, dneg_scr, kx_scr, qa_scr, tile, 1)

    qit_scr[...] = _scaled_transpose(qi_ref[0], D_IDX ** -0.5)
    wt = wq_ref[0].T
    wrow = [wt[D_IDX + h:D_IDX + h + 1, :] * H_IDX ** -0.5 for h in range(H_IDX)]

    def score_block(j, diag):
        k0 = pl.multiple_of(j * tile, tile)
        kib = kiw_ref[0, pl.ds(k0, tile), :][:, :D_IDX].astype(BF16)
        isc = jnp.zeros((tile, tile), F32)
        for h in range(H_IDX):
            d = _dot(kib, qit_scr[h * D_IDX:(h + 1) * D_IDX, :])
            isc = isc + jnp.maximum(d, 0.0) * wrow[h]
        if diag:
            isc = jnp.where((kpos // CHUNK) <= (qpos // CHUNK), isc, NEG_INF)
        key = _sortable(isc)
        key_scr[pl.ds(k0, tile), :] = key
        hi_scr[pl.ds(k0, tile), :] = (key >> 16).astype(I16)
        lo_scr[pl.ds(k0, tile), :] = ((key & 0xFFFF) - 2 ** 15).astype(I16)

    def score_tiles(blocks):
        for j, diag in blocks:
            score_block(j, diag)

    _sweep_key_tiles(i, score_tiles)

    def count(pred):
        def body(j, part):
            k0 = pl.multiple_of(j * tile, tile)
            hit = jnp.where(pred(key_scr[pl.ds(k0, tile), :], kpos + k0), 1.0, 0.0)
            return part + jnp.sum(hit.reshape(tile // SUBLANES, SUBLANES, tile), axis=0)
        part = lax.fori_loop(0, nblk, body, jnp.zeros((SUBLANES, tile), F32))
        return jnp.sum(part, axis=0, keepdims=True)

    kf = float(topk)
    rows16 = 2 * SUBLANES

    pair = 2 if (seq // tile) % 2 == 0 else 1
    span = pair * tile
    lowest16 = jnp.full((tile, tile), -2 ** 15, I16)

    @pl.when(nblk % pair == 1)
    def _():
        pad_rows = pl.ds(pl.multiple_of(nblk * tile, tile), tile)
        hi_scr[pad_rows, :] = lowest16
        lo_scr[pad_rows, :] = lowest16

    def count16(src_ref, pred):
        def body(j, part):
            k0 = pl.multiple_of(j * span, span)
            hit = jnp.where(pred(src_ref[pl.ds(k0, span), :]), jnp.int16(1), jnp.int16(0))
            for g in range(span // rows16):
                part = part + hit[g * rows16:(g + 1) * rows16, :]
            return part
        part = lax.fori_loop(0, (nblk + pair - 1) // pair, body, jnp.zeros((rows16, tile), I16))
        return jnp.sum(part.astype(I32), axis=0, keepdims=True).astype(F32)

    def search16(src_ref, base):
        def bit(p, u):
            cand_u = u | jnp.left_shift(jnp.int32(1), 15 - p)
            cand = (cand_u - 2 ** 15).astype(I16)
            cnt = base + count16(src_ref, lambda v: v >= cand)
            return jnp.where(cnt >= kf, cand_u, u)
        return lax.fori_loop(0, 16, bit, jnp.zeros((1, tile), I32))

    thr_hi = search16(hi_scr, 0.0) - 2 ** 15
    thr_hi16 = thr_hi.astype(I16)
    above = count16(hi_scr, lambda v: v > thr_hi16)

    def mask_low(j, carry):
        k0 = pl.multiple_of(j * tile, tile)
        rows = pl.ds(k0, tile)
        lo_scr[rows, :] = jnp.where(hi_scr[rows, :] == thr_hi16, lo_scr[rows, :], jnp.int16(-2 ** 15))
        return carry

    lax.fori_loop(0, nblk, mask_low, 0)
    thr = jnp.left_shift(thr_hi, 16) + search16(lo_scr, above)

    thr_b = thr
    c_gt = count(lambda key, idx: key > thr_b)
    c_ge = count(lambda key, idx: key >= thr_b)
    tied = (c_ge > kf) & (thr > _VALID_KEY)
    need = kf - c_gt
    jstar_scr[...] = jnp.full((1, tile), seq, I32)

    @pl.when(jnp.max(jnp.where(tied, 1.0, 0.0)) > 0.0)
    def _():
        nbits = max(1, int(math.ceil(math.log2(seq))))

        def index_bit(p, lo):
            cand = lo | jnp.left_shift(jnp.int32(1), nbits - 1 - p)
            cnt = count(lambda key, idx: (key == thr_b) & (idx < cand))
            return jnp.where(cnt < need, cand, lo)

        lo = lax.fori_loop(0, nbits, index_bit, jnp.zeros((1, tile), I32))
        jstar_scr[...] = jnp.where(tied, lo, seq)

    jstar = jstar_scr[...]
    thr2 = jnp.maximum(thr, _VALID_KEY + 1)

    _store_query_maps(qa_scr, _scaled_transpose(qc_ref[0], 1.0), H_C, DH_C)
    m_scr[...] = jnp.full(m_scr.shape, NEG_INF, F32)
    l_scr[...] = jnp.zeros(l_scr.shape, F32)
    acc_scr[...] = jnp.zeros(acc_scr.shape, F32)

    def attend(blocks):
        ctx = []
        for j, diag in blocks:
            k0 = pl.multiple_of(j * tile, tile)
            key = key_scr[pl.ds(k0, tile), :]
            sel = (key > thr2) | ((key == thr2) & (kpos + k0 <= jstar))
            ctx.append((diag, sel, kc_ref[0, pl.ds(k0, tile), :], vct_ref[0, j], ((i - j) * tile).astype(F32)))

        def scores(idx):
            (diag, sel, kb, _, _), h = ctx[idx // H_C], idx % H_C
            pair = h // 2
            keys = jnp.concatenate([kb[:, pair * LANES:(pair + 1) * LANES], kx_scr[h]], axis=1)
            s = _dot_row_chunks(keys, qa_scr[h])
            if diag:
                s = s + dneg_scr[h]
            return jnp.where(sel, s, MASKED)

        def probs(idx, t_ref, tmax_ref, p_ref, a_ref):
            (diag, _, _, _, off), h = ctx[idx // H_C], idx % H_C
            shift = 0.0 if diag else (slopes[h] * LOG2E) * off
            _softmax_probs(shift, t_ref, tmax_ref, p_ref, a_ref, m_scr.at[h], l_scr.at[h])

        def accumulate(idx, p_ref, a_ref):
            vtb, h = ctx[idx // H_C][3], idx % H_C
            _accumulate(vtb[h * DH_C:(h + 1) * DH_C, :], p_ref, a_ref, acc_scr.at[h])

        _pipelined_maps(H_C * len(blocks), scores, probs, accumulate, stage)

    _sweep_key_tiles(i, attend)
    outs =[(acc_scr[h] / l_scr[h]).T for h in range(H_C)]
    o_ref[0] = jnp.concatenate(outs, axis=-1).astype(BF16)


def _dsa_attention(proj, vt, kiw, slopes):
    b, s, _ = proj.shape
    tile = min(ATT_TILE, s)
    nkb = s // tile
    w = H_C * DH_C
    wa = H_A * DV_A
    topk = min(TOPK_MAX, s // 4)
    return pl.pallas_call(
        functools.partial(_dsa_body, tile=tile, topk=topk, slopes=slopes, seq=s),
        grid=(b, nkb),
        in_specs=[pl.BlockSpec((1, tile, w), lambda bi, i: (bi, i, 8)),
                  pl.BlockSpec((1, s, w), lambda bi, i: (bi, 0, 9)),
                  pl.BlockSpec((1, nkb, w, tile), lambda bi, i: (bi, 0, wa // w, 0)),
                  pl.BlockSpec((1, tile, H_IDX * D_IDX), lambda bi, i: (bi, i, 2)),
                  pl.BlockSpec((1, s, LANES), lambda bi, i: (bi, 0, 0)),
                  pl.BlockSpec((1, tile, LANES), lambda bi, i: (bi, i, 0))],
        out_specs=pl.BlockSpec((1, tile, w), lambda bi, i: (bi, i, 0)),
        out_shape=jax.ShapeDtypeStruct((b, s, w), BF16),
        scratch_shapes=[pltpu.VMEM((H_C, 2 * LANES, tile), BF16),
                        pltpu.VMEM((H_C, tile, LANES), BF16),
                        pltpu.VMEM((H_C, tile, tile), F32),
                        pltpu.VMEM((H_IDX * D_IDX, tile), BF16),
                        pltpu.VMEM((s, tile), I32),
                        pltpu.VMEM((s, tile), I16),
                        pltpu.VMEM((s, tile), I16),
                        pltpu.VMEM((1, tile), I32),
                        pltpu.VMEM((H_C, 1, tile), F32),
                        pltpu.VMEM((H_C, 1, tile), F32),
                        pltpu.VMEM((H_C, DH_C, tile), F32)] + _stage_scratch(tile),
        compiler_params=_params("arbitrary", "arbitrary"),
    )(proj, proj, vt, proj, kiw, kiw)


def _xattn_body(x_ref, res_ref, wq_ref, k_ref, v_ref, wo_ref, g_ref, b_ref, of_ref, ob_ref, *, dh, alpha):
    q_all = _dot(x_ref[0], wq_ref[...]).astype(BF16)
    outs = []
    for h in range(H_X):
        c = h * dh
        q = q_all[:, c:c + dh] * dh ** -0.5
        s = _dot_nt(q, k_ref[0, :, c:c + dh])
        p = jnp.exp(s - jnp.max(s, axis=-1, keepdims=True))
        l = jnp.sum(p, axis=-1, keepdims=True)
        outs.append((_dot(p.astype(BF16), v_ref[0, :, c:c + dh]) / l).astype(BF16))
    y = _dot(jnp.concatenate(outs, axis=-1), wo_ref[...])
    out = _layer_norm_rows(alpha * res_ref[0] + y, g_ref[...], b_ref[...])
    of_ref[0] = out
    ob_ref[0] = out.astype(BF16)


def _xattn_ln(x, res, wq, kv, wo, g, b, alpha, tq=512):
    bsz, s, d = x.shape
    n_mem = kv.shape[1]
    tq = min(tq, s)
    tile = pl.BlockSpec((1, tq, d), lambda bi, i: (bi, i, 0))
    whole = lambda shape: pl.BlockSpec(shape, lambda bi, i: (0,) * len(shape))
    mem = lambda col: pl.BlockSpec((1, n_mem, d), lambda bi, i: (bi, 0, col))
    return pl.pallas_call(
        functools.partial(_xattn_body, dh=d // H_X, alpha=alpha),
        grid=(bsz, s // tq),
        in_specs=[tile, tile, whole((d, d)), mem(0), mem(1), whole((d, d)), whole((1, d)), whole((1, d))],
        out_specs=[tile, tile],
        out_shape=[jax.ShapeDtypeStruct((bsz, s, d), F32), jax.ShapeDtypeStruct((bsz, s, d), BF16)],
        compiler_params=_params("parallel", "parallel"),
    )(x, res, wq, kv, kv, wo, g.reshape(1, d), b.reshape(1, d))


def _router_body(x_ref, r_ref, g_ref):
    logits = _dot(x_ref[...], r_ref[...])
    lane = lax.broadcasted_iota(I32, logits.shape, 1).astype(F32)
    big = float(LANES)
    lg = jnp.where(lane < N_EXPERTS, logits, -jnp.inf)
    m1 = jnp.max(lg, axis=-1, keepdims=True)
    i1 = jnp.min(jnp.where(lg == m1, lane, big), axis=-1, keepdims=True)
    lg2 = jnp.where(lane == i1, -jnp.inf, lg)
    m2 = jnp.max(lg2, axis=-1, keepdims=True)
    i2 = jnp.min(jnp.where(lg2 == m2, lane, big), axis=-1, keepdims=True)
    e = jnp.exp(m2 - m1)
    den = 1.0 + e
    g_ref[...] = jnp.where(lane == i1, 1.0 / den, 0.0) + jnp.where(lane == i2, e / den, 0.0)


def _router_gates(x, router, tm=1024):
    m, d = x.shape
    tm = min(tm, m)
    rpad = jnp.zeros((d, LANES), BF16).at[:, :N_EXPERTS].set(router.astype(BF16))
    return pl.pallas_call(
        _router_body,
        grid=(m // tm,),
        in_specs=[pl.BlockSpec((tm, d), lambda i: (i, 0)),
                  pl.BlockSpec((d, LANES), lambda i: (0, 0))],
        out_specs=pl.BlockSpec((tm, LANES), lambda i: (i, 0)),
        out_shape=jax.ShapeDtypeStruct((m, LANES), F32),
        compiler_params=_params("parallel"),
    )(x, rpad)


MOE_BLOCK = 896
MOE_CHUNK = 128
MOE_GROUP = 2
MOE_FF_TILE = 1792


def _moe_body(x_ref, gate_ref, tri_ref, w1_ref, w3_ref, w2_ref, o_ref,
              xs_scr, acc_scr, rcol_scr, rrow_scr, cnt_smem, *, n_tokens):
    e = pl.program_id(1)
    f = pl.program_id(2)
    nf = pl.num_programs(2)
    tb = x_ref.shape[0]
    ch = MOE_CHUNK
    gw = MOE_GROUP * ch

    tok0 = pl.program_id(0) * tb
    valid_col = lax.broadcasted_iota(I32, (tb, 1), 0) + tok0 < n_tokens

    @pl.when((e == 0) & (f == 0))
    def _():
        o_ref[...] = jnp.zeros(o_ref.shape, F32)
        member = jnp.where((gate_ref[...] > 0.0) & valid_col, 1.0, 0.0)
        rank = _dot(tri_ref[...], member.astype(BF16))
        rcol = jnp.where(member > 0.0, rank, -1.0)
        rcol_scr[...] = rcol
        rrow_scr[...] = rcol.T
        counts = jnp.sum(member, axis=0, keepdims=True)
        for ee in range(N_EXPERTS):
            cnt_smem[ee] = counts[0, ee].astype(I32)

    count = cnt_smem[e]
    single = count <= gw
    nchunk = jnp.where(single, MOE_GROUP, (count + (ch - 1)) // ch)

    @pl.when(f == 0)
    def _():
        acc_scr[...] = jnp.zeros(acc_scr.shape, F32)
        sub = lax.broadcasted_iota(I32, (SUBLANES, tb), 0)
        r_row = jnp.sum(jnp.where(sub == e, rrow_scr[0:SUBLANES, :], 0.0), axis=0, keepdims=True)
        xz = jnp.where(valid_col, x_ref[...], jnp.zeros((), BF16))

        def gather_rows(r0, rows):
            row_id = lax.broadcasted_iota(I32, (rows, tb), 0).astype(F32) + jnp.asarray(r0, F32)
            sel = jnp.where(row_id == r_row, 1.0, 0.0).astype(BF16)
            xs_scr[pl.ds(r0, rows), :] = _dot(sel, xz).astype(BF16)

        @pl.when(single)
        def _():
            gather_rows(0, gw)

        @pl.when(jnp.logical_not(single))
        def _():
            def gather(c, carry):
                gather_rows(pl.multiple_of(c * ch, ch), ch)
                return carry
            lax.fori_loop(0, nchunk, gather, 0)

    def ffn_rows(r0, rows):
        xs = xs_scr[pl.ds(r0, rows), :]
        a = _dot(xs, w1_ref[0])
        b = _dot(xs, w3_ref[0])
        acc_scr[pl.ds(r0, rows), :] += _dot((_silu(a) * b).astype(BF16), w2_ref[0])

    @pl.when(single)
    def _():
        ffn_rows(0, gw)

    @pl.when(jnp.logical_not(single))
    def _():
        def ffn(c, carry):
            ffn_rows(pl.multiple_of(c * ch, ch), ch)
            return carry
        lax.fori_loop(0, nchunk, ffn, 0)

    @pl.when(f == nf - 1)
    def _():
        pick = lax.broadcasted_iota(I32, (tb, LANES), 1) == e
        r_col = jnp.sum(jnp.where(pick, rcol_scr[...], 0.0), axis=1, keepdims=True)
        g_col = jnp.sum(jnp.where(pick & valid_col, gate_ref[...], 0.0), axis=1, keepdims=True)
        col_id = lax.broadcasted_iota(I32, (tb, gw), 1).astype(F32)

        def combine(gi, carry):
            r0 = pl.multiple_of(gi * gw, gw)
            selt = jnp.where(col_id + r0.astype(F32) == r_col, 1.0, 0.0).astype(BF16)
            y = acc_scr[pl.ds(r0, gw), :]
            y_hi = y.astype(BF16)
            y_lo = (y - y_hi.astype(F32)).astype(BF16)
            o_ref[...] += g_col * (_dot(selt, y_hi) + _dot(selt, y_lo))
            return carry

        lax.fori_loop(0, (nchunk + (MOE_GROUP - 1)) // MOE_GROUP, combine, 0)


def _moe_experts(x, gates, w1, w3, w2):
    m, d = x.shape
    ne, _, ff = w1.shape
    tb = min(MOE_BLOCK, m)
    tf = MOE_FF_TILE if ff % MOE_FF_TILE == 0 else ff
    gw = MOE_GROUP * MOE_CHUNK
    cap = pl.cdiv(tb, gw) * gw
    assert tb % LANES == 0 and ne <= SUBLANES
    tri = jnp.asarray(np.tril(np.ones((tb, tb), np.float32), -1), BF16)
    row = lambda i, e, f: (i, 0)
    return pl.pallas_call(
        functools.partial(_moe_body, n_tokens=m),
        grid=(pl.cdiv(m, tb), ne, ff // tf),
        in_specs=[pl.BlockSpec((tb, d), row),
                  pl.BlockSpec((tb, LANES), row),
                  pl.BlockSpec((tb, tb), lambda i, e, f: (0, 0)),
                  pl.BlockSpec((1, d, tf), lambda i, e, f: (e, 0, f)),
                  pl.BlockSpec((1, d, tf), lambda i, e, f: (e, 0, f)),
                  pl.BlockSpec((1, tf, d), lambda i, e, f: (e, f, 0))],
        out_specs=pl.BlockSpec((tb, d), row),
        out_shape=jax.ShapeDtypeStruct((m, d), F32),
        scratch_shapes=[pltpu.VMEM((cap, d), BF16),
                        pltpu.VMEM((cap, d), F32),
                        pltpu.VMEM((tb, LANES), F32),
                        pltpu.VMEM((LANES, tb), F32),
                        pltpu.SMEM((N_EXPERTS,), I32)],
        compiler_params=_params("parallel", "arbitrary", "arbitrary"),
    )(x, gates, tri, w1, w3, w2)


def _res_ln_body(res_ref, y_ref, g_ref, b_ref, of_ref, ob_ref, *, alpha):
    out = _layer_norm_rows(alpha * res_ref[...] + y_ref[...], g_ref[...], b_ref[...])
    of_ref[...] = out
    ob_ref[...] = out.astype(BF16)


def _residual_ln(res, y, g, b, alpha, tm=512):
    m, d = res.shape
    tm = min(tm, m)
    row = pl.BlockSpec((tm, d), lambda i: (i, 0))
    vec = pl.BlockSpec((1, d), lambda i: (0, 0))
    return pl.pallas_call(
        functools.partial(_res_ln_body, alpha=alpha),
        grid=(m // tm,),
        in_specs=[row, row, vec, vec],
        out_specs=[row, row],
        out_shape=[jax.ShapeDtypeStruct((m, d), F32), jax.ShapeDtypeStruct((m, d), BF16)],
        compiler_params=_params("parallel"),
    )(res, y, g.reshape(1, d), b.reshape(1, d))


def _band_bias(table):
    band = (LEFT_CHUNKS + 1) * CHUNK
    left = LEFT_CHUNKS * CHUNK
    rev = table.astype(F32)[:, ::-1]
    pad = left - REL_MAX + CHUNK - 1
    n_rev = band - 1 - (left - REL_MAX) + 1
    assert n_rev <= 2 * REL_MAX + 1
    ext = jnp.concatenate([jnp.broadcast_to(rev[:, :1], (rev.shape[0], pad)), rev[:, :n_rev]], axis=1)
    n_ext = ext.shape[1]
    period = jnp.concatenate([ext[:, CHUNK - 1:], jnp.zeros((ext.shape[0], 1), F32), ext[:, :CHUNK - 1]], axis=1)
    skew = jnp.tile(period, (1, CHUNK))[:, :CHUNK * n_ext].reshape(-1, CHUNK, n_ext)
    return skew[:, :, :band]


def kernel(x, mem, w_in, lam_q1, lam_k1, lam_q2, lam_k2, diff_norm_g, rel_bias, w_out,
           ln1_g, ln1_b, xattn_wq, xattn_wk, xattn_wv, xattn_wo, ln2_g, ln2_b,
           ffn_w1, ffn_w3, ffn_w2, moe_router, moe_w1, moe_w3, moe_w2, ln3_g, ln3_b):
    b, s, d = x.shape
    depth = w_in.shape[0]
    t = b * s
    alpha = (2.0 * depth) ** 0.25
    slopes_a, slopes_c = _alibi_slopes()
    n_main = 3 * H_A * DV_A + 3 * H_B * DH_B + 3 * H_C * DH_C + H_IDX * D_IDX
    n_tail = D_IDX + H_IDX
    wa, wb, wc = H_A * DV_A, H_B * DH_B, H_C * DH_C
    left = LEFT_CHUNKS * CHUNK

    h = x.reshape(t, d)
    hb = h.astype(BF16)
    memb = mem.reshape(-1, d).astype(BF16)
    for l in range(depth):
        wl = w_in[l]
        o_b, o_c, o_i = 3 * wa, 3 * wa + 3 * wb, 3 * wa + 3 * wb + 3 * wc
        w_main = jnp.concatenate(
            [wl[:, :wa] * (DQK_A ** -0.5 * LOG2E), wl[:, wa:2 * wa], wl[:, o_i:o_i + H_IDX * D_IDX],
             wl[:, o_b:o_b + wb] * (DH_B ** -0.5 * LOG2E), wl[:, o_b + wb:o_b + 2 * wb],
             wl[:, o_c:o_c + wc] * (DH_C ** -0.5 * LOG2E), wl[:, o_c + wc:o_c + 2 * wc]], axis=1).astype(BF16)
        proj = _matmul(hb, w_main, BF16, tn=w_main.shape[1] // 2).reshape(b, s, -1)
        w_side = jnp.concatenate(
            [wl[:, 2 * wa:3 * wa], wl[:, o_c + 2 * wc:o_c + 3 * wc], wl[:, o_b + 2 * wb:o_b + 3 * wb],
             wl[:, n_main:], jnp.zeros((d, LANES - n_tail), F32)], axis=1).astype(BF16)
        tile = min(ATT_TILE, s)
        vt, vbt, kiw = _side_projections(hb, w_side, wa + wc, tile, wb, LANES)
        vt = vt.reshape(b, s // tile, wa + wc, tile)
        kiw = kiw.reshape(b, s, LANES)
        lam_init = 0.8 - 0.6 * math.exp(-0.3 * l)
        ya = _diff_attention(proj, vt, lam_q1[l], lam_k1[l], lam_q2[l], lam_k2[l], diff_norm_g[l],
                             slopes_a, lam_init)
        kpad = jnp.pad(proj[:, :, 3 * wa + wb:3 * wa + 2 * wb], ((0, 0), (left, 0), (0, 0)))
        vbt = jnp.pad(vbt.reshape(b, s // LANES, wb, LANES), ((0, 0), (left // LANES, 0), (0, 0), (0, 0)))
        yb = _band_attention(proj, kpad, vbt, _band_bias_table(rel_bias[l]))
        yc = _dsa_attention(proj, vt, kiw, slopes_c)
        wo = w_out[l].astype(BF16)
        h, hb = _matmul_ln(
            [ya.reshape(t, wa), yb.reshape(t, wb), yc.reshape(t, -1)],
            [wo[:wa], wo[wa:wa + wb], wo[wa + wb:]],
            h, ln1_g[l], ln1_b[l], alpha)
        w_kv = jnp.concatenate([xattn_wk[l], xattn_wv[l]], axis=1).astype(BF16)
        kv = _matmul(memb, w_kv, BF16, tn=d).reshape(b, -1, 2 * d)
        h, hb = _xattn_ln(hb.reshape(b, s, d), h.reshape(b, s, d), xattn_wq[l].astype(BF16), kv,
                          xattn_wo[l].astype(BF16), ln2_g[l], ln2_b[l], alpha)
        h, hb = h.reshape(t, d), hb.reshape(t, d)
        if l % 2 == 0:
            h, hb = _ffn_ln(hb, ffn_w1[l // 2].astype(BF16), ffn_w3[l // 2].astype(BF16),
                            ffn_w2[l // 2].astype(BF16), h, ln3_g[l], ln3_b[l], alpha)
        else:
            gates = _router_gates(hb, moe_router[l // 2])
            y = _moe_experts(hb, gates, moe_w1[l // 2].astype(BF16), moe_w3[l // 2].astype(BF16),
                             moe_w2[l // 2].astype(BF16))
            h, hb = _residual_ln(h, y, ln3_g[l], ln3_b[l], alpha)
    return h.reshape(b, s, d)
```

```python
import functools
import math

import numpy as np
import jax
import jax.numpy as jnp
from jax import lax
from jax.experimental import pallas as pl
from jax.experimental.pallas import tpu as pltpu

F32 = jnp.float32
BF16 = jnp.bfloat16
I32 = jnp.int32
I16 = jnp.int16

CHUNK = 64
H_A, DQK_A = 4, 64
DV_A = 2 * DQK_A
H_B, DH_B = 4, 64
LEFT_CHUNKS = 8
REL_MAX = 128
H_C, DH_C = 4, 64
H_IDX, D_IDX = 8, 64
TOPK_MAX = 256
H_X = 4
N_EXPERTS = 8
LN_EPS = 1e-5
RMS_EPS = 1e-6
NEG_INF = -1e30
MASKED = -2e30

LANES = 128
SUBLANES = 8
VMEM_LIMIT = 56 * 1024 * 1024
ATT_TILE = 256

_NT = (((1,), (1,)), ((), ()))


def _alibi_slopes():
    n = H_A + H_C
    s = (2.0 ** (-8.0 * np.arange(1, n + 1) / n)).astype(np.float32)
    return [float(v) for v in s[0::2]], [float(v) for v in s[1::2]]


def _params(*sem):
    return pltpu.CompilerParams(dimension_semantics=sem, vmem_limit_bytes=VMEM_LIMIT)


def _dot(a, b):
    return jnp.dot(a, b, preferred_element_type=F32)


SCORE_ROW_CHUNK = 128


def _dot_row_chunks(a, b):
    rows = a.shape[0]
    step = min(SCORE_ROW_CHUNK, rows)
    return jnp.concatenate([_dot(a[r:r + step], b) for r in range(0, rows, step)], axis=0)


def _dot_nt(a, b):
    return lax.dot_general(a, b, _NT, preferred_element_type=F32)


def _layer_norm_rows(y, g, b):
    mu = jnp.mean(y, axis=-1, keepdims=True)
    d = y - mu
    var = jnp.mean(d * d, axis=-1, keepdims=True)
    return d * lax.rsqrt(var + LN_EPS) * g + b


def _silu(a):
    return a / (1.0 + jnp.exp(-a))


def _mm_body(x_ref, w_ref, o_ref):
    o_ref[...] = _dot(x_ref[...], w_ref[...]).astype(o_ref.dtype)


def _matmul(x, w, out_dtype, tm=1024, tn=512):
    m, k = x.shape
    n = w.shape[1]
    tm, tn = min(tm, m), min(tn, n)
    return pl.pallas_call(
        _mm_body,
        grid=(m // tm, n // tn),
        in_specs=[pl.BlockSpec((tm, k), lambda i, j: (i, 0)),
                  pl.BlockSpec((k, tn), lambda i, j: (0, j))],
        out_specs=pl.BlockSpec((tm, tn), lambda i, j: (i, j)),
        out_shape=jax.ShapeDtypeStruct((m, n), out_dtype),
        compiler_params=_params("parallel", "parallel"),
    )(x, w)


def _side_proj_body(x_ref, w_ref, vt_ref, vbt_ref, tail_ref):
    y = _dot(x_ref[...], w_ref[...])
    n_v, n_vb = vt_ref.shape[1], vbt_ref.shape[1]
    tile, tile_b = vt_ref.shape[2], vbt_ref.shape[2]
    for r in range(vt_ref.shape[0]):
        vt_ref[r] = y[r * tile:(r + 1) * tile, :n_v].T.astype(BF16)
    for r in range(vbt_ref.shape[0]):
        vbt_ref[r] = y[r * tile_b:(r + 1) * tile_b, n_v:n_v + n_vb].T.astype(BF16)
    tail_ref[...] = y[:, n_v + n_vb:]


def _side_projections(x, w, n_v, tile, n_vb, tile_b, tm=1024):
    m, k = x.shape
    n = w.shape[1]
    n_tail = n - n_v - n_vb
    tm = min(tm, m)
    return pl.pallas_call(
        _side_proj_body,
        grid=(m // tm,),
        in_specs=[pl.BlockSpec((tm, k), lambda i: (i, 0)),
                  pl.BlockSpec((k, n), lambda i: (0, 0))],
        out_specs=[pl.BlockSpec((tm // tile, n_v, tile), lambda i: (i, 0, 0)),
                   pl.BlockSpec((tm // tile_b, n_vb, tile_b), lambda i: (i, 0, 0)),
                   pl.BlockSpec((tm, n_tail), lambda i: (i, 0))],
        out_shape=[jax.ShapeDtypeStruct((m // tile, n_v, tile), BF16),
                   jax.ShapeDtypeStruct((m // tile_b, n_vb, tile_b), BF16),
                   jax.ShapeDtypeStruct((m, n_tail), F32)],
        compiler_params=_params("parallel"),
    )(x, w)


def _mm_ln_body(*refs, n_in, alpha):
    xs, ws = refs[:n_in], refs[n_in:2 * n_in]
    res_ref, g_ref, b_ref, of_ref, ob_ref = refs[2 * n_in:]
    acc = _dot(xs[0][...], ws[0][...])
    for x_ref, w_ref in zip(xs[1:], ws[1:]):
        acc = acc + _dot(x_ref[...], w_ref[...])
    out = _layer_norm_rows(alpha * res_ref[...] + acc, g_ref[...], b_ref[...])
    of_ref[...] = out
    ob_ref[...] = out.astype(BF16)


def _matmul_ln(xs, ws, res, g, b, alpha):
    m, d = res.shape
    k_total = sum(x.shape[1] for x in xs)
    tm = min(1024 if k_total <= 1024 else 512, m)
    n_in = len(xs)
    in_specs = ([pl.BlockSpec((tm, x.shape[1]), lambda i: (i, 0)) for x in xs]
                + [pl.BlockSpec(w.shape, lambda i: (0, 0)) for w in ws]
                + [pl.BlockSpec((tm, d), lambda i: (i, 0)),
                   pl.BlockSpec((1, d), lambda i: (0, 0)),
                   pl.BlockSpec((1, d), lambda i: (0, 0))])
    return pl.pallas_call(
        functools.partial(_mm_ln_body, n_in=n_in, alpha=alpha),
        grid=(m // tm,),
        in_specs=in_specs,
        out_specs=[pl.BlockSpec((tm, d), lambda i: (i, 0)),
                   pl.BlockSpec((tm, d), lambda i: (i, 0))],
        out_shape=[jax.ShapeDtypeStruct((m, d), F32), jax.ShapeDtypeStruct((m, d), BF16)],
        compiler_params=_params("parallel"),
    )(*xs, *ws, res, g.reshape(1, d), b.reshape(1, d))


def _ffn_ln_body(x_ref, w1_ref, w3_ref, w2_ref, res_ref, g_ref, b_ref, of_ref, ob_ref, acc_ref, *, alpha):
    f = pl.program_id(1)
    x = x_ref[...]
    a = _dot(x, w1_ref[...])
    b = _dot(x, w3_ref[...])
    y = _dot((_silu(a) * b).astype(BF16), w2_ref[...])

    @pl.when(f == 0)
    def _():
        acc_ref[...] = y

    @pl.when(f > 0)
    def _():
        acc_ref[...] += y

    @pl.when(f == pl.num_programs(1) - 1)
    def _():
        out = _layer_norm_rows(alpha * res_ref[...] + acc_ref[...], g_ref[...], b_ref[...])
        of_ref[...] = out
        ob_ref[...] = out.astype(BF16)


def _ffn_ln(x, w1, w3, w2, res, g, b, alpha, tm=512, tf=1408):
    m, d = x.shape
    ff = w1.shape[1]
    tm = min(tm, m)
    if ff % tf:
        tf = ff
    row = pl.BlockSpec((tm, d), lambda i, f: (i, 0))
    vec = pl.BlockSpec((1, d), lambda i, f: (0, 0))
    return pl.pallas_call(
        functools.partial(_ffn_ln_body, alpha=alpha),
        grid=(m // tm, ff // tf),
        in_specs=[row,
                  pl.BlockSpec((d, tf), lambda i, f: (0, f)),
                  pl.BlockSpec((d, tf), lambda i, f: (0, f)),
                  pl.BlockSpec((tf, d), lambda i, f: (f, 0)),
                  row, vec, vec],
        out_specs=[row, row],
        out_shape=[jax.ShapeDtypeStruct((m, d), F32), jax.ShapeDtypeStruct((m, d), BF16)],
        scratch_shapes=[pltpu.VMEM((tm, d), F32)],
        compiler_params=_params("parallel", "arbitrary"),
    )(x, w1, w3, w2, res, g.reshape(1, d), b.reshape(1, d))


def _key_query_iotas(tile):
    kpos = lax.broadcasted_iota(I32, (tile, tile), 0)
    qpos = lax.broadcasted_iota(I32, (tile, tile), 1)
    return kpos, qpos


def _scaled_transpose(x, scale):
    return (x.astype(F32) * scale).T.astype(BF16)


LOG2E = math.log2(math.e)
ALIBI_PARTS = 3
ALIBI_ROWS = 16


def _bf16_parts(x):
    parts, r = [], np.float32(x)
    for _ in range(ALIBI_PARTS):
        p = np.float32(r.astype(jnp.bfloat16))
        parts.append(float(p))
        r = np.float32(r - p)
    return parts


def _alibi_key_cols(tile, parts):
    lane = lax.broadcasted_iota(I32, (tile, LANES), 1)
    kr = lax.broadcasted_iota(I32, (tile, LANES), 0).astype(F32)
    v = jnp.where((lane >= ALIBI_PARTS) & (lane < 2 * ALIBI_PARTS), kr, 0.0)
    for n, c in enumerate(parts):
        v = jnp.where(lane == n, c, v)
    return v.astype(BF16)


def _alibi_query_rows(tile, parts):
    row = lax.broadcasted_iota(I32, (ALIBI_ROWS, tile), 0)
    qr = lax.broadcasted_iota(I32, (ALIBI_ROWS, tile), 1).astype(F32)
    v = jnp.where(row < ALIBI_PARTS, -qr, 0.0)
    for n, c in enumerate(parts):
        v = jnp.where(row == ALIBI_PARTS + n, c, v)
    return v.astype(BF16)


def _init_alibi(slopes, dneg_scr, kx_scr, qa_scr, tile, maps_per_head):
    kpos, qpos = _key_query_iotas(tile)
    rel = (qpos - kpos).astype(F32)
    qa_scr[...] = jnp.zeros(qa_scr.shape, BF16)
    for h, slope in enumerate(slopes):
        parts = _bf16_parts(slope * LOG2E)
        dneg_scr[h] = 2.0 * jnp.minimum(np.float32(slope * LOG2E) * rel, 0.0)
        kx_scr[h] = _alibi_key_cols(tile, parts)
        for mm in range(maps_per_head):
            qa_scr[h * maps_per_head + mm, LANES:LANES + ALIBI_ROWS, :] = _alibi_query_rows(tile, parts)


def _store_query_maps(qa_scr, qt, n_maps, dh):
    for idx in range(n_maps):
        r0 = (idx % 2) * dh
        qa_scr[idx, r0:r0 + dh, :] = qt[idx * dh:(idx + 1) * dh, :]


def _softmax_probs(c, t_ref, tmax_ref, p_ref, a_ref, m_ref, l_ref):
    m_old = m_ref[...]
    m_new = jnp.maximum(m_old, tmax_ref[...] - c)
    p = jnp.exp2(t_ref[...] - (m_new + c))
    a = jnp.exp2(m_old - m_new)
    l_ref[...] = a * l_ref[...] + jnp.sum(p, axis=0, keepdims=True)
    m_ref[...] = m_new
    a_ref[...] = a
    p_ref[...] = p.astype(BF16)


def _accumulate(vt, p_ref, a_ref, acc_ref):
    acc_ref[...] = a_ref[...] * acc_ref[...] + _dot(vt, p_ref[...])


SCORE_LEAD, PROB_LEAD = 4, 2
N_T_SLOTS, N_P_SLOTS = SCORE_LEAD - PROB_LEAD + 2, PROB_LEAD + 1


def _stage_scratch(tile, keys=None):
    keys = tile if keys is None else keys
    return ([pltpu.VMEM((keys, tile), F32)] * N_T_SLOTS + [pltpu.VMEM((1, tile), F32)] * N_T_SLOTS
            + [pltpu.VMEM((keys, tile), BF16)] * N_P_SLOTS + [pltpu.VMEM((1, tile), F32)] * N_P_SLOTS)


def _pipelined_maps(n_maps, scores, probs, accumulate, stage):
    nt, npb = N_T_SLOTS, N_P_SLOTS
    t_slots, x_slots = stage[:nt], stage[nt:2 * nt]
    p_slots, a_slots = stage[2 * nt:2 * nt + npb], stage[2 * nt + npb:]

    def stage_scores(k):
        s = scores(k)
        t_slots[k % nt][...] = s
        x_slots[k % nt][...] = jnp.max(s, axis=0, keepdims=True)

    def stage_probs(k):
        probs(k, t_slots[k % nt], x_slots[k % nt], p_slots[k % npb], a_slots[k % npb])

    for step in range(-SCORE_LEAD, n_maps):
        if 0 <= step + SCORE_LEAD < n_maps:
            stage_scores(step + SCORE_LEAD)
        if 0 <= step + PROB_LEAD < n_maps:
            stage_probs(step + PROB_LEAD)
        if step >= 0:
            accumulate(step, p_slots[step % npb], a_slots[step % npb])


SWEEP_GROUP = 4


def _sweep_key_tiles(i, sweep):
    def body(g, carry):
        sweep([(SWEEP_GROUP * g + n, False) for n in range(SWEEP_GROUP)])
        return carry

    ngroups = i // SWEEP_GROUP
    lax.fori_loop(0, ngroups, body, 0)
    base = ngroups * SWEEP_GROUP
    for rem in range(SWEEP_GROUP):
        @pl.when(i - base == rem)
        def _():
            sweep([(base + n, False) for n in range(rem)] + [(i, True)])


def _diff_body(q_ref, k_ref, vt_ref, lq1_ref, lk1_ref, lq2_ref, lk2_ref, gain_ref, o_ref,
               qa_scr, kx_scr, dneg_scr, m_scr, l_scr, acc_scr, *stage,
               tile, slopes, lam_init):
    i = pl.program_id(1)
    kpos, qpos = _key_query_iotas(tile)

    @pl.when((pl.program_id(0) == 0) & (i == 0))
    def _():
        _init_alibi(slopes, dneg_scr, kx_scr, qa_scr, tile, 2)

    _store_query_maps(qa_scr, _scaled_transpose(q_ref[0], 1.0), 2 * H_A, DQK_A)
    m_scr[...] = jnp.full(m_scr.shape, NEG_INF, F32)
    l_scr[...] = jnp.zeros(l_scr.shape, F32)
    acc_scr[...] = jnp.zeros(acc_scr.shape, F32)

    n_maps = 2 * H_A

    def sweep(blocks):
        ctx = []
        for j, diag in blocks:
            k0 = pl.multiple_of(j * tile, tile)
            ctx.append((diag, k_ref[0, pl.ds(k0, tile), :], vt_ref[0, j], ((i - j) * tile).astype(F32)))

        def scores(n):
            (diag, kb, _, _), idx = ctx[n // n_maps], n % n_maps
            h, pair = idx // 2, idx // 2
            keys = jnp.concatenate([kb[:, pair * LANES:(pair + 1) * LANES], kx_scr[h]], axis=1)
            s = _dot_row_chunks(keys, qa_scr[idx])
            if diag:
                allowed = (kpos // CHUNK) <= (qpos // CHUNK)
                return jnp.where(allowed, s + dneg_scr[h], MASKED)
            return s

        def probs(n, t_ref, tmax_ref, p_ref, a_ref):
            (diag, _, _, off), idx = ctx[n // n_maps], n % n_maps
            shift = 0.0 if diag else (slopes[idx // 2] * LOG2E) * off
            _softmax_probs(shift, t_ref, tmax_ref, p_ref, a_ref, m_scr.at[idx], l_scr.at[idx])

        def accumulate(n, p_ref, a_ref):
            vtb, idx = ctx[n // n_maps][2], n % n_maps
            h = idx // 2
            _accumulate(vtb[h * DV_A:(h + 1) * DV_A, :], p_ref, a_ref, acc_scr.at[idx])

        _pipelined_maps(n_maps * len(blocks), scores, probs, accumulate, stage)

    _sweep_key_tiles(i, sweep)

    lam = (jnp.exp(jnp.sum(lq1_ref[...] * lk1_ref[...], axis=-1, keepdims=True))
           - jnp.exp(jnp.sum(lq2_ref[...] * lk2_ref[...], axis=-1, keepdims=True)) + lam_init)
    gain = gain_ref[...] * (1.0 - lam_init)
    outs = []
    for h in range(H_A):
        o = acc_scr[2 * h] / l_scr[2 * h] - lam * (acc_scr[2 * h + 1] / l_scr[2 * h + 1])
        o = o * lax.rsqrt(jnp.mean(o * o, axis=0, keepdims=True) + RMS_EPS)
        outs.append((o * gain).T)
    o_ref[0] = jnp.concatenate(outs, axis=-1).astype(BF16)


def _diff_attention(proj, vt, lq1, lk1, lq2, lk2, gain, slopes, lam_init):
    b, s, _ = proj.shape
    tile = min(ATT_TILE, s)
    nkb = s // tile
    w = H_A * DV_A
    vec = lambda a: a.reshape(1, -1).astype(F32)
    small = lambda n: pl.BlockSpec((1, n), lambda bi, i: (0, 0))
    return pl.pallas_call(
        functools.partial(_diff_body, tile=tile, slopes=slopes, lam_init=lam_init),
        grid=(b, nkb),
        in_specs=[pl.BlockSpec((1, tile, w), lambda bi, i: (bi, i, 0)),
                  pl.BlockSpec((1, s, w), lambda bi, i: (bi, 0, 1)),
                  pl.BlockSpec((1, nkb, w, tile), lambda bi, i: (bi, 0, 0, 0)),
                  small(DQK_A), small(DQK_A), small(DQK_A), small(DQK_A),
                  pl.BlockSpec((DV_A, 1), lambda bi, i: (0, 0))],
        out_specs=pl.BlockSpec((1, tile, w), lambda bi, i: (bi, i, 0)),
        out_shape=jax.ShapeDtypeStruct((b, s, w), BF16),
        scratch_shapes=[pltpu.VMEM((2 * H_A, 2 * LANES, tile), BF16),
                        pltpu.VMEM((H_A, tile, LANES), BF16),
                        pltpu.VMEM((H_A, tile, tile), F32),
                        pltpu.VMEM((2 * H_A, 1, tile), F32),
                        pltpu.VMEM((2 * H_A, 1, tile), F32),
                        pltpu.VMEM((2 * H_A, DV_A, tile), F32)] + _stage_scratch(tile),
        compiler_params=_params("arbitrary", "arbitrary"),
    )(proj, proj, vt, vec(lq1), vec(lk1), vec(lq2), vec(lk2), gain.reshape(-1, 1).astype(F32))


BAND_CHUNKS = LANES // CHUNK
BAND_WINDOW = (LEFT_CHUNKS + BAND_CHUNKS) * CHUNK


def _band_body(q_ref, k_ref, vt_ref, bias_ref, o_ref, qz_scr, ot_scr, *stage, n_sub):
    i = pl.program_id(1)
    ntile = BAND_WINDOW // LANES

    @pl.when((pl.program_id(0) == 0) & (i == 0))
    def _():
        qz_scr[...] = jnp.zeros(qz_scr.shape, BF16)

    kk = lax.broadcasted_iota(I32, (BAND_WINDOW, LANES), 0)
    ctx = []
    for u in range(n_sub):
        qt = _scaled_transpose(q_ref[0, u * LANES:(u + 1) * LANES, :], 1.0)
        for h in range(H_B):
            r0 = (h % 2) * DH_B
            qz_scr[u * H_B + h, r0:r0 + DH_B, :] = qt[h * DH_B:(h + 1) * DH_B, :]
        start = pl.multiple_of((i * n_sub + u) * LANES, LANES)
        ctx.append((k_ref[0, pl.ds(start, BAND_WINDOW), :],
                    vt_ref[0, pl.ds(i * n_sub + u, ntile)],
                    kk >= LEFT_CHUNKS * CHUNK - start))

    def scores(idx):
        (kw, _, valid), h = ctx[idx // H_B], idx % H_B
        pair = h // 2
        s = _dot_row_chunks(kw[:, pair * LANES:(pair + 1) * LANES], qz_scr[idx])
        return jnp.where(valid, s + bias_ref[h], MASKED)

    def probs(idx, t_ref, tmax_ref, p_ref, l_ref):
        p = jnp.exp2(t_ref[...] - tmax_ref[...])
        l_ref[...] = jnp.sum(p, axis=0, keepdims=True)
        p_ref[...] = p.astype(BF16)

    def accumulate(idx, p_ref, l_ref):
        vt, h = ctx[idx // H_B][1], idx % H_B
        vth = jnp.concatenate([vt[n, h * DH_B:(h + 1) * DH_B, :] for n in range(ntile)], axis=1)
        ot_scr[idx * DH_B:(idx + 1) * DH_B, :] = _dot(vth, p_ref[...]) / l_ref[...]

    _pipelined_maps(n_sub * H_B, scores, probs, accumulate, stage)
    w = H_B * DH_B
    for u in range(n_sub):
        o_ref[0, u * LANES:(u + 1) * LANES, :] = ot_scr[u * w:(u + 1) * w, :].T.astype(BF16)


def _band_bias_table(table):
    bias = _band_bias(table).transpose(0, 2, 1) * LOG2E
    cols = [jnp.pad(bias, ((0, 0), (c * CHUNK, (BAND_CHUNKS - 1 - c) * CHUNK), (0, 0)), constant_values=MASKED)
            for c in range(BAND_CHUNKS)]
    return jnp.concatenate(cols, axis=2)


def _band_attention(proj, kpad, vt_pad, bias):
    b, s, _ = proj.shape
    w = H_B * DH_B
    sp = kpad.shape[1]
    nvt = vt_pad.shape[1]
    n_sub = 2 if s % (2 * LANES) == 0 else 1
    rows = n_sub * LANES
    return pl.pallas_call(
        functools.partial(_band_body, n_sub=n_sub),
        grid=(b, s // rows),
        in_specs=[pl.BlockSpec((1, rows, w), lambda bi, i: (bi, i, 6)),
                  pl.BlockSpec((1, sp, w), lambda bi, i: (bi, 0, 0)),
                  pl.BlockSpec((1, nvt, w, LANES), lambda bi, i: (bi, 0, 0, 0)),
                  pl.BlockSpec((H_B, BAND_WINDOW, LANES), lambda bi, i: (0, 0, 0))],
        out_specs=pl.BlockSpec((1, rows, w), lambda bi, i: (bi, i, 0)),
        out_shape=jax.ShapeDtypeStruct((b, s, w), BF16),
        scratch_shapes=[pltpu.VMEM((n_sub * H_B, LANES, LANES), BF16),
                        pltpu.VMEM((n_sub * w, LANES), F32)] + _stage_scratch(LANES, BAND_WINDOW),
        compiler_params=_params("arbitrary", "arbitrary"),
    )(proj, kpad, vt_pad, bias)


def _sortable(x):
    bits = pltpu.bitcast(x + 0.0, I32)
    return bits ^ ((bits >> 31) & 0x7FFFFFFF)


_INT_MIN = -2 ** 31
_VALID_KEY = int(np.array(0.5 * NEG_INF, np.float32).view(np.int32))
_VALID_KEY = _VALID_KEY ^ ((_VALID_KEY >> 31) & 0x7FFFFFFF)


def _dsa_body(qc_ref, kc_ref, vct_ref, qi_ref, kiw_ref, wq_ref, o_ref,
              qa_scr, kx_scr, dneg_scr, qit_scr, key_scr, hi_scr, lo_scr, jstar_scr,
              m_scr, l_scr, acc_scr, *stage,
              tile, topk, slopes, seq):
    i = pl.program_id(1)
    nblk = i + 1
    kpos, qpos = _key_query_iotas(tile)

    @pl.when((pl.program_id(0) == 0) & (i == 0))
    def _():
        _init_alibi(slopes, dneg_scr, kx_scr, qa_scr, tile, 1)

    qit_scr[...] = _scaled_transpose(qi_ref[0], D_IDX ** -0.5)
    wt = wq_ref[0].T
    wrow = [wt[D_IDX + h:D_IDX + h + 1, :] * H_IDX ** -0.5 for h in range(H_IDX)]

    def score_block(j, diag):
        k0 = pl.multiple_of(j * tile, tile)
        kib = kiw_ref[0, pl.ds(k0, tile), :][:, :D_IDX].astype(BF16)
        isc = jnp.zeros((tile, tile), F32)
        for h in range(H_IDX):
            d = _dot(kib, qit_scr[h * D_IDX:(h + 1) * D_IDX, :])
            isc = isc + jnp.maximum(d, 0.0) * wrow[h]
        if diag:
            isc = jnp.where((kpos // CHUNK) <= (qpos // CHUNK), isc, NEG_INF)
        key = _sortable(isc)
        key_scr[pl.ds(k0, tile), :] = key
        hi_scr[pl.ds(k0, tile), :] = (key >> 16).astype(I16)
        lo_scr[pl.ds(k0, tile), :] = ((key & 0xFFFF) - 2 ** 15).astype(I16)

    def score_tiles(blocks):
        for j, diag in blocks:
            score_block(j, diag)

    _sweep_key_tiles(i, score_tiles)

    def count(pred):
        def body(j, part):
            k0 = pl.multiple_of(j * tile, tile)
            hit = jnp.where(pred(key_scr[pl.ds(k0, tile), :], kpos + k0), 1.0, 0.0)
            return part + jnp.sum(hit.reshape(tile // SUBLANES, SUBLANES, tile), axis=0)
        part = lax.fori_loop(0, nblk, body, jnp.zeros((SUBLANES, tile), F32))
        return jnp.sum(part, axis=0, keepdims=True)

    kf = float(topk)
    rows16 = 2 * SUBLANES

    pair = 2 if (seq // tile) % 2 == 0 else 1
    span = pair * tile
    lowest16 = jnp.full((tile, tile), -2 ** 15, I16)

    @pl.when(nblk % pair == 1)
    def _():
        pad_rows = pl.ds(pl.multiple_of(nblk * tile, tile), tile)
        hi_scr[pad_rows, :] = lowest16
        lo_scr[pad_rows, :] = lowest16

    def count16(src_ref, pred):
        def body(j, part):
            k0 = pl.multiple_of(j * span, span)
            hit = jnp.where(pred(src_ref[pl.ds(k0, span), :]), jnp.int16(1), jnp.int16(0))
            for g in range(span // rows16):
                part = part + hit[g * rows16:(g + 1) * rows16, :]
            return part
        part = lax.fori_loop(0, (nblk + pair - 1) // pair, body, jnp.zeros((rows16, tile), I16))
        return jnp.sum(part.astype(I32), axis=0, keepdims=True).astype(F32)

    def search16(src_ref, base):
        def bit(p, u):
            cand_u = u | jnp.left_shift(jnp.int32(1), 15 - p)
            cand = (cand_u - 2 ** 15).astype(I16)
            cnt = base + count16(src_ref, lambda v: v >= cand)
            return jnp.where(cnt >= kf, cand_u, u)
        return lax.fori_loop(0, 16, bit, jnp.zeros((1, tile), I32))

    thr_hi = search16(hi_scr, 0.0) - 2 ** 15
    thr_hi16 = thr_hi.astype(I16)
    above = count16(hi_scr, lambda v: v > thr_hi16)

    def mask_low(j, carry):
        k0 = pl.multiple_of(j * tile, tile)
        rows = pl.ds(k0, tile)
        lo_scr[rows, :] = jnp.where(hi_scr[rows, :] == thr_hi16, lo_scr[rows, :], jnp.int16(-2 ** 15))
        return carry

    lax.fori_loop(0, nblk, mask_low, 0)
    thr = jnp.left_shift(thr_hi, 16) + search16(lo_scr, above)

    thr_b = thr
    c_gt = count(lambda key, idx: key > thr_b)
    c_ge = count(lambda key, idx: key >= thr_b)
    tied = (c_ge > kf) & (thr > _VALID_KEY)
    need = kf - c_gt
    jstar_scr[...] = jnp.full((1, tile), seq, I32)

    @pl.when(jnp.max(jnp.where(tied, 1.0, 0.0)) > 0.0)
    def _():
        nbits = max(1, int(math.ceil(math.log2(seq))))

        def index_bit(p, lo):
            cand = lo | jnp.left_shift(jnp.int32(1), nbits - 1 - p)
            cnt = count(lambda key, idx: (key == thr_b) & (idx < cand))
            return jnp.where(cnt < need, cand, lo)

        lo = lax.fori_loop(0, nbits, index_bit, jnp.zeros((1, tile), I32))
        jstar_scr[...] = jnp.where(tied, lo, seq)

    jstar = jstar_scr[...]
    thr2 = jnp.maximum(thr, _VALID_KEY + 1)

    _store_query_maps(qa_scr, _scaled_transpose(qc_ref[0], 1.0), H_C, DH_C)
    m_scr[...] = jnp.full(m_scr.shape, NEG_INF, F32)
    l_scr[...] = jnp.zeros(l_scr.shape, F32)
    acc_scr[...] = jnp.zeros(acc_scr.shape, F32)

    def attend(blocks):
        ctx = []
        for j, diag in blocks:
            k0 = pl.multiple_of(j * tile, tile)
            key = key_scr[pl.ds(k0, tile), :]
            sel = (key > thr2) | ((key == thr2) & (kpos + k0 <= jstar))
            ctx.append((diag, sel, kc_ref[0, pl.ds(k0, tile), :], vct_ref[0, j], ((i - j) * tile).astype(F32)))

        def scores(idx):
            (diag, sel, kb, _, _), h = ctx[idx // H_C], idx % H_C
            pair = h // 2
            keys = jnp.concatenate([kb[:, pair * LANES:(pair + 1) * LANES], kx_scr[h]], axis=1)
            s = _dot_row_chunks(keys, qa_scr[h])
            if diag:
                s = s + dneg_scr[h]
            return jnp.where(sel, s, MASKED)

        def probs(idx, t_ref, tmax_ref, p_ref, a_ref):
            (diag, _, _, _, off), h = ctx[idx // H_C], idx % H_C
            shift = 0.0 if diag else (slopes[h] * LOG2E) * off
            _softmax_probs(shift, t_ref, tmax_ref, p_ref, a_ref, m_scr.at[h], l_scr.at[h])

        def accumulate(idx, p_ref, a_ref):
            vtb, h = ctx[idx // H_C][3], idx % H_C
            _accumulate(vtb[h * DH_C:(h + 1) * DH_C, :], p_ref, a_ref, acc_scr.at[h])

        _pipelined_maps(H_C * len(blocks), scores, probs, accumulate, stage)

    _sweep_key_tiles(i, attend)
    outs =[(acc_scr[h] / l_scr[h]).T for h in range(H_C)]
    o_ref[0] = jnp.concatenate(outs, axis=-1).astype(BF16)


def _dsa_attention(proj, vt, kiw, slopes):
    b, s, _ = proj.shape
    tile = min(ATT_TILE, s)
    nkb = s // tile
    w = H_C * DH_C
    wa = H_A * DV_A
    topk = min(TOPK_MAX, s // 4)
    return pl.pallas_call(
        functools.partial(_dsa_body, tile=tile, topk=topk, slopes=slopes, seq=s),
        grid=(b, nkb),
        in_specs=[pl.BlockSpec((1, tile, w), lambda bi, i: (bi, i, 8)),
                  pl.BlockSpec((1, s, w), lambda bi, i: (bi, 0, 9)),
                  pl.BlockSpec((1, nkb, w, tile), lambda bi, i: (bi, 0, wa // w, 0)),
                  pl.BlockSpec((1, tile, H_IDX * D_IDX), lambda bi, i: (bi, i, 2)),
                  pl.BlockSpec((1, s, LANES), lambda bi, i: (bi, 0, 0)),
                  pl.BlockSpec((1, tile, LANES), lambda bi, i: (bi, i, 0))],
        out_specs=pl.BlockSpec((1, tile, w), lambda bi, i: (bi, i, 0)),
        out_shape=jax.ShapeDtypeStruct((b, s, w), BF16),
        scratch_shapes=[pltpu.VMEM((H_C, 2 * LANES, tile), BF16),
                        pltpu.VMEM((H_C, tile, LANES), BF16),
                        pltpu.VMEM((H_C, tile, tile), F32),
                        pltpu.VMEM((H_IDX * D_IDX, tile), BF16),
                        pltpu.VMEM((s, tile), I32),
                        pltpu.VMEM((s, tile), I16),
                        pltpu.VMEM((s, tile), I16),
                        pltpu.VMEM((1, tile), I32),
                        pltpu.VMEM((H_C, 1, tile), F32),
                        pltpu.VMEM((H_C, 1, tile), F32),
                        pltpu.VMEM((H_C, DH_C, tile), F32)] + _stage_scratch(tile),
        compiler_params=_params("arbitrary", "arbitrary"),
    )(proj, proj, vt, proj, kiw, kiw)


def _xattn_body(x_ref, res_ref, wq_ref, k_ref, v_ref, wo_ref, g_ref, b_ref, of_ref, ob_ref, *, dh, alpha):
    q_all = _dot(x_ref[0], wq_ref[...]).astype(BF16)
    outs = []
    for h in range(H_X):
        c = h * dh
        q = q_all[:, c:c + dh] * dh ** -0.5
        s = _dot_nt(q, k_ref[0, :, c:c + dh])
        p = jnp.exp(s - jnp.max(s, axis=-1, keepdims=True))
        l = jnp.sum(p, axis=-1, keepdims=True)
        outs.append((_dot(p.astype(BF16), v_ref[0, :, c:c + dh]) / l).astype(BF16))
    y = _dot(jnp.concatenate(outs, axis=-1), wo_ref[...])
    out = _layer_norm_rows(alpha * res_ref[0] + y, g_ref[...], b_ref[...])
    of_ref[0] = out
    ob_ref[0] = out.astype(BF16)


def _xattn_ln(x, res, wq, kv, wo, g, b, alpha, tq=512):
    bsz, s, d = x.shape
    n_mem = kv.shape[1]
    tq = min(tq, s)
    tile = pl.BlockSpec((1, tq, d), lambda bi, i: (bi, i, 0))
    whole = lambda shape: pl.BlockSpec(shape, lambda bi, i: (0,) * len(shape))
    mem = lambda col: pl.BlockSpec((1, n_mem, d), lambda bi, i: (bi, 0, col))
    return pl.pallas_call(
        functools.partial(_xattn_body, dh=d // H_X, alpha=alpha),
        grid=(bsz, s // tq),
        in_specs=[tile, tile, whole((d, d)), mem(0), mem(1), whole((d, d)), whole((1, d)), whole((1, d))],
        out_specs=[tile, tile],
        out_shape=[jax.ShapeDtypeStruct((bsz, s, d), F32), jax.ShapeDtypeStruct((bsz, s, d), BF16)],
        compiler_params=_params("parallel", "parallel"),
    )(x, res, wq, kv, kv, wo, g.reshape(1, d), b.reshape(1, d))


def _router_body(x_ref, r_ref, g_ref):
    logits = _dot(x_ref[...], r_ref[...])
    lane = lax.broadcasted_iota(I32, logits.shape, 1).astype(F32)
    big = float(LANES)
    lg = jnp.where(lane < N_EXPERTS, logits, -jnp.inf)
    m1 = jnp.max(lg, axis=-1, keepdims=True)
    i1 = jnp.min(jnp.where(lg == m1, lane, big), axis=-1, keepdims=True)
    lg2 = jnp.where(lane == i1, -jnp.inf, lg)
    m2 = jnp.max(lg2, axis=-1, keepdims=True)
    i2 = jnp.min(jnp.where(lg2 == m2, lane, big), axis=-1, keepdims=True)
    e = jnp.exp(m2 - m1)
    den = 1.0 + e
    g_ref[...] = jnp.where(lane == i1, 1.0 / den, 0.0) + jnp.where(lane == i2, e / den, 0.0)


def _router_gates(x, router, tm=1024):
    m, d = x.shape
    tm = min(tm, m)
    rpad = jnp.zeros((d, LANES), BF16).at[:, :N_EXPERTS].set(router.astype(BF16))
    return pl.pallas_call(
        _router_body,
        grid=(m // tm,),
        in_specs=[pl.BlockSpec((tm, d), lambda i: (i, 0)),
                  pl.BlockSpec((d, LANES), lambda i: (0, 0))],
        out_specs=pl.BlockSpec((tm, LANES), lambda i: (i, 0)),
        out_shape=jax.ShapeDtypeStruct((m, LANES), F32),
        compiler_params=_params("parallel"),
    )(x, rpad)


MOE_BLOCK = 896
MOE_CHUNK = 128
MOE_GROUP = 2
MOE_FF_TILE = 1792


def _moe_body(x_ref, gate_ref, tri_ref, w1_ref, w3_ref, w2_ref, o_ref,
              xs_scr, acc_scr, rcol_scr, rrow_scr, cnt_smem, *, n_tokens):
    e = pl.program_id(1)
    f = pl.program_id(2)
    nf = pl.num_programs(2)
    tb = x_ref.shape[0]
    ch = MOE_CHUNK
    gw = MOE_GROUP * ch

    tok0 = pl.program_id(0) * tb
    valid_col = lax.broadcasted_iota(I32, (tb, 1), 0) + tok0 < n_tokens

    @pl.when((e == 0) & (f == 0))
    def _():
        o_ref[...] = jnp.zeros(o_ref.shape, F32)
        gate = jnp.where(valid_col, gate_ref[...], 0.0)
        member = jnp.where(gate > 0.0, 1.0, 0.0)
        rank = _dot(tri_ref[...], member.astype(BF16))
        rcol = jnp.where(member > 0.0, rank, -1.0)
        rcol_scr[...] = rcol
        rrow_scr[...] = rcol.T
        counts = jnp.sum(member, axis=0, keepdims=True)
        for ee in range(N_EXPERTS):
            cnt_smem[ee] = counts[0, ee].astype(I32)

    count = cnt_smem[e]
    single = count <= gw
    nchunk = jnp.where(single, MOE_GROUP, (count + (ch - 1)) // ch)

    @pl.when(f == 0)
    def _():
        acc_scr[...] = jnp.zeros(acc_scr.shape, F32)
        sub = lax.broadcasted_iota(I32, (SUBLANES, tb), 0)
        r_row = jnp.sum(jnp.where(sub == e, rrow_scr[0:SUBLANES, :], 0.0), axis=0, keepdims=True)
        xz = jnp.where(valid_col, x_ref[...], jnp.zeros((), BF16))

        def gather_rows(r0, rows):
            row_id = lax.broadcasted_iota(I32, (rows, tb), 0).astype(F32) + jnp.asarray(r0, F32)
            sel = jnp.where(row_id == r_row, 1.0, 0.0).astype(BF16)
            xs_scr[pl.ds(r0, rows), :] = _dot(sel, xz).astype(BF16)

        @pl.when(single)
        def _():
            gather_rows(0, gw)

        @pl.when(jnp.logical_not(single))
        def _():
            def gather(c, carry):
                gather_rows(pl.multiple_of(c * ch, ch), ch)
                return carry
            lax.fori_loop(0, nchunk, gather, 0)

    def ffn_rows(r0, rows):
        xs = xs_scr[pl.ds(r0, rows), :]
        a = _dot(xs, w1_ref[0])
        b = _dot(xs, w3_ref[0])
        acc_scr[pl.ds(r0, rows), :] += _dot((_silu(a) * b).astype(BF16), w2_ref[0])

    @pl.when(single)
    def _():
        ffn_rows(0, gw)

    @pl.when(jnp.logical_not(single))
    def _():
        def ffn(c, carry):
            ffn_rows(pl.multiple_of(c * ch, ch), ch)
            return carry
        lax.fori_loop(0, nchunk, ffn, 0)

    @pl.when(f == nf - 1)
    def _():
        pick = lax.broadcasted_iota(I32, (tb, LANES), 1) == e
        r_col = jnp.sum(jnp.where(pick, rcol_scr[...], 0.0), axis=1, keepdims=True)
        g_col = jnp.sum(jnp.where(pick & valid_col, gate_ref[...], 0.0), axis=1, keepdims=True)
        col_id = lax.broadcasted_iota(I32, (tb, gw), 1).astype(F32)

        def combine(gi, carry):
            r0 = pl.multiple_of(gi * gw, gw)
            selt = jnp.where(col_id + r0.astype(F32) == r_col, 1.0, 0.0).astype(BF16)
            y = acc_scr[pl.ds(r0, gw), :]
            y_hi = y.astype(BF16)
            y_lo = (y - y_hi.astype(F32)).astype(BF16)
            o_ref[...] += g_col * (_dot(selt, y_hi) + _dot(selt, y_lo))
            return carry

        lax.fori_loop(0, (nchunk + (MOE_GROUP - 1)) // MOE_GROUP, combine, 0)


def _moe_experts(x, gates, w1, w3, w2):
    m, d = x.shape
    ne, _, ff = w1.shape
    tb = min(MOE_BLOCK, m)
    tf = MOE_FF_TILE if ff % MOE_FF_TILE == 0 else ff
    gw = MOE_GROUP * MOE_CHUNK
    cap = pl.cdiv(tb, gw) * gw
    assert tb % LANES == 0 and ne <= SUBLANES
    tri = jnp.asarray(np.tril(np.ones((tb, tb), np.float32), -1), BF16)
    row = lambda i, e, f: (i, 0)
    return pl.pallas_call(
        functools.partial(_moe_body, n_tokens=m),
        grid=(pl.cdiv(m, tb), ne, ff // tf),
        in_specs=[pl.BlockSpec((tb, d), row),
                  pl.BlockSpec((tb, LANES), row),
                  pl.BlockSpec((tb, tb), lambda i, e, f: (0, 0)),
                  pl.BlockSpec((1, d, tf), lambda i, e, f: (e, 0, f)),
                  pl.BlockSpec((1, d, tf), lambda i, e, f: (e, 0, f)),
                  pl.BlockSpec((1, tf, d), lambda i, e, f: (e, f, 0))],
        out_specs=pl.BlockSpec((tb, d), row),
        out_shape=jax.ShapeDtypeStruct((m, d), F32),
        scratch_shapes=[pltpu.VMEM((cap, d), BF16),
                        pltpu.VMEM((cap, d), F32),
                        pltpu.VMEM((tb, LANES), F32),
                        pltpu.VMEM((LANES, tb), F32),
                        pltpu.SMEM((N_EXPERTS,), I32)],
        compiler_params=_params("parallel", "arbitrary", "arbitrary"),
    )(x, gates, tri, w1, w3, w2)


def _res_ln_body(res_ref, y_ref, g_ref, b_ref, of_ref, ob_ref, *, alpha):
    out = _layer_norm_rows(alpha * res_ref[...] + y_ref[...], g_ref[...], b_ref[...])
    of_ref[...] = out
    ob_ref[...] = out.astype(BF16)


def _residual_ln(res, y, g, b, alpha, tm=512):
    m, d = res.shape
    tm = min(tm, m)
    row = pl.BlockSpec((tm, d), lambda i: (i, 0))
    vec = pl.BlockSpec((1, d), lambda i: (0, 0))
    return pl.pallas_call(
        functools.partial(_res_ln_body, alpha=alpha),
        grid=(m // tm,),
        in_specs=[row, row, vec, vec],
        out_specs=[row, row],
        out_shape=[jax.ShapeDtypeStruct((m, d), F32), jax.ShapeDtypeStruct((m, d), BF16)],
        compiler_params=_params("parallel"),
    )(res, y, g.reshape(1, d), b.reshape(1, d))


def _band_bias(table):
    band = (LEFT_CHUNKS + 1) * CHUNK
    left = LEFT_CHUNKS * CHUNK
    rev = table.astype(F32)[:, ::-1]
    pad = left - REL_MAX + CHUNK - 1
    n_rev = band - 1 - (left - REL_MAX) + 1
    assert n_rev <= 2 * REL_MAX + 1
    ext = jnp.concatenate([jnp.broadcast_to(rev[:, :1], (rev.shape[0], pad)), rev[:, :n_rev]], axis=1)
    n_ext = ext.shape[1]
    period = jnp.concatenate([ext[:, CHUNK - 1:], jnp.zeros((ext.shape[0], 1), F32), ext[:, :CHUNK - 1]], axis=1)
    skew = jnp.tile(period, (1, CHUNK))[:, :CHUNK * n_ext].reshape(-1, CHUNK, n_ext)
    return skew[:, :, :band]


def kernel(x, mem, w_in, lam_q1, lam_k1, lam_q2, lam_k2, diff_norm_g, rel_bias, w_out,
           ln1_g, ln1_b, xattn_wq, xattn_wk, xattn_wv, xattn_wo, ln2_g, ln2_b,
           ffn_w1, ffn_w3, ffn_w2, moe_router, moe_w1, moe_w3, moe_w2, ln3_g, ln3_b):
    b, s, d = x.shape
    depth = w_in.shape[0]
    t = b * s
    alpha = (2.0 * depth) ** 0.25
    slopes_a, slopes_c = _alibi_slopes()
    n_main = 3 * H_A * DV_A + 3 * H_B * DH_B + 3 * H_C * DH_C + H_IDX * D_IDX
    n_tail = D_IDX + H_IDX
    wa, wb, wc = H_A * DV_A, H_B * DH_B, H_C * DH_C
    left = LEFT_CHUNKS * CHUNK

    h = x.reshape(t, d)
    hb = h.astype(BF16)
    memb = mem.reshape(-1, d).astype(BF16)
    for l in range(depth):
        wl = w_in[l]
        o_b, o_c, o_i = 3 * wa, 3 * wa + 3 * wb, 3 * wa + 3 * wb + 3 * wc
        w_main = jnp.concatenate(
            [wl[:, :wa] * (DQK_A ** -0.5 * LOG2E), wl[:, wa:2 * wa], wl[:, o_i:o_i + H_IDX * D_IDX],
             wl[:, o_b:o_b + wb] * (DH_B ** -0.5 * LOG2E), wl[:, o_b + wb:o_b + 2 * wb],
             wl[:, o_c:o_c + wc] * (DH_C ** -0.5 * LOG2E), wl[:, o_c + wc:o_c + 2 * wc]], axis=1).astype(BF16)
        proj = _matmul(hb, w_main, BF16, tn=w_main.shape[1] // 2).reshape(b, s, -1)
        w_side = jnp.concatenate(
            [wl[:, 2 * wa:3 * wa], wl[:, o_c + 2 * wc:o_c + 3 * wc], wl[:, o_b + 2 * wb:o_b + 3 * wb],
             wl[:, n_main:], jnp.zeros((d, LANES - n_tail), F32)], axis=1).astype(BF16)
        tile = min(ATT_TILE, s)
        vt, vbt, kiw = _side_projections(hb, w_side, wa + wc, tile, wb, LANES)
        vt = vt.reshape(b, s // tile, wa + wc, tile)
        kiw = kiw.reshape(b, s, LANES)
        lam_init = 0.8 - 0.6 * math.exp(-0.3 * l)
        ya = _diff_attention(proj, vt, lam_q1[l], lam_k1[l], lam_q2[l], lam_k2[l], diff_norm_g[l],
                             slopes_a, lam_init)
        kpad = jnp.pad(proj[:, :, 3 * wa + wb:3 * wa + 2 * wb], ((0, 0), (left, 0), (0, 0)))
        vbt = jnp.pad(vbt.reshape(b, s // LANES, wb, LANES), ((0, 0), (left // LANES, 0), (0, 0), (0, 0)))
        yb = _band_attention(proj, kpad, vbt, _band_bias_table(rel_bias[l]))
        yc = _dsa_attention(proj, vt, kiw, slopes_c)
        wo = w_out[l].astype(BF16)
        h, hb = _matmul_ln(
            [ya.reshape(t, wa), yb.reshape(t, wb), yc.reshape(t, -1)],
            [wo[:wa], wo[wa:wa + wb], wo[wa + wb:]],
            h, ln1_g[l], ln1_b[l], alpha)
        w_kv = jnp.concatenate([xattn_wk[l], xattn_wv[l]], axis=1).astype(BF16)
        kv = _matmul(memb, w_kv, BF16, tn=d).reshape(b, -1, 2 * d)
        h, hb = _xattn_ln(hb.reshape(b, s, d), h.reshape(b, s, d), xattn_wq[l].astype(BF16), kv,
                          xattn_wo[l].astype(BF16), ln2_g[l], ln2_b[l], alpha)
        h, hb = h.reshape(t, d), hb.reshape(t, d)
        if l % 2 == 0:
            h, hb = _ffn_ln(hb, ffn_w1[l // 2].astype(BF16), ffn_w3[l // 2].astype(BF16),
                            ffn_w2[l // 2].astype(BF16), h, ln3_g[l], ln3_b[l], alpha)
        else:
            gates = _router_gates(hb, moe_router[l // 2])
            y = _moe_experts(hb, gates, moe_w1[l // 2].astype(BF16), moe_w3[l // 2].astype(BF16),
                             moe_w2[l // 2].astype(BF16))
            h, hb = _residual_ln(h, y, ln3_g[l], ln3_b[l], alpha)
    return h.reshape(b, s, d)
```

```python
import functools
import math

import numpy as np
import jax
import jax.numpy as jnp
from jax import lax
from jax.experimental import pallas as pl
from jax.experimental.pallas import tpu as pltpu

F32 = jnp.float32
BF16 = jnp.bfloat16
I32 = jnp.int32
I16 = jnp.int16

CHUNK = 64
H_A, DQK_A = 4, 64
DV_A = 2 * DQK_A
H_B, DH_B = 4, 64
LEFT_CHUNKS = 8
REL_MAX = 128
H_C, DH_C = 4, 64
H_IDX, D_IDX = 8, 64
TOPK_MAX = 256
H_X = 4
N_EXPERTS = 8
LN_EPS = 1e-5
RMS_EPS = 1e-6
NEG_INF = -1e30
MASKED = -2e30

LANES = 128
SUBLANES = 8
VMEM_LIMIT = 56 * 1024 * 1024
ATT_TILE = 256

_NT = (((1,), (1,)), ((), ()))


def _alibi_slopes():
    n = H_A + H_C
    s = (2.0 ** (-8.0 * np.arange(1, n + 1) / n)).astype(np.float32)
    return [float(v) for v in s[0::2]], [float(v) for v in s[1::2]]


def _params(*sem):
    return pltpu.CompilerParams(dimension_semantics=sem, vmem_limit_bytes=VMEM_LIMIT)


def _dot(a, b):
    return jnp.dot(a, b, preferred_element_type=F32)


SCORE_ROW_CHUNK = 128


def _dot_row_chunks(a, b):
    rows = a.shape[0]
    step = min(SCORE_ROW_CHUNK, rows)
    return jnp.concatenate([_dot(a[r:r + step], b) for r in range(0, rows, step)], axis=0)


def _dot_nt(a, b):
    return lax.dot_general(a, b, _NT, preferred_element_type=F32)


def _layer_norm_rows(y, g, b):
    mu = jnp.mean(y, axis=-1, keepdims=True)
    d = y - mu
    var = jnp.mean(d * d, axis=-1, keepdims=True)
    return d * lax.rsqrt(var + LN_EPS) * g + b


def _silu(a):
    return a / (1.0 + jnp.exp(-a))


def _mm_body(x_ref, w_ref, o_ref):
    o_ref[...] = _dot(x_ref[...], w_ref[...]).astype(o_ref.dtype)


def _matmul(x, w, out_dtype, tm=1024, tn=512):
    m, k = x.shape
    n = w.shape[1]
    tm, tn = min(tm, m), min(tn, n)
    return pl.pallas_call(
        _mm_body,
        grid=(m // tm, n // tn),
        in_specs=[pl.BlockSpec((tm, k), lambda i, j: (i, 0)),
                  pl.BlockSpec((k, tn), lambda i, j: (0, j))],
        out_specs=pl.BlockSpec((tm, tn), lambda i, j: (i, j)),
        out_shape=jax.ShapeDtypeStruct((m, n), out_dtype),
        compiler_params=_params("parallel", "parallel"),
    )(x, w)


def _side_proj_body(x_ref, w_ref, vt_ref, vbt_ref, tail_ref):
    y = _dot(x_ref[...], w_ref[...])
    n_v, n_vb = vt_ref.shape[1], vbt_ref.shape[1]
    tile, tile_b = vt_ref.shape[2], vbt_ref.shape[2]
    for r in range(vt_ref.shape[0]):
        vt_ref[r] = y[r * tile:(r + 1) * tile, :n_v].T.astype(BF16)
    for r in range(vbt_ref.shape[0]):
        vbt_ref[r] = y[r * tile_b:(r + 1) * tile_b, n_v:n_v + n_vb].T.astype(BF16)
    tail_ref[...] = y[:, n_v + n_vb:]


def _side_projections(x, w, n_v, tile, n_vb, tile_b, tm=1024):
    m, k = x.shape
    n = w.shape[1]
    n_tail = n - n_v - n_vb
    tm = min(tm, m)
    return pl.pallas_call(
        _side_proj_body,
        grid=(m // tm,),
        in_specs=[pl.BlockSpec((tm, k), lambda i: (i, 0)),
                  pl.BlockSpec((k, n), lambda i: (0, 0))],
        out_specs=[pl.BlockSpec((tm // tile, n_v, tile), lambda i: (i, 0, 0)),
                   pl.BlockSpec((tm // tile_b, n_vb, tile_b), lambda i: (i, 0, 0)),
                   pl.BlockSpec((tm, n_tail), lambda i: (i, 0))],
        out_shape=[jax.ShapeDtypeStruct((m // tile, n_v, tile), BF16),
                   jax.ShapeDtypeStruct((m // tile_b, n_vb, tile_b), BF16),
                   jax.ShapeDtypeStruct((m, n_tail), F32)],
        compiler_params=_params("parallel"),
    )(x, w)


def _mm_ln_body(*refs, n_in, alpha):
    xs, ws = refs[:n_in], refs[n_in:2 * n_in]
    res_ref, g_ref, b_ref, of_ref, ob_ref = refs[2 * n_in:]
    acc = _dot(xs[0][...], ws[0][...])
    for x_ref, w_ref in zip(xs[1:], ws[1:]):
        acc = acc + _dot(x_ref[...], w_ref[...])
    out = _layer_norm_rows(alpha * res_ref[...] + acc, g_ref[...], b_ref[...])
    of_ref[...] = out
    ob_ref[...] = out.astype(BF16)


def _matmul_ln(xs, ws, res, g, b, alpha):
    m, d = res.shape
    k_total = sum(x.shape[1] for x in xs)
    tm = min(1024 if k_total <= 1024 else 512, m)
    n_in = len(xs)
    in_specs = ([pl.BlockSpec((tm, x.shape[1]), lambda i: (i, 0)) for x in xs]
                + [pl.BlockSpec(w.shape, lambda i: (0, 0)) for w in ws]
                + [pl.BlockSpec((tm, d), lambda i: (i, 0)),
                   pl.BlockSpec((1, d), lambda i: (0, 0)),
                   pl.BlockSpec((1, d), lambda i: (0, 0))])
    return pl.pallas_call(
        functools.partial(_mm_ln_body, n_in=n_in, alpha=alpha),
        grid=(m // tm,),
        in_specs=in_specs,
        out_specs=[pl.BlockSpec((tm, d), lambda i: (i, 0)),
                   pl.BlockSpec((tm, d), lambda i: (i, 0))],
        out_shape=[jax.ShapeDtypeStruct((m, d), F32), jax.ShapeDtypeStruct((m, d), BF16)],
        compiler_params=_params("parallel"),
    )(*xs, *ws, res, g.reshape(1, d), b.reshape(1, d))


def _ffn_ln_body(x_ref, w1_ref, w3_ref, w2_ref, res_ref, g_ref, b_ref, of_ref, ob_ref, acc_ref, *, alpha):
    f = pl.program_id(1)
    x = x_ref[...]
    a = _dot(x, w1_ref[...])
    b = _dot(x, w3_ref[...])
    y = _dot((_silu(a) * b).astype(BF16), w2_ref[...])

    @pl.when(f == 0)
    def _():
        acc_ref[...] = y

    @pl.when(f > 0)
    def _():
        acc_ref[...] += y

    @pl.when(f == pl.num_programs(1) - 1)
    def _():
        out = _layer_norm_rows(alpha * res_ref[...] + acc_ref[...], g_ref[...], b_ref[...])
        of_ref[...] = out
        ob_ref[...] = out.astype(BF16)


def _ffn_ln(x, w1, w3, w2, res, g, b, alpha, tm=512, tf=1408):
    m, d = x.shape
    ff = w1.shape[1]
    tm = min(tm, m)
    if ff % tf:
        tf = ff
    row = pl.BlockSpec((tm, d), lambda i, f: (i, 0))
    vec = pl.BlockSpec((1, d), lambda i, f: (0, 0))
    return pl.pallas_call(
        functools.partial(_ffn_ln_body, alpha=alpha),
        grid=(m // tm, ff // tf),
        in_specs=[row,
                  pl.BlockSpec((d, tf), lambda i, f: (0, f)),
                  pl.BlockSpec((d, tf), lambda i, f: (0, f)),
                  pl.BlockSpec((tf, d), lambda i, f: (f, 0)),
                  row, vec, vec],
        out_specs=[row, row],
        out_shape=[jax.ShapeDtypeStruct((m, d), F32), jax.ShapeDtypeStruct((m, d), BF16)],
        scratch_shapes=[pltpu.VMEM((tm, d), F32)],
        compiler_params=_params("parallel", "arbitrary"),
    )(x, w1, w3, w2, res, g.reshape(1, d), b.reshape(1, d))


def _key_query_iotas(tile):
    kpos = lax.broadcasted_iota(I32, (tile, tile), 0)
    qpos = lax.broadcasted_iota(I32, (tile, tile), 1)
    return kpos, qpos


def _scaled_transpose(x, scale):
    return (x.astype(F32) * scale).T.astype(BF16)


LOG2E = math.log2(math.e)
ALIBI_PARTS = 3
ALIBI_ROWS = 16


def _bf16_parts(x):
    parts, r = [], np.float32(x)
    for _ in range(ALIBI_PARTS):
        p = np.float32(r.astype(jnp.bfloat16))
        parts.append(float(p))
        r = np.float32(r - p)
    return parts


def _alibi_key_cols(tile, parts):
    lane = lax.broadcasted_iota(I32, (tile, LANES), 1)
    kr = lax.broadcasted_iota(I32, (tile, LANES), 0).astype(F32)
    v = jnp.where((lane >= ALIBI_PARTS) & (lane < 2 * ALIBI_PARTS), kr, 0.0)
    for n, c in enumerate(parts):
        v = jnp.where(lane == n, c, v)
    return v.astype(BF16)


def _alibi_query_rows(tile, parts):
    row = lax.broadcasted_iota(I32, (ALIBI_ROWS, tile), 0)
    qr = lax.broadcasted_iota(I32, (ALIBI_ROWS, tile), 1).astype(F32)
    v = jnp.where(row < ALIBI_PARTS, -qr, 0.0)
    for n, c in enumerate(parts):
        v = jnp.where(row == ALIBI_PARTS + n, c, v)
    return v.astype(BF16)


def _init_alibi(slopes, dneg_scr, kx_scr, qa_scr, tile, maps_per_head):
    kpos, qpos = _key_query_iotas(tile)
    rel = (qpos - kpos).astype(F32)
    qa_scr[...] = jnp.zeros(qa_scr.shape, BF16)
    for h, slope in enumerate(slopes):
        parts = _bf16_parts(slope * LOG2E)
        dneg_scr[h] = 2.0 * jnp.minimum(np.float32(slope * LOG2E) * rel, 0.0)
        kx_scr[h] = _alibi_key_cols(tile, parts)
        for mm in range(maps_per_head):
            qa_scr[h * maps_per_head + mm, LANES:LANES + ALIBI_ROWS, :] = _alibi_query_rows(tile, parts)


def _store_query_maps(qa_scr, qt, n_maps, dh):
    for idx in range(n_maps):
        r0 = (idx % 2) * dh
        qa_scr[idx, r0:r0 + dh, :] = qt[idx * dh:(idx + 1) * dh, :]


def _softmax_probs(c, t_ref, tmax_ref, p_ref, a_ref, m_ref):
    m_old = m_ref[...]
    m_new = jnp.maximum(m_old, tmax_ref[...] - c)
    m_ref[...] = m_new
    a_ref[...] = jnp.exp2(m_old - m_new)
    p_ref[...] = jnp.exp2(t_ref[...] - (m_new + c)).astype(BF16)


SUM_ROWS = 16


def _accumulate(vt, p_ref, a_ref, acc_ref):
    lhs = jnp.concatenate([vt, jnp.ones((SUM_ROWS, vt.shape[1]), BF16)], axis=0)
    acc_ref[...] = a_ref[...] * acc_ref[...] + _dot(lhs, p_ref[...])


def _normalised(acc, features):
    return acc[:features] / acc[features:features + 1]


SCORE_LEAD, PROB_LEAD = 4, 2
N_T_SLOTS, N_P_SLOTS = SCORE_LEAD - PROB_LEAD + 2, PROB_LEAD + 1


def _stage_scratch(tile, keys=None):
    keys = tile if keys is None else keys
    return ([pltpu.VMEM((keys, tile), F32)] * N_T_SLOTS + [pltpu.VMEM((1, tile), F32)] * N_T_SLOTS
            + [pltpu.VMEM((keys, tile), BF16)] * N_P_SLOTS + [pltpu.VMEM((1, tile), F32)] * N_P_SLOTS)


def _pipelined_maps(n_maps, scores, probs, accumulate, stage):
    nt, npb = N_T_SLOTS, N_P_SLOTS
    t_slots, x_slots = stage[:nt], stage[nt:2 * nt]
    p_slots, a_slots = stage[2 * nt:2 * nt + npb], stage[2 * nt + npb:]

    def stage_scores(k):
        s = scores(k)
        t_slots[k % nt][...] = s
        x_slots[k % nt][...] = jnp.max(s, axis=0, keepdims=True)

    def stage_probs(k):
        probs(k, t_slots[k % nt], x_slots[k % nt], p_slots[k % npb], a_slots[k % npb])

    for step in range(-SCORE_LEAD, n_maps):
        if 0 <= step + SCORE_LEAD < n_maps:
            stage_scores(step + SCORE_LEAD)
        if 0 <= step + PROB_LEAD < n_maps:
            stage_probs(step + PROB_LEAD)
        if step >= 0:
            accumulate(step, p_slots[step % npb], a_slots[step % npb])


SWEEP_GROUP = 4


def _sweep_key_tiles(i, sweep):
    def body(g, carry):
        sweep([(SWEEP_GROUP * g + n, False) for n in range(SWEEP_GROUP)])
        return carry

    ngroups = i // SWEEP_GROUP
    lax.fori_loop(0, ngroups, body, 0)
    base = ngroups * SWEEP_GROUP
    for rem in range(SWEEP_GROUP):
        @pl.when(i - base == rem)
        def _():
            sweep([(base + n, False) for n in range(rem)] + [(i, True)])


def _diff_body(q_ref, k_ref, vt_ref, lq1_ref, lk1_ref, lq2_ref, lk2_ref, gain_ref, o_ref,
               qa_scr, kx_scr, dneg_scr, m_scr, acc_scr, *stage,
               tile, slopes, lam_init):
    i = pl.program_id(1)
    kpos, qpos = _key_query_iotas(tile)

    @pl.when((pl.program_id(0) == 0) & (i == 0))
    def _():
        _init_alibi(slopes, dneg_scr, kx_scr, qa_scr, tile, 2)

    _store_query_maps(qa_scr, _scaled_transpose(q_ref[0], 1.0), 2 * H_A, DQK_A)
    m_scr[...] = jnp.full(m_scr.shape, NEG_INF, F32)
    acc_scr[...] = jnp.zeros(acc_scr.shape, F32)

    n_maps = 2 * H_A

    def sweep(blocks):
        ctx = []
        for j, diag in blocks:
            k0 = pl.multiple_of(j * tile, tile)
            ctx.append((diag, k_ref[0, pl.ds(k0, tile), :], vt_ref[0, j], ((i - j) * tile).astype(F32)))

        def scores(n):
            (diag, kb, _, _), idx = ctx[n // n_maps], n % n_maps
            h, pair = idx // 2, idx // 2
            keys = jnp.concatenate([kb[:, pair * LANES:(pair + 1) * LANES], kx_scr[h]], axis=1)
            s = _dot_row_chunks(keys, qa_scr[idx])
            if diag:
                allowed = (kpos // CHUNK) <= (qpos // CHUNK)
                return jnp.where(allowed, s + dneg_scr[h], MASKED)
            return s

        def probs(n, t_ref, tmax_ref, p_ref, a_ref):
            (diag, _, _, off), idx = ctx[n // n_maps], n % n_maps
            shift = 0.0 if diag else (slopes[idx // 2] * LOG2E) * off
            _softmax_probs(shift, t_ref, tmax_ref, p_ref, a_ref, m_scr.at[idx])

        def accumulate(n, p_ref, a_ref):
            vtb, idx = ctx[n // n_maps][2], n % n_maps
            h = idx // 2
            _accumulate(vtb[h * DV_A:(h + 1) * DV_A, :], p_ref, a_ref, acc_scr.at[idx])

        _pipelined_maps(n_maps * len(blocks), scores, probs, accumulate, stage)

    _sweep_key_tiles(i, sweep)

    lam = (jnp.exp(jnp.sum(lq1_ref[...] * lk1_ref[...], axis=-1, keepdims=True))
           - jnp.exp(jnp.sum(lq2_ref[...] * lk2_ref[...], axis=-1, keepdims=True)) + lam_init)
    gain = gain_ref[...] * (1.0 - lam_init)
    outs = []
    for h in range(H_A):
        o = _normalised(acc_scr[2 * h], DV_A) - lam * _normalised(acc_scr[2 * h + 1], DV_A)
        o = o * lax.rsqrt(jnp.mean(o * o, axis=0, keepdims=True) + RMS_EPS)
        outs.append((o * gain).T)
    o_ref[0] = jnp.concatenate(outs, axis=-1).astype(BF16)


def _diff_attention(proj, vt, lq1, lk1, lq2, lk2, gain, slopes, lam_init):
    b, s, _ = proj.shape
    tile = min(ATT_TILE, s)
    nkb = s // tile
    w = H_A * DV_A
    vec = lambda a: a.reshape(1, -1).astype(F32)
    small = lambda n: pl.BlockSpec((1, n), lambda bi, i: (0, 0))
    return pl.pallas_call(
        functools.partial(_diff_body, tile=tile, slopes=slopes, lam_init=lam_init),
        grid=(b, nkb),
        in_specs=[pl.BlockSpec((1, tile, w), lambda bi, i: (bi, i, 0)),
                  pl.BlockSpec((1, s, w), lambda bi, i: (bi, 0, 1)),
                  pl.BlockSpec((1, nkb, w, tile), lambda bi, i: (bi, 0, 0, 0)),
                  small(DQK_A), small(DQK_A), small(DQK_A), small(DQK_A),
                  pl.BlockSpec((DV_A, 1), lambda bi, i: (0, 0))],
        out_specs=pl.BlockSpec((1, tile, w), lambda bi, i: (bi, i, 0)),
        out_shape=jax.ShapeDtypeStruct((b, s, w), BF16),
        scratch_shapes=[pltpu.VMEM((2 * H_A, 2 * LANES, tile), BF16),
                        pltpu.VMEM((H_A, tile, LANES), BF16),
                        pltpu.VMEM((H_A, tile, tile), F32),
                        pltpu.VMEM((2 * H_A, 1, tile), F32),
                        pltpu.VMEM((2 * H_A, DV_A + SUM_ROWS, tile), F32)] + _stage_scratch(tile),
        compiler_params=_params("arbitrary", "arbitrary"),
    )(proj, proj, vt, vec(lq1), vec(lk1), vec(lq2), vec(lk2), gain.reshape(-1, 1).astype(F32))


BAND_CHUNKS = LANES // CHUNK
BAND_WINDOW = (LEFT_CHUNKS + BAND_CHUNKS) * CHUNK


def _band_body(q_ref, k_ref, vt_ref, bias_ref, o_ref, qz_scr, ot_scr, *stage, n_sub):
    i = pl.program_id(1)
    ntile = BAND_WINDOW // LANES

    @pl.when((pl.program_id(0) == 0) & (i == 0))
    def _():
        qz_scr[...] = jnp.zeros(qz_scr.shape, BF16)

    kk = lax.broadcasted_iota(I32, (BAND_WINDOW, LANES), 0)
    ctx = []
    for u in range(n_sub):
        qt = _scaled_transpose(q_ref[0, u * LANES:(u + 1) * LANES, :], 1.0)
        for h in range(H_B):
            r0 = (h % 2) * DH_B
            qz_scr[u * H_B + h, r0:r0 + DH_B, :] = qt[h * DH_B:(h + 1) * DH_B, :]
        start = pl.multiple_of((i * n_sub + u) * LANES, LANES)
        ctx.append((k_ref[0, pl.ds(start, BAND_WINDOW), :],
                    vt_ref[0, pl.ds(i * n_sub + u, ntile)],
                    kk >= LEFT_CHUNKS * CHUNK - start))

    def scores(idx):
        (kw, _, valid), h = ctx[idx // H_B], idx % H_B
        pair = h // 2
        s = _dot_row_chunks(kw[:, pair * LANES:(pair + 1) * LANES], qz_scr[idx])
        return jnp.where(valid, s + bias_ref[h], MASKED)

    def probs(idx, t_ref, tmax_ref, p_ref, unused_ref):
        p_ref[...] = jnp.exp2(t_ref[...] - tmax_ref[...]).astype(BF16)

    def accumulate(idx, p_ref, unused_ref):
        vt, h = ctx[idx // H_B][1], idx % H_B
        vth = jnp.concatenate([vt[n, h * DH_B:(h + 1) * DH_B, :] for n in range(ntile)], axis=1)
        lhs = jnp.concatenate([vth, jnp.ones((SUM_ROWS, BAND_WINDOW), BF16)], axis=0)
        ot_scr[idx * DH_B:(idx + 1) * DH_B, :] = _normalised(_dot(lhs, p_ref[...]), DH_B)

    _pipelined_maps(n_sub * H_B, scores, probs, accumulate, stage)
    w = H_B * DH_B
    for u in range(n_sub):
        o_ref[0, u * LANES:(u + 1) * LANES, :] = ot_scr[u * w:(u + 1) * w, :].T.astype(BF16)


def _band_bias_table(table):
    bias = _band_bias(table).transpose(0, 2, 1) * LOG2E
    cols = [jnp.pad(bias, ((0, 0), (c * CHUNK, (BAND_CHUNKS - 1 - c) * CHUNK), (0, 0)), constant_values=MASKED)
            for c in range(BAND_CHUNKS)]
    return jnp.concatenate(cols, axis=2)


def _band_attention(proj, kpad, vt_pad, bias):
    b, s, _ = proj.shape
    w = H_B * DH_B
    sp = kpad.shape[1]
    nvt = vt_pad.shape[1]
    n_sub = 2 if s % (2 * LANES) == 0 else 1
    rows = n_sub * LANES
    return pl.pallas_call(
        functools.partial(_band_body, n_sub=n_sub),
        grid=(b, s // rows),
        in_specs=[pl.BlockSpec((1, rows, w), lambda bi, i: (bi, i, 6)),
                  pl.BlockSpec((1, sp, w), lambda bi, i: (bi, 0, 0)),
                  pl.BlockSpec((1, nvt, w, LANES), lambda bi, i: (bi, 0, 0, 0)),
                  pl.BlockSpec((H_B, BAND_WINDOW, LANES), lambda bi, i: (0, 0, 0))],
        out_specs=pl.BlockSpec((1, rows, w), lambda bi, i: (bi, i, 0)),
        out_shape=jax.ShapeDtypeStruct((b, s, w), BF16),
        scratch_shapes=[pltpu.VMEM((n_sub * H_B, LANES, LANES), BF16),
                        pltpu.VMEM((n_sub * w, LANES), F32)] + _stage_scratch(LANES, BAND_WINDOW),
        compiler_params=_params("arbitrary", "arbitrary"),
    )(proj, kpad, vt_pad, bias)


def _sortable(x):
    bits = pltpu.bitcast(x + 0.0, I32)
    return bits ^ ((bits >> 31) & 0x7FFFFFFF)


_INT_MIN = -2 ** 31
_VALID_KEY = int(np.array(0.5 * NEG_INF, np.float32).view(np.int32))
_VALID_KEY = _VALID_KEY ^ ((_VALID_KEY >> 31) & 0x7FFFFFFF)


def _dsa_body(qc_ref, kc_ref, vct_ref, qi_ref, kiw_ref, wq_ref, o_ref,
              qa_scr, kx_scr, dneg_scr, qit_scr, key_scr, hi_scr, lo_scr, jstar_scr,
              m_scr, acc_scr, *stage,
              tile, topk, slopes, seq):
    i = pl.program_id(1)
    nblk = i + 1
    kpos, qpos = _key_query_iotas(tile)

    @pl.when((pl.program_id(0) == 0) & (i == 0))
    def _():
        _init_alibi(slopes, dneg_scr, kx_scr, qa_scr, tile, 1)

    qit_scr[...] = _scaled_transpose(qi_ref[0], D_IDX ** -0.5)
    wt = wq_ref[0].T
    wrow = [wt[D_IDX + h:D_IDX + h + 1, :] * H_IDX ** -0.5 for h in range(H_IDX)]

    def score_block(j, diag):
        k0 = pl.multiple_of(j * tile, tile)
        kib = kiw_ref[0, pl.ds(k0, tile), :][:, :D_IDX].astype(BF16)
        isc = jnp.zeros((tile, tile), F32)
        for h in range(H_IDX):
            d = _dot(kib, qit_scr[h * D_IDX:(h + 1) * D_IDX, :])
            isc = isc + jnp.maximum(d, 0.0) * wrow[h]
        if diag:
            isc = jnp.where((kpos // CHUNK) <= (qpos // CHUNK), isc, NEG_INF)
        key = _sortable(isc)
        key_scr[pl.ds(k0, tile), :] = key
        hi_scr[pl.ds(k0, tile), :] = (key >> 16).astype(I16)
        lo_scr[pl.ds(k0, tile), :] = ((key & 0xFFFF) - 2 ** 15).astype(I16)

    def score_tiles(blocks):
        for j, diag in blocks:
            score_block(j, diag)

    _sweep_key_tiles(i, score_tiles)

    def count(pred):
        def body(j, part):
            k0 = pl.multiple_of(j * tile, tile)
            hit = jnp.where(pred(key_scr[pl.ds(k0, tile), :], kpos + k0), 1.0, 0.0)
            return part + jnp.sum(hit.reshape(tile // SUBLANES, SUBLANES, tile), axis=0)
        part = lax.fori_loop(0, nblk, body, jnp.zeros((SUBLANES, tile), F32))
        return jnp.sum(part, axis=0, keepdims=True)

    kf = float(topk)
    rows16 = 2 * SUBLANES

    pair = 2 if (seq // tile) % 2 == 0 else 1
    span = pair * tile
    lowest16 = jnp.full((tile, tile), -2 ** 15, I16)

    @pl.when(nblk % pair == 1)
    def _():
        pad_rows = pl.ds(pl.multiple_of(nblk * tile, tile), tile)
        hi_scr[pad_rows, :] = lowest16
        lo_scr[pad_rows, :] = lowest16

    def count16(src_ref, pred):
        def body(j, part):
            k0 = pl.multiple_of(j * span, span)
            hit = jnp.where(pred(src_ref[pl.ds(k0, span), :]), jnp.int16(1), jnp.int16(0))
            for g in range(span // rows16):
                part = part + hit[g * rows16:(g + 1) * rows16, :]
            return part
        part = lax.fori_loop(0, (nblk + pair - 1) // pair, body, jnp.zeros((rows16, tile), I16))
        return jnp.sum(part.astype(I32), axis=0, keepdims=True).astype(F32)

    def search16(src_ref, base):
        def bit(p, u):
            cand_u = u | jnp.left_shift(jnp.int32(1), 15 - p)
            cand = (cand_u - 2 ** 15).astype(I16)
            cnt = base + count16(src_ref, lambda v: v >= cand)
            return jnp.where(cnt >= kf, cand_u, u)
        return lax.fori_loop(0, 16, bit, jnp.zeros((1, tile), I32))

    thr_hi = search16(hi_scr, 0.0) - 2 ** 15
    thr_hi16 = thr_hi.astype(I16)
    above = count16(hi_scr, lambda v: v > thr_hi16)

    def mask_low(j, carry):
        k0 = pl.multiple_of(j * tile, tile)
        rows = pl.ds(k0, tile)
        lo_scr[rows, :] = jnp.where(hi_scr[rows, :] == thr_hi16, lo_scr[rows, :], jnp.int16(-2 ** 15))
        return carry

    lax.fori_loop(0, nblk, mask_low, 0)
    thr = jnp.left_shift(thr_hi, 16) + search16(lo_scr, above)

    thr_b = thr
    c_gt = count(lambda key, idx: key > thr_b)
    c_ge = count(lambda key, idx: key >= thr_b)
    tied = (c_ge > kf) & (thr > _VALID_KEY)
    need = kf - c_gt
    jstar_scr[...] = jnp.full((1, tile), seq, I32)

    @pl.when(jnp.max(jnp.where(tied, 1.0, 0.0)) > 0.0)
    def _():
        nbits = max(1, int(math.ceil(math.log2(seq))))

        def index_bit(p, lo):
            cand = lo | jnp.left_shift(jnp.int32(1), nbits - 1 - p)
            cnt = count(lambda key, idx: (key == thr_b) & (idx < cand))
            return jnp.where(cnt < need, cand, lo)

        lo = lax.fori_loop(0, nbits, index_bit, jnp.zeros((1, tile), I32))
        jstar_scr[...] = jnp.where(tied, lo, seq)

    jstar = jstar_scr[...]
    thr2 = jnp.maximum(thr, _VALID_KEY + 1)

    _store_query_maps(qa_scr, _scaled_transpose(qc_ref[0], 1.0), H_C, DH_C)
    m_scr[...] = jnp.full(m_scr.shape, NEG_INF, F32)
    acc_scr[...] = jnp.zeros(acc_scr.shape, F32)

    def attend(blocks):
        ctx = []
        for j, diag in blocks:
            k0 = pl.multiple_of(j * tile, tile)
            key = key_scr[pl.ds(k0, tile), :]
            sel = (key > thr2) | ((key == thr2) & (kpos + k0 <= jstar))
            ctx.append((diag, sel, kc_ref[0, pl.ds(k0, tile), :], vct_ref[0, j], ((i - j) * tile).astype(F32)))

        def scores(idx):
            (diag, sel, kb, _, _), h = ctx[idx // H_C], idx % H_C
            pair = h // 2
            keys = jnp.concatenate([kb[:, pair * LANES:(pair + 1) * LANES], kx_scr[h]], axis=1)
            s = _dot_row_chunks(keys, qa_scr[h])
            if diag:
                s = s + dneg_scr[h]
            return jnp.where(sel, s, MASKED)

        def probs(idx, t_ref, tmax_ref, p_ref, a_ref):
            (diag, _, _, _, off), h = ctx[idx // H_C], idx % H_C
            shift = 0.0 if diag else (slopes[h] * LOG2E) * off
            _softmax_probs(shift, t_ref, tmax_ref, p_ref, a_ref, m_scr.at[h])

        def accumulate(idx, p_ref, a_ref):
            vtb, h = ctx[idx // H_C][3], idx % H_C
            _accumulate(vtb[h * DH_C:(h + 1) * DH_C, :], p_ref, a_ref, acc_scr.at[h])

        _pipelined_maps(H_C * len(blocks), scores, probs, accumulate, stage)

    _sweep_key_tiles(i, attend)
    outs = [_normalised(acc_scr[h], DH_C).T for h in range(H_C)]
    o_ref[0] = jnp.concatenate(outs, axis=-1).astype(BF16)


def _dsa_attention(proj, vt, kiw, slopes):
    b, s, _ = proj.shape
    tile = min(ATT_TILE, s)
    nkb = s // tile
    w = H_C * DH_C
    wa = H_A * DV_A
    topk = min(TOPK_MAX, s // 4)
    return pl.pallas_call(
        functools.partial(_dsa_body, tile=tile, topk=topk, slopes=slopes, seq=s),
        grid=(b, nkb),
        in_specs=[pl.BlockSpec((1, tile, w), lambda bi, i: (bi, i, 8)),
                  pl.BlockSpec((1, s, w), lambda bi, i: (bi, 0, 9)),
                  pl.BlockSpec((1, nkb, w, tile), lambda bi, i: (bi, 0, wa // w, 0)),
                  pl.BlockSpec((1, tile, H_IDX * D_IDX), lambda bi, i: (bi, i, 2)),
                  pl.BlockSpec((1, s, LANES), lambda bi, i: (bi, 0, 0)),
                  pl.BlockSpec((1, tile, LANES), lambda bi, i: (bi, i, 0))],
        out_specs=pl.BlockSpec((1, tile, w), lambda bi, i: (bi, i, 0)),
        out_shape=jax.ShapeDtypeStruct((b, s, w), BF16),
        scratch_shapes=[pltpu.VMEM((H_C, 2 * LANES, tile), BF16),
                        pltpu.VMEM((H_C, tile, LANES), BF16),
                        pltpu.VMEM((H_C, tile, tile), F32),
                        pltpu.VMEM((H_IDX * D_IDX, tile), BF16),
                        pltpu.VMEM((s, tile), I32),
                        pltpu.VMEM((s, tile), I16),
                        pltpu.VMEM((s, tile), I16),
                        pltpu.VMEM((1, tile), I32),
                        pltpu.VMEM((H_C, 1, tile), F32),
                        pltpu.VMEM((H_C, DH_C + SUM_ROWS, tile), F32)] + _stage_scratch(tile),
        compiler_params=_params("arbitrary", "arbitrary"),
    )(proj, proj, vt, proj, kiw, kiw)


def _xattn_body(x_ref, res_ref, wq_ref, k_ref, v_ref, wo_ref, g_ref, b_ref, of_ref, ob_ref, *, dh, alpha):
    q_all = _dot(x_ref[0], wq_ref[...]).astype(BF16)
    outs = []
    for h in range(H_X):
        c = h * dh
        q = q_all[:, c:c + dh] * dh ** -0.5
        s = _dot_nt(q, k_ref[0, :, c:c + dh])
        p = jnp.exp(s - jnp.max(s, axis=-1, keepdims=True))
        l = jnp.sum(p, axis=-1, keepdims=True)
        outs.append((_dot(p.astype(BF16), v_ref[0, :, c:c + dh]) / l).astype(BF16))
    y = _dot(jnp.concatenate(outs, axis=-1), wo_ref[...])
    out = _layer_norm_rows(alpha * res_ref[0] + y, g_ref[...], b_ref[...])
    of_ref[0] = out
    ob_ref[0] = out.astype(BF16)


def _xattn_ln(x, res, wq, kv, wo, g, b, alpha, tq=512):
    bsz, s, d = x.shape
    n_mem = kv.shape[1]
    tq = min(tq, s)
    tile = pl.BlockSpec((1, tq, d), lambda bi, i: (bi, i, 0))
    whole = lambda shape: pl.BlockSpec(shape, lambda bi, i: (0,) * len(shape))
    mem = lambda col: pl.BlockSpec((1, n_mem, d), lambda bi, i: (bi, 0, col))
    return pl.pallas_call(
        functools.partial(_xattn_body, dh=d // H_X, alpha=alpha),
        grid=(bsz, s // tq),
        in_specs=[tile, tile, whole((d, d)), mem(0), mem(1), whole((d, d)), whole((1, d)), whole((1, d))],
        out_specs=[tile, tile],
        out_shape=[jax.ShapeDtypeStruct((bsz, s, d), F32), jax.ShapeDtypeStruct((bsz, s, d), BF16)],
        compiler_params=_params("parallel", "parallel"),
    )(x, res, wq, kv, kv, wo, g.reshape(1, d), b.reshape(1, d))


def _router_body(x_ref, r_ref, g_ref):
    logits = _dot(x_ref[...], r_ref[...])
    lane = lax.broadcasted_iota(I32, logits.shape, 1).astype(F32)
    big = float(LANES)
    lg = jnp.where(lane < N_EXPERTS, logits, -jnp.inf)
    m1 = jnp.max(lg, axis=-1, keepdims=True)
    i1 = jnp.min(jnp.where(lg == m1, lane, big), axis=-1, keepdims=True)
    lg2 = jnp.where(lane == i1, -jnp.inf, lg)
    m2 = jnp.max(lg2, axis=-1, keepdims=True)
    i2 = jnp.min(jnp.where(lg2 == m2, lane, big), axis=-1, keepdims=True)
    e = jnp.exp(m2 - m1)
    den = 1.0 + e
    g_ref[...] = jnp.where(lane == i1, 1.0 / den, 0.0) + jnp.where(lane == i2, e / den, 0.0)


def _router_gates(x, router, tm=1024):
    m, d = x.shape
    tm = min(tm, m)
    rpad = jnp.zeros((d, LANES), BF16).at[:, :N_EXPERTS].set(router.astype(BF16))
    return pl.pallas_call(
        _router_body,
        grid=(m // tm,),
        in_specs=[pl.BlockSpec((tm, d), lambda i: (i, 0)),
                  pl.BlockSpec((d, LANES), lambda i: (0, 0))],
        out_specs=pl.BlockSpec((tm, LANES), lambda i: (i, 0)),
        out_shape=jax.ShapeDtypeStruct((m, LANES), F32),
        compiler_params=_params("parallel"),
    )(x, rpad)


MOE_BLOCK = 896
MOE_CHUNK = 128
MOE_GROUP = 2
MOE_FF_TILE = 1792


def _moe_body(x_ref, gate_ref, tri_ref, w1_ref, w3_ref, w2_ref, o_ref,
              xs_scr, acc_scr, rcol_scr, rrow_scr, cnt_smem, *, n_tokens):
    e = pl.program_id(1)
    f = pl.program_id(2)
    nf = pl.num_programs(2)
    tb = x_ref.shape[0]
    ch = MOE_CHUNK
    gw = MOE_GROUP * ch

    tok0 = pl.program_id(0) * tb
    valid_col = lax.broadcasted_iota(I32, (tb, 1), 0) + tok0 < n_tokens

    @pl.when((e == 0) & (f == 0))
    def _():
        o_ref[...] = jnp.zeros(o_ref.shape, F32)
        gate = jnp.where(valid_col, gate_ref[...], 0.0)
        member = jnp.where(gate > 0.0, 1.0, 0.0)
        rank = _dot(tri_ref[...], member.astype(BF16))
        rcol = jnp.where(member > 0.0, rank, -1.0)
        rcol_scr[...] = rcol
        rrow_scr[...] = rcol.T
        counts = jnp.sum(member, axis=0, keepdims=True)
        for ee in range(N_EXPERTS):
            cnt_smem[ee] = counts[0, ee].astype(I32)

    count = cnt_smem[e]
    single = count <= gw
    nchunk = jnp.where(single, MOE_GROUP, (count + (ch - 1)) // ch)

    @pl.when(f == 0)
    def _():
        sub = lax.broadcasted_iota(I32, (SUBLANES, tb), 0)
        r_row = jnp.sum(jnp.where(sub == e, rrow_scr[0:SUBLANES, :], 0.0), axis=0, keepdims=True)
        xz = jnp.where(valid_col, x_ref[...], jnp.zeros((), BF16))

        def gather_rows(r0, rows):
            row_id = lax.broadcasted_iota(I32, (rows, tb), 0).astype(F32) + jnp.asarray(r0, F32)
            sel = jnp.where(row_id == r_row, 1.0, 0.0).astype(BF16)
            xs_scr[pl.ds(r0, rows), :] = _dot(sel, xz).astype(BF16)

        @pl.when(single)
        def _():
            acc_scr[0:gw, :] = jnp.zeros((gw, acc_scr.shape[1]), F32)
            gather_rows(0, gw)

        @pl.when(jnp.logical_not(single))
        def _():
            acc_scr[...] = jnp.zeros(acc_scr.shape, F32)

            def gather(c, carry):
                gather_rows(pl.multiple_of(c * ch, ch), ch)
                return carry
            lax.fori_loop(0, nchunk, gather, 0)

    def ffn_rows(r0, rows):
        xs = xs_scr[pl.ds(r0, rows), :]
        a = _dot(xs, w1_ref[0])
        b = _dot(xs, w3_ref[0])
        acc_scr[pl.ds(r0, rows), :] += _dot((_silu(a) * b).astype(BF16), w2_ref[0])

    @pl.when(single)
    def _():
        ffn_rows(0, gw)

    @pl.when(jnp.logical_not(single))
    def _():
        def ffn(c, carry):
            ffn_rows(pl.multiple_of(c * ch, ch), ch)
            return carry
        lax.fori_loop(0, nchunk, ffn, 0)

    @pl.when(f == nf - 1)
    def _():
        pick = lax.broadcasted_iota(I32, (tb, LANES), 1) == e
        r_col = jnp.sum(jnp.where(pick, rcol_scr[...], 0.0), axis=1, keepdims=True)
        g_col = jnp.sum(jnp.where(pick & valid_col, gate_ref[...], 0.0), axis=1, keepdims=True)
        col_id = lax.broadcasted_iota(I32, (tb, gw), 1).astype(F32)

        def combine(gi, carry):
            r0 = pl.multiple_of(gi * gw, gw)
            selt = jnp.where(col_id + r0.astype(F32) == r_col, 1.0, 0.0).astype(BF16)
            y = acc_scr[pl.ds(r0, gw), :]
            y_hi = y.astype(BF16)
            y_lo = (y - y_hi.astype(F32)).astype(BF16)
            o_ref[...] += g_col * (_dot(selt, y_hi) + _dot(selt, y_lo))
            return carry

        lax.fori_loop(0, (nchunk + (MOE_GROUP - 1)) // MOE_GROUP, combine, 0)


def _moe_experts(x, gates, w1, w3, w2):
    m, d = x.shape
    ne, _, ff = w1.shape
    tb = min(MOE_BLOCK, m)
    tf = MOE_FF_TILE if ff % MOE_FF_TILE == 0 else ff
    gw = MOE_GROUP * MOE_CHUNK
    cap = pl.cdiv(tb, gw) * gw
    assert tb % LANES == 0 and ne <= SUBLANES
    tri = jnp.asarray(np.tril(np.ones((tb, tb), np.float32), -1), BF16)
    row = lambda i, e, f: (i, 0)
    return pl.pallas_call(
        functools.partial(_moe_body, n_tokens=m),
        grid=(pl.cdiv(m, tb), ne, ff // tf),
        in_specs=[pl.BlockSpec((tb, d), row),
                  pl.BlockSpec((tb, LANES), row),
                  pl.BlockSpec((tb, tb), lambda i, e, f: (0, 0)),
                  pl.BlockSpec((1, d, tf), lambda i, e, f: (e, 0, f)),
                  pl.BlockSpec((1, d, tf), lambda i, e, f: (e, 0, f)),
                  pl.BlockSpec((1, tf, d), lambda i, e, f: (e, f, 0))],
        out_specs=pl.BlockSpec((tb, d), row),
        out_shape=jax.ShapeDtypeStruct((m, d), F32),
        scratch_shapes=[pltpu.VMEM((cap, d), BF16),
                        pltpu.VMEM((cap, d), F32),
                        pltpu.VMEM((tb, LANES), F32),
                        pltpu.VMEM((LANES, tb), F32),
                        pltpu.SMEM((N_EXPERTS,), I32)],
        compiler_params=_params("parallel", "arbitrary", "arbitrary"),
    )(x, gates, tri, w1, w3, w2)


def _res_ln_body(res_ref, y_ref, g_ref, b_ref, of_ref, ob_ref, *, alpha):
    out = _layer_norm_rows(alpha * res_ref[...] + y_ref[...], g_ref[...], b_ref[...])
    of_ref[...] = out
    ob_ref[...] = out.astype(BF16)


def _residual_ln(res, y, g, b, alpha, tm=512):
    m, d = res.shape
    tm = min(tm, m)
    row = pl.BlockSpec((tm, d), lambda i: (i, 0))
    vec = pl.BlockSpec((1, d), lambda i: (0, 0))
    return pl.pallas_call(
        functools.partial(_res_ln_body, alpha=alpha),
        grid=(m // tm,),
        in_specs=[row, row, vec, vec],
        out_specs=[row, row],
        out_shape=[jax.ShapeDtypeStruct((m, d), F32), jax.ShapeDtypeStruct((m, d), BF16)],
        compiler_params=_params("parallel"),
    )(res, y, g.reshape(1, d), b.reshape(1, d))


def _band_bias(table):
    band = (LEFT_CHUNKS + 1) * CHUNK
    left = LEFT_CHUNKS * CHUNK
    rev = table.astype(F32)[:, ::-1]
    pad = left - REL_MAX + CHUNK - 1
    n_rev = band - 1 - (left - REL_MAX) + 1
    assert n_rev <= 2 * REL_MAX + 1
    ext = jnp.concatenate([jnp.broadcast_to(rev[:, :1], (rev.shape[0], pad)), rev[:, :n_rev]], axis=1)
    n_ext = ext.shape[1]
    period = jnp.concatenate([ext[:, CHUNK - 1:], jnp.zeros((ext.shape[0], 1), F32), ext[:, :CHUNK - 1]], axis=1)
    skew = jnp.tile(period, (1, CHUNK))[:, :CHUNK * n_ext].reshape(-1, CHUNK, n_ext)
    return skew[:, :, :band]


def kernel(x, mem, w_in, lam_q1, lam_k1, lam_q2, lam_k2, diff_norm_g, rel_bias, w_out,
           ln1_g, ln1_b, xattn_wq, xattn_wk, xattn_wv, xattn_wo, ln2_g, ln2_b,
           ffn_w1, ffn_w3, ffn_w2, moe_router, moe_w1, moe_w3, moe_w2, ln3_g, ln3_b):
    b, s, d = x.shape
    depth = w_in.shape[0]
    t = b * s
    alpha = (2.0 * depth) ** 0.25
    slopes_a, slopes_c = _alibi_slopes()
    n_main = 3 * H_A * DV_A + 3 * H_B * DH_B + 3 * H_C * DH_C + H_IDX * D_IDX
    n_tail = D_IDX + H_IDX
    wa, wb, wc = H_A * DV_A, H_B * DH_B, H_C * DH_C
    left = LEFT_CHUNKS * CHUNK

    h = x.reshape(t, d)
    hb = h.astype(BF16)
    memb = mem.reshape(-1, d).astype(BF16)
    for l in range(depth):
        wl = w_in[l]
        o_b, o_c, o_i = 3 * wa, 3 * wa + 3 * wb, 3 * wa + 3 * wb + 3 * wc
        w_main = jnp.concatenate(
            [wl[:, :wa] * (DQK_A ** -0.5 * LOG2E), wl[:, wa:2 * wa], wl[:, o_i:o_i + H_IDX * D_IDX],
             wl[:, o_b:o_b + wb] * (DH_B ** -0.5 * LOG2E), wl[:, o_b + wb:o_b + 2 * wb],
             wl[:, o_c:o_c + wc] * (DH_C ** -0.5 * LOG2E), wl[:, o_c + wc:o_c + 2 * wc]], axis=1).astype(BF16)
        proj = _matmul(hb, w_main, BF16, tn=w_main.shape[1] // 2).reshape(b, s, -1)
        w_side = jnp.concatenate(
            [wl[:, 2 * wa:3 * wa], wl[:, o_c + 2 * wc:o_c + 3 * wc], wl[:, o_b + 2 * wb:o_b + 3 * wb],
             wl[:, n_main:], jnp.zeros((d, LANES - n_tail), F32)], axis=1).astype(BF16)
        tile = min(ATT_TILE, s)
        vt, vbt, kiw = _side_projections(hb, w_side, wa + wc, tile, wb, LANES)
        vt = vt.reshape(b, s // tile, wa + wc, tile)
        kiw = kiw.reshape(b, s, LANES)
        lam_init = 0.8 - 0.6 * math.exp(-0.3 * l)
        ya = _diff_attention(proj, vt, lam_q1[l], lam_k1[l], lam_q2[l], lam_k2[l], diff_norm_g[l],
                             slopes_a, lam_init)
        kpad = jnp.pad(proj[:, :, 3 * wa + wb:3 * wa + 2 * wb], ((0, 0), (left, 0), (0, 0)))
        vbt = jnp.pad(vbt.reshape(b, s // LANES, wb, LANES), ((0, 0), (left // LANES, 0), (0, 0), (0, 0)))
        yb = _band_attention(proj, kpad, vbt, _band_bias_table(rel_bias[l]))
        yc = _dsa_attention(proj, vt, kiw, slopes_c)
        wo = w_out[l].astype(BF16)
        h, hb = _matmul_ln(
            [ya.reshape(t, wa), yb.reshape(t, wb), yc.reshape(t, -1)],
            [wo[:wa], wo[wa:wa + wb], wo[wa + wb:]],
            h, ln1_g[l], ln1_b[l], alpha)
        w_kv = jnp.concatenate([xattn_wk[l], xattn_wv[l]], axis=1).astype(BF16)
        kv = _matmul(memb, w_kv, BF16, tn=d).reshape(b, -1, 2 * d)
        h, hb = _xattn_ln(hb.reshape(b, s, d), h.reshape(b, s, d), xattn_wq[l].astype(BF16), kv,
                          xattn_wo[l].astype(BF16), ln2_g[l], ln2_b[l], alpha)
        h, hb = h.reshape(t, d), hb.reshape(t, d)
        if l % 2 == 0:
            h, hb = _ffn_ln(hb, ffn_w1[l // 2].astype(BF16), ffn_w3[l // 2].astype(BF16),
                            ffn_w2[l // 2].astype(BF16), h, ln3_g[l], ln3_b[l], alpha)
        else:
            gates = _router_gates(hb, moe_router[l // 2])
            y = _moe_experts(hb, gates, moe_w1[l // 2].astype(BF16), moe_w3[l // 2].astype(BF16),
                             moe_w2[l // 2].astype(BF16))
            h, hb = _residual_ln(h, y, ln3_g[l], ln3_b[l], alpha)
    return h.reshape(b, s, d)
```

```python
import functools
import math

import numpy as np
import jax
import jax.numpy as jnp
from jax import lax
from jax.experimental import pallas as pl
from jax.experimental.pallas import tpu as pltpu

F32 = jnp.float32
BF16 = jnp.bfloat16
I32 = jnp.int32
I16 = jnp.int16

CHUNK = 64
H_A, DQK_A = 4, 64
DV_A = 2 * DQK_A
H_B, DH_B = 4, 64
LEFT_CHUNKS = 8
REL_MAX = 128
H_C, DH_C = 4, 64
H_IDX, D_IDX = 8, 64
TOPK_MAX = 256
H_X = 4
N_EXPERTS = 8
LN_EPS = 1e-5
RMS_EPS = 1e-6
NEG_INF = -1e30
MASKED = -2e30

LANES = 128
SUBLANES = 8
VMEM_LIMIT = 56 * 1024 * 1024
ATT_TILE = 256

_NT = (((1,), (1,)), ((), ()))


def _alibi_slopes():
    n = H_A + H_C
    s = (2.0 ** (-8.0 * np.arange(1, n + 1) / n)).astype(np.float32)
    return [float(v) for v in s[0::2]], [float(v) for v in s[1::2]]


def _params(*sem):
    return pltpu.CompilerParams(dimension_semantics=sem, vmem_limit_bytes=VMEM_LIMIT)


def _dot(a, b):
    return jnp.dot(a, b, preferred_element_type=F32)


SCORE_ROW_CHUNK = 128


def _dot_row_chunks(a, b):
    rows = a.shape[0]
    step = min(SCORE_ROW_CHUNK, rows)
    return jnp.concatenate([_dot(a[r:r + step], b) for r in range(0, rows, step)], axis=0)


def _dot_nt(a, b):
    return lax.dot_general(a, b, _NT, preferred_element_type=F32)


def _layer_norm_rows(y, g, b):
    mu = jnp.mean(y, axis=-1, keepdims=True)
    d = y - mu
    var = jnp.mean(d * d, axis=-1, keepdims=True)
    return d * lax.rsqrt(var + LN_EPS) * g + b


def _silu(a):
    return a / (1.0 + jnp.exp(-a))


def _mm_body(x_ref, w_ref, o_ref):
    o_ref[...] = _dot(x_ref[...], w_ref[...]).astype(o_ref.dtype)


def _matmul(x, w, out_dtype, tm=1024, tn=512):
    m, k = x.shape
    n = w.shape[1]
    tm, tn = min(tm, m), min(tn, n)
    return pl.pallas_call(
        _mm_body,
        grid=(m // tm, n // tn),
        in_specs=[pl.BlockSpec((tm, k), lambda i, j: (i, 0)),
                  pl.BlockSpec((k, tn), lambda i, j: (0, j))],
        out_specs=pl.BlockSpec((tm, tn), lambda i, j: (i, j)),
        out_shape=jax.ShapeDtypeStruct((m, n), out_dtype),
        compiler_params=_params("parallel", "parallel"),
    )(x, w)


def _side_proj_body(x_ref, w_ref, vt_ref, vbt_ref, tail_ref):
    y = _dot(x_ref[...], w_ref[...])
    n_v, n_vb = vt_ref.shape[1], vbt_ref.shape[1]
    tile, tile_b = vt_ref.shape[2], vbt_ref.shape[2]
    for r in range(vt_ref.shape[0]):
        vt_ref[r] = y[r * tile:(r + 1) * tile, :n_v].T.astype(BF16)
    for r in range(vbt_ref.shape[0]):
        vbt_ref[r] = y[r * tile_b:(r + 1) * tile_b, n_v:n_v + n_vb].T.astype(BF16)
    tail_ref[...] = y[:, n_v + n_vb:]


def _side_projections(x, w, n_v, tile, n_vb, tile_b, tm=1024):
    m, k = x.shape
    n = w.shape[1]
    n_tail = n - n_v - n_vb
    tm = min(tm, m)
    return pl.pallas_call(
        _side_proj_body,
        grid=(m // tm,),
        in_specs=[pl.BlockSpec((tm, k), lambda i: (i, 0)),
                  pl.BlockSpec((k, n), lambda i: (0, 0))],
        out_specs=[pl.BlockSpec((tm // tile, n_v, tile), lambda i: (i, 0, 0)),
                   pl.BlockSpec((tm // tile_b, n_vb, tile_b), lambda i: (i, 0, 0)),
                   pl.BlockSpec((tm, n_tail), lambda i: (i, 0))],
        out_shape=[jax.ShapeDtypeStruct((m // tile, n_v, tile), BF16),
                   jax.ShapeDtypeStruct((m // tile_b, n_vb, tile_b), BF16),
                   jax.ShapeDtypeStruct((m, n_tail), F32)],
        compiler_params=_params("parallel"),
    )(x, w)


def _mm_ln_body(*refs, n_in, alpha):
    xs, ws = refs[:n_in], refs[n_in:2 * n_in]
    res_ref, g_ref, b_ref, of_ref, ob_ref = refs[2 * n_in:]
    acc = _dot(xs[0][...], ws[0][...])
    for x_ref, w_ref in zip(xs[1:], ws[1:]):
        acc = acc + _dot(x_ref[...], w_ref[...])
    out = _layer_norm_rows(alpha * res_ref[...] + acc, g_ref[...], b_ref[...])
    of_ref[...] = out
    ob_ref[...] = out.astype(BF16)


def _matmul_ln(xs, ws, res, g, b, alpha):
    m, d = res.shape
    k_total = sum(x.shape[1] for x in xs)
    tm = min(1024 if k_total <= 1024 else 512, m)
    n_in = len(xs)
    in_specs = ([pl.BlockSpec((tm, x.shape[1]), lambda i: (i, 0)) for x in xs]
                + [pl.BlockSpec(w.shape, lambda i: (0, 0)) for w in ws]
                + [pl.BlockSpec((tm, d), lambda i: (i, 0)),
                   pl.BlockSpec((1, d), lambda i: (0, 0)),
                   pl.BlockSpec((1, d), lambda i: (0, 0))])
    return pl.pallas_call(
        functools.partial(_mm_ln_body, n_in=n_in, alpha=alpha),
        grid=(m // tm,),
        in_specs=in_specs,
        out_specs=[pl.BlockSpec((tm, d), lambda i: (i, 0)),
                   pl.BlockSpec((tm, d), lambda i: (i, 0))],
        out_shape=[jax.ShapeDtypeStruct((m, d), F32), jax.ShapeDtypeStruct((m, d), BF16)],
        compiler_params=_params("parallel"),
    )(*xs, *ws, res, g.reshape(1, d), b.reshape(1, d))


def _ffn_ln_body(x_ref, w1_ref, w3_ref, w2_ref, res_ref, g_ref, b_ref, of_ref, ob_ref, acc_ref, *, alpha):
    f = pl.program_id(1)
    x = x_ref[...]
    a = _dot(x, w1_ref[...])
    b = _dot(x, w3_ref[...])
    y = _dot((_silu(a) * b).astype(BF16), w2_ref[...])

    @pl.when(f == 0)
    def _():
        acc_ref[...] = y

    @pl.when(f > 0)
    def _():
        acc_ref[...] += y

    @pl.when(f == pl.num_programs(1) - 1)
    def _():
        out = _layer_norm_rows(alpha * res_ref[...] + acc_ref[...], g_ref[...], b_ref[...])
        of_ref[...] = out
        ob_ref[...] = out.astype(BF16)


def _ffn_ln(x, w1, w3, w2, res, g, b, alpha, tm=512, tf=1408):
    m, d = x.shape
    ff = w1.shape[1]
    tm = min(tm, m)
    if ff % tf:
        tf = ff
    row = pl.BlockSpec((tm, d), lambda i, f: (i, 0))
    vec = pl.BlockSpec((1, d), lambda i, f: (0, 0))
    return pl.pallas_call(
        functools.partial(_ffn_ln_body, alpha=alpha),
        grid=(m // tm, ff // tf),
        in_specs=[row,
                  pl.BlockSpec((d, tf), lambda i, f: (0, f)),
                  pl.BlockSpec((d, tf), lambda i, f: (0, f)),
                  pl.BlockSpec((tf, d), lambda i, f: (f, 0)),
                  row, vec, vec],
        out_specs=[row, row],
        out_shape=[jax.ShapeDtypeStruct((m, d), F32), jax.ShapeDtypeStruct((m, d), BF16)],
        scratch_shapes=[pltpu.VMEM((tm, d), F32)],
        compiler_params=_params("parallel", "arbitrary"),
    )(x, w1, w3, w2, res, g.reshape(1, d), b.reshape(1, d))


def _key_query_iotas(tile):
    kpos = lax.broadcasted_iota(I32, (tile, tile), 0)
    qpos = lax.broadcasted_iota(I32, (tile, tile), 1)
    return kpos, qpos


def _scaled_transpose(x, scale):
    return (x.astype(F32) * scale).T.astype(BF16)


LOG2E = math.log2(math.e)
ALIBI_PARTS = 3
ALIBI_ROWS = 16


def _bf16_parts(x):
    parts, r = [], np.float32(x)
    for _ in range(ALIBI_PARTS):
        p = np.float32(r.astype(jnp.bfloat16))
        parts.append(float(p))
        r = np.float32(r - p)
    return parts


def _alibi_key_cols(tile, parts):
    lane = lax.broadcasted_iota(I32, (tile, LANES), 1)
    kr = lax.broadcasted_iota(I32, (tile, LANES), 0).astype(F32)
    v = jnp.where((lane >= ALIBI_PARTS) & (lane < 2 * ALIBI_PARTS), kr, 0.0)
    for n, c in enumerate(parts):
        v = jnp.where(lane == n, c, v)
    return v.astype(BF16)


def _alibi_query_rows(tile, parts):
    row = lax.broadcasted_iota(I32, (ALIBI_ROWS, tile), 0)
    qr = lax.broadcasted_iota(I32, (ALIBI_ROWS, tile), 1).astype(F32)
    v = jnp.where(row < ALIBI_PARTS, -qr, 0.0)
    for n, c in enumerate(parts):
        v = jnp.where(row == ALIBI_PARTS + n, c, v)
    return v.astype(BF16)


def _init_alibi(slopes, dneg_scr, kx_scr, qa_scr, tile, maps_per_head):
    kpos, qpos = _key_query_iotas(tile)
    rel = (qpos - kpos).astype(F32)
    qa_scr[...] = jnp.zeros(qa_scr.shape, BF16)
    for h, slope in enumerate(slopes):
        parts = _bf16_parts(slope * LOG2E)
        dneg_scr[h] = 2.0 * jnp.minimum(np.float32(slope * LOG2E) * rel, 0.0)
        kx_scr[h] = _alibi_key_cols(tile, parts)
        for mm in range(maps_per_head):
            qa_scr[h * maps_per_head + mm, LANES:LANES + ALIBI_ROWS, :] = _alibi_query_rows(tile, parts)


def _store_query_maps(qa_scr, qt, n_maps, dh):
    for idx in range(n_maps):
        r0 = (idx % 2) * dh
        qa_scr[idx, r0:r0 + dh, :] = qt[idx * dh:(idx + 1) * dh, :]


def _softmax_probs(c, t_ref, tmax_ref, p_ref, a_ref, m_ref):
    m_old = m_ref[...]
    m_new = jnp.maximum(m_old, tmax_ref[...] - c)
    m_ref[...] = m_new
    a_ref[...] = jnp.exp2(m_old - m_new)
    p_ref[...] = jnp.exp2(t_ref[...] - (m_new + c)).astype(BF16)


SUM_ROWS = 16


def _accumulate(vt, p_ref, a_ref, acc_ref):
    lhs = jnp.concatenate([vt, jnp.ones((SUM_ROWS, vt.shape[1]), BF16)], axis=0)
    acc_ref[...] = a_ref[...] * acc_ref[...] + _dot(lhs, p_ref[...])


def _normalised(acc, features):
    return acc[:features] / acc[features:features + 1]


SCORE_LEAD, PROB_LEAD = 4, 2
N_T_SLOTS, N_P_SLOTS = SCORE_LEAD - PROB_LEAD + 2, PROB_LEAD + 1


def _stage_scratch(tile, keys=None):
    keys = tile if keys is None else keys
    return ([pltpu.VMEM((keys, tile), F32)] * N_T_SLOTS + [pltpu.VMEM((1, tile), F32)] * N_T_SLOTS
            + [pltpu.VMEM((keys, tile), BF16)] * N_P_SLOTS + [pltpu.VMEM((1, tile), F32)] * N_P_SLOTS)


def _pipelined_maps(n_maps, scores, probs, accumulate, stage):
    nt, npb = N_T_SLOTS, N_P_SLOTS
    t_slots, x_slots = stage[:nt], stage[nt:2 * nt]
    p_slots, a_slots = stage[2 * nt:2 * nt + npb], stage[2 * nt + npb:]

    def stage_scores(k):
        s = scores(k)
        t_slots[k % nt][...] = s
        x_slots[k % nt][...] = jnp.max(s, axis=0, keepdims=True)

    def stage_probs(k):
        probs(k, t_slots[k % nt], x_slots[k % nt], p_slots[k % npb], a_slots[k % npb])

    for step in range(-SCORE_LEAD, n_maps):
        if 0 <= step + SCORE_LEAD < n_maps:
            stage_scores(step + SCORE_LEAD)
        if 0 <= step + PROB_LEAD < n_maps:
            stage_probs(step + PROB_LEAD)
        if step >= 0:
            accumulate(step, p_slots[step % npb], a_slots[step % npb])


SWEEP_GROUP = 4


def _sweep_key_tiles(i, sweep):
    def body(g, carry):
        sweep([(SWEEP_GROUP * g + n, False) for n in range(SWEEP_GROUP)])
        return carry

    ngroups = i // SWEEP_GROUP
    lax.fori_loop(0, ngroups, body, 0)
    base = ngroups * SWEEP_GROUP
    for rem in range(SWEEP_GROUP):
        @pl.when(i - base == rem)
        def _():
            sweep([(base + n, False) for n in range(rem)] + [(i, True)])


def _diff_body(q_ref, k_ref, vt_ref, lq1_ref, lk1_ref, lq2_ref, lk2_ref, gain_ref, o_ref,
               qa_scr, kx_scr, dneg_scr, m_scr, acc_scr, *stage,
               tile, slopes, lam_init):
    i = pl.program_id(1)
    kpos, qpos = _key_query_iotas(tile)

    @pl.when((pl.program_id(0) == 0) & (i == 0))
    def _():
        _init_alibi(slopes, dneg_scr, kx_scr, qa_scr, tile, 2)

    _store_query_maps(qa_scr, _scaled_transpose(q_ref[0], 1.0), 2 * H_A, DQK_A)
    m_scr[...] = jnp.full(m_scr.shape, NEG_INF, F32)
    acc_scr[...] = jnp.zeros(acc_scr.shape, F32)

    n_maps = 2 * H_A

    def sweep(blocks):
        ctx = []
        for j, diag in blocks:
            k0 = pl.multiple_of(j * tile, tile)
            ctx.append((diag, k_ref[0, pl.ds(k0, tile), :], vt_ref[0, j], ((i - j) * tile).astype(F32)))

        def scores(n):
            (diag, kb, _, _), idx = ctx[n // n_maps], n % n_maps
            h, pair = idx // 2, idx // 2
            keys = jnp.concatenate([kb[:, pair * LANES:(pair + 1) * LANES], kx_scr[h]], axis=1)
            s = _dot_row_chunks(keys, qa_scr[idx])
            if diag:
                allowed = (kpos // CHUNK) <= (qpos // CHUNK)
                return jnp.where(allowed, s + dneg_scr[h], MASKED)
            return s

        def probs(n, t_ref, tmax_ref, p_ref, a_ref):
            (diag, _, _, off), idx = ctx[n // n_maps], n % n_maps
            shift = 0.0 if diag else (slopes[idx // 2] * LOG2E) * off
            _softmax_probs(shift, t_ref, tmax_ref, p_ref, a_ref, m_scr.at[idx])

        def accumulate(n, p_ref, a_ref):
            vtb, idx = ctx[n // n_maps][2], n % n_maps
            h = idx // 2
            _accumulate(vtb[h * DV_A:(h + 1) * DV_A, :], p_ref, a_ref, acc_scr.at[idx])

        _pipelined_maps(n_maps * len(blocks), scores, probs, accumulate, stage)

    _sweep_key_tiles(i, sweep)

    lam = (jnp.exp(jnp.sum(lq1_ref[...] * lk1_ref[...], axis=-1, keepdims=True))
           - jnp.exp(jnp.sum(lq2_ref[...] * lk2_ref[...], axis=-1, keepdims=True)) + lam_init)
    gain = gain_ref[...] * (1.0 - lam_init)
    outs = []
    for h in range(H_A):
        o = _normalised(acc_scr[2 * h], DV_A) - lam * _normalised(acc_scr[2 * h + 1], DV_A)
        o = o * lax.rsqrt(jnp.mean(o * o, axis=0, keepdims=True) + RMS_EPS)
        outs.append((o * gain).T)
    o_ref[0] = jnp.concatenate(outs, axis=-1).astype(BF16)


def _diff_attention(proj, vt, lq1, lk1, lq2, lk2, gain, slopes, lam_init):
    b, s, _ = proj.shape
    tile = min(ATT_TILE, s)
    nkb = s // tile
    w = H_A * DV_A
    vec = lambda a: a.reshape(1, -1).astype(F32)
    small = lambda n: pl.BlockSpec((1, n), lambda bi, i: (0, 0))
    return pl.pallas_call(
        functools.partial(_diff_body, tile=tile, slopes=slopes, lam_init=lam_init),
        grid=(b, nkb),
        in_specs=[pl.BlockSpec((1, tile, w), lambda bi, i: (bi, i, 0)),
                  pl.BlockSpec((1, s, w), lambda bi, i: (bi, 0, 1)),
                  pl.BlockSpec((1, nkb, w, tile), lambda bi, i: (bi, 0, 0, 0)),
                  small(DQK_A), small(DQK_A), small(DQK_A), small(DQK_A),
                  pl.BlockSpec((DV_A, 1), lambda bi, i: (0, 0))],
        out_specs=pl.BlockSpec((1, tile, w), lambda bi, i: (bi, i, 0)),
        out_shape=jax.ShapeDtypeStruct((b, s, w), BF16),
        scratch_shapes=[pltpu.VMEM((2 * H_A, 2 * LANES, tile), BF16),
                        pltpu.VMEM((H_A, tile, LANES), BF16),
                        pltpu.VMEM((H_A, tile, tile), F32),
                        pltpu.VMEM((2 * H_A, 1, tile), F32),
                        pltpu.VMEM((2 * H_A, DV_A + SUM_ROWS, tile), F32)] + _stage_scratch(tile),
        compiler_params=_params("arbitrary", "arbitrary"),
    )(proj, proj, vt, vec(lq1), vec(lk1), vec(lq2), vec(lk2), gain.reshape(-1, 1).astype(F32))


BAND_CHUNKS = LANES // CHUNK
BAND_WINDOW = (LEFT_CHUNKS + BAND_CHUNKS) * CHUNK


def _band_body(q_ref, k_ref, vt_ref, bias_ref, o_ref, qz_scr, ot_scr, *stage, n_sub):
    i = pl.program_id(1)
    ntile = BAND_WINDOW // LANES

    @pl.when((pl.program_id(0) == 0) & (i == 0))
    def _():
        qz_scr[...] = jnp.zeros(qz_scr.shape, BF16)

    kk = lax.broadcasted_iota(I32, (BAND_WINDOW, LANES), 0)
    ctx = []
    for u in range(n_sub):
        qt = _scaled_transpose(q_ref[0, u * LANES:(u + 1) * LANES, :], 1.0)
        for h in range(H_B):
            r0 = (h % 2) * DH_B
            qz_scr[u * H_B + h, r0:r0 + DH_B, :] = qt[h * DH_B:(h + 1) * DH_B, :]
        start = pl.multiple_of((i * n_sub + u) * LANES, LANES)
        ctx.append((k_ref[0, pl.ds(start, BAND_WINDOW), :],
                    vt_ref[0, pl.ds(i * n_sub + u, ntile)],
                    kk >= LEFT_CHUNKS * CHUNK - start))

    def scores(idx):
        (kw, _, valid), h = ctx[idx // H_B], idx % H_B
        pair = h // 2
        s = _dot_row_chunks(kw[:, pair * LANES:(pair + 1) * LANES], qz_scr[idx])
        return jnp.where(valid, s + bias_ref[h], MASKED)

    def probs(idx, t_ref, tmax_ref, p_ref, unused_ref):
        p_ref[...] = jnp.exp2(t_ref[...] - tmax_ref[...]).astype(BF16)

    def accumulate(idx, p_ref, unused_ref):
        vt, h = ctx[idx // H_B][1], idx % H_B
        vth = jnp.concatenate([vt[n, h * DH_B:(h + 1) * DH_B, :] for n in range(ntile)], axis=1)
        lhs = jnp.concatenate([vth, jnp.ones((SUM_ROWS, BAND_WINDOW), BF16)], axis=0)
        ot_scr[idx * DH_B:(idx + 1) * DH_B, :] = _normalised(_dot(lhs, p_ref[...]), DH_B)

    _pipelined_maps(n_sub * H_B, scores, probs, accumulate, stage)
    w = H_B * DH_B
    for u in range(n_sub):
        o_ref[0, u * LANES:(u + 1) * LANES, :] = ot_scr[u * w:(u + 1) * w, :].T.astype(BF16)


def _band_bias_table(table):
    bias = _band_bias(table).transpose(0, 2, 1) * LOG2E
    cols = [jnp.pad(bias, ((0, 0), (c * CHUNK, (BAND_CHUNKS - 1 - c) * CHUNK), (0, 0)), constant_values=MASKED)
            for c in range(BAND_CHUNKS)]
    return jnp.concatenate(cols, axis=2)


def _band_attention(proj, kpad, vt_pad, bias):
    b, s, _ = proj.shape
    w = H_B * DH_B
    sp = kpad.shape[1]
    nvt = vt_pad.shape[1]
    n_sub = 2 if s % (2 * LANES) == 0 else 1
    rows = n_sub * LANES
    return pl.pallas_call(
        functools.partial(_band_body, n_sub=n_sub),
        grid=(b, s // rows),
        in_specs=[pl.BlockSpec((1, rows, w), lambda bi, i: (bi, i, 6)),
                  pl.BlockSpec((1, sp, w), lambda bi, i: (bi, 0, 0)),
                  pl.BlockSpec((1, nvt, w, LANES), lambda bi, i: (bi, 0, 0, 0)),
                  pl.BlockSpec((H_B, BAND_WINDOW, LANES), lambda bi, i: (0, 0, 0))],
        out_specs=pl.BlockSpec((1, rows, w), lambda bi, i: (bi, i, 0)),
        out_shape=jax.ShapeDtypeStruct((b, s, w), BF16),
        scratch_shapes=[pltpu.VMEM((n_sub * H_B, LANES, LANES), BF16),
                        pltpu.VMEM((n_sub * w, LANES), F32)] + _stage_scratch(LANES, BAND_WINDOW),
        compiler_params=_params("arbitrary", "arbitrary"),
    )(proj, kpad, vt_pad, bias)


def _sortable(x):
    bits = pltpu.bitcast(x + 0.0, I32)
    return bits ^ ((bits >> 31) & 0x7FFFFFFF)


_INT_MIN = -2 ** 31
_VALID_KEY = int(np.array(0.5 * NEG_INF, np.float32).view(np.int32))
_VALID_KEY = _VALID_KEY ^ ((_VALID_KEY >> 31) & 0x7FFFFFFF)


def _dsa_body(qc_ref, kc_ref, vct_ref, qi_ref, kiw_ref, wq_ref, o_ref,
              qa_scr, kx_scr, dneg_scr, qit_scr, key_scr, hi_scr, lo_scr,
              m_scr, acc_scr, *stage,
              tile, topk, slopes, seq):
    i = pl.program_id(1)
    nblk = i + 1
    kpos, qpos = _key_query_iotas(tile)

    @pl.when((pl.program_id(0) == 0) & (i == 0))
    def _():
        _init_alibi(slopes, dneg_scr, kx_scr, qa_scr, tile, 1)

    qit_scr[...] = _scaled_transpose(qi_ref[0], D_IDX ** -0.5)
    wt = wq_ref[0].T
    wrow = [wt[D_IDX + h:D_IDX + h + 1, :] * H_IDX ** -0.5 for h in range(H_IDX)]

    def score_block(j, diag):
        k0 = pl.multiple_of(j * tile, tile)
        kib = kiw_ref[0, pl.ds(k0, tile), :][:, :D_IDX].astype(BF16)
        isc = jnp.zeros((tile, tile), F32)
        for h in range(H_IDX):
            d = _dot(kib, qit_scr[h * D_IDX:(h + 1) * D_IDX, :])
            isc = isc + jnp.maximum(d, 0.0) * wrow[h]
        if diag:
            isc = jnp.where((kpos // CHUNK) <= (qpos // CHUNK), isc, NEG_INF)
        key = _sortable(isc)
        key_scr[pl.ds(k0, tile), :] = key
        hi_scr[pl.ds(k0, tile), :] = (key >> 16).astype(I16)
        lo_scr[pl.ds(k0, tile), :] = ((key & 0xFFFF) - 2 ** 15).astype(I16)

    def score_tiles(blocks):
        for j, diag in blocks:
            score_block(j, diag)

    _sweep_key_tiles(i, score_tiles)

    def count(pred):
        def body(j, part):
            k0 = pl.multiple_of(j * tile, tile)
            hit = jnp.where(pred(key_scr[pl.ds(k0, tile), :], kpos + k0), 1.0, 0.0)
            return part + jnp.sum(hit.reshape(tile // SUBLANES, SUBLANES, tile), axis=0)
        part = lax.fori_loop(0, nblk, body, jnp.zeros((SUBLANES, tile), F32))
        return jnp.sum(part, axis=0, keepdims=True)

    kf = float(topk)
    rows16 = 2 * SUBLANES

    pair = 2 if (seq // tile) % 2 == 0 else 1
    span = pair * tile
    lowest16 = jnp.full((tile, tile), -2 ** 15, I16)

    @pl.when(nblk % pair == 1)
    def _():
        pad_rows = pl.ds(pl.multiple_of(nblk * tile, tile), tile)
        hi_scr[pad_rows, :] = lowest16
        lo_scr[pad_rows, :] = lowest16

    def count16(src_ref, pred):
        def body(j, part):
            k0 = pl.multiple_of(j * span, span)
            hit = jnp.where(pred(src_ref[pl.ds(k0, span), :]), jnp.int16(1), jnp.int16(0))
            for g in range(span // rows16):
                part = part + hit[g * rows16:(g + 1) * rows16, :]
            return part
        part = lax.fori_loop(0, (nblk + pair - 1) // pair, body, jnp.zeros((rows16, tile), I16))
        return jnp.sum(part.astype(I32), axis=0, keepdims=True).astype(F32)

    def search16(src_ref, base):
        def bit(p, u):
            cand_u = u | jnp.left_shift(jnp.int32(1), 15 - p)
            cand = (cand_u - 2 ** 15).astype(I16)
            cnt = base + count16(src_ref, lambda v: v >= cand)
            return jnp.where(cnt >= kf, cand_u, u)
        return lax.fori_loop(0, 16, bit, jnp.zeros((1, tile), I32))

    thr_hi = search16(hi_scr, 0.0) - 2 ** 15
    thr_hi16 = thr_hi.astype(I16)
    above = count16(hi_scr, lambda v: v > thr_hi16)

    def mask_low(j, carry):
        k0 = pl.multiple_of(j * tile, tile)
        rows = pl.ds(k0, tile)
        lo_scr[rows, :] = jnp.where(hi_scr[rows, :] == thr_hi16, lo_scr[rows, :], jnp.int16(-2 ** 15))
        return carry

    lax.fori_loop(0, nblk, mask_low, 0)
    thr = jnp.left_shift(thr_hi, 16) + search16(lo_scr, above)

    thr_b = thr
    c_gt = count(lambda key, idx: key > thr_b)
    c_ge = count(lambda key, idx: key >= thr_b)
    tied = (c_ge > kf) & (thr > _VALID_KEY)
    need = kf - c_gt

    @pl.when(jnp.max(jnp.where(tied, 1.0, 0.0)) > 0.0)
    def _():
        nbits = max(1, int(math.ceil(math.log2(seq))))

        def index_bit(p, lo):
            cand = lo | jnp.left_shift(jnp.int32(1), nbits - 1 - p)
            cnt = count(lambda key, idx: (key == thr_b) & (idx < cand))
            return jnp.where(cnt < need, cand, lo)

        jstar = jnp.where(tied, lax.fori_loop(0, nbits, index_bit, jnp.zeros((1, tile), I32)), seq)

        def drop_late_ties(j, carry):
            rows = pl.ds(pl.multiple_of(j * tile, tile), tile)
            key = key_scr[rows, :]
            key_scr[rows, :] = jnp.where((key == thr_b) & (kpos + j * tile > jstar), thr_b - 1, key)
            return carry

        lax.fori_loop(0, nblk, drop_late_ties, 0)

    thr2 = jnp.maximum(thr, _VALID_KEY + 1)

    _store_query_maps(qa_scr, _scaled_transpose(qc_ref[0], 1.0), H_C, DH_C)
    m_scr[...] = jnp.full(m_scr.shape, NEG_INF, F32)
    acc_scr[...] = jnp.zeros(acc_scr.shape, F32)

    def attend(blocks):
        ctx = []
        for j, diag in blocks:
            k0 = pl.multiple_of(j * tile, tile)
            key = key_scr[pl.ds(k0, tile), :]
            sel = key >= thr2
            ctx.append((diag, sel, kc_ref[0, pl.ds(k0, tile), :], vct_ref[0, j], ((i - j) * tile).astype(F32)))

        def scores(idx):
            (diag, sel, kb, _, _), h = ctx[idx // H_C], idx % H_C
            pair = h // 2
            keys = jnp.concatenate([kb[:, pair * LANES:(pair + 1) * LANES], kx_scr[h]], axis=1)
            s = _dot_row_chunks(keys, qa_scr[h])
            if diag:
                s = s + dneg_scr[h]
            return jnp.where(sel, s, MASKED)

        def probs(idx, t_ref, tmax_ref, p_ref, a_ref):
            (diag, _, _, _, off), h = ctx[idx // H_C], idx % H_C
            shift = 0.0 if diag else (slopes[h] * LOG2E) * off
            _softmax_probs(shift, t_ref, tmax_ref, p_ref, a_ref, m_scr.at[h])

        def accumulate(idx, p_ref, a_ref):
            vtb, h = ctx[idx // H_C][3], idx % H_C
            _accumulate(vtb[h * DH_C:(h + 1) * DH_C, :], p_ref, a_ref, acc_scr.at[h])

        _pipelined_maps(H_C * len(blocks), scores, probs, accumulate, stage)

    _sweep_key_tiles(i, attend)
    outs = [_normalised(acc_scr[h], DH_C).T for h in range(H_C)]
    o_ref[0] = jnp.concatenate(outs, axis=-1).astype(BF16)


def _dsa_attention(proj, vt, kiw, slopes):
    b, s, _ = proj.shape
    tile = min(ATT_TILE, s)
    nkb = s // tile
    w = H_C * DH_C
    wa = H_A * DV_A
    topk = min(TOPK_MAX, s // 4)
    return pl.pallas_call(
        functools.partial(_dsa_body, tile=tile, topk=topk, slopes=slopes, seq=s),
        grid=(b, nkb),
        in_specs=[pl.BlockSpec((1, tile, w), lambda bi, i: (bi, i, 8)),
                  pl.BlockSpec((1, s, w), lambda bi, i: (bi, 0, 9)),
                  pl.BlockSpec((1, nkb, w, tile), lambda bi, i: (bi, 0, wa // w, 0)),
                  pl.BlockSpec((1, tile, H_IDX * D_IDX), lambda bi, i: (bi, i, 2)),
                  pl.BlockSpec((1, s, LANES), lambda bi, i: (bi, 0, 0)),
                  pl.BlockSpec((1, tile, LANES), lambda bi, i: (bi, i, 0))],
        out_specs=pl.BlockSpec((1, tile, w), lambda bi, i: (bi, i, 0)),
        out_shape=jax.ShapeDtypeStruct((b, s, w), BF16),
        scratch_shapes=[pltpu.VMEM((H_C, 2 * LANES, tile), BF16),
                        pltpu.VMEM((H_C, tile, LANES), BF16),
                        pltpu.VMEM((H_C, tile, tile), F32),
                        pltpu.VMEM((H_IDX * D_IDX, tile), BF16),
                        pltpu.VMEM((s, tile), I32),
                        pltpu.VMEM((s, tile), I16),
                        pltpu.VMEM((s, tile), I16),
                        pltpu.VMEM((H_C, 1, tile), F32),
                        pltpu.VMEM((H_C, DH_C + SUM_ROWS, tile), F32)] + _stage_scratch(tile),
        compiler_params=_params("arbitrary", "arbitrary"),
    )(proj, proj, vt, proj, kiw, kiw)


def _xattn_body(x_ref, res_ref, wq_ref, k_ref, v_ref, wo_ref, g_ref, b_ref, of_ref, ob_ref, *, dh, alpha):
    q_all = _dot(x_ref[0], wq_ref[...]).astype(BF16)
    outs = []
    for h in range(H_X):
        c = h * dh
        q = q_all[:, c:c + dh] * dh ** -0.5
        s = _dot_nt(q, k_ref[0, :, c:c + dh])
        p = jnp.exp(s - jnp.max(s, axis=-1, keepdims=True))
        l = jnp.sum(p, axis=-1, keepdims=True)
        outs.append((_dot(p.astype(BF16), v_ref[0, :, c:c + dh]) / l).astype(BF16))
    y = _dot(jnp.concatenate(outs, axis=-1), wo_ref[...])
    out = _layer_norm_rows(alpha * res_ref[0] + y, g_ref[...], b_ref[...])
    of_ref[0] = out
    ob_ref[0] = out.astype(BF16)


def _xattn_ln(x, res, wq, kv, wo, g, b, alpha, tq=512):
    bsz, s, d = x.shape
    n_mem = kv.shape[1]
    tq = min(tq, s)
    tile = pl.BlockSpec((1, tq, d), lambda bi, i: (bi, i, 0))
    whole = lambda shape: pl.BlockSpec(shape, lambda bi, i: (0,) * len(shape))
    mem = lambda col: pl.BlockSpec((1, n_mem, d), lambda bi, i: (bi, 0, col))
    return pl.pallas_call(
        functools.partial(_xattn_body, dh=d // H_X, alpha=alpha),
        grid=(bsz, s // tq),
        in_specs=[tile, tile, whole((d, d)), mem(0), mem(1), whole((d, d)), whole((1, d)), whole((1, d))],
        out_specs=[tile, tile],
        out_shape=[jax.ShapeDtypeStruct((bsz, s, d), F32), jax.ShapeDtypeStruct((bsz, s, d), BF16)],
        compiler_params=_params("parallel", "parallel"),
    )(x, res, wq, kv, kv, wo, g.reshape(1, d), b.reshape(1, d))


def _router_body(x_ref, r_ref, g_ref):
    logits = _dot(x_ref[...], r_ref[...])
    lane = lax.broadcasted_iota(I32, logits.shape, 1).astype(F32)
    big = float(LANES)
    lg = jnp.where(lane < N_EXPERTS, logits, -jnp.inf)
    m1 = jnp.max(lg, axis=-1, keepdims=True)
    i1 = jnp.min(jnp.where(lg == m1, lane, big), axis=-1, keepdims=True)
    lg2 = jnp.where(lane == i1, -jnp.inf, lg)
    m2 = jnp.max(lg2, axis=-1, keepdims=True)
    i2 = jnp.min(jnp.where(lg2 == m2, lane, big), axis=-1, keepdims=True)
    e = jnp.exp(m2 - m1)
    den = 1.0 + e
    g_ref[...] = jnp.where(lane == i1, 1.0 / den, 0.0) + jnp.where(lane == i2, e / den, 0.0)


def _router_gates(x, router, tm=1024):
    m, d = x.shape
    tm = min(tm, m)
    rpad = jnp.zeros((d, LANES), BF16).at[:, :N_EXPERTS].set(router.astype(BF16))
    return pl.pallas_call(
        _router_body,
        grid=(m // tm,),
        in_specs=[pl.BlockSpec((tm, d), lambda i: (i, 0)),
                  pl.BlockSpec((d, LANES), lambda i: (0, 0))],
        out_specs=pl.BlockSpec((tm, LANES), lambda i: (i, 0)),
        out_shape=jax.ShapeDtypeStruct((m, LANES), F32),
        compiler_params=_params("parallel"),
    )(x, rpad)


MOE_BLOCK = 896
MOE_CHUNK = 128
MOE_GROUP = 2
MOE_FF_TILE = 1792


def _moe_body(x_ref, gate_ref, tri_ref, w1_ref, w3_ref, w2_ref, o_ref,
              xs_scr, acc_scr, rcol_scr, rrow_scr, cnt_smem, *, n_tokens):
    e = pl.program_id(1)
    f = pl.program_id(2)
    nf = pl.num_programs(2)
    tb = x_ref.shape[0]
    ch = MOE_CHUNK
    gw = MOE_GROUP * ch

    tok0 = pl.program_id(0) * tb
    valid_col = lax.broadcasted_iota(I32, (tb, 1), 0) + tok0 < n_tokens

    @pl.when((e == 0) & (f == 0))
    def _():
        o_ref[...] = jnp.zeros(o_ref.shape, F32)
        gate = jnp.where(valid_col, gate_ref[...], 0.0)
        member = jnp.where(gate > 0.0, 1.0, 0.0)
        rank = _dot(tri_ref[...], member.astype(BF16))
        rcol = jnp.where(member > 0.0, rank, -1.0)
        rcol_scr[...] = rcol
        rrow_scr[...] = rcol.T
        counts = jnp.sum(member, axis=0, keepdims=True)
        for ee in range(N_EXPERTS):
            cnt_smem[ee] = counts[0, ee].astype(I32)

    count = cnt_smem[e]
    single = count <= gw
    nchunk = jnp.where(single, MOE_GROUP, (count + (ch - 1)) // ch)

    @pl.when(f == 0)
    def _():
        sub = lax.broadcasted_iota(I32, (SUBLANES, tb), 0)
        r_row = jnp.sum(jnp.where(sub == e, rrow_scr[0:SUBLANES, :], 0.0), axis=0, keepdims=True)
        xz = jnp.where(valid_col, x_ref[...], jnp.zeros((), BF16))

        def gather_rows(r0, rows):
            row_id = lax.broadcasted_iota(I32, (rows, tb), 0).astype(F32) + jnp.asarray(r0, F32)
            sel = jnp.where(row_id == r_row, 1.0, 0.0).astype(BF16)
            xs_scr[pl.ds(r0, rows), :] = _dot(sel, xz).astype(BF16)

        @pl.when(single)
        def _():
            acc_scr[0:gw, :] = jnp.zeros((gw, acc_scr.shape[1]), F32)
            gather_rows(0, gw)

        @pl.when(jnp.logical_not(single))
        def _():
            acc_scr[...] = jnp.zeros(acc_scr.shape, F32)

            def gather(c, carry):
                gather_rows(pl.multiple_of(c * ch, ch), ch)
                return carry
            lax.fori_loop(0, nchunk, gather, 0)

    def ffn_rows(r0, rows):
        xs = xs_scr[pl.ds(r0, rows), :]
        a = _dot(xs, w1_ref[0])
        b = _dot(xs, w3_ref[0])
        acc_scr[pl.ds(r0, rows), :] += _dot((_silu(a) * b).astype(BF16), w2_ref[0])

    @pl.when(single)
    def _():
        ffn_rows(0, gw)

    @pl.when(jnp.logical_not(single))
    def _():
        def ffn(c, carry):
            ffn_rows(pl.multiple_of(c * ch, ch), ch)
            return carry
        lax.fori_loop(0, nchunk, ffn, 0)

    @pl.when(f == nf - 1)
    def _():
        pick = lax.broadcasted_iota(I32, (tb, LANES), 1) == e
        r_col = jnp.sum(jnp.where(pick, rcol_scr[...], 0.0), axis=1, keepdims=True)
        g_col = jnp.sum(jnp.where(pick & valid_col, gate_ref[...], 0.0), axis=1, keepdims=True)
        col_id = lax.broadcasted_iota(I32, (tb, gw), 1).astype(F32)

        def combine(gi, carry):
            r0 = pl.multiple_of(gi * gw, gw)
            selt = jnp.where(col_id + r0.astype(F32) == r_col, 1.0, 0.0).astype(BF16)
            y = acc_scr[pl.ds(r0, gw), :]
            y_hi = y.astype(BF16)
            y_lo = (y - y_hi.astype(F32)).astype(BF16)
            o_ref[...] += g_col * (_dot(selt, y_hi) + _dot(selt, y_lo))
            return carry

        lax.fori_loop(0, (nchunk + (MOE_GROUP - 1)) // MOE_GROUP, combine, 0)


def _moe_experts(x, gates, w1, w3, w2):
    m, d = x.shape
    ne, _, ff = w1.shape
    tb = min(MOE_BLOCK, m)
    tf = MOE_FF_TILE if ff % MOE_FF_TILE == 0 else ff
    gw = MOE_GROUP * MOE_CHUNK
    cap = pl.cdiv(tb, gw) * gw
    assert tb % LANES == 0 and ne <= SUBLANES
    tri = jnp.asarray(np.tril(np.ones((tb, tb), np.float32), -1), BF16)
    row = lambda i, e, f: (i, 0)
    return pl.pallas_call(
        functools.partial(_moe_body, n_tokens=m),
        grid=(pl.cdiv(m, tb), ne, ff // tf),
        in_specs=[pl.BlockSpec((tb, d), row),
                  pl.BlockSpec((tb, LANES), row),
                  pl.BlockSpec((tb, tb), lambda i, e, f: (0, 0)),
                  pl.BlockSpec((1, d, tf), lambda i, e, f: (e, 0, f)),
                  pl.BlockSpec((1, d, tf), lambda i, e, f: (e, 0, f)),
                  pl.BlockSpec((1, tf, d), lambda i, e, f: (e, f, 0))],
        out_specs=pl.BlockSpec((tb, d), row),
        out_shape=jax.ShapeDtypeStruct((m, d), F32),
        scratch_shapes=[pltpu.VMEM((cap, d), BF16),
                        pltpu.VMEM((cap, d), F32),
                        pltpu.VMEM((tb, LANES), F32),
                        pltpu.VMEM((LANES, tb), F32),
                        pltpu.SMEM((N_EXPERTS,), I32)],
        compiler_params=_params("parallel", "arbitrary", "arbitrary"),
    )(x, gates, tri, w1, w3, w2)


def _res_ln_body(res_ref, y_ref, g_ref, b_ref, of_ref, ob_ref, *, alpha):
    out = _layer_norm_rows(alpha * res_ref[...] + y_ref[...], g_ref[...], b_ref[...])
    of_ref[...] = out
    ob_ref[...] = out.astype(BF16)


def _residual_ln(res, y, g, b, alpha, tm=512):
    m, d = res.shape
    tm = min(tm, m)
    row = pl.BlockSpec((tm, d), lambda i: (i, 0))
    vec = pl.BlockSpec((1, d), lambda i: (0, 0))
    return pl.pallas_call(
        functools.partial(_res_ln_body, alpha=alpha),
        grid=(m // tm,),
        in_specs=[row, row, vec, vec],
        out_specs=[row, row],
        out_shape=[jax.ShapeDtypeStruct((m, d), F32), jax.ShapeDtypeStruct((m, d), BF16)],
        compiler_params=_params("parallel"),
    )(res, y, g.reshape(1, d), b.reshape(1, d))


def _band_bias(table):
    band = (LEFT_CHUNKS + 1) * CHUNK
    left = LEFT_CHUNKS * CHUNK
    rev = table.astype(F32)[:, ::-1]
    pad = left - REL_MAX + CHUNK - 1
    n_rev = band - 1 - (left - REL_MAX) + 1
    assert n_rev <= 2 * REL_MAX + 1
    ext = jnp.concatenate([jnp.broadcast_to(rev[:, :1], (rev.shape[0], pad)), rev[:, :n_rev]], axis=1)
    n_ext = ext.shape[1]
    period = jnp.concatenate([ext[:, CHUNK - 1:], jnp.zeros((ext.shape[0], 1), F32), ext[:, :CHUNK - 1]], axis=1)
    skew = jnp.tile(period, (1, CHUNK))[:, :CHUNK * n_ext].reshape(-1, CHUNK, n_ext)
    return skew[:, :, :band]


def kernel(x, mem, w_in, lam_q1, lam_k1, lam_q2, lam_k2, diff_norm_g, rel_bias, w_out,
           ln1_g, ln1_b, xattn_wq, xattn_wk, xattn_wv, xattn_wo, ln2_g, ln2_b,
           ffn_w1, ffn_w3, ffn_w2, moe_router, moe_w1, moe_w3, moe_w2, ln3_g, ln3_b):
    b, s, d = x.shape
    depth = w_in.shape[0]
    t = b * s
    alpha = (2.0 * depth) ** 0.25
    slopes_a, slopes_c = _alibi_slopes()
    n_main = 3 * H_A * DV_A + 3 * H_B * DH_B + 3 * H_C * DH_C + H_IDX * D_IDX
    n_tail = D_IDX + H_IDX
    wa, wb, wc = H_A * DV_A, H_B * DH_B, H_C * DH_C
    left = LEFT_CHUNKS * CHUNK

    h = x.reshape(t, d)
    hb = h.astype(BF16)
    memb = mem.reshape(-1, d).astype(BF16)
    for l in range(depth):
        wl = w_in[l]
        o_b, o_c, o_i = 3 * wa, 3 * wa + 3 * wb, 3 * wa + 3 * wb + 3 * wc
        w_main = jnp.concatenate(
            [wl[:, :wa] * (DQK_A ** -0.5 * LOG2E), wl[:, wa:2 * wa], wl[:, o_i:o_i + H_IDX * D_IDX],
             wl[:, o_b:o_b + wb] * (DH_B ** -0.5 * LOG2E), wl[:, o_b + wb:o_b + 2 * wb],
             wl[:, o_c:o_c + wc] * (DH_C ** -0.5 * LOG2E), wl[:, o_c + wc:o_c + 2 * wc]], axis=1).astype(BF16)
        proj = _matmul(hb, w_main, BF16, tn=w_main.shape[1] // 2).reshape(b, s, -1)
        w_side = jnp.concatenate(
            [wl[:, 2 * wa:3 * wa], wl[:, o_c + 2 * wc:o_c + 3 * wc], wl[:, o_b + 2 * wb:o_b + 3 * wb],
             wl[:, n_main:], jnp.zeros((d, LANES - n_tail), F32)], axis=1).astype(BF16)
        tile = min(ATT_TILE, s)
        vt, vbt, kiw = _side_projections(hb, w_side, wa + wc, tile, wb, LANES)
        vt = vt.reshape(b, s // tile, wa + wc, tile)
        kiw = kiw.reshape(b, s, LANES)
        lam_init = 0.8 - 0.6 * math.exp(-0.3 * l)
        ya = _diff_attention(proj, vt, lam_q1[l], lam_k1[l], lam_q2[l], lam_k2[l], diff_norm_g[l],
                             slopes_a, lam_init)
        kpad = jnp.pad(proj[:, :, 3 * wa + wb:3 * wa + 2 * wb], ((0, 0), (left, 0), (0, 0)))
        vbt = jnp.pad(vbt.reshape(b, s // LANES, wb, LANES), ((0, 0), (left // LANES, 0), (0, 0), (0, 0)))
        yb = _band_attention(proj, kpad, vbt, _band_bias_table(rel_bias[l]))
        yc = _dsa_attention(proj, vt, kiw, slopes_c)
        wo = w_out[l].astype(BF16)
        h, hb = _matmul_ln(
            [ya.reshape(t, wa), yb.reshape(t, wb), yc.reshape(t, -1)],
            [wo[:wa], wo[wa:wa + wb], wo[wa + wb:]],
            h, ln1_g[l], ln1_b[l], alpha)
        w_kv = jnp.concatenate([xattn_wk[l], xattn_wv[l]], axis=1).astype(BF16)
        kv = _matmul(memb, w_kv, BF16, tn=d).reshape(b, -1, 2 * d)
        h, hb = _xattn_ln(hb.reshape(b, s, d), h.reshape(b, s, d), xattn_wq[l].astype(BF16), kv,
                          xattn_wo[l].astype(BF16), ln2_g[l], ln2_b[l], alpha)
        h, hb = h.reshape(t, d), hb.reshape(t, d)
        if l % 2 == 0:
            h, hb = _ffn_ln(hb, ffn_w1[l // 2].astype(BF16), ffn_w3[l // 2].astype(BF16),
                            ffn_w2[l // 2].astype(BF16), h, ln3_g[l], ln3_b[l], alpha)
        else:
            gates = _router_gates(hb, moe_router[l // 2])
            y = _moe_experts(hb, gates, moe_w1[l // 2].astype(BF16), moe_w3[l // 2].astype(BF16),
                             moe_w2[l // 2].astype(BF16))
            h, hb = _residual_ln(h, y, ln3_g[l], ln3_b[l], alpha)
    return h.reshape(b, s, d)
```

```python
import functools
import math

import numpy as np
import jax
import jax.numpy as jnp
from jax import lax
from jax.experimental import pallas as pl
from jax.experimental.pallas import tpu as pltpu

F32 = jnp.float32
BF16 = jnp.bfloat16
I32 = jnp.int32
I16 = jnp.int16

CHUNK = 64
H_A, DQK_A = 4, 64
DV_A = 2 * DQK_A
H_B, DH_B = 4, 64
LEFT_CHUNKS = 8
REL_MAX = 128
H_C, DH_C = 4, 64
H_IDX, D_IDX = 8, 64
TOPK_MAX = 256
H_X = 4
N_EXPERTS = 8
LN_EPS = 1e-5
RMS_EPS = 1e-6
NEG_INF = -1e30
MASKED = -2e30

LANES = 128
SUBLANES = 8
VMEM_LIMIT = 56 * 1024 * 1024
ATT_TILE = 256

_NT = (((1,), (1,)), ((), ()))


def _alibi_slopes():
    n = H_A + H_C
    s = (2.0 ** (-8.0 * np.arange(1, n + 1) / n)).astype(np.float32)
    return [float(v) for v in s[0::2]], [float(v) for v in s[1::2]]


def _params(*sem):
    return pltpu.CompilerParams(dimension_semantics=sem, vmem_limit_bytes=VMEM_LIMIT)


def _dot(a, b):
    return jnp.dot(a, b, preferred_element_type=F32)


SCORE_ROW_CHUNK = 128


def _dot_row_chunks(a, b):
    rows = a.shape[0]
    step = min(SCORE_ROW_CHUNK, rows)
    return jnp.concatenate([_dot(a[r:r + step], b) for r in range(0, rows, step)], axis=0)


def _dot_nt(a, b):
    return lax.dot_general(a, b, _NT, preferred_element_type=F32)


def _layer_norm_rows(y, g, b):
    mu = jnp.mean(y, axis=-1, keepdims=True)
    d = y - mu
    var = jnp.mean(d * d, axis=-1, keepdims=True)
    return d * lax.rsqrt(var + LN_EPS) * g + b


def _silu(a):
    return a / (1.0 + jnp.exp(-a))


def _mm_body(x_ref, w_ref, o_ref):
    o_ref[...] = _dot(x_ref[...], w_ref[...]).astype(o_ref.dtype)


def _matmul(x, w, out_dtype, tm=1024, tn=512):
    m, k = x.shape
    n = w.shape[1]
    tm, tn = min(tm, m), min(tn, n)
    return pl.pallas_call(
        _mm_body,
        grid=(m // tm, n // tn),
        in_specs=[pl.BlockSpec((tm, k), lambda i, j: (i, 0)),
                  pl.BlockSpec((k, tn), lambda i, j: (0, j))],
        out_specs=pl.BlockSpec((tm, tn), lambda i, j: (i, j)),
        out_shape=jax.ShapeDtypeStruct((m, n), out_dtype),
        compiler_params=_params("parallel", "parallel"),
    )(x, w)


def _side_proj_body(x_ref, w_ref, vt_ref, vbt_ref, tail_ref):
    y = _dot(x_ref[...], w_ref[...])
    n_v, n_vb = vt_ref.shape[1], vbt_ref.shape[1]
    tile, tile_b = vt_ref.shape[2], vbt_ref.shape[2]
    for r in range(vt_ref.shape[0]):
        vt_ref[r] = y[r * tile:(r + 1) * tile, :n_v].T.astype(BF16)
    for r in range(vbt_ref.shape[0]):
        vbt_ref[r] = y[r * tile_b:(r + 1) * tile_b, n_v:n_v + n_vb].T.astype(BF16)
    tail_ref[...] = y[:, n_v + n_vb:]


def _side_projections(x, w, n_v, tile, n_vb, tile_b, tm=1024):
    m, k = x.shape
    n = w.shape[1]
    n_tail = n - n_v - n_vb
    tm = min(tm, m)
    return pl.pallas_call(
        _side_proj_body,
        grid=(m // tm,),
        in_specs=[pl.BlockSpec((tm, k), lambda i: (i, 0)),
                  pl.BlockSpec((k, n), lambda i: (0, 0))],
        out_specs=[pl.BlockSpec((tm // tile, n_v, tile), lambda i: (i, 0, 0)),
                   pl.BlockSpec((tm // tile_b, n_vb, tile_b), lambda i: (i, 0, 0)),
                   pl.BlockSpec((tm, n_tail), lambda i: (i, 0))],
        out_shape=[jax.ShapeDtypeStruct((m // tile, n_v, tile), BF16),
                   jax.ShapeDtypeStruct((m // tile_b, n_vb, tile_b), BF16),
                   jax.ShapeDtypeStruct((m, n_tail), F32)],
        compiler_params=_params("parallel"),
    )(x, w)


def _mm_ln_body(*refs, n_in, alpha):
    xs, ws = refs[:n_in], refs[n_in:2 * n_in]
    res_ref, g_ref, b_ref, of_ref, ob_ref = refs[2 * n_in:]
    acc = _dot(xs[0][...], ws[0][...])
    for x_ref, w_ref in zip(xs[1:], ws[1:]):
        acc = acc + _dot(x_ref[...], w_ref[...])
    out = _layer_norm_rows(alpha * res_ref[...] + acc, g_ref[...], b_ref[...])
    of_ref[...] = out
    ob_ref[...] = out.astype(BF16)


def _matmul_ln(xs, ws, res, g, b, alpha):
    m, d = res.shape
    k_total = sum(x.shape[1] for x in xs)
    tm = min(1024 if k_total <= 1024 else 512, m)
    n_in = len(xs)
    in_specs = ([pl.BlockSpec((tm, x.shape[1]), lambda i: (i, 0)) for x in xs]
                + [pl.BlockSpec(w.shape, lambda i: (0, 0)) for w in ws]
                + [pl.BlockSpec((tm, d), lambda i: (i, 0)),
                   pl.BlockSpec((1, d), lambda i: (0, 0)),
                   pl.BlockSpec((1, d), lambda i: (0, 0))])
    return pl.pallas_call(
        functools.partial(_mm_ln_body, n_in=n_in, alpha=alpha),
        grid=(m // tm,),
        in_specs=in_specs,
        out_specs=[pl.BlockSpec((tm, d), lambda i: (i, 0)),
                   pl.BlockSpec((tm, d), lambda i: (i, 0))],
        out_shape=[jax.ShapeDtypeStruct((m, d), F32), jax.ShapeDtypeStruct((m, d), BF16)],
        compiler_params=_params("parallel"),
    )(*xs, *ws, res, g.reshape(1, d), b.reshape(1, d))


def _ffn_ln_body(x_ref, w1_ref, w3_ref, w2_ref, res_ref, g_ref, b_ref, of_ref, ob_ref, acc_ref, *, alpha):
    f = pl.program_id(1)
    x = x_ref[...]
    a = _dot(x, w1_ref[...])
    b = _dot(x, w3_ref[...])
    y = _dot((_silu(a) * b).astype(BF16), w2_ref[...])

    @pl.when(f == 0)
    def _():
        acc_ref[...] = y

    @pl.when(f > 0)
    def _():
        acc_ref[...] += y

    @pl.when(f == pl.num_programs(1) - 1)
    def _():
        out = _layer_norm_rows(alpha * res_ref[...] + acc_ref[...], g_ref[...], b_ref[...])
        of_ref[...] = out
        ob_ref[...] = out.astype(BF16)


def _ffn_ln(x, w1, w3, w2, res, g, b, alpha, tm=512, tf=1408):
    m, d = x.shape
    ff = w1.shape[1]
    tm = min(tm, m)
    if ff % tf:
        tf = ff
    row = pl.BlockSpec((tm, d), lambda i, f: (i, 0))
    vec = pl.BlockSpec((1, d), lambda i, f: (0, 0))
    return pl.pallas_call(
        functools.partial(_ffn_ln_body, alpha=alpha),
        grid=(m // tm, ff // tf),
        in_specs=[row,
                  pl.BlockSpec((d, tf), lambda i, f: (0, f)),
                  pl.BlockSpec((d, tf), lambda i, f: (0, f)),
                  pl.BlockSpec((tf, d), lambda i, f: (f, 0)),
                  row, vec, vec],
        out_specs=[row, row],
        out_shape=[jax.ShapeDtypeStruct((m, d), F32), jax.ShapeDtypeStruct((m, d), BF16)],
        scratch_shapes=[pltpu.VMEM((tm, d), F32)],
        compiler_params=_params("parallel", "arbitrary"),
    )(x, w1, w3, w2, res, g.reshape(1, d), b.reshape(1, d))


def _key_query_iotas(tile):
    kpos = lax.broadcasted_iota(I32, (tile, tile), 0)
    qpos = lax.broadcasted_iota(I32, (tile, tile), 1)
    return kpos, qpos


def _scaled_transpose(x, scale):
    xf = x.astype(F32)
    return (xf if scale == 1.0 else xf * scale).T.astype(BF16)


LOG2E = math.log2(math.e)
ALIBI_PARTS = 3
ALIBI_ROWS = 16


def _bf16_parts(x):
    parts, r = [], np.float32(x)
    for _ in range(ALIBI_PARTS):
        p = np.float32(r.astype(jnp.bfloat16))
        parts.append(float(p))
        r = np.float32(r - p)
    return parts


def _alibi_key_cols(tile, parts):
    lane = lax.broadcasted_iota(I32, (tile, LANES), 1)
    kr = lax.broadcasted_iota(I32, (tile, LANES), 0).astype(F32)
    v = jnp.where((lane >= ALIBI_PARTS) & (lane < 2 * ALIBI_PARTS), kr, 0.0)
    for n, c in enumerate(parts):
        v = jnp.where(lane == n, c, v)
    return v.astype(BF16)


def _alibi_query_rows(tile, parts):
    row = lax.broadcasted_iota(I32, (ALIBI_ROWS, tile), 0)
    qr = lax.broadcasted_iota(I32, (ALIBI_ROWS, tile), 1).astype(F32)
    v = jnp.where(row < ALIBI_PARTS, -qr, 0.0)
    for n, c in enumerate(parts):
        v = jnp.where(row == ALIBI_PARTS + n, c, v)
    return v.astype(BF16)


def _init_alibi(slopes, dneg_scr, kx_scr, qa_scr, tile, maps_per_head):
    kpos, qpos = _key_query_iotas(tile)
    rel = (qpos - kpos).astype(F32)
    qa_scr[...] = jnp.zeros(qa_scr.shape, BF16)
    for h, slope in enumerate(slopes):
        parts = _bf16_parts(slope * LOG2E)
        dneg_scr[h] = 2.0 * jnp.minimum(np.float32(slope * LOG2E) * rel, 0.0)
        kx_scr[h] = _alibi_key_cols(tile, parts)
        for mm in range(maps_per_head):
            qa_scr[h * maps_per_head + mm, LANES:LANES + ALIBI_ROWS, :] = _alibi_query_rows(tile, parts)


def _store_query_maps(qa_scr, qt, n_maps, dh):
    for idx in range(n_maps):
        r0 = (idx % 2) * dh
        qa_scr[idx, r0:r0 + dh, :] = qt[idx * dh:(idx + 1) * dh, :]


def _softmax_probs(c, t_ref, tmax_ref, p_ref, a_ref, m_ref):
    m_old = m_ref[...]
    m_new = jnp.maximum(m_old, tmax_ref[...] - c)
    m_ref[...] = m_new
    a_ref[...] = jnp.exp2(m_old - m_new)
    p_ref[...] = jnp.exp2(t_ref[...] - (m_new + c)).astype(BF16)


SUM_ROWS = 16


def _accumulate(vt, p_ref, a_ref, acc_ref):
    lhs = jnp.concatenate([vt, jnp.ones((SUM_ROWS, vt.shape[1]), BF16)], axis=0)
    acc_ref[...] = a_ref[...] * acc_ref[...] + _dot(lhs, p_ref[...])


def _normalised(acc, features):
    return acc[:features] / acc[features:features + 1]


SCORE_LEAD, PROB_LEAD = 4, 2
N_T_SLOTS, N_P_SLOTS = SCORE_LEAD - PROB_LEAD + 2, PROB_LEAD + 1


def _stage_scratch(tile, keys=None):
    keys = tile if keys is None else keys
    return ([pltpu.VMEM((keys, tile), F32)] * N_T_SLOTS + [pltpu.VMEM((1, tile), F32)] * N_T_SLOTS
            + [pltpu.VMEM((keys, tile), BF16)] * N_P_SLOTS + [pltpu.VMEM((1, tile), F32)] * N_P_SLOTS)


def _pipelined_maps(n_maps, scores, probs, accumulate, stage):
    nt, npb = N_T_SLOTS, N_P_SLOTS
    t_slots, x_slots = stage[:nt], stage[nt:2 * nt]
    p_slots, a_slots = stage[2 * nt:2 * nt + npb], stage[2 * nt + npb:]

    def stage_scores(k):
        s = scores(k)
        t_slots[k % nt][...] = s
        x_slots[k % nt][...] = jnp.max(s, axis=0, keepdims=True)

    def stage_probs(k):
        probs(k, t_slots[k % nt], x_slots[k % nt], p_slots[k % npb], a_slots[k % npb])

    for step in range(-SCORE_LEAD, n_maps):
        if 0 <= step + SCORE_LEAD < n_maps:
            stage_scores(step + SCORE_LEAD)
        if 0 <= step + PROB_LEAD < n_maps:
            stage_probs(step + PROB_LEAD)
        if step >= 0:
            accumulate(step, p_slots[step % npb], a_slots[step % npb])


SWEEP_GROUP = 4


def _sweep_key_tiles(i, sweep):
    def body(g, carry):
        sweep([(SWEEP_GROUP * g + n, False) for n in range(SWEEP_GROUP)])
        return carry

    ngroups = i // SWEEP_GROUP
    lax.fori_loop(0, ngroups, body, 0)
    base = ngroups * SWEEP_GROUP
    for rem in range(SWEEP_GROUP):
        @pl.when(i - base == rem)
        def _():
            sweep([(base + n, False) for n in range(rem)] + [(i, True)])


def _diff_body(q_ref, k_ref, vt_ref, lq1_ref, lk1_ref, lq2_ref, lk2_ref, gain_ref, o_ref,
               qa_scr, kx_scr, dneg_scr, m_scr, acc_scr, *stage,
               tile, slopes, lam_init):
    i = pl.program_id(1)
    kpos, qpos = _key_query_iotas(tile)

    @pl.when((pl.program_id(0) == 0) & (i == 0))
    def _():
        _init_alibi(slopes, dneg_scr, kx_scr, qa_scr, tile, 2)

    _store_query_maps(qa_scr, _scaled_transpose(q_ref[0], 1.0), 2 * H_A, DQK_A)
    m_scr[...] = jnp.full(m_scr.shape, NEG_INF, F32)
    acc_scr[...] = jnp.zeros(acc_scr.shape, F32)

    n_maps = 2 * H_A

    def sweep(blocks):
        ctx = []
        for j, diag in blocks:
            k0 = pl.multiple_of(j * tile, tile)
            ctx.append((diag, k_ref[0, pl.ds(k0, tile), :], vt_ref[0, j], ((i - j) * tile).astype(F32)))

        def scores(n):
            (diag, kb, _, _), idx = ctx[n // n_maps], n % n_maps
            h, pair = idx // 2, idx // 2
            keys = jnp.concatenate([kb[:, pair * LANES:(pair + 1) * LANES], kx_scr[h]], axis=1)
            s = _dot_row_chunks(keys, qa_scr[idx])
            if diag:
                allowed = (kpos // CHUNK) <= (qpos // CHUNK)
                return jnp.where(allowed, s + dneg_scr[h], MASKED)
            return s

        def probs(n, t_ref, tmax_ref, p_ref, a_ref):
            (diag, _, _, off), idx = ctx[n // n_maps], n % n_maps
            shift = 0.0 if diag else (slopes[idx // 2] * LOG2E) * off
            _softmax_probs(shift, t_ref, tmax_ref, p_ref, a_ref, m_scr.at[idx])

        def accumulate(n, p_ref, a_ref):
            vtb, idx = ctx[n // n_maps][2], n % n_maps
            h = idx // 2
            _accumulate(vtb[h * DV_A:(h + 1) * DV_A, :], p_ref, a_ref, acc_scr.at[idx])

        _pipelined_maps(n_maps * len(blocks), scores, probs, accumulate, stage)

    _sweep_key_tiles(i, sweep)

    lam = (jnp.exp(jnp.sum(lq1_ref[...] * lk1_ref[...], axis=-1, keepdims=True))
           - jnp.exp(jnp.sum(lq2_ref[...] * lk2_ref[...], axis=-1, keepdims=True)) + lam_init)
    gain = gain_ref[...] * (1.0 - lam_init)
    outs = []
    for h in range(H_A):
        o = _normalised(acc_scr[2 * h], DV_A) - lam * _normalised(acc_scr[2 * h + 1], DV_A)
        o = o * lax.rsqrt(jnp.mean(o * o, axis=0, keepdims=True) + RMS_EPS)
        outs.append((o * gain).T)
    o_ref[0] = jnp.concatenate(outs, axis=-1).astype(BF16)


def _diff_attention(proj, vt, lq1, lk1, lq2, lk2, gain, slopes, lam_init):
    b, s, _ = proj.shape
    tile = min(ATT_TILE, s)
    nkb = s // tile
    w = H_A * DV_A
    vec = lambda a: a.reshape(1, -1).astype(F32)
    small = lambda n: pl.BlockSpec((1, n), lambda bi, i: (0, 0))
    return pl.pallas_call(
        functools.partial(_diff_body, tile=tile, slopes=slopes, lam_init=lam_init),
        grid=(b, nkb),
        in_specs=[pl.BlockSpec((1, tile, w), lambda bi, i: (bi, i, 0)),
                  pl.BlockSpec((1, s, w), lambda bi, i: (bi, 0, 1)),
                  pl.BlockSpec((1, nkb, w, tile), lambda bi, i: (bi, 0, 0, 0)),
                  small(DQK_A), small(DQK_A), small(DQK_A), small(DQK_A),
                  pl.BlockSpec((DV_A, 1), lambda bi, i: (0, 0))],
        out_specs=pl.BlockSpec((1, tile, w), lambda bi, i: (bi, i, 0)),
        out_shape=jax.ShapeDtypeStruct((b, s, w), BF16),
        scratch_shapes=[pltpu.VMEM((2 * H_A, 2 * LANES, tile), BF16),
                        pltpu.VMEM((H_A, tile, LANES), BF16),
                        pltpu.VMEM((H_A, tile, tile), F32),
                        pltpu.VMEM((2 * H_A, 1, tile), F32),
                        pltpu.VMEM((2 * H_A, DV_A + SUM_ROWS, tile), F32)] + _stage_scratch(tile),
        compiler_params=_params("arbitrary", "arbitrary"),
    )(proj, proj, vt, vec(lq1), vec(lk1), vec(lq2), vec(lk2), gain.reshape(-1, 1).astype(F32))


BAND_CHUNKS = LANES // CHUNK
BAND_WINDOW = (LEFT_CHUNKS + BAND_CHUNKS) * CHUNK


def _band_body(q_ref, k_ref, vt_ref, bias_ref, o_ref, qz_scr, ot_scr, *stage, n_sub):
    i = pl.program_id(1)
    ntile = BAND_WINDOW // LANES

    @pl.when((pl.program_id(0) == 0) & (i == 0))
    def _():
        qz_scr[...] = jnp.zeros(qz_scr.shape, BF16)

    kk = lax.broadcasted_iota(I32, (BAND_WINDOW, LANES), 0)
    ctx = []
    for u in range(n_sub):
        qt = _scaled_transpose(q_ref[0, u * LANES:(u + 1) * LANES, :], 1.0)
        for h in range(H_B):
            r0 = (h % 2) * DH_B
            qz_scr[u * H_B + h, r0:r0 + DH_B, :] = qt[h * DH_B:(h + 1) * DH_B, :]
        start = pl.multiple_of((i * n_sub + u) * LANES, LANES)
        ctx.append((k_ref[0, pl.ds(start, BAND_WINDOW), :],
                    vt_ref[0, pl.ds(i * n_sub + u, ntile)],
                    kk >= LEFT_CHUNKS * CHUNK - start))

    def scores(idx):
        (kw, _, valid), h = ctx[idx // H_B], idx % H_B
        pair = h // 2
        s = _dot_row_chunks(kw[:, pair * LANES:(pair + 1) * LANES], qz_scr[idx])
        return jnp.where(valid, s + bias_ref[h], MASKED)

    def probs(idx, t_ref, tmax_ref, p_ref, unused_ref):
        p_ref[...] = jnp.exp2(t_ref[...] - tmax_ref[...]).astype(BF16)

    def accumulate(idx, p_ref, unused_ref):
        vt, h = ctx[idx // H_B][1], idx % H_B
        vth = jnp.concatenate([vt[n, h * DH_B:(h + 1) * DH_B, :] for n in range(ntile)], axis=1)
        lhs = jnp.concatenate([vth, jnp.ones((SUM_ROWS, BAND_WINDOW), BF16)], axis=0)
        ot_scr[idx * DH_B:(idx + 1) * DH_B, :] = _normalised(_dot(lhs, p_ref[...]), DH_B)

    _pipelined_maps(n_sub * H_B, scores, probs, accumulate, stage)
    w = H_B * DH_B
    for u in range(n_sub):
        o_ref[0, u * LANES:(u + 1) * LANES, :] = ot_scr[u * w:(u + 1) * w, :].T.astype(BF16)


def _band_bias_table(table):
    bias = _band_bias(table).transpose(0, 2, 1) * LOG2E
    cols = [jnp.pad(bias, ((0, 0), (c * CHUNK, (BAND_CHUNKS - 1 - c) * CHUNK), (0, 0)), constant_values=MASKED)
            for c in range(BAND_CHUNKS)]
    return jnp.concatenate(cols, axis=2)


def _band_attention(proj, kpad, vt_pad, bias):
    b, s, _ = proj.shape
    w = H_B * DH_B
    sp = kpad.shape[1]
    nvt = vt_pad.shape[1]
    n_sub = 2 if s % (2 * LANES) == 0 else 1
    rows = n_sub * LANES
    return pl.pallas_call(
        functools.partial(_band_body, n_sub=n_sub),
        grid=(b, s // rows),
        in_specs=[pl.BlockSpec((1, rows, w), lambda bi, i: (bi, i, 6)),
                  pl.BlockSpec((1, sp, w), lambda bi, i: (bi, 0, 0)),
                  pl.BlockSpec((1, nvt, w, LANES), lambda bi, i: (bi, 0, 0, 0)),
                  pl.BlockSpec((H_B, BAND_WINDOW, LANES), lambda bi, i: (0, 0, 0))],
        out_specs=pl.BlockSpec((1, rows, w), lambda bi, i: (bi, i, 0)),
        out_shape=jax.ShapeDtypeStruct((b, s, w), BF16),
        scratch_shapes=[pltpu.VMEM((n_sub * H_B, LANES, LANES), BF16),
                        pltpu.VMEM((n_sub * w, LANES), F32)] + _stage_scratch(LANES, BAND_WINDOW),
        compiler_params=_params("arbitrary", "arbitrary"),
    )(proj, kpad, vt_pad, bias)


def _sortable(x):
    bits = pltpu.bitcast(x + 0.0, I32)
    return bits ^ ((bits >> 31) & 0x7FFFFFFF)


_VALID_KEY =int(np.array(0.5 * NEG_INF, np.float32).view(np.int32))
_VALID_KEY = _VALID_KEY ^ ((_VALID_KEY >> 31) & 0x7FFFFFFF)


def _dsa_body(qc_ref, kc_ref, vct_ref, qi_ref, kiw_ref, wq_ref, o_ref,
              qa_scr, kx_scr, dneg_scr, qit_scr, key_scr, hi_scr, lo_scr,
              m_scr, acc_scr, *stage,
              tile, topk, slopes, seq):
    i = pl.program_id(1)
    nblk = i + 1
    kpos, qpos = _key_query_iotas(tile)

    @pl.when((pl.program_id(0) == 0) & (i == 0))
    def _():
        _init_alibi(slopes, dneg_scr, kx_scr, qa_scr, tile, 1)

    qit_scr[...] = _scaled_transpose(qi_ref[0], D_IDX ** -0.5)
    wt = wq_ref[0].T
    wrow = [wt[D_IDX + h:D_IDX + h + 1, :] * H_IDX ** -0.5 for h in range(H_IDX)]

    def score_block(j, diag):
        k0 = pl.multiple_of(j * tile, tile)
        kib = kiw_ref[0, pl.ds(k0, tile), :][:, :D_IDX].astype(BF16)
        isc = jnp.zeros((tile, tile), F32)
        for h in range(H_IDX):
            d = _dot(kib, qit_scr[h * D_IDX:(h + 1) * D_IDX, :])
            isc = isc + jnp.maximum(d, 0.0) * wrow[h]
        if diag:
            isc = jnp.where((kpos // CHUNK) <= (qpos // CHUNK), isc, NEG_INF)
        key = _sortable(isc)
        key_scr[pl.ds(k0, tile), :] = key
        hi_scr[pl.ds(k0, tile), :] = (key >> 16).astype(I16)
        lo_scr[pl.ds(k0, tile), :] = ((key & 0xFFFF) - 2 ** 15).astype(I16)

    def score_tiles(blocks):
        for j, diag in blocks:
            score_block(j, diag)

    _sweep_key_tiles(i, score_tiles)

    def count(pred):
        def body(j, part):
            k0 = pl.multiple_of(j * tile, tile)
            hit = jnp.where(pred(key_scr[pl.ds(k0, tile), :], kpos + k0), 1.0, 0.0)
            return part + jnp.sum(hit.reshape(tile // SUBLANES, SUBLANES, tile), axis=0)
        part = lax.fori_loop(0, nblk, body, jnp.zeros((SUBLANES, tile), F32))
        return jnp.sum(part, axis=0, keepdims=True)

    kf = float(topk)
    rows16 = 2 * SUBLANES

    pair = 2 if (seq // tile) % 2 == 0 else 1
    span = pair * tile
    lowest16 = jnp.full((tile, tile), -2 ** 15, I16)

    @pl.when(nblk % pair == 1)
    def _():
        pad_rows = pl.ds(pl.multiple_of(nblk * tile, tile), tile)
        hi_scr[pad_rows, :] = lowest16
        lo_scr[pad_rows, :] = lowest16

    def count16(src_ref, pred):
        def body(j, part):
            k0 = pl.multiple_of(j * span, span)
            hit = jnp.where(pred(src_ref[pl.ds(k0, span), :]), jnp.int16(1), jnp.int16(0))
            for g in range(span // rows16):
                part = part + hit[g * rows16:(g + 1) * rows16, :]
            return part
        part = lax.fori_loop(0, (nblk + pair - 1) // pair, body, jnp.zeros((rows16, tile), I16))
        return jnp.sum(part.astype(I32), axis=0, keepdims=True).astype(F32)

    def search16(src_ref, base):
        def bit(p, u):
            cand_u = u | jnp.left_shift(jnp.int32(1), 15 - p)
            cand = (cand_u - 2 ** 15).astype(I16)
            cnt = base + count16(src_ref, lambda v: v >= cand)
            return jnp.where(cnt >= kf, cand_u, u)
        return lax.fori_loop(0, 16, bit, jnp.zeros((1, tile), I32))

    thr_hi = search16(hi_scr, 0.0) - 2 ** 15
    thr_hi16 = thr_hi.astype(I16)
    above = count16(hi_scr, lambda v: v > thr_hi16)

    def mask_low(j, carry):
        k0 = pl.multiple_of(j * tile, tile)
        rows = pl.ds(k0, tile)
        lo_scr[rows, :] = jnp.where(hi_scr[rows, :] == thr_hi16, lo_scr[rows, :], jnp.int16(-2 ** 15))
        return carry

    lax.fori_loop(0, nblk, mask_low, 0)
    thr = jnp.left_shift(thr_hi, 16) + search16(lo_scr, above)

    thr_b = thr
    c_gt = count(lambda key, idx: key > thr_b)
    c_ge = count(lambda key, idx: key >= thr_b)
    tied = (c_ge > kf) & (thr > _VALID_KEY)
    need = kf - c_gt

    @pl.when(jnp.max(jnp.where(tied, 1.0, 0.0)) > 0.0)
    def _():
        nbits = max(1, int(math.ceil(math.log2(seq))))

        def index_bit(p, lo):
            cand = lo | jnp.left_shift(jnp.int32(1), nbits - 1 - p)
            cnt = count(lambda key, idx: (key == thr_b) & (idx < cand))
            return jnp.where(cnt < need, cand, lo)

        jstar = jnp.where(tied, lax.fori_loop(0, nbits, index_bit, jnp.zeros((1, tile), I32)), seq)

        def drop_late_ties(j, carry):
            rows = pl.ds(pl.multiple_of(j * tile, tile), tile)
            key = key_scr[rows, :]
            key_scr[rows, :] = jnp.where((key == thr_b) & (kpos + j * tile > jstar), thr_b - 1, key)
            return carry

        lax.fori_loop(0, nblk, drop_late_ties, 0)

    thr2 = jnp.maximum(thr, _VALID_KEY + 1)

    _store_query_maps(qa_scr, _scaled_transpose(qc_ref[0], 1.0), H_C, DH_C)
    m_scr[...] = jnp.full(m_scr.shape, NEG_INF, F32)
    acc_scr[...] = jnp.zeros(acc_scr.shape, F32)

    def attend(blocks):
        ctx = []
        for j, diag in blocks:
            k0 = pl.multiple_of(j * tile, tile)
            key = key_scr[pl.ds(k0, tile), :]
            sel = key >= thr2
            ctx.append((diag, sel, kc_ref[0, pl.ds(k0, tile), :], vct_ref[0, j], ((i - j) * tile).astype(F32)))

        def scores(idx):
            (diag, sel, kb, _, _), h = ctx[idx // H_C], idx % H_C
            pair = h // 2
            keys = jnp.concatenate([kb[:, pair * LANES:(pair + 1) * LANES], kx_scr[h]], axis=1)
            s = _dot_row_chunks(keys, qa_scr[h])
            if diag:
                s = s + dneg_scr[h]
            return jnp.where(sel, s, MASKED)

        def probs(idx, t_ref, tmax_ref, p_ref, a_ref):
            (diag, _, _, _, off), h = ctx[idx // H_C], idx % H_C
            shift = 0.0 if diag else (slopes[h] * LOG2E) * off
            _softmax_probs(shift, t_ref, tmax_ref, p_ref, a_ref, m_scr.at[h])

        def accumulate(idx, p_ref, a_ref):
            vtb, h = ctx[idx // H_C][3], idx % H_C
            _accumulate(vtb[h * DH_C:(h + 1) * DH_C, :], p_ref, a_ref, acc_scr.at[h])

        _pipelined_maps(H_C * len(blocks), scores, probs, accumulate, stage)

    _sweep_key_tiles(i, attend)
    outs = [_normalised(acc_scr[h], DH_C).T for h in range(H_C)]
    o_ref[0] = jnp.concatenate(outs, axis=-1).astype(BF16)


def _dsa_attention(proj, vt, kiw, slopes):
    b, s, _ = proj.shape
    tile = min(ATT_TILE, s)
    nkb = s // tile
    w = H_C * DH_C
    wa = H_A * DV_A
    topk = min(TOPK_MAX, s // 4)
    return pl.pallas_call(
        functools.partial(_dsa_body, tile=tile, topk=topk, slopes=slopes, seq=s),
        grid=(b, nkb),
        in_specs=[pl.BlockSpec((1, tile, w), lambda bi, i: (bi, i, 8)),
                  pl.BlockSpec((1, s, w), lambda bi, i: (bi, 0, 9)),
                  pl.BlockSpec((1, nkb, w, tile), lambda bi, i: (bi, 0, wa // w, 0)),
                  pl.BlockSpec((1, tile, H_IDX * D_IDX), lambda bi, i: (bi, i, 2)),
                  pl.BlockSpec((1, s, LANES), lambda bi, i: (bi, 0, 0)),
                  pl.BlockSpec((1, tile, LANES), lambda bi, i: (bi, i, 0))],
        out_specs=pl.BlockSpec((1, tile, w), lambda bi, i: (bi, i, 0)),
        out_shape=jax.ShapeDtypeStruct((b, s, w), BF16),
        scratch_shapes=[pltpu.VMEM((H_C, 2 * LANES, tile), BF16),
                        pltpu.VMEM((H_C, tile, LANES), BF16),
                        pltpu.VMEM((H_C, tile, tile), F32),
                        pltpu.VMEM((H_IDX * D_IDX, tile), BF16),
                        pltpu.VMEM((s, tile), I32),
                        pltpu.VMEM((s, tile), I16),
                        pltpu.VMEM((s, tile), I16),
                        pltpu.VMEM((H_C, 1, tile), F32),
                        pltpu.VMEM((H_C, DH_C + SUM_ROWS, tile), F32)] + _stage_scratch(tile),
        compiler_params=_params("arbitrary", "arbitrary"),
    )(proj, proj, vt, proj, kiw, kiw)


def _xattn_body(x_ref, res_ref, wq_ref, k_ref, v_ref, wo_ref, g_ref, b_ref, of_ref, ob_ref, *, dh, alpha):
    q_all = _dot(x_ref[0], wq_ref[...]).astype(BF16)
    outs = []
    for h in range(H_X):
        c = h * dh
        q = q_all[:, c:c + dh] * dh ** -0.5
        s = _dot_nt(q, k_ref[0, :, c:c + dh])
        p = jnp.exp(s - jnp.max(s, axis=-1, keepdims=True))
        l = jnp.sum(p, axis=-1, keepdims=True)
        outs.append((_dot(p.astype(BF16), v_ref[0, :, c:c + dh]) / l).astype(BF16))
    y = _dot(jnp.concatenate(outs, axis=-1), wo_ref[...])
    out = _layer_norm_rows(alpha * res_ref[0] + y, g_ref[...], b_ref[...])
    of_ref[0] = out
    ob_ref[0] = out.astype(BF16)


def _xattn_ln(x, res, wq, kv, wo, g, b, alpha, tq=512):
    bsz, s, d = x.shape
    n_mem = kv.shape[1]
    tq = min(tq, s)
    tile = pl.BlockSpec((1, tq, d), lambda bi, i: (bi, i, 0))
    whole = lambda shape: pl.BlockSpec(shape, lambda bi, i: (0,) * len(shape))
    mem = lambda col: pl.BlockSpec((1, n_mem, d), lambda bi, i: (bi, 0, col))
    return pl.pallas_call(
        functools.partial(_xattn_body, dh=d // H_X, alpha=alpha),
        grid=(bsz, s // tq),
        in_specs=[tile, tile, whole((d, d)), mem(0), mem(1), whole((d, d)), whole((1, d)), whole((1, d))],
        out_specs=[tile, tile],
        out_shape=[jax.ShapeDtypeStruct((bsz, s, d), F32), jax.ShapeDtypeStruct((bsz, s, d), BF16)],
        compiler_params=_params("parallel", "parallel"),
    )(x, res, wq, kv, kv, wo, g.reshape(1, d), b.reshape(1, d))


def _router_body(x_ref, r_ref, g_ref):
    logits = _dot(x_ref[...], r_ref[...])
    lane = lax.broadcasted_iota(I32, logits.shape, 1).astype(F32)
    big = float(LANES)
    lg = jnp.where(lane < N_EXPERTS, logits, -jnp.inf)
    m1 = jnp.max(lg, axis=-1, keepdims=True)
    i1 = jnp.min(jnp.where(lg == m1, lane, big), axis=-1, keepdims=True)
    lg2 = jnp.where(lane == i1, -jnp.inf, lg)
    m2 = jnp.max(lg2, axis=-1, keepdims=True)
    i2 = jnp.min(jnp.where(lg2 == m2, lane, big), axis=-1, keepdims=True)
    e = jnp.exp(m2 - m1)
    den = 1.0 + e
    g_ref[...] = jnp.where(lane == i1, 1.0 / den, 0.0) + jnp.where(lane == i2, e / den, 0.0)


def _router_gates(x, router, tm=1024):
    m, d = x.shape
    tm = min(tm, m)
    rpad = jnp.zeros((d, LANES), BF16).at[:, :N_EXPERTS].set(router.astype(BF16))
    return pl.pallas_call(
        _router_body,
        grid=(m // tm,),
        in_specs=[pl.BlockSpec((tm, d), lambda i: (i, 0)),
                  pl.BlockSpec((d, LANES), lambda i: (0, 0))],
        out_specs=pl.BlockSpec((tm, LANES), lambda i: (i, 0)),
        out_shape=jax.ShapeDtypeStruct((m, LANES), F32),
        compiler_params=_params("parallel"),
    )(x, rpad)


MOE_BLOCK = 896
MOE_CHUNK = 128
MOE_GROUP = 2
MOE_FF_TILE = 1792


def _moe_body(x_ref, gate_ref, tri_ref, w1_ref, w3_ref, w2_ref, o_ref,
              xs_scr, acc_scr, rcol_scr, rrow_scr, cnt_smem, *, n_tokens):
    e = pl.program_id(1)
    f = pl.program_id(2)
    nf = pl.num_programs(2)
    tb = x_ref.shape[0]
    ch = MOE_CHUNK
    gw = MOE_GROUP * ch

    tok0 = pl.program_id(0) * tb
    valid_col = lax.broadcasted_iota(I32, (tb, 1), 0) + tok0 < n_tokens

    @pl.when((e == 0) & (f == 0))
    def _():
        o_ref[...] = jnp.zeros(o_ref.shape, F32)
        gate = jnp.where(valid_col, gate_ref[...], 0.0)
        member = jnp.where(gate > 0.0, 1.0, 0.0)
        rank = _dot(tri_ref[...], member.astype(BF16))
        rcol = jnp.where(member > 0.0, rank, -1.0)
        rcol_scr[...] = rcol
        rrow_scr[...] = rcol.T
        counts = jnp.sum(member, axis=0, keepdims=True)
        for ee in range(N_EXPERTS):
            cnt_smem[ee] = counts[0, ee].astype(I32)

    count = cnt_smem[e]
    single = count <= gw
    nchunk = jnp.where(single, MOE_GROUP, (count + (ch - 1)) // ch)

    @pl.when(f == 0)
    def _():
        sub = lax.broadcasted_iota(I32, (SUBLANES, tb), 0)
        r_row = jnp.sum(jnp.where(sub == e, rrow_scr[0:SUBLANES, :], 0.0), axis=0, keepdims=True)
        xz = jnp.where(valid_col, x_ref[...], jnp.zeros((), BF16))

        def gather_rows(r0, rows):
            row_id = lax.broadcasted_iota(I32, (rows, tb), 0).astype(F32) + jnp.asarray(r0, F32)
            sel = jnp.where(row_id == r_row, 1.0, 0.0).astype(BF16)
            xs_scr[pl.ds(r0, rows), :] = _dot(sel, xz).astype(BF16)

        @pl.when(single)
        def _():
            acc_scr[0:gw, :] = jnp.zeros((gw, acc_scr.shape[1]), F32)
            gather_rows(0, gw)

        @pl.when(jnp.logical_not(single))
        def _():
            acc_scr[...] = jnp.zeros(acc_scr.shape, F32)

            def gather(c, carry):
                gather_rows(pl.multiple_of(c * ch, ch), ch)
                return carry
            lax.fori_loop(0, nchunk, gather, 0)

    def ffn_rows(r0, rows):
        xs = xs_scr[pl.ds(r0, rows), :]
        a = _dot(xs, w1_ref[0])
        b = _dot(xs, w3_ref[0])
        acc_scr[pl.ds(r0, rows), :] += _dot((_silu(a) * b).astype(BF16), w2_ref[0])

    @pl.when(single)
    def _():
        ffn_rows(0, gw)

    @pl.when(jnp.logical_not(single))
    def _():
        def ffn(c, carry):
            ffn_rows(pl.multiple_of(c * ch, ch), ch)
            return carry
        lax.fori_loop(0, nchunk, ffn, 0)

    @pl.when(f == nf - 1)
    def _():
        pick = lax.broadcasted_iota(I32, (tb, LANES), 1) == e
        r_col = jnp.sum(jnp.where(pick, rcol_scr[...], 0.0), axis=1, keepdims=True)
        g_col = jnp.sum(jnp.where(pick & valid_col, gate_ref[...], 0.0), axis=1, keepdims=True)
        col_id = lax.broadcasted_iota(I32, (tb, gw), 1).astype(F32)

        def combine(gi, carry):
            r0 = pl.multiple_of(gi * gw, gw)
            selt = jnp.where(col_id + r0.astype(F32) == r_col, 1.0, 0.0).astype(BF16)
            y = acc_scr[pl.ds(r0, gw), :]
            y_hi = y.astype(BF16)
            y_lo = (y - y_hi.astype(F32)).astype(BF16)
            o_ref[...] += g_col * (_dot(selt, y_hi) + _dot(selt, y_lo))
            return carry

        lax.fori_loop(0, (nchunk + (MOE_GROUP - 1)) // MOE_GROUP, combine, 0)


def _moe_experts(x, gates, w1, w3, w2):
    m, d = x.shape
    ne, _, ff = w1.shape
    tb = min(MOE_BLOCK, m)
    tf = MOE_FF_TILE if ff % MOE_FF_TILE == 0 else ff
    gw = MOE_GROUP * MOE_CHUNK
    cap = pl.cdiv(tb, gw) * gw
    assert tb % LANES == 0 and ne <= SUBLANES
    tri = jnp.asarray(np.tril(np.ones((tb, tb), np.float32), -1), BF16)
    row = lambda i, e, f: (i, 0)
    return pl.pallas_call(
        functools.partial(_moe_body, n_tokens=m),
        grid=(pl.cdiv(m, tb), ne, ff // tf),
        in_specs=[pl.BlockSpec((tb, d), row),
                  pl.BlockSpec((tb, LANES), row),
                  pl.BlockSpec((tb, tb), lambda i, e, f: (0, 0)),
                  pl.BlockSpec((1, d, tf), lambda i, e, f: (e, 0, f)),
                  pl.BlockSpec((1, d, tf), lambda i, e, f: (e, 0, f)),
                  pl.BlockSpec((1, tf, d), lambda i, e, f: (e, f, 0))],
        out_specs=pl.BlockSpec((tb, d), row),
        out_shape=jax.ShapeDtypeStruct((m, d), F32),
        scratch_shapes=[pltpu.VMEM((cap, d), BF16),
                        pltpu.VMEM((cap, d), F32),
                        pltpu.VMEM((tb, LANES), F32),
                        pltpu.VMEM((LANES, tb), F32),
                        pltpu.SMEM((N_EXPERTS,), I32)],
        compiler_params=_params("parallel", "arbitrary", "arbitrary"),
    )(x, gates, tri, w1, w3, w2)


def _res_ln_body(res_ref, y_ref, g_ref, b_ref, of_ref, ob_ref, *, alpha):
    out = _layer_norm_rows(alpha * res_ref[...] + y_ref[...], g_ref[...], b_ref[...])
    of_ref[...] = out
    ob_ref[...] = out.astype(BF16)


def _residual_ln(res, y, g, b, alpha, tm=512):
    m, d = res.shape
    tm = min(tm, m)
    row = pl.BlockSpec((tm, d), lambda i: (i, 0))
    vec = pl.BlockSpec((1, d), lambda i: (0, 0))
    return pl.pallas_call(
        functools.partial(_res_ln_body, alpha=alpha),
        grid=(m // tm,),
        in_specs=[row, row, vec, vec],
        out_specs=[row, row],
        out_shape=[jax.ShapeDtypeStruct((m, d), F32), jax.ShapeDtypeStruct((m, d), BF16)],
        compiler_params=_params("parallel"),
    )(res, y, g.reshape(1, d), b.reshape(1, d))


def _band_bias(table):
    band = (LEFT_CHUNKS + 1) * CHUNK
    left = LEFT_CHUNKS * CHUNK
    rev = table.astype(F32)[:, ::-1]
    pad = left - REL_MAX + CHUNK - 1
    n_rev = band - 1 - (left - REL_MAX) + 1
    assert n_rev <= 2 * REL_MAX + 1
    ext = jnp.concatenate([jnp.broadcast_to(rev[:, :1], (rev.shape[0], pad)), rev[:, :n_rev]], axis=1)
    n_ext = ext.shape[1]
    period = jnp.concatenate([ext[:, CHUNK - 1:], jnp.zeros((ext.shape[0], 1), F32), ext[:, :CHUNK - 1]], axis=1)
    skew = jnp.tile(period, (1, CHUNK))[:, :CHUNK * n_ext].reshape(-1, CHUNK, n_ext)
    return skew[:, :, :band]


def kernel(x, mem, w_in, lam_q1, lam_k1, lam_q2, lam_k2, diff_norm_g, rel_bias, w_out,
           ln1_g, ln1_b, xattn_wq, xattn_wk, xattn_wv, xattn_wo, ln2_g, ln2_b,
           ffn_w1, ffn_w3, ffn_w2, moe_router, moe_w1, moe_w3, moe_w2, ln3_g, ln3_b):
    b, s, d = x.shape
    depth = w_in.shape[0]
    t = b * s
    alpha = (2.0 * depth) ** 0.25
    slopes_a, slopes_c = _alibi_slopes()
    n_main = 3 * H_A * DV_A + 3 * H_B * DH_B + 3 * H_C * DH_C + H_IDX * D_IDX
    n_tail = D_IDX + H_IDX
    wa, wb, wc = H_A * DV_A, H_B * DH_B, H_C * DH_C
    left = LEFT_CHUNKS * CHUNK

    h = x.reshape(t, d)
    hb = h.astype(BF16)
    memb = mem.reshape(-1, d).astype(BF16)
    for l in range(depth):
        wl = w_in[l]
        o_b, o_c, o_i = 3 * wa, 3 * wa + 3 * wb, 3 * wa + 3 * wb + 3 * wc
        w_main = jnp.concatenate(
            [wl[:, :wa] * (DQK_A ** -0.5 * LOG2E), wl[:, wa:2 * wa], wl[:, o_i:o_i + H_IDX * D_IDX],
             wl[:, o_b:o_b + wb] * (DH_B ** -0.5 * LOG2E), wl[:, o_b + wb:o_b + 2 * wb],
             wl[:, o_c:o_c + wc] * (DH_C ** -0.5 * LOG2E), wl[:, o_c + wc:o_c + 2 * wc]], axis=1).astype(BF16)
        proj = _matmul(hb, w_main, BF16, tn=w_main.shape[1] // 2).reshape(b, s, -1)
        w_side = jnp.concatenate(
            [wl[:, 2 * wa:3 * wa], wl[:, o_c + 2 * wc:o_c + 3 * wc], wl[:, o_b + 2 * wb:o_b + 3 * wb],
             wl[:, n_main:], jnp.zeros((d, LANES - n_tail), F32)], axis=1).astype(BF16)
        tile = min(ATT_TILE, s)
        vt, vbt, kiw = _side_projections(hb, w_side, wa + wc, tile, wb, LANES)
        vt = vt.reshape(b, s // tile, wa + wc, tile)
        kiw = kiw.reshape(b, s, LANES)
        lam_init = 0.8 - 0.6 * math.exp(-0.3 * l)
        ya = _diff_attention(proj, vt, lam_q1[l], lam_k1[l], lam_q2[l], lam_k2[l], diff_norm_g[l],
                             slopes_a, lam_init)
        kpad = jnp.pad(proj[:, :, 3 * wa + wb:3 * wa + 2 * wb], ((0, 0), (left, 0), (0, 0)))
        vbt = jnp.pad(vbt.reshape(b, s // LANES, wb, LANES), ((0, 0), (left // LANES, 0), (0, 0), (0, 0)))
        yb = _band_attention(proj, kpad, vbt, _band_bias_table(rel_bias[l]))
        yc = _dsa_attention(proj, vt, kiw, slopes_c)
        wo = w_out[l].astype(BF16)
        h, hb = _matmul_ln(
            [ya.reshape(t, wa), yb.reshape(t, wb), yc.reshape(t, -1)],
            [wo[:wa], wo[wa:wa + wb], wo[wa + wb:]],
            h, ln1_g[l], ln1_b[l], alpha)
        w_kv = jnp.concatenate([xattn_wk[l], xattn_wv[l]], axis=1).astype(BF16)
        kv = _matmul(memb, w_kv, BF16, tn=d).reshape(b, -1, 2 * d)
        h, hb = _xattn_ln(hb.reshape(b, s, d), h.reshape(b, s, d), xattn_wq[l].astype(BF16), kv,
                          xattn_wo[l].astype(BF16), ln2_g[l], ln2_b[l], alpha)
        h, hb = h.reshape(t, d), hb.reshape(t, d)
        if l % 2 == 0:
            h, hb = _ffn_ln(hb, ffn_w1[l // 2].astype(BF16), ffn_w3[l // 2].astype(BF16),
                            ffn_w2[l // 2].astype(BF16), h, ln3_g[l], ln3_b[l], alpha)
        else:
            gates = _router_gates(hb, moe_router[l // 2])
            y = _moe_experts(hb, gates, moe_w1[l // 2].astype(BF16), moe_w3[l // 2].astype(BF16),
                             moe_w2[l // 2].astype(BF16))
            h, hb = _residual_ln(h, y, ln3_g[l], ln3_b[l], alpha)
    return h.reshape(b, s, d)
```
